```python
import jax
import jax.numpy as jnp
from jax import lax
import numpy as np

D_MODEL = 2048
BATCH = 4
SEQ = 2048
DEPTH = 2
DEC_BATCH = 32
DEC_SEQ = 4
PAST_LEN = 8192
PAGE_SIZE = 128

N_BRANCH = 4
BRANCH_WIDTH = D_MODEL // N_BRANCH
ATT_WINDOWS = (128, 512, 2048)
ATT_DILATIONS = (1, 4, 16)
ATT_GROUPS = 3
ATT_HEADS_PER_GROUP = 4
ATT_HEADS = ATT_GROUPS * ATT_HEADS_PER_GROUP
ATT_HEAD_DIM = BRANCH_WIDTH // ATT_HEADS_PER_GROUP
ATT_WIDTH = ATT_HEADS * ATT_HEAD_DIM
ATT_Q_BLOCK = 128
REL_BUCKETS = 32
REL_MAX_DISTANCE = 2048
GMLP_WIDTH = BRANCH_WIDTH
GMLP_GROUPS = 4
GMLP_GROUP_WIDTH = GMLP_WIDTH // GMLP_GROUPS
GMLP_CHUNK = 128
POOL_WINDOWS = (2, 4, 8, 16)
POOL_WIDTH = BRANCH_WIDTH
POOL_GROUP_WIDTH = POOL_WIDTH // 4
POOL_STATE = 15
RET_HEADS = 4
RET_HEAD_DIM = BRANCH_WIDTH // RET_HEADS
RET_WIDTH = BRANCH_WIDTH
RET_CHUNK = 128
ROPE_BASE = 10000.0
D_FF = 5632
EPS = 1e-6
IN_SIZES = (ATT_WIDTH, ATT_WIDTH, ATT_WIDTH, GMLP_WIDTH, GMLP_WIDTH, POOL_WIDTH,
            RET_WIDTH, RET_WIDTH, RET_WIDTH, RET_WIDTH, N_BRANCH * D_MODEL)
IN_WIDTH = 16384

kernel_name = "hybrid_dilated_gmlp_pool_retention_decoder_step"


def rmsnorm(x, g):
    xf = x.astype(jnp.float32)
    y = xf * lax.rsqrt(jnp.mean(xf * xf, axis=-1, keepdims=True) + EPS)
    return (y * g.astype(jnp.float32)).astype(x.dtype)


def swiglu(x, w_gate, w_up, w_down):
    return (jax.nn.silu(x @ w_gate) * (x @ w_up)) @ w_down


def t5_causal_buckets(dist):
    max_exact = REL_BUCKETS // 2
    d = np.maximum(dist, 1).astype(np.float32)
    large = max_exact + (np.log(d / max_exact) / np.log(REL_MAX_DISTANCE / max_exact)
                         * (REL_BUCKETS - max_exact)).astype(np.int32)
    large = np.minimum(large, REL_BUCKETS - 1)
    return np.where(dist < max_exact, dist, large).astype(np.int32)


def dilated_group_attention(q, k_ext, v_ext, n_past, dil, bias):
    B, T, H, hd = q.shape
    nk = bias.shape[0]
    qb = T if T <= ATT_Q_BLOCK else ATT_Q_BLOCK
    nb = T // qb
    steps = dil * jnp.arange(nk)
    scale = hd ** -0.5
    bias_t = bias.astype(jnp.float32).T

    def block(bi):
        s0 = bi * qb
        q_blk = lax.dynamic_slice_in_dim(q, s0, qb, axis=1)
        idx = n_past + s0 + jnp.arange(qb)[:, None] - steps[None, :]
        valid = idx >= 0
        idx = jnp.maximum(idx, 0)
        k_g = k_ext[:, idx]
        v_g = v_ext[:, idx]
        s = jnp.einsum('bqhd,bqnhd->bqhn', q_blk, k_g).astype(jnp.float32) * scale + bias_t
        s = jnp.where(valid[None, :, None, :], s, -jnp.inf)
        m = jnp.max(s, axis=-1, keepdims=True)
        p = jnp.exp(s - m)
        den = jnp.sum(p, axis=-1)
        o = jnp.einsum('bqhn,bqnhd->bqhd', p, v_g.astype(jnp.float32)) / den[..., None]
        return o, m[..., 0] + jnp.log(den)

    o, lse = lax.map(block, jnp.arange(nb))
    o = jnp.moveaxis(o, 0, 1).reshape(B, T, H, hd)
    lse = jnp.moveaxis(lse, 0, 1).reshape(B, T, H)
    return o, lse


def pool_mixer(xc, buf, start, w_pool, pool_scale):
    B, T, C = xc.shape
    L = buf.shape[1]
    ext = jnp.concatenate([buf.astype(xc.dtype), xc], axis=1)
    csum = jnp.cumsum(ext.astype(jnp.float32), axis=1)
    csum = jnp.concatenate([jnp.zeros((B, 1, C), jnp.float32), csum], axis=1)
    end = L + 1 + jnp.arange(T)
    pos = start + jnp.arange(T)
    pooled = []
    for gi, win in enumerate(POOL_WINDOWS):
        sl = slice(gi * POOL_GROUP_WIDTH, (gi + 1) * POOL_GROUP_WIDTH)
        wsum = csum[:, end, sl] - csum[:, end - win, sl]
        cnt = jnp.minimum(pos + 1, win).astype(jnp.float32)
        pooled.append(wsum / cnt[None, :, None])
    diff = jnp.concatenate(pooled, axis=-1) - xc.astype(jnp.float32)
    diff = diff.reshape(B, T, len(POOL_WINDOWS), POOL_GROUP_WIDTH)
    y = jnp.einsum('btgc,gcd->btgd', diff, w_pool.astype(jnp.float32)).reshape(B, T, C)
    y = y * pool_scale.astype(jnp.float32)
    return y.astype(xc.dtype), ext[:, ext.shape[1] - L:]


def rotate(x, pos):
    d = x.shape[-1]
    half = d // 2
    inv = ROPE_BASE ** (-jnp.arange(half, dtype=jnp.float32) / half)
    ang = pos.astype(jnp.float32)[:, None] * inv[None, :]
    cos = jnp.cos(ang)[None, :, None, :]
    sin = jnp.sin(ang)[None, :, None, :]
    x1 = x[..., :half].astype(jnp.float32)
    x2 = x[..., half:].astype(jnp.float32)
    return jnp.concatenate([x1 * cos - x2 * sin, x1 * sin + x2 * cos], axis=-1)


def retention(q, k, v, s0):
    B, T, H, d = q.shape
    c = RET_CHUNK if T % RET_CHUNK == 0 else T
    n = T // c
    lg = jnp.log1p(-jnp.power(2.0, -5.0 - jnp.arange(H, dtype=jnp.float32)))
    i = jnp.arange(c, dtype=jnp.float32)
    diff = i[:, None] - i[None, :]
    inner_decay = jnp.where(diff >= 0, jnp.exp(jnp.maximum(diff, 0.0)[None] * lg[:, None, None]), 0.0)
    q_decay = jnp.exp((i + 1.0)[None, :] * lg[:, None])
    k_decay = jnp.exp((c - 1.0 - i)[None, :] * lg[:, None])
    chunk_decay = jnp.exp(c * lg)

    def to_chunks(a):
        return a.astype(jnp.float32).reshape(B, n, c, H, d).transpose(1, 0, 3, 2, 4)

    def step(S, inp):
        qi, ki, vi = inp
        att = jnp.einsum('bhtd,bhsd->bhts', qi, ki) * inner_decay
        o = (jnp.einsum('bhts,bhsd->bhtd', att, vi)
             + jnp.einsum('bhtd,bhde->bhte', qi, S) * q_decay[None, :, :, None])
        S = (S * chunk_decay[None, :, None, None]
             + jnp.einsum('bhsd,bhse->bhde', ki * k_decay[None, :, :, None], vi))
        return S, o

    S, o = lax.scan(step, s0.astype(jnp.float32), (to_chunks(q), to_chunks(k), to_chunks(v)))
    o = o.transpose(1, 0, 3, 2, 4).reshape(B, T, H, d)
    return o, S


def decoder_layer(x, start, kv_bufs, pool_buf, ret_state, rel_bias,
                  g_ffn1, w_ffn1_gate, w_ffn1_up, w_ffn1_down,
                  g_mix, w_in, g_gmlp, w_spatial, b_spatial, w_pool, pool_scale, g_ret,
                  w_branch, w_out, g_ffn2, w_ffn2_gate, w_ffn2_up, w_ffn2_down):
    B, T, _ = x.shape
    f32 = jnp.float32
    x = x + 0.5 * swiglu(rmsnorm(x, g_ffn1), w_ffn1_gate, w_ffn1_up, w_ffn1_down)
    h = rmsnorm(x, g_mix)
    z = h @ w_in
    splits = [int(s) for s in np.cumsum(IN_SIZES)[:-1]]
    a_q, a_k, a_v, b_u, b_v, c_in, d_q, d_k, d_v, d_g, z_gate = jnp.split(z, splits, axis=-1)

    hg, hd = ATT_HEADS_PER_GROUP, ATT_HEAD_DIM
    shp = (B, T, ATT_GROUPS, hg, hd)
    a_q, a_k, a_v = a_q.reshape(shp), a_k.reshape(shp), a_v.reshape(shp)
    outs, lses, kv_new = [], [], []
    for gi in range(ATT_GROUPS):
        win, dil = ATT_WINDOWS[gi], ATT_DILATIONS[gi]
        kv_g = jnp.stack([a_k[:, :, gi], a_v[:, :, gi]], axis=2)
        kv_new.append(kv_g)
        kv_ext = jnp.concatenate([kv_bufs[gi].astype(kv_g.dtype), kv_g], axis=1)
        n_key = win // dil + 1
        bucket = t5_causal_buckets(dil * np.arange(n_key))
        bias = rel_bias[bucket][:, gi * hg:(gi + 1) * hg]
        o, lse = dilated_group_attention(a_q[:, :, gi], kv_ext[:, :, 0], kv_ext[:, :, 1],
                                         kv_bufs[gi].shape[1], dil, bias)
        outs.append(o)
        lses.append(lse)
    w_den = jax.nn.softmax(jnp.stack(lses), axis=0)
    o_a = jnp.sum(w_den[..., None] * jnp.stack(outs), axis=0).reshape(B, T, BRANCH_WIDTH).astype(x.dtype)

    u = jax.nn.gelu(b_u)
    vn = rmsnorm(jax.nn.gelu(b_v), g_gmlp)
    c = T if T < GMLP_CHUNK else GMLP_CHUNK
    vc = vn.reshape(B, T // c, c, GMLP_GROUPS, GMLP_GROUP_WIDTH)
    ws = w_spatial[:, :c, :c] * jnp.tril(jnp.ones((c, c), w_spatial.dtype))
    mixed = jnp.einsum('gts,bnsgc->bntgc', ws, vc) + b_spatial[:, :c].T[None, None, :, :, None]
    o_b = u * mixed.reshape(B, T, GMLP_WIDTH)

    o_c, pool_new = pool_mixer(c_in, pool_buf, start, w_pool, pool_scale)

    hs = (B, T, RET_HEADS, RET_HEAD_DIM)
    pos = start + jnp.arange(T)
    rq = rotate(d_q.reshape(hs), pos)
    rk = rotate(d_k.reshape(hs), pos) * (RET_HEAD_DIM ** -0.5)
    o_d, ret_new = retention(rq, rk, d_v.reshape(hs), ret_state)
    o_d = o_d * lax.rsqrt(jnp.mean(o_d * o_d, axis=-1, keepdims=True) + EPS)
    o_d = (o_d.reshape(B, T, RET_WIDTH) * g_ret.astype(f32) * jax.nn.silu(d_g.astype(f32))).astype(x.dtype)

    br = jnp.stack([o_a, o_b.astype(x.dtype), o_c, o_d], axis=2)
    proj = jnp.einsum('btnc,ncd->btnd', br, w_branch)
    gate = jax.nn.sigmoid(z_gate.reshape(B, T, N_BRANCH, D_MODEL))
    x = x + jnp.sum(gate * proj, axis=2) @ w_out
    x = x + 0.5 * swiglu(rmsnorm(x, g_ffn2), w_ffn2_gate, w_ffn2_up, w_ffn2_down)
    return x, kv_new, pool_new, ret_new, vn


def setup_inputs(seed: int = 0) -> dict:
    key = jax.random.key(seed)
    keys = jax.random.split(key, 32)
    f32 = jnp.float32
    hg, hd = ATT_HEADS_PER_GROUP, ATT_HEAD_DIM

    def nrm(i, shape, scale):
        return jax.random.normal(keys[i], shape, f32) * scale

    def gain(i, shape):
        return 1.0 + 0.05 * jax.random.normal(keys[i], shape, f32)

    return {
        "x_prompt": nrm(0, (BATCH, SEQ, D_MODEL), 1.0),
        "x_sample": nrm(1, (DEC_BATCH, DEC_SEQ, D_MODEL), 1.0),
        "cache_attn_kv_w128": nrm(2, (DEPTH, DEC_BATCH, min(ATT_WINDOWS[0], PAST_LEN), 2, hg, hd), 1.0),
        "cache_attn_kv_w512": nrm(3, (DEPTH, DEC_BATCH, min(ATT_WINDOWS[1], PAST_LEN), 2, hg, hd), 1.0),
        "cache_attn_kv_w2048": nrm(4, (DEPTH, DEC_BATCH, min(ATT_WINDOWS[2], PAST_LEN), 2, hg, hd), 1.0),
        "state_pool": nrm(5, (DEPTH, DEC_BATCH, POOL_STATE, POOL_WIDTH), 1.0),
        "state_ret": nrm(6, (DEPTH, DEC_BATCH, RET_HEADS, RET_HEAD_DIM, RET_HEAD_DIM), 0.3),
        "rel_bias": nrm(7, (REL_BUCKETS, ATT_HEADS), 0.5),
        "g_ffn1": gain(8, (DEPTH, D_MODEL)),
        "w_ffn1_gate": nrm(9, (DEPTH, D_MODEL, D_FF), D_MODEL ** -0.5),
        "w_ffn1_up": nrm(10, (DEPTH, D_MODEL, D_FF), D_MODEL ** -0.5),
        "w_ffn1_down": nrm(11, (DEPTH, D_FF, D_MODEL), D_FF ** -0.5),
        "g_mix": gain(12, (DEPTH, D_MODEL)),
        "w_in": nrm(13, (DEPTH, D_MODEL, IN_WIDTH), D_MODEL ** -0.5),
        "g_gmlp": gain(14, (DEPTH, GMLP_WIDTH)),
        "w_spatial": nrm(15, (DEPTH, GMLP_GROUPS, GMLP_CHUNK, GMLP_CHUNK), GMLP_CHUNK ** -0.5),
        "b_spatial": 1.0 + nrm(16, (DEPTH, GMLP_GROUPS, GMLP_CHUNK), 0.02),
        "w_pool": nrm(17, (DEPTH, len(POOL_WINDOWS), POOL_GROUP_WIDTH, POOL_GROUP_WIDTH), POOL_GROUP_WIDTH ** -0.5),
        "pool_scale": gain(18, (DEPTH, POOL_WIDTH)),
        "g_ret": gain(19, (DEPTH, RET_WIDTH)),
        "w_branch": nrm(20, (DEPTH, N_BRANCH, BRANCH_WIDTH, D_MODEL), BRANCH_WIDTH ** -0.5),
        "w_out": nrm(21, (DEPTH, D_MODEL, D_MODEL), D_MODEL ** -0.5),
        "g_ffn2": gain(22, (DEPTH, D_MODEL)),
        "w_ffn2_gate": nrm(23, (DEPTH, D_MODEL, D_FF), D_MODEL ** -0.5),
        "w_ffn2_up": nrm(24, (DEPTH, D_MODEL, D_FF), D_MODEL ** -0.5),
        "w_ffn2_down": nrm(25, (DEPTH, D_FF, D_MODEL), D_FF ** -0.5),
        "g_final": gain(26, (D_MODEL,)),
    }


def reference(x_prompt, x_sample, cache_attn_kv_w128, cache_attn_kv_w512, cache_attn_kv_w2048,
              state_pool, state_ret, rel_bias,
              g_ffn1, w_ffn1_gate, w_ffn1_up, w_ffn1_down,
              g_mix, w_in, g_gmlp, w_spatial, b_spatial, w_pool, pool_scale, g_ret,
              w_branch, w_out, g_ffn2, w_ffn2_gate, w_ffn2_up, w_ffn2_down, g_final):
    caches = (cache_attn_kv_w128, cache_attn_kv_w512, cache_attn_kv_w2048)
    bp, tp = x_prompt.shape[0], x_prompt.shape[1]
    hp, hs = x_prompt, x_sample
    kv_p = [[], [], []]
    kv_s = [[], [], []]
    pool_p, pool_s, ret_p, ret_s, gv_s = [], [], [], [], []
    for l in range(DEPTH):
        lw = (rel_bias, g_ffn1[l], w_ffn1_gate[l], w_ffn1_up[l], w_ffn1_down[l],
              g_mix[l], w_in[l], g_gmlp[l], w_spatial[l], b_spatial[l], w_pool[l], pool_scale[l], g_ret[l],
              w_branch[l], w_out[l], g_ffn2[l], w_ffn2_gate[l], w_ffn2_up[l], w_ffn2_down[l])
        empty_kv = tuple(jnp.zeros((bp, 0, 2, ATT_HEADS_PER_GROUP, ATT_HEAD_DIM), hp.dtype)
                         for _ in ATT_WINDOWS)
        hp, kvn, pooln, retn, _ = decoder_layer(
            hp, 0, empty_kv, jnp.zeros((bp, POOL_STATE, POOL_WIDTH), hp.dtype),
            jnp.zeros((bp, RET_HEADS, RET_HEAD_DIM, RET_HEAD_DIM), jnp.float32), *lw)
        for gi in range(ATT_GROUPS):
            keep = min(ATT_WINDOWS[gi], tp)
            kv_p[gi].append(kvn[gi][:, tp - keep:])
        pool_p.append(pooln)
        ret_p.append(retn)
        hs, kvn, pooln, retn, gvn = decoder_layer(
            hs, PAST_LEN, tuple(cc[l] for cc in caches), state_pool[l], state_ret[l], *lw)
        for gi in range(ATT_GROUPS):
            kv_s[gi].append(kvn[gi])
        pool_s.append(pooln)
        ret_s.append(retn)
        gv_s.append(gvn)
    y_prompt = rmsnorm(hp, g_final)
    y_sample = rmsnorm(hs, g_final)
    return (y_prompt, y_sample,
            jnp.stack(kv_p[0]), jnp.stack(kv_p[1]), jnp.stack(kv_p[2]),
            jnp.stack(kv_s[0]), jnp.stack(kv_s[1]), jnp.stack(kv_s[2]),
            jnp.stack(pool_p), jnp.stack(pool_s),
            jnp.stack(ret_p), jnp.stack(ret_s),
            jnp.stack(gv_s))
```

```python
import functools

import numpy as np
import jax
import jax.numpy as jnp
from jax import lax
from jax.experimental import pallas as pl
from jax.experimental.pallas import tpu as pltpu

F32 = jnp.float32
BF16 = jnp.bfloat16

D_MODEL = 2048
BATCH = 4
SEQ = 2048
DEPTH = 2
DEC_BATCH = 32
DEC_SEQ = 4
PAST_LEN = 8192
D_FF = 5632
IN_WIDTH = 16384
EPS = 1e-6
BW = 512
HD = 128
NH = 4
ATT_WINDOWS = (128, 512, 2048)
ATT_DILATIONS = (1, 4, 16)
NKEY = 129
REL_BUCKETS = 32
REL_MAX_DISTANCE = 2048
POOL_WINDOWS = (2, 4, 8, 16)
POOL_STATE = 15
ROPE_BASE = 10000.0
SROWS = 8
CH = 128
NEG = -1e30

COL_Q, COL_K, COL_V = 0, 3, 6
COL_BU, COL_BV, COL_CIN = 9, 10, 11
COL_DQ, COL_DK, COL_DV, COL_DG = 12, 13, 14, 15
COL_GATE = 16
ZBLK = IN_WIDTH // BW

VMEM_BIG = 56 * 1024 * 1024


def _cparams(n_axes, vmem=None):
    return pltpu.CompilerParams(dimension_semantics=("arbitrary",) * n_axes, vmem_limit_bytes=vmem)


def _rms(x, g):
    return x * lax.rsqrt(jnp.mean(x * x, axis=-1, keepdims=True) + EPS) * g


def _ffn_body(x_ref, gpre_ref, wg_ref, wu_ref, wd_ref, gpost_ref, *refs, emit_x, emit_post, nf):
    outs = refs[:int(emit_x) + int(emit_post)]
    xn_ref, acc_ref = refs[-2:]
    f = pl.program_id(1)

    @pl.when(f == 0)
    def _():
        xn_ref[...] = _rms(x_ref[...], gpre_ref[...]).astype(BF16)
        acc_ref[...] = jnp.zeros_like(acc_ref)

    xn = xn_ref[...]
    g = jnp.dot(xn, wg_ref[...], preferred_element_type=F32)
    u = jnp.dot(xn, wu_ref[...], preferred_element_type=F32)
    h = (g * jax.nn.sigmoid(g) * u).astype(BF16)
    acc_ref[...] += jnp.dot(h, wd_ref[...], preferred_element_type=F32)

    @pl.when(f == nf - 1)
    def _():
        xo = x_ref[...] + 0.5 * acc_ref[...]
        k = 0
        if emit_x:
            outs[k][...] = xo
            k += 1
        if emit_post:
            outs[k][...] = _rms(xo, gpost_ref[...]).astype(outs[k].dtype)


def _ffn(x, g_pre, wg, wu, wd, g_post, *, emit_x, post_dtype, tm, tf=512):
    m = x.shape[0]
    nf = D_FF // tf
    emit_post = post_dtype is not None
    out_shape, out_specs = [], []
    if emit_x:
        out_shape.append(jax.ShapeDtypeStruct((m, D_MODEL), F32))
        out_specs.append(pl.BlockSpec((tm, D_MODEL), lambda i, f: (i, 0)))
    if emit_post:
        out_shape.append(jax.ShapeDtypeStruct((m, D_MODEL), post_dtype))
        out_specs.append(pl.BlockSpec((tm, D_MODEL), lambda i, f: (i, 0)))
    return pl.pallas_call(
        functools.partial(_ffn_body, emit_x=emit_x, emit_post=emit_post, nf=nf),
        grid=(m // tm, nf),
        in_specs=[
            pl.BlockSpec((tm, D_MODEL), lambda i, f: (i, 0)),
            pl.BlockSpec((1, D_MODEL), lambda i, f: (0, 0)),
            pl.BlockSpec((D_MODEL, tf), lambda i, f: (0, f)),
            pl.BlockSpec((D_MODEL, tf), lambda i, f: (0, f)),
            pl.BlockSpec((tf, D_MODEL), lambda i, f: (f, 0)),
            pl.BlockSpec((1, D_MODEL), lambda i, f: (0, 0)),
        ],
        out_specs=out_specs,
        out_shape=out_shape,
        scratch_shapes=[pltpu.VMEM((tm, D_MODEL), BF16), pltpu.VMEM((tm, D_MODEL), F32)],
        compiler_params=_cparams(2, VMEM_BIG),
        name="ffn",
    )(x, g_pre.reshape(1, -1), wg, wu, wd, g_post.reshape(1, -1))


def _mm_body(a_ref, b_ref, o_ref):
    o_ref[...] = jnp.dot(a_ref[...], b_ref[...], preferred_element_type=F32)


def _mm_res_body(a_ref, b_ref, r_ref, o_ref):
    o_ref[...] = r_ref[...] + jnp.dot(a_ref[...], b_ref[...], preferred_element_type=F32)


def _matmul(a, b, res=None, *, tm, tn, name):
    m, k = a.shape
    n = b.shape[1]
    in_specs = [pl.BlockSpec((tm, k), lambda i, j: (i, 0)), pl.BlockSpec((k, tn), lambda i, j: (0, j))]
    args = [a, b]
    body = _mm_body
    if res is not None:
        in_specs.append(pl.BlockSpec((tm, tn), lambda i, j: (i, j)))
        args.append(res)
        body = _mm_res_body
    return pl.pallas_call(
        body,
        grid=(m // tm, n // tn),
        in_specs=in_specs,
        out_specs=pl.BlockSpec((tm, tn), lambda i, j: (i, j)),
        out_shape=jax.ShapeDtypeStruct((m, n), F32),
        compiler_params=_cparams(2, VMEM_BIG),
        name=name,
    )(*args)


def _branch_body(oa_ref, ob_ref, oc_ref, od_ref, z0_ref, z1_ref, z2_ref, z3_ref, wb_ref, o_ref):
    acc = None
    for n, (br, zg) in enumerate(((oa_ref, z0_ref), (ob_ref, z1_ref), (oc_ref, z2_ref), (od_ref, z3_ref))):
        proj = jnp.dot(br[...], wb_ref[n], preferred_element_type=F32)
        t = jax.nn.sigmoid(zg[...]) * proj
        acc = t if acc is None else acc + t
    o_ref[...] = acc.astype(BF16)


def _branch_merge(oa, ob, oc, od, z, wb, *, tm, tn=512):
    m = oa.shape[0]
    gate_blk = (COL_GATE * BW) // tn
    per_branch = D_MODEL // tn
    br_spec = pl.BlockSpec((tm, BW), lambda i, j: (i, 0))
    z_specs = [pl.BlockSpec((tm, tn), functools.partial(lambda i, j, n: (i, gate_blk + n * per_branch + j), n=n))
               for n in range(4)]
    return pl.pallas_call(
        _branch_body,
        grid=(m // tm, D_MODEL // tn),
        in_specs=[br_spec] * 4 + z_specs + [pl.BlockSpec((4, BW, tn), lambda i, j: (0, 0, j))],
        out_specs=pl.BlockSpec((tm, tn), lambda i, j: (i, j)),
        out_shape=jax.ShapeDtypeStruct((m, D_MODEL), BF16),
        compiler_params=_cparams(2, VMEM_BIG),
        name="branch_merge",
    )(oa, ob, oc, od, z, z, z, z, wb)


def _t5_buckets(dist):
    max_exact = REL_BUCKETS // 2
    d = np.maximum(dist, 1).astype(np.float32)
    large = max_exact + (np.log(d / max_exact) / np.log(REL_MAX_DISTANCE / max_exact)
                         * (REL_BUCKETS - max_exact)).astype(np.int32)
    large = np.minimum(large, REL_BUCKETS - 1)
    return np.where(dist < max_exact, dist, large).astype(np.int32)


def _bucket_of_step(dil):
    return _t5_buckets(dil * np.arange(NKEY))


def _prompt_bucket_matrix(dil):
    iq = np.arange(CH)[:, None]
    col = np.arange(2 * CH)[None, :]
    j = iq + CH - col
    valid = (j >= 0) & (j < NKEY)
    return np.where(valid, _bucket_of_step(dil)[np.clip(j, 0, NKEY - 1)], -1).astype(np.int32)


def _sample_bucket_blocks():
    t = np.tile(np.arange(SROWS), NH)[:, None]
    lane = np.arange(CH)[None, :]
    blocks, groups = [], []
    for gi, dil in enumerate(ATT_DILATIONS):
        bos = _bucket_of_step(dil)
        if dil == 1:
            j = CH + t - lane
            valid = (t < DEC_SEQ) & (j <= CH)
            blocks.append(np.where(valid, bos[np.clip(j, 0, NKEY - 1)], -1))
            groups.append(gi)
            jn = t - lane
            validn = (t < DEC_SEQ) & (lane < DEC_SEQ) & (jn >= 0)
            blocks.append(np.where(validn, bos[np.clip(jn, 0, NKEY - 1)], -1))
            groups.append(gi)
        else:
            for r in range(DEC_SEQ):
                j = np.broadcast_to(CH - lane, (NH * SROWS, CH))
                valid = np.broadcast_to(t == r, (NH * SROWS, CH))
                blocks.append(np.where(valid, bos[j], -1))
                groups.append(gi)
            validn = (t < DEC_SEQ) & (lane == t)
            blocks.append(np.where(validn, bos[0], -1))
            groups.append(gi)
    return np.stack(blocks).astype(np.int32), tuple(groups)


def _bias_from_buckets(bk, rb_ref, col):
    out = jnp.full(bk.shape, NEG, F32)
    for b in range(REL_BUCKETS):
        out = jnp.where(bk == b, rb_ref[b, col], out)
    return out


def _attn_prompt_body(rb_ref, bk_ref, q_ref, kc_ref, vc_ref, *refs, gi, has_prev):
    if has_prev:
        kp_ref, vp_ref = refs[:2]
        refs = refs[2:]
    o_ref, lse_ref, kvo_ref, bias_ref = refs
    first = (pl.program_id(0) == 0) & (pl.program_id(1) == 0) & (pl.program_id(2) == 0)

    @pl.when(first)
    def _():
        bk = bk_ref[...]
        for h in range(NH):
            bias_ref[h] = _bias_from_buckets(bk, rb_ref, gi * NH + h)

    ib = pl.program_id(2)
    kc = kc_ref[...]
    vc = vc_ref[...]
    kvo_ref[:, :BW] = kc
    kvo_ref[:, BW:] = vc
    q = q_ref[...]
    scale = HD ** -0.5
    if has_prev:
        col = lax.broadcasted_iota(jnp.int32, (CH, 2 * CH), 1)
        no_prev = (ib == 0) & (col < CH)
    for h in range(NH):
        sl = slice(h * HD, (h + 1) * HD)
        qh = q[:, sl].astype(BF16)
        if has_prev:
            kh = jnp.concatenate([kp_ref[:, sl], kc[:, sl]], axis=0).astype(BF16)
            vh = jnp.concatenate([vp_ref[:, sl], vc[:, sl]], axis=0).astype(BF16)
            bias = jnp.where(no_prev, NEG, bias_ref[h])
        else:
            kh = kc[:, sl].astype(BF16)
            vh = vc[:, sl].astype(BF16)
            bias = bias_ref[h][:, CH:]
        s = lax.dot_general(qh, kh, (((1,), (1,)), ((), ())), preferred_element_type=F32) * scale + bias
        m = jnp.max(s, axis=-1, keepdims=True)
        p = jnp.exp(s - m)
        den = jnp.sum(p, axis=-1, keepdims=True)
        acc = jnp.dot(p.astype(BF16), vh, preferred_element_type=F32)
        o_ref[:, sl] = acc / den
        lse_ref[:, sl] = jnp.broadcast_to(m + jnp.log(den), (CH, HD))


def _attn_prompt(z, rel_bias, gi):
    dil = ATT_DILATIONS[gi]
    rows = BATCH * SEQ
    nb = SEQ // dil // CH
    has_prev = nb > 1
    zv = z.reshape(rows // dil, dil * IN_WIDTH)
    bk = jnp.asarray(_prompt_bucket_matrix(dil))

    def zspec(colblk, prev=False):
        if prev:
            return pl.BlockSpec((CH, BW), lambda b, r, i: (b * nb + jnp.maximum(i - 1, 0), r * ZBLK + colblk))
        return pl.BlockSpec((CH, BW), lambda b, r, i: (b * nb + i, r * ZBLK + colblk))

    in_specs = [pl.BlockSpec(memory_space=pltpu.SMEM),
                pl.BlockSpec((CH, 2 * CH), lambda b, r, i: (0, 0)),
                zspec(COL_Q + gi), zspec(COL_K + gi), zspec(COL_V + gi)]
    args = [rel_bias, bk, zv, zv, zv]
    if has_prev:
        in_specs += [zspec(COL_K + gi, True), zspec(COL_V + gi, True)]
        args += [zv, zv]
    o_spec = pl.BlockSpec((CH, BW), lambda b, r, i: (b * nb + i, r))
    o, lse, kvo = pl.pallas_call(
        functools.partial(_attn_prompt_body, gi=gi, has_prev=has_prev),
        grid=(BATCH, dil, nb),
        in_specs=in_specs,
        out_specs=[o_spec, o_spec, pl.BlockSpec((CH, 2 * BW), lambda b, r, i: (b, r))],
        out_shape=[jax.ShapeDtypeStruct((rows // dil, dil * BW), F32),
                   jax.ShapeDtypeStruct((rows // dil, dil * BW), F32),
                   jax.ShapeDtypeStruct((BATCH * CH, dil * 2 * BW), F32)],
        scratch_shapes=[pltpu.VMEM((NH, CH, 2 * CH), F32)],
        compiler_params=_cparams(3),
        name=f"attn_prompt_g{gi}",
    )(*args)
    kv = kvo.reshape(BATCH, ATT_WINDOWS[gi], 2, NH, HD)
    return o.reshape(rows, BW), lse.reshape(rows, BW), kv


def _combine_body(o0, l0, o1, l1, o2, l2, out_ref):
    a0, a1, a2 = l0[...], l1[...], l2[...]
    m = jnp.maximum(jnp.maximum(a0, a1), a2)
    w0, w1, w2 = jnp.exp(a0 - m), jnp.exp(a1 - m), jnp.exp(a2 - m)
    out_ref[...] = ((w0 * o0[...] + w1 * o1[...] + w2 * o2[...]) / (w0 + w1 + w2)).astype(BF16)


def _attn_combine(parts, *, tm=1024):
    rows = parts[0].shape[0]
    spec = pl.BlockSpec((tm, BW), lambda i: (i, 0))
    return pl.pallas_call(
        _combine_body,
        grid=(rows // tm,),
        in_specs=[spec] * 6,
        out_specs=spec,
        out_shape=jax.ShapeDtypeStruct((rows, BW), BF16),
        compiler_params=_cparams(1),
        name="attn_combine",
    )(*parts)


def _attn_sample_body(rb_ref, bk_ref, z_ref, c0_ref, c1_ref, c2_ref, oa_ref, kv0_ref, kv1_ref, kv2_ref, bias_ref,
                      *, blk_groups):
    nblk = len(blk_groups)

    @pl.when(pl.program_id(0) == 0)
    def _():
        for k in range(nblk):
            for h in range(NH):
                rs = slice(h * SROWS, (h + 1) * SROWS)
                bias_ref[k, rs, :] = _bias_from_buckets(bk_ref[k, rs, :], rb_ref, blk_groups[k] * NH + h)

    scale = HD ** -0.5
    rows = NH * SROWS
    head_of_row = lax.broadcasted_iota(jnp.int32, (rows, BW), 0) // SROWS
    head_of_lane = lax.broadcasted_iota(jnp.int32, (rows, BW), 1) // HD
    head_mask = head_of_row == head_of_lane
    caches = (c0_ref, c1_ref, c2_ref)
    kv_refs = (kv0_ref, kv1_ref, kv2_ref)
    stats = []
    k = 0
    for gi, dil in enumerate(ATT_DILATIONS):
        q = z_ref[:, (COL_Q + gi) * BW:(COL_Q + gi + 1) * BW]
        kn = z_ref[:, (COL_K + gi) * BW:(COL_K + gi + 1) * BW]
        vn = z_ref[:, (COL_V + gi) * BW:(COL_V + gi + 1) * BW]
        kv_refs[gi][:, :BW] = kn
        kv_refs[gi][:, BW:] = vn
        qrows = jnp.where(head_mask, jnp.concatenate([q] * NH, axis=0), 0.0)
        qb = qrows.astype(BF16)
        c_ref = caches[gi]
        n_cache = 1 if dil == 1 else DEC_SEQ
        s_blocks = []
        for r in range(n_cache):
            kblk = c_ref[:, r * 2 * BW:r * 2 * BW + BW].astype(BF16)
            s = lax.dot_general(qb, kblk, (((1,), (1,)), ((), ())), preferred_element_type=F32)
            s_blocks.append(s * scale + bias_ref[k])
            k += 1
        bias_n = bias_ref[k]
        k += 1
        s_new = []
        for tp in range(DEC_SEQ):
            dotp = jnp.sum(qrows * kn[tp:tp + 1, :], axis=-1, keepdims=True)
            s_new.append(dotp * scale + bias_n[:, tp:tp + 1])
        m = s_new[0]
        for s in s_new[1:]:
            m = jnp.maximum(m, s)
        for s in s_blocks:
            m = jnp.maximum(m, jnp.max(s, axis=-1, keepdims=True))
        den = jnp.zeros((rows, 1), F32)
        acc = jnp.zeros((rows, BW), F32)
        for r, s in enumerate(s_blocks):
            p = jnp.exp(s - m)
            den = den + jnp.sum(p, axis=-1, keepdims=True)
            vblk = c_ref[:, r * 2 * BW + BW:(r + 1) * 2 * BW].astype(BF16)
            acc = acc + jnp.dot(p.astype(BF16), vblk, preferred_element_type=F32)
        for tp, s in enumerate(s_new):
            p = jnp.exp(s - m)
            den = den + p
            acc = acc + p * vn[tp:tp + 1, :]
        stats.append((m, den, acc))
    mm = jnp.maximum(jnp.maximum(stats[0][0], stats[1][0]), stats[2][0])
    den = jnp.zeros((rows, 1), F32)
    acc = jnp.zeros((rows, BW), F32)
    for m, d, a in stats:
        w = jnp.exp(m - mm)
        den = den + w * d
        acc = acc + w * a
    o = acc / den
    for h in range(NH):
        oa_ref[:, h * HD:(h + 1) * HD] = o[h * SROWS:(h + 1) * SROWS, h * HD:(h + 1) * HD].astype(BF16)


def _attn_sample(z, caches, rel_bias, layer):
    bk_np, blk_groups = _sample_bucket_blocks()
    nblk = bk_np.shape[0]
    rows = DEC_BATCH * SROWS
    c0 = caches[0].reshape(DEPTH, DEC_BATCH, CH, 2 * BW)
    c1 = caches[1].reshape(DEPTH, DEC_BATCH, CH, 4 * 2 * BW)
    c2 = caches[2].reshape(DEPTH, DEC_BATCH, CH, 16 * 2 * BW)
    kv_spec = pl.BlockSpec((SROWS, 2 * BW), lambda b: (b, 0))
    kv_shape = jax.ShapeDtypeStruct((rows, 2 * BW), F32)
    return pl.pallas_call(
        functools.partial(_attn_sample_body, blk_groups=blk_groups),
        grid=(DEC_BATCH,),
        in_specs=[pl.BlockSpec(memory_space=pltpu.SMEM),
                  pl.BlockSpec((nblk, NH * SROWS, CH), lambda b: (0, 0, 0)),
                  pl.BlockSpec((SROWS, IN_WIDTH // 2), lambda b: (b, 0)),
                  pl.BlockSpec((None, None, CH, 2 * BW), lambda b: (layer, b, 0, 0)),
                  pl.BlockSpec((None, None, CH, 4 * 2 * BW), lambda b: (layer, b, 0, 0)),
                  pl.BlockSpec((None, None, CH, 4 * 2 * BW), lambda b: (layer, b, 0, 0))],
        out_specs=[pl.BlockSpec((SROWS, BW), lambda b: (b, 0)), kv_spec, kv_spec, kv_spec],
        out_shape=[jax.ShapeDtypeStruct((rows, BW), BF16), kv_shape, kv_shape, kv_shape],
        scratch_shapes=[pltpu.VMEM((nblk, NH * SROWS, CH), F32)],
        compiler_params=_cparams(1),
        name="attn_sample",
    )(rel_bias, jnp.asarray(bk_np), z, c0, c1, c2)


def _load_chunk(ref, pad_ref):
    if ref.shape[0] == CH:
        return ref[...]
    pad_ref[...] = jnp.zeros_like(pad_ref)
    pad_ref[0:ref.shape[0], :] = ref[...]
    return pad_ref[...]


def _gmlp_body(bu_ref, bv_ref, g_ref, ws_ref, bs_ref, *refs, rows, emit_vn):
    refs = list(refs)
    o_ref = refs.pop(0)
    vn_ref = refs.pop(0) if emit_vn else None
    pad_u, pad_v = refs if rows != CH else (None, None)
    u = jax.nn.gelu(_load_chunk(bu_ref, pad_u))
    v = jax.nn.gelu(_load_chunk(bv_ref, pad_v))
    vn = _rms(v, g_ref[...])
    if emit_vn:
        vn_ref[...] = vn[:rows]
    tril = lax.broadcasted_iota(jnp.int32, (CH, CH), 0) >= lax.broadcasted_iota(jnp.int32, (CH, CH), 1)
    for g in range(NH):
        sl = slice(g * HD, (g + 1) * HD)
        w = jnp.where(tril, ws_ref[g], 0.0).astype(BF16)
        mixed = jnp.dot(w, vn[:, sl].astype(BF16), preferred_element_type=F32) + bs_ref[:, sl]
        o_ref[:, sl] = (u[:, sl] * mixed)[:rows].astype(BF16)


def _gmlp(z, g_gmlp, w_spatial, b_spatial, *, rows, emit_vn):
    m = z.shape[0]
    bs_full = jnp.repeat(b_spatial.T, HD, axis=1)
    spec = pl.BlockSpec((rows, BW), lambda i: (i, 0))
    out_shape = [jax.ShapeDtypeStruct((m, BW), BF16)]
    out_specs = [spec]
    if emit_vn:
        out_shape.append(jax.ShapeDtypeStruct((m, BW), F32))
        out_specs.append(spec)
    scratch = [] if rows == CH else [pltpu.VMEM((CH, BW), F32)] * 2
    return pl.pallas_call(
        functools.partial(_gmlp_body, rows=rows, emit_vn=emit_vn),
        grid=(m // rows,),
        in_specs=[pl.BlockSpec((rows, BW), lambda i: (i, COL_BU)),
                  pl.BlockSpec((rows, BW), lambda i: (i, COL_BV)),
                  pl.BlockSpec((1, BW), lambda i: (0, 0)),
                  pl.BlockSpec((NH, CH, CH), lambda i: (0, 0, 0)),
                  pl.BlockSpec((CH, BW), lambda i: (0, 0))],
        out_specs=out_specs,
        out_shape=out_shape,
        scratch_shapes=scratch,
        compiler_params=_cparams(1),
        name="gmlp",
    )(z, z, g_gmlp.reshape(1, -1), w_spatial, bs_full)


def _pool_body(x_ref, prev_ref, wp_ref, sc_ref, o_ref, st_ref, ext_ref, *, rows, n_new, start, zero_first_prev):
    ib = pl.program_id(1)
    prev = prev_ref[...]
    if zero_first_prev:
        prev = jnp.where(ib == 0, 0.0, prev)
    x = x_ref[...]
    ext_ref[0:16, :] = prev
    ext_ref[16:16 + rows, :] = x
    st_ref[...] = ext_ref[pl.ds(n_new + 1, POOL_STATE), :]
    ext = ext_ref[...]
    pos = start + ib * rows + lax.broadcasted_iota(jnp.int32, (rows, 1), 0)
    for gi, win in enumerate(POOL_WINDOWS):
        sl = slice(gi * HD, (gi + 1) * HD)
        s = ext[:, sl]
        k = 1
        while k < win:
            s = s + pltpu.roll(s, k, axis=0)
            k *= 2
        cnt = jnp.minimum(pos + 1, win).astype(F32)
        diff = s[16:] / cnt - x[:, sl]
        y = jnp.dot(diff.astype(BF16), wp_ref[gi].astype(BF16), preferred_element_type=F32)
        o_ref[:, sl] = (y * sc_ref[:, sl]).astype(BF16)


def _pool(z, prev, w_pool, pool_scale, *, nseq, rows, n_new, start, layer=None):
    m = z.shape[0]
    nblk = m // nseq // rows
    if prev is None:
        per16 = rows // 16
        prev_arr = z
        prev_spec = pl.BlockSpec((16, BW), lambda b, i: (jnp.maximum((b * nblk + i) * per16 - 1, 0), COL_CIN))
    else:
        prev_arr = prev
        prev_spec = pl.BlockSpec((None, None, 16, BW), lambda b, i: (layer, b, 0, 0))
    return pl.pallas_call(
        functools.partial(_pool_body, rows=rows, n_new=n_new, start=start, zero_first_prev=prev is None),
        grid=(nseq, nblk),
        in_specs=[pl.BlockSpec((rows, BW), lambda b, i: (b * nblk + i, COL_CIN)),
                  prev_spec,
                  pl.BlockSpec((NH, HD, HD), lambda b, i: (0, 0, 0)),
                  pl.BlockSpec((1, BW), lambda b, i: (0, 0))],
        out_specs=[pl.BlockSpec((rows, BW), lambda b, i: (b * nblk + i, 0)),
                   pl.BlockSpec((None, POOL_STATE, BW), lambda b, i: (b, 0, 0))],
        out_shape=[jax.ShapeDtypeStruct((m, BW), BF16),
                   jax.ShapeDtypeStruct((nseq, POOL_STATE, BW), F32)],
        scratch_shapes=[pltpu.VMEM((16 + rows, BW), F32)],
        compiler_params=_cparams(2),
        name="pool",
    )(z, prev_arr, w_pool, pool_scale.reshape(1, -1))


def _ret_tables(c_eff, positions):
    lg = np.log1p(-np.power(2.0, -5.0 - np.arange(NH, dtype=np.float64)))
    i = np.arange(CH, dtype=np.float64)
    live = (i < c_eff)
    diff = i[:, None] - i[None, :]
    inner = np.where((diff >= 0) & live[:, None] & live[None, :], np.exp(np.maximum(diff, 0.0)[None] * lg[:, None, None]), 0.0)
    qd = np.where(live[None, :], np.exp((i + 1.0)[None, :] * lg[:, None]), 0.0)
    kd = np.where(live[None, :], np.exp((c_eff - 1.0 - i)[None, :] * lg[:, None]), 0.0)
    chunk = tuple(float(v) for v in np.exp(c_eff * lg))
    qd_full = np.repeat(qd.T, HD, axis=1)
    kd_full = np.repeat(kd.T, HD, axis=1) * (HD ** -0.5)
    half = HD // 2
    inv = ROPE_BASE ** (-np.arange(half, dtype=np.float64) / half)
    ang = np.asarray(positions, np.float64)[:, None] * inv[None, :]
    cosf = np.concatenate([np.cos(ang), np.cos(ang)], axis=1)
    sinf = np.concatenate([-np.sin(ang), np.sin(ang)], axis=1)
    to32 = lambda a: jnp.asarray(a.astype(np.float32))
    return to32(inner), to32(qd_full), to32(kd_full), chunk, to32(cosf), to32(sinf)


def _ret_body(q_ref, k_ref, v_ref, g_ref, cos_ref, sin_ref, inner_ref, qd_ref, kd_ref, gr_ref, *refs,
              rows, chunk_decay, has_state):
    refs = list(refs)
    s0_ref = refs.pop(0) if has_state else None
    o_ref, sn_ref, s_ref = refs[:3]
    pads = refs[3:] if rows != CH else (None,) * 4
    ic = pl.program_id(1)

    @pl.when(ic == 0)
    def _():
        if has_state:
            s_ref[...] = s0_ref[...]
        else:
            s_ref[...] = jnp.zeros_like(s_ref)

    q = _load_chunk(q_ref, pads[0])
    k = _load_chunk(k_ref, pads[1])
    v = _load_chunk(v_ref, pads[2])
    gate = _load_chunk(g_ref, pads[3])
    cosf = cos_ref[...]
    sinf = sin_ref[...]
    for h in range(NH):
        sl = slice(h * HD, (h + 1) * HD)
        qh = q[:, sl]
        kh = k[:, sl]
        rq = qh * cosf + pltpu.roll(qh, HD // 2, axis=1) * sinf
        rk = kh * cosf + pltpu.roll(kh, HD // 2, axis=1) * sinf
        vb = v[:, sl].astype(BF16)
        rqb = rq.astype(BF16)
        state = s_ref[h]
        att = lax.dot_general(rqb, (rk * (HD ** -0.5)).astype(BF16), (((1,), (1,)), ((), ())),
                              preferred_element_type=F32) * inner_ref[h]
        o = (jnp.dot(att.astype(BF16), vb, preferred_element_type=F32)
             + jnp.dot(rqb, state.astype(BF16), preferred_element_type=F32) * qd_ref[:, sl])
        kdec = (rk * kd_ref[:, sl]).T.astype(BF16)
        new_state = state * chunk_decay[h] + jnp.dot(kdec, vb, preferred_element_type=F32)
        s_ref[h] = new_state
        sn_ref[h] = new_state
        o = o * lax.rsqrt(jnp.mean(o * o, axis=-1, keepdims=True) + EPS)
        gt = gate[:, sl]
        o_ref[:, sl] = (o * gr_ref[:, sl] * (gt * jax.nn.sigmoid(gt)))[:rows].astype(BF16)


def _retention(z, g_ret, state, *, nseq, rows, c_eff, positions, layer=None):
    m = z.shape[0]
    nchunk = m // nseq // rows
    inner, qd, kd, chunk_decay, cosf, sinf = _ret_tables(c_eff, positions)
    has_state = state is not None

    def zspec(col):
        return pl.BlockSpec((rows, BW), lambda b, i: (b * nchunk + i, col))

    const2 = lambda b, i: (0, 0)
    in_specs = [zspec(COL_DQ), zspec(COL_DK), zspec(COL_DV), zspec(COL_DG),
                pl.BlockSpec((CH, HD), lambda b, i: (i, 0)), pl.BlockSpec((CH, HD), lambda b, i: (i, 0)),
                pl.BlockSpec((NH, CH, CH), lambda b, i: (0, 0, 0)),
                pl.BlockSpec((CH, BW), const2), pl.BlockSpec((CH, BW), const2), pl.BlockSpec((1, BW), const2)]
    args = [z, z, z, z, cosf, sinf, inner, qd, kd, g_ret.reshape(1, -1)]
    if has_state:
        in_specs.append(pl.BlockSpec((None, None, NH, HD, HD), lambda b, i: (layer, b, 0, 0, 0)))
        args.append(state)
    scratch = [pltpu.VMEM((NH, HD, HD), F32)]
    if rows != CH:
        scratch += [pltpu.VMEM((CH, BW), F32)] * 4
    return pl.pallas_call(
        functools.partial(_ret_body, rows=rows, chunk_decay=chunk_decay, has_state=has_state),
        grid=(nseq, nchunk),
        in_specs=in_specs,
        out_specs=[pl.BlockSpec((rows, BW), lambda b, i: (b * nchunk + i, 0)),
                   pl.BlockSpec((None, NH, HD, HD), lambda b, i: (b, 0, 0, 0))],
        out_shape=[jax.ShapeDtypeStruct((m, BW), BF16),
                   jax.ShapeDtypeStruct((nseq, NH, HD, HD), F32)],
        scratch_shapes=scratch,
        compiler_params=_cparams(2),
        name="retention",
    )(*args)


def kernel(x_prompt, x_sample, cache_attn_kv_w128, cache_attn_kv_w512, cache_attn_kv_w2048, state_pool, state_ret, rel_bias, g_ffn1, w_ffn1_gate, w_ffn1_up, w_ffn1_down, g_mix, w_in, g_gmlp, w_spatial, b_spatial, w_pool, pool_scale, g_ret, w_branch, w_out, g_ffn2, w_ffn2_gate, w_ffn2_up, w_ffn2_down, g_final):
    caches = (cache_attn_kv_w128, cache_attn_kv_w512, cache_attn_kv_w2048)
    xp = x_prompt.reshape(BATCH * SEQ, D_MODEL)
    xs = jnp.pad(x_sample, ((0, 0), (0, SROWS - DEC_SEQ), (0, 0))).reshape(DEC_BATCH * SROWS, D_MODEL)
    pool_state = jnp.pad(state_pool, ((0, 0), (0, 0), (1, 0), (0, 0)))
    tm_p, tm_s = 512, DEC_BATCH * SROWS
    sample_pos = PAST_LEN + np.arange(CH)

    kv_p = [[], [], []]
    kv_s = [[], [], []]
    pool_p, pool_s, ret_p, ret_s, gv_s = [], [], [], [], []
    yp = ys = None
    for l in range(DEPTH):
        last = l == DEPTH - 1
        w1 = [w.astype(BF16) for w in (w_ffn1_gate[l], w_ffn1_up[l], w_ffn1_down[l])]
        w2 = [w.astype(BF16) for w in (w_ffn2_gate[l], w_ffn2_up[l], w_ffn2_down[l])]
        win = w_in[l].astype(BF16)
        wb = w_branch[l].astype(BF16)
        wo = w_out[l].astype(BF16)

        xp, hp = _ffn(xp, g_ffn1[l], *w1, g_mix[l], emit_x=True, post_dtype=BF16, tm=tm_p)
        zp = _matmul(hp, win, tm=1024, tn=1024, name="in_proj")
        parts = []
        for gi in range(3):
            o, lse, kv = _attn_prompt(zp, rel_bias, gi)
            parts += [o, lse]
            kv_p[gi].append(kv)
        oa = _attn_combine(parts)
        ob, = _gmlp(zp, g_gmlp[l], w_spatial[l], b_spatial[l], rows=CH, emit_vn=False)
        oc, pn = _pool(zp, None, w_pool[l], pool_scale[l], nseq=BATCH, rows=512, n_new=512, start=0)
        od, rn = _retention(zp, g_ret[l], None, nseq=BATCH, rows=CH, c_eff=CH, positions=np.arange(SEQ))
        pool_p.append(pn)
        ret_p.append(rn)
        mix = _branch_merge(oa, ob, oc, od, zp, wb, tm=512)
        xp = _matmul(mix, wo, xp, tm=1024, tn=512, name="out_proj")
        if last:
            yp, = _ffn(xp, g_ffn2[l], *w2, g_final, emit_x=False, post_dtype=F32, tm=tm_p)
        else:
            xp, = _ffn(xp, g_ffn2[l], *w2, g_final, emit_x=True, post_dtype=None, tm=tm_p)

        xs, hs = _ffn(xs, g_ffn1[l], *w1, g_mix[l], emit_x=True, post_dtype=BF16, tm=tm_s)
        zs = _matmul(hs, win, tm=tm_s, tn=1024, name="in_proj")
        oa, k0, k1, k2 = _attn_sample(zs, caches, rel_bias, l)
        for gi, kv in enumerate((k0, k1, k2)):
            kv_s[gi].append(kv.reshape(DEC_BATCH, SROWS, 2, NH, HD)[:, :DEC_SEQ])
        ob, vn = _gmlp(zs, g_gmlp[l], w_spatial[l], b_spatial[l], rows=SROWS, emit_vn=True)
        gv_s.append(vn.reshape(DEC_BATCH, SROWS, BW)[:, :DEC_SEQ])
        oc, pn = _pool(zs, pool_state, w_pool[l], pool_scale[l], nseq=DEC_BATCH, rows=SROWS, n_new=DEC_SEQ,
                      start=PAST_LEN, layer=l)
        od, rn = _retention(zs, g_ret[l], state_ret, nseq=DEC_BATCH, rows=SROWS, c_eff=DEC_SEQ,
                            positions=sample_pos, layer=l)
        pool_s.append(pn)
        ret_s.append(rn)
        mix = _branch_merge(oa, ob, oc, od, zs, wb, tm=tm_s)
        xs = _matmul(mix, wo, xs, tm=tm_s, tn=512, name="out_proj")
        if last:
            ys, = _ffn(xs, g_ffn2[l], *w2, g_final, emit_x=False, post_dtype=F32, tm=tm_s)
        else:
            xs, = _ffn(xs, g_ffn2[l], *w2, g_final, emit_x=True, post_dtype=None, tm=tm_s)

    y_prompt = yp.reshape(BATCH, SEQ, D_MODEL)
    y_sample = ys.reshape(DEC_BATCH, SROWS, D_MODEL)[:, :DEC_SEQ]
    return (y_prompt, y_sample,
            jnp.stack(kv_p[0]), jnp.stack(kv_p[1]), jnp.stack(kv_p[2]),
            jnp.stack(kv_s[0]), jnp.stack(kv_s[1]), jnp.stack(kv_s[2]),
            jnp.stack(pool_p), jnp.stack(pool_s),
            jnp.stack(ret_p), jnp.stack(ret_s),
            jnp.stack(gv_s))
```

```python
import functools

import numpy as np
import jax
import jax.numpy as jnp
from jax import lax
from jax.experimental import pallas as pl
from jax.experimental.pallas import tpu as pltpu

F32 = jnp.float32
BF16 = jnp.bfloat16

D_MODEL = 2048
BATCH = 4
SEQ = 2048
DEPTH = 2
DEC_BATCH = 32
DEC_SEQ = 4
PAST_LEN = 8192
D_FF = 5632
IN_WIDTH = 16384
EPS = 1e-6
BW = 512
HD = 128
NH = 4
ATT_WINDOWS = (128, 512, 2048)
ATT_DILATIONS = (1, 4, 16)
NKEY = 129
REL_BUCKETS = 32
REL_MAX_DISTANCE = 2048
POOL_WINDOWS = (2, 4, 8, 16)
POOL_STATE = 15
ROPE_BASE = 10000.0
SROWS = 8
CH = 128
NEG = -1e30
LSE_W = 32

W_IN_ORDER = (0, 3, 6) + tuple(range(9, 32)) + (1, 4, 7) + (2, 5, 8)
COL_QKV = ((0, 1, 2), (26, 27, 28), (29, 30, 31))
COL_BU, COL_BV, COL_CIN = 3, 4, 5
COL_DQ, COL_DK, COL_DV, COL_DG = 6, 7, 8, 9
COL_GATE = 10
N_MAIN = 26
ZBLK = IN_WIDTH // BW

VMEM_BIG = 56 * 1024 * 1024


def _cparams(n_axes, vmem=None):
    return pltpu.CompilerParams(dimension_semantics=("arbitrary",) * n_axes, vmem_limit_bytes=vmem)


def _rms(x, g):
    return x * lax.rsqrt(jnp.mean(x * x, axis=-1, keepdims=True) + EPS) * g


def _ffn_body(x_ref, gpre_ref, wg_ref, wu_ref, wd_ref, gpost_ref, *refs, emit_x, emit_post, nf):
    outs = refs[:int(emit_x) + int(emit_post)]
    xn_ref, acc_ref = refs[-2:]
    f = pl.program_id(1)

    @pl.when(f == 0)
    def _():
        xn_ref[...] = _rms(x_ref[...], gpre_ref[...]).astype(BF16)
        acc_ref[...] = jnp.zeros_like(acc_ref)

    xn = xn_ref[...]
    g = jnp.dot(xn, wg_ref[...], preferred_element_type=F32)
    u = jnp.dot(xn, wu_ref[...], preferred_element_type=F32)
    h = (g * jax.nn.sigmoid(g) * u).astype(BF16)
    acc_ref[...] += jnp.dot(h, wd_ref[...], preferred_element_type=F32)

    @pl.when(f == nf - 1)
    def _():
        xo = x_ref[...] + 0.5 * acc_ref[...]
        k = 0
        if emit_x:
            outs[k][...] = xo
            k += 1
        if emit_post:
            outs[k][...] = _rms(xo, gpost_ref[...]).astype(outs[k].dtype)


def _ffn(x, g_pre, wg, wu, wd, g_post, *, layer, emit_x, post_dtype, tm, tf=512):
    m = x.shape[0]
    nf = D_FF // tf
    emit_post = post_dtype is not None
    out_shape, out_specs = [], []
    if emit_x:
        out_shape.append(jax.ShapeDtypeStruct((m, D_MODEL), F32))
        out_specs.append(pl.BlockSpec((tm, D_MODEL), lambda i, f: (i, 0)))
    if emit_post:
        out_shape.append(jax.ShapeDtypeStruct((m, D_MODEL), post_dtype))
        out_specs.append(pl.BlockSpec((tm, D_MODEL), lambda i, f: (i, 0)))
    return pl.pallas_call(
        functools.partial(_ffn_body, emit_x=emit_x, emit_post=emit_post, nf=nf),
        grid=(m // tm, nf),
        in_specs=[
            pl.BlockSpec((tm, D_MODEL), lambda i, f: (i, 0)),
            pl.BlockSpec((1, D_MODEL), lambda i, f: (0, 0)),
            pl.BlockSpec((None, D_MODEL, tf), lambda i, f: (layer, 0, f)),
            pl.BlockSpec((None, D_MODEL, tf), lambda i, f: (layer, 0, f)),
            pl.BlockSpec((None, tf, D_MODEL), lambda i, f: (layer, f, 0)),
            pl.BlockSpec((1, D_MODEL), lambda i, f: (0, 0)),
        ],
        out_specs=out_specs,
        out_shape=out_shape,
        scratch_shapes=[pltpu.VMEM((tm, D_MODEL), BF16), pltpu.VMEM((tm, D_MODEL), F32)],
        compiler_params=_cparams(2, VMEM_BIG),
        name="ffn",
    )(x, g_pre.reshape(1, -1), wg, wu, wd, g_post.reshape(1, -1))


def _mm_body(a_ref, b_ref, o_ref):
    o_ref[...] = jnp.dot(a_ref[...], b_ref[...], preferred_element_type=F32)


def _mm_res_body(a_ref, b_ref, r_ref, o_ref):
    o_ref[...] = r_ref[...] + jnp.dot(a_ref[...], b_ref[...], preferred_element_type=F32)


def _matmul(a, b, res=None, *, layer, col0=0, n=None, tm, tn, name):
    m, k = a.shape
    n = b.shape[2] if n is None else n
    in_specs = [pl.BlockSpec((tm, k), lambda i, j: (i, 0)),
                pl.BlockSpec((None, k, tn), lambda i, j: (layer, 0, col0 + j))]
    args = [a, b]
    body = _mm_body
    if res is not None:
        in_specs.append(pl.BlockSpec((tm, tn), lambda i, j: (i, j)))
        args.append(res)
        body = _mm_res_body
    return pl.pallas_call(
        body,
        grid=(m // tm, n // tn),
        in_specs=in_specs,
        out_specs=pl.BlockSpec((tm, tn), lambda i, j: (i, j)),
        out_shape=jax.ShapeDtypeStruct((m, n), F32),
        compiler_params=_cparams(2, VMEM_BIG),
        name=name,
    )(*args)


def _branch_body(oa_ref, ob_ref, oc_ref, od_ref, z0_ref, z1_ref, z2_ref, z3_ref, wb_ref, o_ref):
    acc = None
    for n, (br, zg) in enumerate(((oa_ref, z0_ref), (ob_ref, z1_ref), (oc_ref, z2_ref), (od_ref, z3_ref))):
        proj = jnp.dot(br[...], wb_ref[n], preferred_element_type=F32)
        t = jax.nn.sigmoid(zg[...]) * proj
        acc = t if acc is None else acc + t
    o_ref[...] = acc.astype(BF16)


def _branch_merge(oa, ob, oc, od, z, wb, *, layer, tm, tn=512):
    m = oa.shape[0]
    gate_blk = (COL_GATE * BW) // tn
    per_branch = D_MODEL // tn
    br_spec = pl.BlockSpec((tm, BW), lambda i, j: (i, 0))
    z_specs = [pl.BlockSpec((tm, tn), functools.partial(lambda i, j, n: (i, gate_blk + n * per_branch + j), n=n))
               for n in range(4)]
    return pl.pallas_call(
        _branch_body,
        grid=(m // tm, D_MODEL // tn),
        in_specs=[br_spec] * 4 + z_specs + [pl.BlockSpec((None, 4, BW, tn), lambda i, j: (layer, 0, 0, j))],
        out_specs=pl.BlockSpec((tm, tn), lambda i, j: (i, j)),
        out_shape=jax.ShapeDtypeStruct((m, D_MODEL), BF16),
        compiler_params=_cparams(2, VMEM_BIG),
        name="branch_merge",
    )(oa, ob, oc, od, z, z, z, z, wb)


def _t5_buckets(dist):
    max_exact = REL_BUCKETS // 2
    d = np.maximum(dist, 1).astype(np.float32)
    large = max_exact + (np.log(d / max_exact) / np.log(REL_MAX_DISTANCE / max_exact)
                         * (REL_BUCKETS - max_exact)).astype(np.int32)
    large = np.minimum(large, REL_BUCKETS - 1)
    return np.where(dist < max_exact, dist, large).astype(np.int32)


def _bucket_of_step(dil):
    return _t5_buckets(dil * np.arange(NKEY))


def _prompt_bucket_matrix(dil):
    iq = np.arange(CH)[:, None]
    col = np.arange(2 * CH)[None, :]
    j = iq + CH - col
    valid = (j >= 0) & (j < NKEY)
    return np.where(valid, _bucket_of_step(dil)[np.clip(j, 0, NKEY - 1)], -1).astype(np.int32)


SAMPLE_CACHE_ROWS = (8 * ATT_WINDOWS[0], 8 * ATT_WINDOWS[1], 8 * DEC_SEQ * CH)


def _sample_bucket_tables():
    h = np.repeat(np.arange(NH), SROWS)[:, None]
    t = np.tile(np.arange(SROWS), NH)[:, None]
    lane = np.arange(CH)[None, :]
    tables = []
    for gi, dil in enumerate(ATT_DILATIONS):
        bos = _bucket_of_step(dil)
        flat = np.arange(SAMPLE_CACHE_ROWS[gi])[None, :]
        head, is_k = flat % NH, (flat // NH) % 2 == 0
        if gi < 2:
            w = flat // 8
            steps = ATT_WINDOWS[gi] + t - w
            valid = (steps % dil == 0) & (steps // dil < NKEY)
            j = steps // dil
        else:
            res, i = (flat // 8) % DEC_SEQ, flat // (8 * DEC_SEQ)
            valid = res == t
            j = np.broadcast_to(CH - i, valid.shape)
        valid = valid & is_k & (head == h) & (t < DEC_SEQ)
        cache_tbl = np.where(valid, bos[np.clip(j, 0, NKEY - 1)], -1)
        jn = t - lane
        validn = (t < DEC_SEQ) & (lane < DEC_SEQ) & (jn >= 0) & ((dil == 1) | (jn == 0))
        new_tbl = np.where(validn, bos[np.clip(jn, 0, NKEY - 1)], -1)
        tables.append((cache_tbl.astype(np.int32), new_tbl.astype(np.int32)))
    return tables


def _bias_from_buckets(bk, rb_ref, col):
    out = jnp.full(bk.shape, NEG, F32)
    for b in range(REL_BUCKETS):
        out = jnp.where(bk == b, rb_ref[b, col], out)
    return out


def _attn_prompt_body(rb_ref, bk_ref, q_ref, kc_ref, vc_ref, *refs, gi, has_prev):
    if has_prev:
        kp_ref, vp_ref = refs[:2]
        refs = refs[2:]
    o_ref, lse_ref, kvo_ref, bias_ref = refs
    first = (pl.program_id(0) == 0) & (pl.program_id(1) == 0) & (pl.program_id(2) == 0)

    @pl.when(first)
    def _():
        bk = bk_ref[...]
        for h in range(NH):
            bias_ref[h] = _bias_from_buckets(bk, rb_ref, gi * NH + h)

    ib = pl.program_id(2)
    kc = kc_ref[...]
    vc = vc_ref[...]
    kvo_ref[:, :BW] = kc
    kvo_ref[:, BW:] = vc
    q = q_ref[...]
    scale = HD ** -0.5
    if has_prev:
        col = lax.broadcasted_iota(jnp.int32, (CH, 2 * CH), 1)
        no_prev = (ib == 0) & (col < CH)
    for h in range(NH):
        sl = slice(h * HD, (h + 1) * HD)
        qh = q[:, sl].astype(BF16)
        if has_prev:
            kh = jnp.concatenate([kp_ref[:, sl], kc[:, sl]], axis=0).astype(BF16)
            vh = jnp.concatenate([vp_ref[:, sl], vc[:, sl]], axis=0).astype(BF16)
            bias = jnp.where(no_prev, NEG, bias_ref[h])
        else:
            kh = kc[:, sl].astype(BF16)
            vh = vc[:, sl].astype(BF16)
            bias = bias_ref[h][:, CH:]
        s = lax.dot_general(qh, kh, (((1,), (1,)), ((), ())), preferred_element_type=F32) * scale + bias
        m = jnp.max(s, axis=-1, keepdims=True)
        p = jnp.exp(s - m)
        den = jnp.sum(p, axis=-1, keepdims=True)
        acc = jnp.dot(p.astype(BF16), vh, preferred_element_type=F32)
        o_ref[:, sl] = acc / den
        lse_ref[:, h * LSE_W:(h + 1) * LSE_W] = jnp.broadcast_to(m + jnp.log(den), (CH, LSE_W))


def _attn_prompt(zq, cols, rel_bias, gi):
    dil = ATT_DILATIONS[gi]
    rows = BATCH * SEQ
    nb = SEQ // dil // CH
    has_prev = nb > 1
    bk = jnp.asarray(_prompt_bucket_matrix(dil))

    def zspec(colblk, prev=False):
        if prev:
            return pl.BlockSpec((CH, BW), lambda b, r, i: ((b * dil + r) * nb + jnp.maximum(i - 1, 0), colblk))
        return pl.BlockSpec((CH, BW), lambda b, r, i: ((b * dil + r) * nb + i, colblk))

    in_specs = [pl.BlockSpec(memory_space=pltpu.SMEM),
                pl.BlockSpec((CH, 2 * CH), lambda b, r, i: (0, 0)),
                zspec(cols[0]), zspec(cols[1]), zspec(cols[2])]
    args = [rel_bias, bk, zq, zq, zq]
    if has_prev:
        in_specs += [zspec(cols[1], True), zspec(cols[2], True)]
        args += [zq, zq]
    o, lse, kvo = pl.pallas_call(
        functools.partial(_attn_prompt_body, gi=gi, has_prev=has_prev),
        grid=(BATCH, dil, nb),
        in_specs=in_specs,
        out_specs=[pl.BlockSpec((CH, BW), lambda b, r, i: ((b * dil + r) * nb + i, 0)),
                   pl.BlockSpec((CH, NH * LSE_W), lambda b, r, i: ((b * dil + r) * nb + i, 0)),
                   pl.BlockSpec((CH, 2 * BW), lambda b, r, i: (b, r))],
        out_shape=[jax.ShapeDtypeStruct((rows, BW), F32),
                   jax.ShapeDtypeStruct((rows, NH * LSE_W), F32),
                   jax.ShapeDtypeStruct((BATCH * CH, dil * 2 * BW), F32)],
        scratch_shapes=[pltpu.VMEM((NH, CH, 2 * CH), F32)],
        compiler_params=_cparams(3),
        name=f"attn_prompt_g{gi}",
    )(*args)
    kv = kvo.reshape(BATCH, ATT_WINDOWS[gi], 2, NH, HD)
    return o, lse, kv


def _deinterleave(a, dil):
    w = a.shape[-1]
    return a.reshape(BATCH, SEQ // dil, dil, w).transpose(0, 2, 1, 3).reshape(BATCH * SEQ, w)


def _interleave(a, dil):
    w = a.shape[-1]
    return a.reshape(BATCH, dil, SEQ // dil, w).transpose(0, 2, 1, 3).reshape(BATCH * SEQ, w)


def _combine_body(o0, l0, o1, l1, o2, l2, out_ref):
    for h in range(NH):
        sl = slice(h * HD, (h + 1) * HD)
        a0, a1, a2 = (l[:, h * LSE_W:h * LSE_W + 1] for l in (l0, l1, l2))
        m = jnp.maximum(jnp.maximum(a0, a1), a2)
        w0, w1, w2 = jnp.exp(a0 - m), jnp.exp(a1 - m), jnp.exp(a2 - m)
        out_ref[:, sl] = ((w0 * o0[:, sl] + w1 * o1[:, sl] + w2 * o2[:, sl]) / (w0 + w1 + w2)).astype(BF16)


def _attn_combine(parts, *, tm=1024):
    rows = parts[0].shape[0]
    ospec = pl.BlockSpec((tm, BW), lambda i: (i, 0))
    lspec = pl.BlockSpec((tm, NH * LSE_W), lambda i: (i, 0))
    return pl.pallas_call(
        _combine_body,
        grid=(rows // tm,),
        in_specs=[ospec, lspec] * 3,
        out_specs=ospec,
        out_shape=jax.ShapeDtypeStruct((rows, BW), BF16),
        compiler_params=_cparams(1),
        name="attn_combine",
    )(*parts)


def _attn_sample_body(rb_ref, bc0_ref, bc1_ref, bc2_ref, bn_ref, z_ref, c0_ref, c1_ref, c2_ref,
                      oa_ref, kv0_ref, kv1_ref, kv2_ref, b0_ref, b1_ref, b2_ref, bnew_ref):
    bucket_refs = (bc0_ref, bc1_ref, bc2_ref)
    bias_refs = (b0_ref, b1_ref, b2_ref)

    @pl.when(pl.program_id(0) == 0)
    def _():
        for gi in range(3):
            for h in range(NH):
                rs = slice(h * SROWS, (h + 1) * SROWS)
                bias_refs[gi][rs, :] = _bias_from_buckets(bucket_refs[gi][rs, :], rb_ref, gi * NH + h)
                bnew_ref[gi, rs, :] = _bias_from_buckets(bn_ref[gi, rs, :], rb_ref, gi * NH + h)

    scale = HD ** -0.5
    rows = NH * SROWS
    head_of_row = lax.broadcasted_iota(jnp.int32, (rows, BW), 0) // SROWS
    head_of_lane = lax.broadcasted_iota(jnp.int32, (rows, BW), 1) // HD
    head_mask = head_of_row == head_of_lane
    caches = (c0_ref, c1_ref, c2_ref)
    kv_refs = (kv0_ref, kv1_ref, kv2_ref)
    stats = []
    for gi in range(3):
        cq, ck, cv = COL_QKV[gi]
        q = z_ref[:, cq * BW:(cq + 1) * BW]
        kn = z_ref[:, ck * BW:(ck + 1) * BW]
        vn = z_ref[:, cv * BW:(cv + 1) * BW]
        kv_refs[gi][:, :BW] = kn
        kv_refs[gi][:, BW:] = vn
        qm = jnp.concatenate([q[:, h * HD:(h + 1) * HD] for h in range(NH)], axis=0).astype(BF16)
        kf = caches[gi][...].reshape(SAMPLE_CACHE_ROWS[gi], HD).astype(BF16)
        s = lax.dot_general(qm, kf, (((1,), (1,)), ((), ())), preferred_element_type=F32) * scale + bias_refs[gi][...]
        qrows = jnp.where(head_mask, jnp.concatenate([q] * NH, axis=0), 0.0)
        bias_n = bnew_ref[gi]
        s_new = []
        for tp in range(DEC_SEQ):
            dotp = jnp.sum(qrows * kn[tp:tp + 1, :], axis=-1, keepdims=True)
            s_new.append(dotp * scale + bias_n[:, tp:tp + 1])
        m = jnp.max(s, axis=-1, keepdims=True)
        for sn in s_new:
            m = jnp.maximum(m, sn)
        p = jnp.exp(s - m)
        den = jnp.sum(p, axis=-1, keepdims=True)
        pv = pltpu.roll(p, NH, axis=1).astype(BF16)
        acc = jnp.dot(pv, kf, preferred_element_type=F32)
        acc_n = jnp.zeros((rows, BW), F32)
        for tp, sn in enumerate(s_new):
            pn = jnp.exp(sn - m)
            den = den + pn
            acc_n = acc_n + pn * vn[tp:tp + 1, :]
        acc = acc + jnp.concatenate([acc_n[h * SROWS:(h + 1) * SROWS, h * HD:(h + 1) * HD] for h in range(NH)], axis=0)
        stats.append((m, den, acc))
    mm = jnp.maximum(jnp.maximum(stats[0][0], stats[1][0]), stats[2][0])
    den = jnp.zeros((rows, 1), F32)
    acc = jnp.zeros((rows, HD), F32)
    for m, d, a in stats:
        w = jnp.exp(m - mm)
        den = den + w * d
        acc = acc + w * a
    o = acc / den
    for h in range(NH):
        oa_ref[:, h * HD:(h + 1) * HD] = o[h * SROWS:(h + 1) * SROWS, :].astype(BF16)


def _attn_sample(z, caches, rel_bias, layer):
    tables = _sample_bucket_tables()
    rows = DEC_BATCH * SROWS
    qrows = NH * SROWS
    n0, n1, n2 = SAMPLE_CACHE_ROWS
    c0 = caches[0].reshape(DEPTH, DEC_BATCH, n0, HD)
    c1 = caches[1].reshape(DEPTH, DEC_BATCH, n1, HD)
    c2 = caches[2].reshape(DEPTH, DEC_BATCH, CH, 16 * 8, HD)
    new_tbl = jnp.asarray(np.stack([t[1] for t in tables]))
    kv_spec = pl.BlockSpec((SROWS, 2 * BW), lambda b: (b, 0))
    kv_shape = jax.ShapeDtypeStruct((rows, 2 * BW), F32)
    const2 = lambda b: (0, 0)
    return pl.pallas_call(
        _attn_sample_body,
        grid=(DEC_BATCH,),
        in_specs=[pl.BlockSpec(memory_space=pltpu.SMEM),
                  pl.BlockSpec((qrows, n0), const2), pl.BlockSpec((qrows, n1), const2),
                  pl.BlockSpec((qrows, n2), const2),
                  pl.BlockSpec((3, qrows, CH), lambda b: (0, 0, 0)),
                  pl.BlockSpec((SROWS, IN_WIDTH), lambda b: (b, 0)),
                  pl.BlockSpec((None, None, n0, HD), lambda b: (layer, b, 0, 0)),
                  pl.BlockSpec((None, None, n1, HD), lambda b: (layer, b, 0, 0)),
                  pl.BlockSpec((None, None, CH, 8 * DEC_SEQ, HD), lambda b: (layer, b, 0, 0, 0))],
        out_specs=[pl.BlockSpec((SROWS, BW), lambda b: (b, 0)), kv_spec, kv_spec, kv_spec],
        out_shape=[jax.ShapeDtypeStruct((rows, BW), BF16), kv_shape, kv_shape, kv_shape],
        scratch_shapes=[pltpu.VMEM((qrows, n0), F32), pltpu.VMEM((qrows, n1), F32), pltpu.VMEM((qrows, n2), F32),
                        pltpu.VMEM((3, qrows, CH), F32)],
        compiler_params=_cparams(1),
        name="attn_sample",
    )(rel_bias, jnp.asarray(tables[0][0]), jnp.asarray(tables[1][0]), jnp.asarray(tables[2][0]), new_tbl,
      z, c0, c1, c2)


def _load_chunk(ref, pad_ref):
    if ref.shape[0] == CH:
        return ref[...]
    pad_ref[...] = jnp.zeros_like(pad_ref)
    pad_ref[0:ref.shape[0], :] = ref[...]
    return pad_ref[...]


def _gmlp_body(bu_ref, bv_ref, g_ref, ws_ref, bs_ref, *refs, rows, emit_vn):
    refs = list(refs)
    o_ref = refs.pop(0)
    vn_ref = refs.pop(0) if emit_vn else None
    pad_u, pad_v = refs if rows != CH else (None, None)
    u = jax.nn.gelu(_load_chunk(bu_ref, pad_u))
    v = jax.nn.gelu(_load_chunk(bv_ref, pad_v))
    vn = _rms(v, g_ref[...])
    if emit_vn:
        vn_ref[...] = vn[:rows]
    tril = lax.broadcasted_iota(jnp.int32, (CH, CH), 0) >= lax.broadcasted_iota(jnp.int32, (CH, CH), 1)
    for g in range(NH):
        sl = slice(g * HD, (g + 1) * HD)
        w = jnp.where(tril, ws_ref[g], 0.0).astype(BF16)
        mixed = jnp.dot(w, vn[:, sl].astype(BF16), preferred_element_type=F32) + bs_ref[:, sl]
        o_ref[:, sl] = (u[:, sl] * mixed)[:rows].astype(BF16)


def _gmlp(z, g_gmlp, w_spatial, b_spatial, *, rows, emit_vn):
    m = z.shape[0]
    bs_full = jnp.repeat(b_spatial.T, HD, axis=1)
    spec = pl.BlockSpec((rows, BW), lambda i: (i, 0))
    out_shape = [jax.ShapeDtypeStruct((m, BW), BF16)]
    out_specs = [spec]
    if emit_vn:
        out_shape.append(jax.ShapeDtypeStruct((m, BW), F32))
        out_specs.append(spec)
    scratch = [] if rows == CH else [pltpu.VMEM((CH, BW), F32)] * 2
    return pl.pallas_call(
        functools.partial(_gmlp_body, rows=rows, emit_vn=emit_vn),
        grid=(m // rows,),
        in_specs=[pl.BlockSpec((rows, BW), lambda i: (i, COL_BU)),
                  pl.BlockSpec((rows, BW), lambda i: (i, COL_BV)),
                  pl.BlockSpec((1, BW), lambda i: (0, 0)),
                  pl.BlockSpec((NH, CH, CH), lambda i: (0, 0, 0)),
                  pl.BlockSpec((CH, BW), lambda i: (0, 0))],
        out_specs=out_specs,
        out_shape=out_shape,
        scratch_shapes=scratch,
        compiler_params=_cparams(1),
        name="gmlp",
    )(z, z, g_gmlp.reshape(1, -1), w_spatial, bs_full)


def _pool_body(x_ref, prev_ref, wp_ref, sc_ref, o_ref, st_ref, ext_ref, *, rows, n_new, start, zero_first_prev):
    ib = pl.program_id(1)
    prev = prev_ref[...]
    if zero_first_prev:
        prev = jnp.where(ib == 0, 0.0, prev)
    x = x_ref[...]
    ext_ref[0:16, :] = prev
    ext_ref[16:16 + rows, :] = x
    st_ref[...] = ext_ref[pl.ds(n_new + 1, POOL_STATE), :]
    ext = ext_ref[...]
    pos = start + ib * rows + lax.broadcasted_iota(jnp.int32, (rows, 1), 0)
    for gi, win in enumerate(POOL_WINDOWS):
        sl = slice(gi * HD, (gi + 1) * HD)
        s = ext[:, sl]
        k = 1
        while k < win:
            s = s + pltpu.roll(s, k, axis=0)
            k *= 2
        cnt = jnp.minimum(pos + 1, win).astype(F32)
        diff = s[16:] / cnt - x[:, sl]
        y = jnp.dot(diff.astype(BF16), wp_ref[gi].astype(BF16), preferred_element_type=F32)
        o_ref[:, sl] = (y * sc_ref[:, sl]).astype(BF16)


def _pool(z, prev, w_pool, pool_scale, *, nseq, rows, n_new, start, layer=None):
    m = z.shape[0]
    nblk = m // nseq // rows
    if prev is None:
        per16 = rows // 16
        prev_arr = z
        prev_spec = pl.BlockSpec((16, BW), lambda b, i: (jnp.maximum((b * nblk + i) * per16 - 1, 0), COL_CIN))
    else:
        prev_arr = prev
        prev_spec = pl.BlockSpec((None, None, 16, BW), lambda b, i: (layer, b, 0, 0))
    return pl.pallas_call(
        functools.partial(_pool_body, rows=rows, n_new=n_new, start=start, zero_first_prev=prev is None),
        grid=(nseq, nblk),
        in_specs=[pl.BlockSpec((rows, BW), lambda b, i: (b * nblk + i, COL_CIN)),
                  prev_spec,
                  pl.BlockSpec((NH, HD, HD), lambda b, i: (0, 0, 0)),
                  pl.BlockSpec((1, BW), lambda b, i: (0, 0))],
        out_specs=[pl.BlockSpec((rows, BW), lambda b, i: (b * nblk + i, 0)),
                   pl.BlockSpec((None, POOL_STATE, BW), lambda b, i: (b, 0, 0))],
        out_shape=[jax.ShapeDtypeStruct((m, BW), BF16),
                   jax.ShapeDtypeStruct((nseq, POOL_STATE, BW), F32)],
        scratch_shapes=[pltpu.VMEM((16 + rows, BW), F32)],
        compiler_params=_cparams(2),
        name="pool",
    )(z, prev_arr, w_pool, pool_scale.reshape(1, -1))


def _ret_tables(c_eff, positions):
    lg = np.log1p(-np.power(2.0, -5.0 - np.arange(NH, dtype=np.float64)))
    i = np.arange(CH, dtype=np.float64)
    live = (i < c_eff)
    diff = i[:, None] - i[None, :]
    inner = np.where((diff >= 0) & live[:, None] & live[None, :], np.exp(np.maximum(diff, 0.0)[None] * lg[:, None, None]), 0.0)
    qd = np.where(live[None, :], np.exp((i + 1.0)[None, :] * lg[:, None]), 0.0)
    kd = np.where(live[None, :], np.exp((c_eff - 1.0 - i)[None, :] * lg[:, None]), 0.0)
    chunk = tuple(float(v) for v in np.exp(c_eff * lg))
    qd_full = np.repeat(qd.T, HD, axis=1)
    kd_full = np.repeat(kd.T, HD, axis=1) * (HD ** -0.5)
    half = HD // 2
    inv = ROPE_BASE ** (-np.arange(half, dtype=np.float64) / half)
    ang = np.asarray(positions, np.float64)[:, None] * inv[None, :]
    cosf = np.concatenate([np.cos(ang), np.cos(ang)], axis=1)
    sinf = np.concatenate([-np.sin(ang), np.sin(ang)], axis=1)
    to32 = lambda a: jnp.asarray(a.astype(np.float32))
    return to32(inner), to32(qd_full), to32(kd_full), chunk, to32(cosf), to32(sinf)


def _ret_body(q_ref, k_ref, v_ref, g_ref, cos_ref, sin_ref, inner_ref, qd_ref, kd_ref, gr_ref, *refs,
              rows, chunk_decay, has_state):
    refs = list(refs)
    s0_ref = refs.pop(0) if has_state else None
    o_ref, sn_ref, s_ref = refs[:3]
    pads = refs[3:] if rows != CH else (None,) * 4
    ic = pl.program_id(1)

    @pl.when(ic == 0)
    def _():
        if has_state:
            s_ref[...] = s0_ref[...]
        else:
            s_ref[...] = jnp.zeros_like(s_ref)

    q = _load_chunk(q_ref, pads[0])
    k = _load_chunk(k_ref, pads[1])
    v = _load_chunk(v_ref, pads[2])
    gate = _load_chunk(g_ref, pads[3])
    cosf = cos_ref[...]
    sinf = sin_ref[...]
    for h in range(NH):
        sl = slice(h * HD, (h + 1) * HD)
        qh = q[:, sl]
        kh = k[:, sl]
        rq = qh * cosf + pltpu.roll(qh, HD // 2, axis=1) * sinf
        rk = kh * cosf + pltpu.roll(kh, HD // 2, axis=1) * sinf
        vb = v[:, sl].astype(BF16)
        rqb = rq.astype(BF16)
        state = s_ref[h]
        att = lax.dot_general(rqb, (rk * (HD ** -0.5)).astype(BF16), (((1,), (1,)), ((), ())),
                              preferred_element_type=F32) * inner_ref[h]
        o = (jnp.dot(att.astype(BF16), vb, preferred_element_type=F32)
             + jnp.dot(rqb, state.astype(BF16), preferred_element_type=F32) * qd_ref[:, sl])
        kdec = (rk * kd_ref[:, sl]).T.astype(BF16)
        new_state = state * chunk_decay[h] + jnp.dot(kdec, vb, preferred_element_type=F32)
        s_ref[h] = new_state
        sn_ref[h] = new_state
        o = o * lax.rsqrt(jnp.mean(o * o, axis=-1, keepdims=True) + EPS)
        gt = gate[:, sl]
        o_ref[:, sl] = (o * gr_ref[:, sl] * (gt * jax.nn.sigmoid(gt)))[:rows].astype(BF16)


def _retention(z, g_ret, state, *, nseq, rows, c_eff, positions, layer=None):
    m = z.shape[0]
    nchunk = m // nseq // rows
    inner, qd, kd, chunk_decay, cosf, sinf = _ret_tables(c_eff, positions)
    has_state = state is not None

    def zspec(col):
        return pl.BlockSpec((rows, BW), lambda b, i: (b * nchunk + i, col))

    const2 = lambda b, i: (0, 0)
    in_specs = [zspec(COL_DQ), zspec(COL_DK), zspec(COL_DV), zspec(COL_DG),
                pl.BlockSpec((CH, HD), lambda b, i: (i, 0)), pl.BlockSpec((CH, HD), lambda b, i: (i, 0)),
                pl.BlockSpec((NH, CH, CH), lambda b, i: (0, 0, 0)),
                pl.BlockSpec((CH, BW), const2), pl.BlockSpec((CH, BW), const2), pl.BlockSpec((1, BW), const2)]
    args = [z, z, z, z, cosf, sinf, inner, qd, kd, g_ret.reshape(1, -1)]
    if has_state:
        in_specs.append(pl.BlockSpec((None, None, NH, HD, HD), lambda b, i: (layer, b, 0, 0, 0)))
        args.append(state)
    scratch = [pltpu.VMEM((NH, HD, HD), F32)]
    if rows != CH:
        scratch += [pltpu.VMEM((CH, BW), F32)] * 4
    return pl.pallas_call(
        functools.partial(_ret_body, rows=rows, chunk_decay=chunk_decay, has_state=has_state),
        grid=(nseq, nchunk),
        in_specs=in_specs,
        out_specs=[pl.BlockSpec((rows, BW), lambda b, i: (b * nchunk + i, 0)),
                   pl.BlockSpec((None, NH, HD, HD), lambda b, i: (b, 0, 0, 0))],
        out_shape=[jax.ShapeDtypeStruct((m, BW), BF16),
                   jax.ShapeDtypeStruct((nseq, NH, HD, HD), F32)],
        scratch_shapes=scratch,
        compiler_params=_cparams(2),
        name="retention",
    )(*args)


def kernel(x_prompt, x_sample, cache_attn_kv_w128, cache_attn_kv_w512, cache_attn_kv_w2048, state_pool, state_ret, rel_bias, g_ffn1, w_ffn1_gate, w_ffn1_up, w_ffn1_down, g_mix, w_in, g_gmlp, w_spatial, b_spatial, w_pool, pool_scale, g_ret, w_branch, w_out, g_ffn2, w_ffn2_gate, w_ffn2_up, w_ffn2_down, g_final):
    caches = (cache_attn_kv_w128, cache_attn_kv_w512, cache_attn_kv_w2048)
    xp = x_prompt.reshape(BATCH * SEQ, D_MODEL)
    xs = jnp.pad(x_sample, ((0, 0), (0, SROWS - DEC_SEQ), (0, 0))).reshape(DEC_BATCH * SROWS, D_MODEL)
    pool_state = jnp.pad(state_pool, ((0, 0), (0, 0), (1, 0), (0, 0)))
    tm_p, tm_s = 512, DEC_BATCH * SROWS
    sample_pos = PAST_LEN + np.arange(CH)
    w1 = [w.astype(BF16) for w in (w_ffn1_gate, w_ffn1_up, w_ffn1_down)]
    w2 = [w.astype(BF16) for w in (w_ffn2_gate, w_ffn2_up, w_ffn2_down)]
    win = jnp.concatenate([w_in[:, :, c * BW:(c + 1) * BW] for c in W_IN_ORDER], axis=-1).astype(BF16)
    wb = w_branch.astype(BF16)
    wo = w_out.astype(BF16)
    n_main = N_MAIN * BW

    kv_p = [[], [], []]
    kv_s = [[], [], []]
    pool_p, pool_s, ret_p, ret_s, gv_s = [], [], [], [], []
    yp = ys = None
    for l in range(DEPTH):
        last = l == DEPTH - 1

        xp, hp = _ffn(xp, g_ffn1[l], *w1, g_mix[l], layer=l, emit_x=True, post_dtype=BF16, tm=tm_p)
        zp = _matmul(hp, win, layer=l, n=n_main, tm=1024, tn=1024, name="in_proj")
        zq = (zp,
              _matmul(_deinterleave(hp, ATT_DILATIONS[1]), win, layer=l, col0=N_MAIN, n=3 * BW,
                      tm=2048, tn=BW, name="in_proj_g1"),
              _matmul(_deinterleave(hp, ATT_DILATIONS[2]), win, layer=l, col0=N_MAIN + 3, n=3 * BW,
                      tm=2048, tn=BW, name="in_proj_g2"))
        parts = []
        for gi in range(3):
            o, lse, kv = _attn_prompt(zq[gi], COL_QKV[0], rel_bias, gi)
            if gi:
                o, lse = _interleave(o, ATT_DILATIONS[gi]), _interleave(lse, ATT_DILATIONS[gi])
            parts += [o, lse]
            kv_p[gi].append(kv)
        oa = _attn_combine(parts)
        ob, = _gmlp(zp, g_gmlp[l], w_spatial[l], b_spatial[l], rows=CH, emit_vn=False)
        oc, pn = _pool(zp, None, w_pool[l], pool_scale[l], nseq=BATCH, rows=512, n_new=512, start=0)
        od, rn = _retention(zp, g_ret[l], None, nseq=BATCH, rows=CH, c_eff=CH, positions=np.arange(SEQ))
        pool_p.append(pn)
        ret_p.append(rn)
        mix = _branch_merge(oa, ob, oc, od, zp, wb, layer=l, tm=512)
        xp = _matmul(mix, wo, xp, layer=l, tm=1024, tn=512, name="out_proj")
        if last:
            yp, = _ffn(xp, g_ffn2[l], *w2, g_final, layer=l, emit_x=False, post_dtype=F32, tm=tm_p)
        else:
            xp, = _ffn(xp, g_ffn2[l], *w2, g_final, layer=l, emit_x=True, post_dtype=None, tm=tm_p)

        xs, hs = _ffn(xs, g_ffn1[l], *w1, g_mix[l], layer=l, emit_x=True, post_dtype=BF16, tm=tm_s)
        zs = _matmul(hs, win, layer=l, tm=tm_s, tn=1024, name="in_proj")
        oa, k0, k1, k2 = _attn_sample(zs, caches, rel_bias, l)
        for gi, kv in enumerate((k0, k1, k2)):
            kv_s[gi].append(kv.reshape(DEC_BATCH, SROWS, 2, NH, HD)[:, :DEC_SEQ])
        ob, vn = _gmlp(zs, g_gmlp[l], w_spatial[l], b_spatial[l], rows=SROWS, emit_vn=True)
        gv_s.append(vn.reshape(DEC_BATCH, SROWS, BW)[:, :DEC_SEQ])
        oc, pn = _pool(zs, pool_state, w_pool[l], pool_scale[l], nseq=DEC_BATCH, rows=SROWS, n_new=DEC_SEQ,
                      start=PAST_LEN, layer=l)
        od, rn = _retention(zs, g_ret[l], state_ret, nseq=DEC_BATCH, rows=SROWS, c_eff=DEC_SEQ,
                            positions=sample_pos, layer=l)
        pool_s.append(pn)
        ret_s.append(rn)
        mix = _branch_merge(oa, ob, oc, od, zs, wb, layer=l, tm=tm_s)
        xs = _matmul(mix, wo, xs, layer=l, tm=tm_s, tn=512, name="out_proj")
        if last:
            ys, = _ffn(xs, g_ffn2[l], *w2, g_final, layer=l, emit_x=False, post_dtype=F32, tm=tm_s)
        else:
            xs, = _ffn(xs, g_ffn2[l], *w2, g_final, layer=l, emit_x=True, post_dtype=None, tm=tm_s)

    y_prompt = yp.reshape(BATCH, SEQ, D_MODEL)
    y_sample = ys.reshape(DEC_BATCH, SROWS, D_MODEL)[:, :DEC_SEQ]
    return (y_prompt, y_sample,
            jnp.stack(kv_p[0]), jnp.stack(kv_p[1]), jnp.stack(kv_p[2]),
            jnp.stack(kv_s[0]), jnp.stack(kv_s[1]), jnp.stack(kv_s[2]),
            jnp.stack(pool_p), jnp.stack(pool_s),
            jnp.stack(ret_p), jnp.stack(ret_s),
            jnp.stack(gv_s))
```

```python
import functools

import numpy as np
import jax
import jax.numpy as jnp
from jax import lax
from jax.experimental import pallas as pl
from jax.experimental.pallas import tpu as pltpu

F32 = jnp.float32
BF16 = jnp.bfloat16

D_MODEL = 2048
BATCH = 4
SEQ = 2048
DEPTH = 2
DEC_BATCH = 32
DEC_SEQ = 4
PAST_LEN = 8192
D_FF = 5632
IN_WIDTH = 16384
EPS = 1e-6
BW = 512
HD = 128
NH = 4
ATT_WINDOWS = (128, 512, 2048)
ATT_DILATIONS = (1, 4, 16)
NKEY = 129
REL_BUCKETS = 32
REL_MAX_DISTANCE = 2048
POOL_WINDOWS = (2, 4, 8, 16)
POOL_STATE = 15
ROPE_BASE = 10000.0
SROWS = 8
CH = 128
NEG = -1e30
LSE_W = 32

W_IN_ORDER = (0, 3, 6) + tuple(range(9, 32)) + (1, 4, 7) + (2, 5, 8)
COL_QKV = ((0, 1, 2), (26, 27, 28), (29, 30, 31))
COL_BU, COL_BV, COL_CIN = 3, 4, 5
COL_DQ, COL_DK, COL_DV, COL_DG = 6, 7, 8, 9
N_PLAIN = 10
N_GATE = 16
GATE_W = N_GATE * BW
ZBLK = IN_WIDTH // BW

VMEM_BIG = 56 * 1024 * 1024


def _cparams(n_axes, vmem=None):
    return pltpu.CompilerParams(dimension_semantics=("arbitrary",) * n_axes, vmem_limit_bytes=vmem)


def _rms(x, g):
    return x * lax.rsqrt(jnp.mean(x * x, axis=-1, keepdims=True) + EPS) * g


def _ffn_body(x_ref, gpre_ref, wg_ref, wu_ref, wd_ref, gpost_ref, *refs, emit_x, emit_post, nf):
    outs = refs[:int(emit_x) + int(emit_post)]
    xn_ref, acc_ref = refs[-2:]
    f = pl.program_id(1)

    @pl.when(f == 0)
    def _():
        xn_ref[...] = _rms(x_ref[...], gpre_ref[...]).astype(BF16)
        acc_ref[...] = jnp.zeros_like(acc_ref)

    xn = xn_ref[...]
    g = jnp.dot(xn, wg_ref[...], preferred_element_type=F32)
    u = jnp.dot(xn, wu_ref[...], preferred_element_type=F32)
    h = (g * jax.nn.sigmoid(g) * u).astype(BF16)
    acc_ref[...] += jnp.dot(h, wd_ref[...], preferred_element_type=F32)

    @pl.when(f == nf - 1)
    def _():
        xo = x_ref[...] + 0.5 * acc_ref[...]
        k = 0
        if emit_x:
            outs[k][...] = xo
            k += 1
        if emit_post:
            outs[k][...] = _rms(xo, gpost_ref[...]).astype(outs[k].dtype)


def _ffn(x, g_pre, wg, wu, wd, g_post, *, layer, emit_x, post_dtype, tm, tf=512):
    m = x.shape[0]
    nf = D_FF // tf
    emit_post = post_dtype is not None
    out_shape, out_specs = [], []
    if emit_x:
        out_shape.append(jax.ShapeDtypeStruct((m, D_MODEL), F32))
        out_specs.append(pl.BlockSpec((tm, D_MODEL), lambda i, f: (i, 0)))
    if emit_post:
        out_shape.append(jax.ShapeDtypeStruct((m, D_MODEL), post_dtype))
        out_specs.append(pl.BlockSpec((tm, D_MODEL), lambda i, f: (i, 0)))
    return pl.pallas_call(
        functools.partial(_ffn_body, emit_x=emit_x, emit_post=emit_post, nf=nf),
        grid=(m // tm, nf),
        in_specs=[
            pl.BlockSpec((tm, D_MODEL), lambda i, f: (i, 0)),
            pl.BlockSpec((1, D_MODEL), lambda i, f: (0, 0)),
            pl.BlockSpec((None, D_MODEL, tf), lambda i, f: (layer, 0, f)),
            pl.BlockSpec((None, D_MODEL, tf), lambda i, f: (layer, 0, f)),
            pl.BlockSpec((None, tf, D_MODEL), lambda i, f: (layer, f, 0)),
            pl.BlockSpec((1, D_MODEL), lambda i, f: (0, 0)),
        ],
        out_specs=out_specs,
        out_shape=out_shape,
        scratch_shapes=[pltpu.VMEM((tm, D_MODEL), BF16), pltpu.VMEM((tm, D_MODEL), F32)],
        compiler_params=_cparams(2, VMEM_BIG),
        name="ffn",
    )(x, g_pre.reshape(1, -1), wg, wu, wd, g_post.reshape(1, -1))


def _mm_body(a_ref, b_ref, o_ref):
    o_ref[...] = jnp.dot(a_ref[...], b_ref[...], preferred_element_type=F32)


def _mm_res_body(a_ref, b_ref, r_ref, o_ref):
    o_ref[...] = r_ref[...] + jnp.dot(a_ref[...], b_ref[...], preferred_element_type=F32)


def _mm_gate_body(a_ref, b_ref, o_ref):
    o_ref[...] = jax.nn.sigmoid(jnp.dot(a_ref[...], b_ref[...], preferred_element_type=F32)).astype(BF16)


def _matmul(a, b, res=None, *, layer, col0=0, n=None, gate=False, tm, tn, name):
    m, k = a.shape
    n = b.shape[2] if n is None else n
    in_specs = [pl.BlockSpec((tm, k), lambda i, j: (i, 0)),
                pl.BlockSpec((None, k, tn), lambda i, j: (layer, 0, col0 + j))]
    args = [a, b]
    body = _mm_gate_body if gate else _mm_body
    if res is not None:
        in_specs.append(pl.BlockSpec((tm, tn), lambda i, j: (i, j)))
        args.append(res)
        body = _mm_res_body
    return pl.pallas_call(
        body,
        grid=(m // tm, n // tn),
        in_specs=in_specs,
        out_specs=pl.BlockSpec((tm, tn), lambda i, j: (i, j)),
        out_shape=jax.ShapeDtypeStruct((m, n), BF16 if gate else F32),
        compiler_params=_cparams(2, VMEM_BIG),
        name=name,
    )(*args)


def _mm_deint_body(a_ref, b_ref, o_ref, s_ref, *, dil):
    r = jnp.dot(a_ref[...], b_ref[...], preferred_element_type=F32)
    for h in range(NH):
        s_ref[h] = r[:, h * HD:(h + 1) * HD]
    n = SEQ // dil
    for h in range(NH):
        for res in range(dil):
            o_ref[h, pl.ds(res * n, n), :] = s_ref[h, pl.ds(res, n, stride=dil), :]


def _matmul_deint(a, b, *, layer, col0, dil, name):
    m, k = a.shape
    return pl.pallas_call(
        functools.partial(_mm_deint_body, dil=dil),
        grid=(BATCH, 3),
        in_specs=[pl.BlockSpec((SEQ, k), lambda b, j: (b, 0)),
                  pl.BlockSpec((None, k, BW), lambda b, j: (layer, 0, col0 + j))],
        out_specs=pl.BlockSpec((None, NH, SEQ, HD), lambda b, j: (j, 0, b, 0)),
        out_shape=jax.ShapeDtypeStruct((3, NH, m, HD), F32),
        scratch_shapes=[pltpu.VMEM((NH, SEQ, HD), F32)],
        compiler_params=_cparams(2, VMEM_BIG),
        name=name,
    )(a, b)


def _branch_body(oa_ref, ob_ref, oc_ref, od_ref, gate_ref, wb_ref, o_ref):
    acc = None
    for n, br in enumerate((oa_ref, ob_ref, oc_ref, od_ref)):
        proj = jnp.dot(br[...], wb_ref[n], preferred_element_type=F32)
        t = gate_ref[:, n * D_MODEL:(n + 1) * D_MODEL].astype(F32) * proj
        acc = t if acc is None else acc + t
    o_ref[...] = acc.astype(BF16)


def _branch_merge(oa, ob, oc, od, gates, wb, *, layer, tm):
    m = oa.shape[0]
    br_spec = pl.BlockSpec((tm, BW), lambda i: (i, 0))
    return pl.pallas_call(
        _branch_body,
        grid=(m // tm,),
        in_specs=[br_spec] * 4 + [pl.BlockSpec((tm, GATE_W), lambda i: (i, 0)),
                                  pl.BlockSpec((None, 4, BW, D_MODEL), lambda i: (layer, 0, 0, 0))],
        out_specs=pl.BlockSpec((tm, D_MODEL), lambda i: (i, 0)),
        out_shape=jax.ShapeDtypeStruct((m, D_MODEL), BF16),
        compiler_params=_cparams(1, VMEM_BIG),
        name="branch_merge",
    )(oa, ob, oc, od, gates, wb)


def _t5_buckets(dist):
    max_exact = REL_BUCKETS // 2
    d = np.maximum(dist, 1).astype(np.float32)
    large = max_exact + (np.log(d / max_exact) / np.log(REL_MAX_DISTANCE / max_exact)
                         * (REL_BUCKETS - max_exact)).astype(np.int32)
    large = np.minimum(large, REL_BUCKETS - 1)
    return np.where(dist < max_exact, dist, large).astype(np.int32)


def _bucket_of_step(dil):
    return _t5_buckets(dil * np.arange(NKEY))


def _prompt_bucket_matrix(dil):
    iq = np.arange(CH)[:, None]
    col = np.arange(2 * CH)[None, :]
    j = iq + CH - col
    valid = (j >= 0) & (j < NKEY)
    return np.where(valid, _bucket_of_step(dil)[np.clip(j, 0, NKEY - 1)], -1).astype(np.int32)


SAMPLE_CACHE_ROWS = (8 * ATT_WINDOWS[0], 8 * ATT_WINDOWS[1], 8 * DEC_SEQ * CH)


def _sample_bucket_tables():
    h = np.repeat(np.arange(NH), SROWS)[:, None]
    t = np.tile(np.arange(SROWS), NH)[:, None]
    lane = np.arange(CH)[None, :]
    tables = []
    for gi, dil in enumerate(ATT_DILATIONS):
        bos = _bucket_of_step(dil)
        flat = np.arange(SAMPLE_CACHE_ROWS[gi])[None, :]
        head, is_k = flat % NH, (flat // NH) % 2 == 0
        if gi < 2:
            w = flat // 8
            steps = ATT_WINDOWS[gi] + t - w
            valid = (steps % dil == 0) & (steps // dil < NKEY)
            j = steps // dil
        else:
            res, i = (flat // 8) % DEC_SEQ, flat // (8 * DEC_SEQ)
            valid = res == t
            j = np.broadcast_to(CH - i, valid.shape)
        valid = valid & is_k & (head == h) & (t < DEC_SEQ)
        cache_tbl = np.where(valid, bos[np.clip(j, 0, NKEY - 1)], -1)
        jn = t - lane
        validn = (t < DEC_SEQ) & (lane < DEC_SEQ) & (jn >= 0) & ((dil == 1) | (jn == 0))
        new_tbl = np.where(validn, bos[np.clip(jn, 0, NKEY - 1)], -1)
        tables.append((cache_tbl.astype(np.int32), new_tbl.astype(np.int32)))
    return tables


def _bias_from_buckets(bk, rb_ref, col):
    out = jnp.full(bk.shape, NEG, F32)
    for b in range(REL_BUCKETS):
        out = jnp.where(bk == b, rb_ref[b, col], out)
    return out


def _attn_prompt_body(rb_ref, bk_ref, q_ref, kc_ref, vc_ref, *refs, gi, has_prev, headed):
    dil = ATT_DILATIONS[gi]
    if has_prev:
        kp_ref, vp_ref = refs[:2]
        refs = refs[2:]
    o_ref, lse_ref, kvo_ref, bias_ref = refs
    first = (pl.program_id(0) == 0) & (pl.program_id(1) == 0) & (pl.program_id(2) == 0)

    @pl.when(first)
    def _():
        bk = bk_ref[...]
        for h in range(NH):
            bias_ref[h] = _bias_from_buckets(bk, rb_ref, gi * NH + h)

    ib = pl.program_id(1)
    res = pl.program_id(2)
    rows = pl.ds(res, CH, stride=dil) if dil > 1 else slice(None)
    head = (lambda ref, h: ref[h]) if headed else (lambda ref, h: ref[:, h * HD:(h + 1) * HD])
    scale = HD ** -0.5
    lane_head = lax.broadcasted_iota(jnp.int32, (CH, NH * LSE_W), 1) // LSE_W
    lse_tile = None
    if has_prev:
        col = lax.broadcasted_iota(jnp.int32, (CH, 2 * CH), 1)
        no_prev = (ib == 0) & (col < CH)
    for h in range(NH):
        sl = slice(h * HD, (h + 1) * HD)
        kc = head(kc_ref, h)
        vc = head(vc_ref, h)
        kvo_ref[:, sl] = kc
        kvo_ref[:, BW + h * HD:BW + (h + 1) * HD] = vc
        qh = head(q_ref, h).astype(BF16)
        if has_prev:
            kh = jnp.concatenate([head(kp_ref, h), kc], axis=0).astype(BF16)
            vh = jnp.concatenate([head(vp_ref, h), vc], axis=0).astype(BF16)
            bias = jnp.where(no_prev, NEG, bias_ref[h])
        else:
            kh = kc.astype(BF16)
            vh = vc.astype(BF16)
            bias = bias_ref[h][:, CH:]
        s = lax.dot_general(qh, kh, (((1,), (1,)), ((), ())), preferred_element_type=F32) * scale + bias
        m = jnp.max(s, axis=-1, keepdims=True)
        p = jnp.exp(s - m)
        den = jnp.sum(p, axis=-1, keepdims=True)
        acc = jnp.dot(p.astype(BF16), vh, preferred_element_type=F32)
        o_ref[h, rows, :] = acc / den
        lse = m + jnp.log(den)
        lse_tile = lse if h == 0 else jnp.where(lane_head == h, lse, lse_tile)
    lse_ref[rows, :] = lse_tile


def _attn_prompt(zq, rel_bias, gi):
    dil = ATT_DILATIONS[gi]
    rows = BATCH * SEQ
    nb = SEQ // dil // CH
    has_prev = nb > 1
    headed = gi > 0
    bk = jnp.asarray(_prompt_bucket_matrix(dil))

    def rowblk(b, i, r, prev):
        return (b * dil + r) * nb + (jnp.maximum(i - 1, 0) if prev else i)

    def zspec(c, prev=False):
        if headed:
            return pl.BlockSpec((None, NH, CH, HD), lambda b, i, r: (c, 0, rowblk(b, i, r, prev), 0))
        return pl.BlockSpec((CH, BW), lambda b, i, r: (rowblk(b, i, r, prev), COL_QKV[0][c]))

    in_specs = [pl.BlockSpec(memory_space=pltpu.SMEM),
                pl.BlockSpec((CH, 2 * CH), lambda b, i, r: (0, 0)),
                zspec(0), zspec(1), zspec(2)]
    args = [rel_bias, bk, zq, zq, zq]
    if has_prev:
        in_specs += [zspec(1, True), zspec(2, True)]
        args += [zq, zq]
    o, lse, kvo = pl.pallas_call(
        functools.partial(_attn_prompt_body, gi=gi, has_prev=has_prev, headed=headed),
        grid=(BATCH, nb, dil),
        in_specs=in_specs,
        out_specs=[pl.BlockSpec((NH, CH * dil, HD), lambda b, i, r: (0, b * nb + i, 0)),
                   pl.BlockSpec((CH * dil, NH * LSE_W), lambda b, i, r: (b * nb + i, 0)),
                   pl.BlockSpec((CH, 2 * BW), lambda b, i, r: (b, jnp.where(i == nb - 1, r, 0)))],
        out_shape=[jax.ShapeDtypeStruct((NH, rows, HD), F32),
                   jax.ShapeDtypeStruct((rows, NH * LSE_W), F32),
                   jax.ShapeDtypeStruct((BATCH * CH, dil * 2 * BW), F32)],
        scratch_shapes=[pltpu.VMEM((NH, CH, 2 * CH), F32)],
        compiler_params=_cparams(3),
        name=f"attn_prompt_g{gi}",
    )(*args)
    kv = kvo.reshape(BATCH, ATT_WINDOWS[gi], 2, NH, HD)
    return o, lse, kv


def _combine_body(o0, l0, o1, l1, o2, l2, out_ref):
    for h in range(NH):
        a0, a1, a2 = (l[:, h * LSE_W:h * LSE_W + 1] for l in (l0, l1, l2))
        m = jnp.maximum(jnp.maximum(a0, a1), a2)
        w0, w1, w2 = jnp.exp(a0 - m), jnp.exp(a1 - m), jnp.exp(a2 - m)
        out_ref[:, h * HD:(h + 1) * HD] = ((w0 * o0[h] + w1 * o1[h] + w2 * o2[h]) / (w0 + w1 + w2)).astype(BF16)


def _attn_combine(parts, *, tm=1024):
    rows = parts[1].shape[0]
    ospec = pl.BlockSpec((NH, tm, HD), lambda i: (0, i, 0))
    lspec = pl.BlockSpec((tm, NH * LSE_W), lambda i: (i, 0))
    return pl.pallas_call(
        _combine_body,
        grid=(rows // tm,),
        in_specs=[ospec, lspec] * 3,
        out_specs=pl.BlockSpec((tm, BW), lambda i: (i, 0)),
        out_shape=jax.ShapeDtypeStruct((rows, BW), BF16),
        compiler_params=_cparams(1),
        name="attn_combine",
    )(*parts)


def _attn_sample_body(rb_ref, bc0_ref, bc1_ref, bc2_ref, bn_ref, z_ref, c0_ref, c1_ref, c2_ref,
                      oa_ref, kv0_ref, kv1_ref, kv2_ref, b0_ref, b1_ref, b2_ref, bnew_ref):
    bucket_refs = (bc0_ref, bc1_ref, bc2_ref)
    bias_refs = (b0_ref, b1_ref, b2_ref)

    @pl.when(pl.program_id(0) == 0)
    def _():
        for gi in range(3):
            for h in range(NH):
                rs = slice(h * SROWS, (h + 1) * SROWS)
                bias_refs[gi][rs, :] = _bias_from_buckets(bucket_refs[gi][rs, :], rb_ref, gi * NH + h)
                bnew_ref[gi, rs, :] = _bias_from_buckets(bn_ref[gi, rs, :], rb_ref, gi * NH + h)

    scale = HD ** -0.5
    rows = NH * SROWS
    head_of_row = lax.broadcasted_iota(jnp.int32, (rows, BW), 0) // SROWS
    head_of_lane = lax.broadcasted_iota(jnp.int32, (rows, BW), 1) // HD
    head_mask = head_of_row == head_of_lane
    caches = (c0_ref, c1_ref, c2_ref)
    kv_refs = (kv0_ref, kv1_ref, kv2_ref)
    stats = []
    for gi in range(3):
        cq, ck, cv = COL_QKV[gi]
        q = z_ref[:, cq * BW:(cq + 1) * BW]
        kn = z_ref[:, ck * BW:(ck + 1) * BW]
        vn = z_ref[:, cv * BW:(cv + 1) * BW]
        kv_refs[gi][:, :BW] = kn
        kv_refs[gi][:, BW:] = vn
        qm = jnp.concatenate([q[:, h * HD:(h + 1) * HD] for h in range(NH)], axis=0).astype(BF16)
        kf = caches[gi][...].reshape(SAMPLE_CACHE_ROWS[gi], HD).astype(BF16)
        s = lax.dot_general(qm, kf, (((1,), (1,)), ((), ())), preferred_element_type=F32) * scale + bias_refs[gi][...]
        qrows = jnp.where(head_mask, jnp.concatenate([q] * NH, axis=0), 0.0)
        bias_n = bnew_ref[gi]
        s_new = []
        for tp in range(DEC_SEQ):
            dotp = jnp.sum(qrows * kn[tp:tp + 1, :], axis=-1, keepdims=True)
            s_new.append(dotp * scale + bias_n[:, tp:tp + 1])
        m = jnp.max(s, axis=-1, keepdims=True)
        for sn in s_new:
            m = jnp.maximum(m, sn)
        p = jnp.exp(s - m)
        den = jnp.sum(p, axis=-1, keepdims=True)
        pv = pltpu.roll(p, NH, axis=1).astype(BF16)
        acc = jnp.dot(pv, kf, preferred_element_type=F32)
        acc_n = jnp.zeros((rows, BW), F32)
        for tp, sn in enumerate(s_new):
            pn = jnp.exp(sn - m)
            den = den + pn
            acc_n = acc_n + pn * vn[tp:tp + 1, :]
        acc = acc + jnp.concatenate([acc_n[h * SROWS:(h + 1) * SROWS, h * HD:(h + 1) * HD] for h in range(NH)], axis=0)
        stats.append((m, den, acc))
    mm = jnp.maximum(jnp.maximum(stats[0][0], stats[1][0]), stats[2][0])
    den = jnp.zeros((rows, 1), F32)
    acc = jnp.zeros((rows, HD), F32)
    for m, d, a in stats:
        w = jnp.exp(m - mm)
        den = den + w * d
        acc = acc + w * a
    o = acc / den
    for h in range(NH):
        oa_ref[:, h * HD:(h + 1) * HD] = o[h * SROWS:(h + 1) * SROWS, :].astype(BF16)


def _attn_sample(z, caches, rel_bias, layer):
    tables = _sample_bucket_tables()
    rows = DEC_BATCH * SROWS
    qrows = NH * SROWS
    n0, n1, n2 = SAMPLE_CACHE_ROWS
    c0 = caches[0].reshape(DEPTH, DEC_BATCH, n0, HD)
    c1 = caches[1].reshape(DEPTH, DEC_BATCH, n1, HD)
    c2 = caches[2].reshape(DEPTH, DEC_BATCH, CH, 16 * 8, HD)
    new_tbl = jnp.asarray(np.stack([t[1] for t in tables]))
    kv_spec = pl.BlockSpec((SROWS, 2 * BW), lambda b: (b, 0))
    kv_shape = jax.ShapeDtypeStruct((rows, 2 * BW), F32)
    const2 = lambda b: (0, 0)
    return pl.pallas_call(
        _attn_sample_body,
        grid=(DEC_BATCH,),
        in_specs=[pl.BlockSpec(memory_space=pltpu.SMEM),
                  pl.BlockSpec((qrows, n0), const2), pl.BlockSpec((qrows, n1), const2),
                  pl.BlockSpec((qrows, n2), const2),
                  pl.BlockSpec((3, qrows, CH), lambda b: (0, 0, 0)),
                  pl.BlockSpec((SROWS, IN_WIDTH), lambda b: (b, 0)),
                  pl.BlockSpec((None, None, n0, HD), lambda b: (layer, b, 0, 0)),
                  pl.BlockSpec((None, None, n1, HD), lambda b: (layer, b, 0, 0)),
                  pl.BlockSpec((None, None, CH, 8 * DEC_SEQ, HD), lambda b: (layer, b, 0, 0, 0))],
        out_specs=[pl.BlockSpec((SROWS, BW), lambda b: (b, 0)), kv_spec, kv_spec, kv_spec],
        out_shape=[jax.ShapeDtypeStruct((rows, BW), BF16), kv_shape, kv_shape, kv_shape],
        scratch_shapes=[pltpu.VMEM((qrows, n0), F32), pltpu.VMEM((qrows, n1), F32), pltpu.VMEM((qrows, n2), F32),
                        pltpu.VMEM((3, qrows, CH), F32)],
        compiler_params=_cparams(1),
        name="attn_sample",
    )(rel_bias, jnp.asarray(tables[0][0]), jnp.asarray(tables[1][0]), jnp.asarray(tables[2][0]), new_tbl,
      z, c0, c1, c2)


def _gmlp_body(bu_ref, bv_ref, g_ref, ws_ref, bs_ref, *refs, rows, emit_vn):
    refs = list(refs)
    o_ref = refs.pop(0)
    vn_ref = refs.pop(0) if emit_vn else None
    tril = lax.broadcasted_iota(jnp.int32, (CH, CH), 0) >= lax.broadcasted_iota(jnp.int32, (CH, CH), 1)
    w = [jnp.where(tril, ws_ref[g], 0.0).astype(BF16) for g in range(NH)]
    if rows < CH:
        pad_u, pad_v = refs
        pad_u[...] = jnp.zeros_like(pad_u)
        pad_v[...] = jnp.zeros_like(pad_v)
        pad_u[0:rows, :] = bu_ref[...]
        pad_v[0:rows, :] = bv_ref[...]
        bu_ref, bv_ref = pad_u, pad_v
    for c in range(max(rows // CH, 1)):
        rs = slice(c * CH, (c + 1) * CH)
        u = jax.nn.gelu(bu_ref[rs, :])
        vn = _rms(jax.nn.gelu(bv_ref[rs, :]), g_ref[...])
        if emit_vn:
            vn_ref[...] = vn[:rows]
        for g in range(NH):
            sl = slice(g * HD, (g + 1) * HD)
            mixed = jnp.dot(w[g], vn[:, sl].astype(BF16), preferred_element_type=F32) + bs_ref[:, sl]
            res = (u[:, sl] * mixed).astype(BF16)
            if rows < CH:
                o_ref[:, sl] = res[:rows]
            else:
                o_ref[rs, sl] = res


def _gmlp(z, g_gmlp, w_spatial, b_spatial, *, rows, emit_vn):
    m = z.shape[0]
    bs_full = jnp.repeat(b_spatial.T, HD, axis=1)
    spec = pl.BlockSpec((rows, BW), lambda i: (i, 0))
    out_shape = [jax.ShapeDtypeStruct((m, BW), BF16)]
    out_specs = [spec]
    if emit_vn:
        out_shape.append(jax.ShapeDtypeStruct((m, BW), F32))
        out_specs.append(spec)
    scratch = [] if rows >= CH else [pltpu.VMEM((CH, BW), F32)] * 2
    return pl.pallas_call(
        functools.partial(_gmlp_body, rows=rows, emit_vn=emit_vn),
        grid=(m // rows,),
        in_specs=[pl.BlockSpec((rows, BW), lambda i: (i, COL_BU)),
                  pl.BlockSpec((rows, BW), lambda i: (i, COL_BV)),
                  pl.BlockSpec((1, BW), lambda i: (0, 0)),
                  pl.BlockSpec((NH, CH, CH), lambda i: (0, 0, 0)),
                  pl.BlockSpec((CH, BW), lambda i: (0, 0))],
        out_specs=out_specs,
        out_shape=out_shape,
        scratch_shapes=scratch,
        compiler_params=_cparams(1),
        name="gmlp",
    )(z, z, g_gmlp.reshape(1, -1), w_spatial, bs_full)


def _pool_body(x_ref, prev_ref, wp_ref, sc_ref, o_ref, st_ref, ext_ref, *, rows, n_new, start, zero_first_prev):
    ib = pl.program_id(1)
    prev = prev_ref[...]
    if zero_first_prev:
        prev = jnp.where(ib == 0, 0.0, prev)
    x = x_ref[...]
    ext_ref[0:16, :] = prev
    ext_ref[16:16 + rows, :] = x
    st_ref[...] = ext_ref[pl.ds(n_new + 1, POOL_STATE), :]
    ext = ext_ref[...]
    pos = start + ib * rows + lax.broadcasted_iota(jnp.int32, (rows, 1), 0)
    for gi, win in enumerate(POOL_WINDOWS):
        sl = slice(gi * HD, (gi + 1) * HD)
        s = ext[:, sl]
        k = 1
        while k < win:
            s = s + pltpu.roll(s, k, axis=0)
            k *= 2
        cnt = jnp.minimum(pos + 1, win).astype(F32)
        diff = s[16:] / cnt - x[:, sl]
        y = jnp.dot(diff.astype(BF16), wp_ref[gi].astype(BF16), preferred_element_type=F32)
        o_ref[:, sl] = (y * sc_ref[:, sl]).astype(BF16)


def _pool(z, prev, w_pool, pool_scale, *, nseq, rows, n_new, start, layer=None):
    m = z.shape[0]
    nblk = m // nseq // rows
    if prev is None:
        per16 = rows // 16
        prev_arr = z
        prev_spec = pl.BlockSpec((16, BW), lambda b, i: (jnp.maximum((b * nblk + i) * per16 - 1, 0), COL_CIN))
    else:
        prev_arr = prev
        prev_spec = pl.BlockSpec((None, None, 16, BW), lambda b, i: (layer, b, 0, 0))
    return pl.pallas_call(
        functools.partial(_pool_body, rows=rows, n_new=n_new, start=start, zero_first_prev=prev is None),
        grid=(nseq, nblk),
        in_specs=[pl.BlockSpec((rows, BW), lambda b, i: (b * nblk + i, COL_CIN)),
                  prev_spec,
                  pl.BlockSpec((NH, HD, HD), lambda b, i: (0, 0, 0)),
                  pl.BlockSpec((1, BW), lambda b, i: (0, 0))],
        out_specs=[pl.BlockSpec((rows, BW), lambda b, i: (b * nblk + i, 0)),
                   pl.BlockSpec((None, POOL_STATE, BW), lambda b, i: (b, 0, 0))],
        out_shape=[jax.ShapeDtypeStruct((m, BW), BF16),
                   jax.ShapeDtypeStruct((nseq, POOL_STATE, BW), F32)],
        scratch_shapes=[pltpu.VMEM((16 + rows, BW), F32)],
        compiler_params=_cparams(2),
        name="pool",
    )(z, prev_arr, w_pool, pool_scale.reshape(1, -1))


def _ret_tables(c_eff, positions):
    lg = np.log1p(-np.power(2.0, -5.0 - np.arange(NH, dtype=np.float64)))
    i = np.arange(CH, dtype=np.float64)
    live = (i < c_eff)
    diff = i[:, None] - i[None, :]
    inner = np.where((diff >= 0) & live[:, None] & live[None, :], np.exp(np.maximum(diff, 0.0)[None] * lg[:, None, None]), 0.0)
    qd = np.where(live[None, :], np.exp((i + 1.0)[None, :] * lg[:, None]), 0.0)
    kd = np.where(live[None, :], np.exp((c_eff - 1.0 - i)[None, :] * lg[:, None]), 0.0)
    chunk = tuple(float(v) for v in np.exp(c_eff * lg))
    qd_full = np.repeat(qd.T, HD, axis=1)
    kd_full = np.repeat(kd.T, HD, axis=1) * (HD ** -0.5)
    half = HD // 2
    inv = ROPE_BASE ** (-np.arange(half, dtype=np.float64) / half)
    ang = np.asarray(positions, np.float64)[:, None] * inv[None, :]
    cosf = np.concatenate([np.cos(ang), np.cos(ang)], axis=1)
    sinf = np.concatenate([-np.sin(ang), np.sin(ang)], axis=1)
    to32 = lambda a: jnp.asarray(a.astype(np.float32))
    return to32(inner), to32(qd_full), to32(kd_full), chunk, to32(cosf), to32(sinf)


def _ret_body(q_ref, k_ref, v_ref, g_ref, cos_ref, sin_ref, inner_ref, qd_ref, kd_ref, gr_ref, *refs,
              nseq, rows, chunk_decay, has_state):
    refs = list(refs)
    s0_ref = refs.pop(0) if has_state else None
    o_ref, sn_ref, s_ref = refs[:3]
    pad_ref = refs[3] if rows != CH else None
    ic = pl.program_id(1)

    @pl.when(ic == 0)
    def _():
        if has_state:
            s_ref[...] = s0_ref[...]
        else:
            s_ref[...] = jnp.zeros_like(s_ref)

    def chunk(ref, b, k):
        if rows == CH:
            return ref[b]
        pad_ref[k] = jnp.zeros((CH, BW), F32)
        pad_ref[k, 0:rows, :] = ref[b]
        return pad_ref[k]

    cosf = cos_ref[...]
    sinf = sin_ref[...]
    for b in range(nseq):
        q = chunk(q_ref, b, 0)
        k = chunk(k_ref, b, 1)
        v = chunk(v_ref, b, 2)
        gate = chunk(g_ref, b, 3)
        for h in range(NH):
            sl = slice(h * HD, (h + 1) * HD)
            qh = q[:, sl]
            kh = k[:, sl]
            rq = qh * cosf + pltpu.roll(qh, HD // 2, axis=1) * sinf
            rk = kh * cosf + pltpu.roll(kh, HD // 2, axis=1) * sinf
            vb = v[:, sl].astype(BF16)
            rqb = rq.astype(BF16)
            state = s_ref[b, h]
            att = lax.dot_general(rqb, (rk * (HD ** -0.5)).astype(BF16), (((1,), (1,)), ((), ())),
                                  preferred_element_type=F32) * inner_ref[h]
            o = (jnp.dot(att.astype(BF16), vb, preferred_element_type=F32)
                 + jnp.dot(rqb, state.astype(BF16), preferred_element_type=F32) * qd_ref[:, sl])
            kdec = (rk * kd_ref[:, sl]).T.astype(BF16)
            new_state = state * chunk_decay[h] + jnp.dot(kdec, vb, preferred_element_type=F32)
            s_ref[b, h] = new_state
            sn_ref[b, h] = new_state
            o = o * lax.rsqrt(jnp.mean(o * o, axis=-1, keepdims=True) + EPS)
            gt = gate[:, sl]
            o_ref[b, :, sl] = (o * gr_ref[:, sl] * (gt * jax.nn.sigmoid(gt)))[:rows].astype(BF16)


def _retention(z, g_ret, state, *, nseq, per_step, rows, c_eff, positions, layer=None):
    m, zw = z.shape
    seq_rows = m // nseq
    nchunk = seq_rows // rows
    inner, qd, kd, chunk_decay, cosf, sinf = _ret_tables(c_eff, positions)
    has_state = state is not None
    z3 = z.reshape(nseq, seq_rows, zw)

    def zspec(col):
        return pl.BlockSpec((per_step, rows, BW), lambda g, i: (g, i, col))

    const2 = lambda g, i: (0, 0)
    in_specs = [zspec(COL_DQ), zspec(COL_DK), zspec(COL_DV), zspec(COL_DG),
                pl.BlockSpec((CH, HD), lambda g, i: (i, 0)), pl.BlockSpec((CH, HD), lambda g, i: (i, 0)),
                pl.BlockSpec((NH, CH, CH), lambda g, i: (0, 0, 0)),
                pl.BlockSpec((CH, BW), const2), pl.BlockSpec((CH, BW), const2), pl.BlockSpec((1, BW), const2)]
    args = [z3, z3, z3, z3, cosf, sinf, inner, qd, kd, g_ret.reshape(1, -1)]
    if has_state:
        in_specs.append(pl.BlockSpec((None, per_step, NH, HD, HD), lambda g, i: (layer, g, 0, 0, 0)))
        args.append(state)
    scratch = [pltpu.VMEM((per_step, NH, HD, HD), F32)]
    if rows != CH:
        scratch.append(pltpu.VMEM((4, CH, BW), F32))
    o, sn = pl.pallas_call(
        functools.partial(_ret_body, nseq=per_step, rows=rows, chunk_decay=chunk_decay, has_state=has_state),
        grid=(nseq // per_step, nchunk),
        in_specs=in_specs,
        out_specs=[pl.BlockSpec((per_step, rows, BW), lambda g, i: (g, i, 0)),
                   pl.BlockSpec((per_step, NH, HD, HD), lambda g, i: (g, 0, 0, 0))],
        out_shape=[jax.ShapeDtypeStruct((nseq, seq_rows, BW), BF16),
                   jax.ShapeDtypeStruct((nseq, NH, HD, HD), F32)],
        scratch_shapes=scratch,
        compiler_params=_cparams(2),
        name="retention",
    )(*args)
    return o.reshape(m, BW), sn


def kernel(x_prompt, x_sample, cache_attn_kv_w128, cache_attn_kv_w512, cache_attn_kv_w2048, state_pool, state_ret, rel_bias, g_ffn1, w_ffn1_gate, w_ffn1_up, w_ffn1_down, g_mix, w_in, g_gmlp, w_spatial, b_spatial, w_pool, pool_scale, g_ret, w_branch, w_out, g_ffn2, w_ffn2_gate, w_ffn2_up, w_ffn2_down, g_final):
    caches = (cache_attn_kv_w128, cache_attn_kv_w512, cache_attn_kv_w2048)
    xp = x_prompt.reshape(BATCH * SEQ, D_MODEL)
    xs = jnp.pad(x_sample, ((0, 0), (0, SROWS - DEC_SEQ), (0, 0))).reshape(DEC_BATCH * SROWS, D_MODEL)
    pool_state = jnp.pad(state_pool, ((0, 0), (0, 0), (1, 0), (0, 0)))
    tm_p, tm_s = 512, DEC_BATCH * SROWS
    sample_pos = PAST_LEN + np.arange(CH)
    w1 = [w.astype(BF16) for w in (w_ffn1_gate, w_ffn1_up, w_ffn1_down)]
    w2 = [w.astype(BF16) for w in (w_ffn2_gate, w_ffn2_up, w_ffn2_down)]
    win = jnp.concatenate([w_in[:, :, c * BW:(c + 1) * BW] for c in W_IN_ORDER], axis=-1).astype(BF16)
    wb = w_branch.astype(BF16)
    wo = w_out.astype(BF16)
    gate_col0 = N_PLAIN * BW // 1024
    qkv1, qkv2 = COL_QKV[1][0], COL_QKV[2][0]

    kv_p = [[], [], []]
    kv_s = [[], [], []]
    pool_p, pool_s, ret_p, ret_s, gv_s = [], [], [], [], []
    yp = ys = None
    for l in range(DEPTH):
        last = l == DEPTH - 1

        xp, hp = _ffn(xp, g_ffn1[l], *w1, g_mix[l], layer=l, emit_x=True, post_dtype=BF16, tm=tm_p)
        zp = _matmul(hp, win, layer=l, n=N_PLAIN * BW, tm=1024, tn=1024, name="in_proj")
        gp = _matmul(hp, win, layer=l, col0=gate_col0, n=GATE_W, gate=True, tm=1024, tn=1024, name="in_proj_gate")
        zq = (zp,
              _matmul_deint(hp, win, layer=l, col0=qkv1, dil=ATT_DILATIONS[1], name="in_proj_g1"),
              _matmul_deint(hp, win, layer=l, col0=qkv2, dil=ATT_DILATIONS[2], name="in_proj_g2"))
        parts = []
        for gi in range(3):
            o, lse, kv = _attn_prompt(zq[gi], rel_bias, gi)
            parts += [o, lse]
            kv_p[gi].append(kv)
        oa = _attn_combine(parts)
        ob, = _gmlp(zp, g_gmlp[l], w_spatial[l], b_spatial[l], rows=4 * CH, emit_vn=False)
        oc, pn = _pool(zp, None, w_pool[l], pool_scale[l], nseq=BATCH, rows=512, n_new=512, start=0)
        od, rn = _retention(zp, g_ret[l], None, nseq=BATCH, per_step=BATCH, rows=CH, c_eff=CH,
                            positions=np.arange(SEQ))
        pool_p.append(pn)
        ret_p.append(rn)
        mix = _branch_merge(oa, ob, oc, od, gp, wb, layer=l, tm=256)
        xp = _matmul(mix, wo, xp, layer=l, tm=1024, tn=512, name="out_proj")
        if last:
            yp, = _ffn(xp, g_ffn2[l], *w2, g_final, layer=l, emit_x=False, post_dtype=F32, tm=tm_p)
        else:
            xp, = _ffn(xp, g_ffn2[l], *w2, g_final, layer=l, emit_x=True, post_dtype=None, tm=tm_p)

        xs, hs = _ffn(xs, g_ffn1[l], *w1, g_mix[l], layer=l, emit_x=True, post_dtype=BF16, tm=tm_s)
        zs = _matmul(hs, win, layer=l, tm=tm_s, tn=1024, name="in_proj")
        gs = _matmul(hs, win, layer=l, col0=gate_col0, n=GATE_W, gate=True, tm=tm_s, tn=1024, name="in_proj_gate")
        oa, k0, k1, k2 = _attn_sample(zs, caches, rel_bias, l)
        for gi, kv in enumerate((k0, k1, k2)):
            kv_s[gi].append(kv.reshape(DEC_BATCH, SROWS, 2, NH, HD)[:, :DEC_SEQ])
        ob, vn = _gmlp(zs, g_gmlp[l], w_spatial[l], b_spatial[l], rows=SROWS, emit_vn=True)
        gv_s.append(vn.reshape(DEC_BATCH, SROWS, BW)[:, :DEC_SEQ])
        oc, pn = _pool(zs, pool_state, w_pool[l], pool_scale[l], nseq=DEC_BATCH, rows=SROWS, n_new=DEC_SEQ,
                      start=PAST_LEN, layer=l)
        od, rn = _retention(zs, g_ret[l], state_ret, nseq=DEC_BATCH, per_step=1, rows=SROWS, c_eff=DEC_SEQ,
                            positions=sample_pos, layer=l)
        pool_s.append(pn)
        ret_s.append(rn)
        mix = _branch_merge(oa, ob, oc, od, gs, wb, layer=l, tm=tm_s)
        xs = _matmul(mix, wo, xs, layer=l, tm=tm_s, tn=512, name="out_proj")
        if last:
            ys, = _ffn(xs, g_ffn2[l], *w2, g_final, layer=l, emit_x=False, post_dtype=F32, tm=tm_s)
        else:
            xs, = _ffn(xs, g_ffn2[l], *w2, g_final, layer=l, emit_x=True, post_dtype=None, tm=tm_s)

    y_prompt = yp.reshape(BATCH, SEQ, D_MODEL)
    y_sample = ys.reshape(DEC_BATCH, SROWS, D_MODEL)[:, :DEC_SEQ]
    return (y_prompt, y_sample,
            jnp.stack(kv_p[0]), jnp.stack(kv_p[1]), jnp.stack(kv_p[2]),
            jnp.stack(kv_s[0]), jnp.stack(kv_s[1]), jnp.stack(kv_s[2]),
            jnp.stack(pool_p), jnp.stack(pool_s),
            jnp.stack(ret_p), jnp.stack(ret_s),
            jnp.stack(gv_s))
```

```python
import functools

import numpy as np
import jax
import jax.numpy as jnp
from jax import lax
from jax.experimental import pallas as pl
from jax.experimental.pallas import tpu as pltpu

F32 = jnp.float32
BF16 = jnp.bfloat16

D_MODEL = 2048
BATCH = 4
SEQ = 2048
DEPTH = 2
DEC_BATCH = 32
DEC_SEQ = 4
PAST_LEN = 8192
D_FF = 5632
IN_WIDTH = 16384
EPS = 1e-6
BW = 512
HD = 128
NH = 4
ATT_WINDOWS = (128, 512, 2048)
ATT_DILATIONS = (1, 4, 16)
NKEY = 129
REL_BUCKETS = 32
REL_MAX_DISTANCE = 2048
POOL_WINDOWS = (2, 4, 8, 16)
POOL_STATE = 15
ROPE_BASE = 10000.0
SROWS = 8
CH = 128
NEG = -1e30
LSE_W = 32

W_IN_ORDER = (0, 3, 6) + tuple(range(9, 32)) + (1, 4, 7) + (2, 5, 8)
COL_QKV = ((0, 1, 2), (26, 27, 28), (29, 30, 31))
COL_BU, COL_BV, COL_CIN = 3, 4, 5
COL_DQ, COL_DK, COL_DV, COL_DG = 6, 7, 8, 9
N_PLAIN = 10
N_GATE = 16
GATE_W = N_GATE * BW
ZBLK = IN_WIDTH // BW

VMEM_BIG = 56 * 1024 * 1024


def _cparams(n_axes, vmem=None):
    return pltpu.CompilerParams(dimension_semantics=("arbitrary",) * n_axes, vmem_limit_bytes=vmem)


def _rms(x, g):
    return x * lax.rsqrt(jnp.mean(x * x, axis=-1, keepdims=True) + EPS) * g


def _ffn_body(x_ref, gpre_ref, wg_ref, wu_ref, wd_ref, gpost_ref, *refs, emit_x, emit_post, convert, nf):
    outs = refs[:int(emit_x) + int(emit_post)]
    xn_ref, acc_ref = refs[-2:]
    f = pl.program_id(1)

    @pl.when(f == 0)
    def _():
        xn_ref[...] = _rms(x_ref[...], gpre_ref[...]).astype(BF16)
        acc_ref[...] = jnp.zeros_like(acc_ref)

    wg, wu, wd = wg_ref[...], wu_ref[...], wd_ref[...]
    if convert:
        wg, wu, wd = wg.astype(BF16), wu.astype(BF16), wd.astype(BF16)
        for o_ref, w in zip(refs[len(outs):len(outs) + 3], (wg, wu, wd)):
            o_ref[...] = w
    xn = xn_ref[...]
    g = jnp.dot(xn, wg, preferred_element_type=F32)
    u = jnp.dot(xn, wu, preferred_element_type=F32)
    h = (g * jax.nn.sigmoid(g) * u).astype(BF16)
    acc_ref[...] += jnp.dot(h, wd, preferred_element_type=F32)

    @pl.when(f == nf - 1)
    def _():
        xo = x_ref[...] + 0.5 * acc_ref[...]
        k = 0
        if emit_x:
            outs[k][...] = xo
            k += 1
        if emit_post:
            outs[k][...] = _rms(xo, gpost_ref[...]).astype(outs[k].dtype)


def _ffn(x, g_pre, wg, wu, wd, g_post, *, layer=None, emit_x, post_dtype, tm, tf=512):
    m = x.shape[0]
    nf = D_FF // tf
    emit_post = post_dtype is not None
    convert = layer is not None
    out_shape, out_specs = [], []
    if emit_x:
        out_shape.append(jax.ShapeDtypeStruct((m, D_MODEL), F32))
        out_specs.append(pl.BlockSpec((tm, D_MODEL), lambda i, f: (i, 0)))
    if emit_post:
        out_shape.append(jax.ShapeDtypeStruct((m, D_MODEL), post_dtype))
        out_specs.append(pl.BlockSpec((tm, D_MODEL), lambda i, f: (i, 0)))
    up_spec = pl.BlockSpec((D_MODEL, tf), lambda i, f: (0, f))
    down_spec = pl.BlockSpec((tf, D_MODEL), lambda i, f: (f, 0))
    if convert:
        assert m == tm, "the bf16 weights are written once, by a single row tile"
        w_specs = [pl.BlockSpec((None, D_MODEL, tf), lambda i, f: (layer, 0, f)),
                   pl.BlockSpec((None, D_MODEL, tf), lambda i, f: (layer, 0, f)),
                   pl.BlockSpec((None, tf, D_MODEL), lambda i, f: (layer, f, 0))]
        out_shape += [jax.ShapeDtypeStruct((D_MODEL, D_FF), BF16)] * 2 + [jax.ShapeDtypeStruct((D_FF, D_MODEL), BF16)]
        out_specs += [up_spec, up_spec, down_spec]
    else:
        w_specs = [up_spec, up_spec, down_spec]
    return pl.pallas_call(
        functools.partial(_ffn_body, emit_x=emit_x, emit_post=emit_post, convert=convert, nf=nf),
        grid=(m // tm, nf),
        in_specs=[
            pl.BlockSpec((tm, D_MODEL), lambda i, f: (i, 0)),
            pl.BlockSpec((1, D_MODEL), lambda i, f: (0, 0)),
            *w_specs,
            pl.BlockSpec((1, D_MODEL), lambda i, f: (0, 0)),
        ],
        out_specs=out_specs,
        out_shape=out_shape,
        scratch_shapes=[pltpu.VMEM((tm, D_MODEL), BF16), pltpu.VMEM((tm, D_MODEL), F32)],
        compiler_params=_cparams(2, VMEM_BIG),
        name="ffn",
    )(x, g_pre.reshape(1, -1), wg, wu, wd, g_post.reshape(1, -1))


def _mm_body(a_ref, b_ref, o_ref):
    o_ref[...] = jnp.dot(a_ref[...], b_ref[...], preferred_element_type=F32)


def _mm_res_body(a_ref, b_ref, r_ref, o_ref):
    o_ref[...] = r_ref[...] + jnp.dot(a_ref[...], b_ref[...], preferred_element_type=F32)


def _mm_gate_body(a_ref, b_ref, o_ref):
    o_ref[...] = jax.nn.sigmoid(jnp.dot(a_ref[...], b_ref[...], preferred_element_type=F32)).astype(BF16)


def _matmul(a, b, res=None, *, layer=None, col0=0, n=None, gate=False, tm, tn, name):
    m, k = a.shape
    n = b.shape[-1] if n is None else n
    if layer is None:
        b_spec = pl.BlockSpec((k, tn), lambda i, j: (0, col0 + j))
    else:
        b_spec = pl.BlockSpec((None, k, tn), lambda i, j: (layer, 0, col0 + j))
    in_specs = [pl.BlockSpec((tm, k), lambda i, j: (i, 0)), b_spec]
    args = [a, b]
    body = _mm_gate_body if gate else _mm_body
    if res is not None:
        in_specs.append(pl.BlockSpec((tm, tn), lambda i, j: (i, j)))
        args.append(res)
        body = _mm_res_body
    return pl.pallas_call(
        body,
        grid=(m // tm, n // tn),
        in_specs=in_specs,
        out_specs=pl.BlockSpec((tm, tn), lambda i, j: (i, j)),
        out_shape=jax.ShapeDtypeStruct((m, n), BF16 if gate else F32),
        compiler_params=_cparams(2, VMEM_BIG),
        name=name,
    )(*args)


def _mm_deint_body(a_ref, b_ref, o_ref, s_ref, *, dil):
    r = jnp.dot(a_ref[...], b_ref[...], preferred_element_type=F32)
    for h in range(NH):
        s_ref[h] = r[:, h * HD:(h + 1) * HD]
    n = SEQ // dil
    for h in range(NH):
        for res in range(dil):
            o_ref[h, pl.ds(res * n, n), :] = s_ref[h, pl.ds(res, n, stride=dil), :]


def _w_in_source_block(j, where=jnp.where):
    return where(j < 3, 3 * j, where(j < 26, j + 6, where(j < 29, 3 * (j - 26) + 1, 3 * (j - 29) + 2)))


assert tuple(_w_in_source_block(np.arange(ZBLK), np.where)) == W_IN_ORDER


def _in_proj_convert_body(a_ref, b_ref, z_ref, gate_ref, w_ref):
    j = pl.program_id(0)
    w = b_ref[...].astype(BF16)
    w_ref[...] = w
    r = jnp.dot(a_ref[...], w, preferred_element_type=F32)
    z_ref[...] = r

    @pl.when((j >= N_PLAIN) & (j < N_PLAIN + N_GATE))
    def _():
        gate_ref[...] = jax.nn.sigmoid(r).astype(BF16)


def _in_proj_convert(a, w_in, *, layer):
    m, k = a.shape
    return pl.pallas_call(
        _in_proj_convert_body,
        grid=(ZBLK,),
        in_specs=[pl.BlockSpec((m, k), lambda j: (0, 0)),
                  pl.BlockSpec((None, k, BW), lambda j: (layer, 0, _w_in_source_block(j)))],
        out_specs=[pl.BlockSpec((m, BW), lambda j: (0, j)),
                   pl.BlockSpec((m, BW), lambda j: (0, jnp.clip(j - N_PLAIN, 0, N_GATE - 1))),
                   pl.BlockSpec((k, BW), lambda j: (0, j))],
        out_shape=[jax.ShapeDtypeStruct((m, IN_WIDTH), F32),
                   jax.ShapeDtypeStruct((m, GATE_W), BF16),
                   jax.ShapeDtypeStruct((k, IN_WIDTH), BF16)],
        compiler_params=_cparams(1, VMEM_BIG),
        name="in_proj_convert",
    )(a, w_in)


def _matmul_deint(a, b, *, col0, dil, name):
    m, k = a.shape
    return pl.pallas_call(
        functools.partial(_mm_deint_body, dil=dil),
        grid=(BATCH, 3),
        in_specs=[pl.BlockSpec((SEQ, k), lambda b, j: (b, 0)),
                  pl.BlockSpec((k, BW), lambda b, j: (0, col0 + j))],
        out_specs=pl.BlockSpec((None, NH, SEQ, HD), lambda b, j: (j, 0, b, 0)),
        out_shape=jax.ShapeDtypeStruct((3, NH, m, HD), F32),
        scratch_shapes=[pltpu.VMEM((NH, SEQ, HD), F32)],
        compiler_params=_cparams(2, VMEM_BIG),
        name=name,
    )(a, b)


def _branch_body(oa_ref, ob_ref, oc_ref, od_ref, gate_ref, wb_ref, o_ref):
    acc = None
    for n, br in enumerate((oa_ref, ob_ref, oc_ref, od_ref)):
        proj = jnp.dot(br[...], wb_ref[n], preferred_element_type=F32)
        t = gate_ref[:, n * D_MODEL:(n + 1) * D_MODEL].astype(F32) * proj
        acc = t if acc is None else acc + t
    o_ref[...] = acc.astype(BF16)


def _branch_merge(oa, ob, oc, od, gates, wb, *, layer, tm):
    m = oa.shape[0]
    br_spec = pl.BlockSpec((tm, BW), lambda i: (i, 0))
    return pl.pallas_call(
        _branch_body,
        grid=(m // tm,),
        in_specs=[br_spec] * 4 + [pl.BlockSpec((tm, GATE_W), lambda i: (i, 0)),
                                  pl.BlockSpec((None, 4, BW, D_MODEL), lambda i: (layer, 0, 0, 0))],
        out_specs=pl.BlockSpec((tm, D_MODEL), lambda i: (i, 0)),
        out_shape=jax.ShapeDtypeStruct((m, D_MODEL), BF16),
        compiler_params=_cparams(1, VMEM_BIG),
        name="branch_merge",
    )(oa, ob, oc, od, gates, wb)


def _t5_buckets(dist):
    max_exact = REL_BUCKETS // 2
    d = np.maximum(dist, 1).astype(np.float32)
    large = max_exact + (np.log(d / max_exact) / np.log(REL_MAX_DISTANCE / max_exact)
                         * (REL_BUCKETS - max_exact)).astype(np.int32)
    large = np.minimum(large, REL_BUCKETS - 1)
    return np.where(dist < max_exact, dist, large).astype(np.int32)


def _bucket_of_step(dil):
    return _t5_buckets(dil * np.arange(NKEY))


def _prompt_bucket_matrix(dil):
    iq = np.arange(CH)[:, None]
    col = np.arange(2 * CH)[None, :]
    j = iq + CH - col
    valid = (j >= 0) & (j < NKEY)
    return np.where(valid, _bucket_of_step(dil)[np.clip(j, 0, NKEY - 1)], -1).astype(np.int32)


SAMPLE_CACHE_ROWS = (8 * ATT_WINDOWS[0], 8 * ATT_WINDOWS[1], 8 * DEC_SEQ * CH)


def _sample_bucket_tables():
    h = np.repeat(np.arange(NH), SROWS)[:, None]
    t = np.tile(np.arange(SROWS), NH)[:, None]
    lane = np.arange(CH)[None, :]
    tables = []
    for gi, dil in enumerate(ATT_DILATIONS):
        bos = _bucket_of_step(dil)
        flat = np.arange(SAMPLE_CACHE_ROWS[gi])[None, :]
        head, is_k = flat % NH, (flat // NH) % 2 == 0
        if gi < 2:
            w = flat // 8
            steps = ATT_WINDOWS[gi] + t - w
            valid = (steps % dil == 0) & (steps // dil < NKEY)
            j = steps // dil
        else:
            res, i = (flat // 8) % DEC_SEQ, flat // (8 * DEC_SEQ)
            valid = res == t
            j = np.broadcast_to(CH - i, valid.shape)
        valid = valid & is_k & (head == h) & (t < DEC_SEQ)
        cache_tbl = np.where(valid, bos[np.clip(j, 0, NKEY - 1)], -1)
        jn = t - lane
        validn = (t < DEC_SEQ) & (lane < DEC_SEQ) & (jn >= 0) & ((dil == 1) | (jn == 0))
        new_tbl = np.where(validn, bos[np.clip(jn, 0, NKEY - 1)], -1)
        tables.append((cache_tbl.astype(np.int32), new_tbl.astype(np.int32)))
    return tables


def _bias_from_buckets(bk, rb_ref, col):
    out = jnp.full(bk.shape, NEG, F32)
    for b in range(REL_BUCKETS):
        out = jnp.where(bk == b, rb_ref[b, col], out)
    return out


def _attn_prompt_body(rb_ref, bk_ref, q_ref, kc_ref, vc_ref, *refs, gi, has_prev, headed):
    dil = ATT_DILATIONS[gi]
    if has_prev:
        kp_ref, vp_ref = refs[:2]
        refs = refs[2:]
    o_ref, lse_ref, kvo_ref, bias_ref = refs[-4:]
    first = (pl.program_id(0) == 0) & (pl.program_id(1) == 0) & (pl.program_id(2) == 0)

    @pl.when(first)
    def _():
        bk = bk_ref[...]
        for h in range(NH):
            bias_ref[h] = _bias_from_buckets(bk, rb_ref, gi * NH + h)

    ib = pl.program_id(1)
    res = pl.program_id(2)
    rows = pl.ds(res, CH, stride=dil) if dil > 1 else slice(None)
    head = (lambda ref, h: ref[h]) if headed else (lambda ref, h: ref[:, h * HD:(h + 1) * HD])
    scale = HD ** -0.5
    lane_head = lax.broadcasted_iota(jnp.int32, (CH, NH * LSE_W), 1) // LSE_W
    lse_tile = None
    if has_prev:
        col = lax.broadcasted_iota(jnp.int32, (CH, 2 * CH), 1)
        no_prev = (ib == 0) & (col < CH)
    for h in range(NH):
        sl = slice(h * HD, (h + 1) * HD)
        kc = head(kc_ref, h)
        vc = head(vc_ref, h)
        kvo_ref[:, sl] = kc
        kvo_ref[:, BW + h * HD:BW + (h + 1) * HD] = vc
        qh = head(q_ref, h).astype(BF16)
        if has_prev:
            kh = jnp.concatenate([head(kp_ref, h), kc], axis=0).astype(BF16)
            vh = jnp.concatenate([head(vp_ref, h), vc], axis=0).astype(BF16)
            bias = jnp.where(no_prev, NEG, bias_ref[h])
        else:
            kh = kc.astype(BF16)
            vh = vc.astype(BF16)
            bias = bias_ref[h][:, CH:]
        s = lax.dot_general(qh, kh, (((1,), (1,)), ((), ())), preferred_element_type=F32) * scale + bias
        m = jnp.max(s, axis=-1, keepdims=True)
        p = jnp.exp(s - m)
        den = jnp.sum(p, axis=-1, keepdims=True)
        acc = jnp.dot(p.astype(BF16), vh, preferred_element_type=F32)
        o_ref[h, rows, :] = acc / den
        lse = m + jnp.log(den)
        lse_tile = lse if h == 0 else jnp.where(lane_head == h, lse, lse_tile)
    lse_ref[rows, :] = lse_tile


def _attn_prompt(zq, rel_bias, gi, layer, kv_all):
    dil = ATT_DILATIONS[gi]
    rows = BATCH * SEQ
    nb = SEQ // dil // CH
    has_prev = nb > 1
    headed = gi > 0
    bk = jnp.asarray(_prompt_bucket_matrix(dil))

    def rowblk(b, i, r, prev):
        return (b * dil + r) * nb + (jnp.maximum(i - 1, 0) if prev else i)

    def zspec(c, prev=False):
        if headed:
            return pl.BlockSpec((None, NH, CH, HD), lambda b, i, r: (c, 0, rowblk(b, i, r, prev), 0))
        return pl.BlockSpec((CH, BW), lambda b, i, r: (rowblk(b, i, r, prev), COL_QKV[0][c]))

    in_specs = [pl.BlockSpec(memory_space=pltpu.SMEM),
                pl.BlockSpec((CH, 2 * CH), lambda b, i, r: (0, 0)),
                zspec(0), zspec(1), zspec(2)]
    args = [rel_bias, bk, zq, zq, zq]
    if has_prev:
        in_specs += [zspec(1, True), zspec(2, True)]
        args += [zq, zq]
    aliases = {}
    if kv_all is not None:
        in_specs.append(pl.BlockSpec(memory_space=pl.ANY))
        args.append(kv_all)
        aliases = {len(args) - 1: 2}
    return pl.pallas_call(
        functools.partial(_attn_prompt_body, gi=gi, has_prev=has_prev, headed=headed),
        grid=(BATCH, nb, dil),
        in_specs=in_specs,
        out_specs=[pl.BlockSpec((NH, CH * dil, HD), lambda b, i, r: (0, b * nb + i, 0)),
                   pl.BlockSpec((CH * dil, NH * LSE_W), lambda b, i, r: (b * nb + i, 0)),
                   pl.BlockSpec((None, CH, 2 * BW), lambda b, i, r: (layer, b, jnp.where(i == nb - 1, r, 0)))],
        out_shape=[jax.ShapeDtypeStruct((NH, rows, HD), F32),
                   jax.ShapeDtypeStruct((rows, NH * LSE_W), F32),
                   jax.ShapeDtypeStruct((DEPTH, BATCH * CH, dil * 2 * BW), F32)],
        scratch_shapes=[pltpu.VMEM((NH, CH, 2 * CH), F32)],
        input_output_aliases=aliases,
        compiler_params=_cparams(3),
        name=f"attn_prompt_g{gi}",
    )(*args)


def _combine_body(o0, l0, o1, l1, o2, l2, out_ref):
    for h in range(NH):
        a0, a1, a2 = (l[:, h * LSE_W:h * LSE_W + 1] for l in (l0, l1, l2))
        m = jnp.maximum(jnp.maximum(a0, a1), a2)
        w0, w1, w2 = jnp.exp(a0 - m), jnp.exp(a1 - m), jnp.exp(a2 - m)
        out_ref[:, h * HD:(h + 1) * HD] = ((w0 * o0[h] + w1 * o1[h] + w2 * o2[h]) / (w0 + w1 + w2)).astype(BF16)


def _attn_combine(parts, *, tm=1024):
    rows = parts[1].shape[0]
    ospec = pl.BlockSpec((NH, tm, HD), lambda i: (0, i, 0))
    lspec = pl.BlockSpec((tm, NH * LSE_W), lambda i: (i, 0))
    return pl.pallas_call(
        _combine_body,
        grid=(rows // tm,),
        in_specs=[ospec, lspec] * 3,
        out_specs=pl.BlockSpec((tm, BW), lambda i: (i, 0)),
        out_shape=jax.ShapeDtypeStruct((rows, BW), BF16),
        compiler_params=_cparams(1),
        name="attn_combine",
    )(*parts)


def _attn_sample_body(rb_ref, bc0_ref, bc1_ref, bc2_ref, bn_ref, z_ref, c0_ref, c1_ref, c2_ref,
                      oa_ref, kv0_ref, kv1_ref, kv2_ref, b0_ref, b1_ref, b2_ref, bnew_ref):
    bucket_refs = (bc0_ref, bc1_ref, bc2_ref)
    bias_refs = (b0_ref, b1_ref, b2_ref)

    @pl.when(pl.program_id(0) == 0)
    def _():
        for gi in range(3):
            for h in range(NH):
                rs = slice(h * SROWS, (h + 1) * SROWS)
                bias_refs[gi][rs, :] = _bias_from_buckets(bucket_refs[gi][rs, :], rb_ref, gi * NH + h)
                bnew_ref[gi, rs, :] = _bias_from_buckets(bn_ref[gi, rs, :], rb_ref, gi * NH + h)

    scale = HD ** -0.5
    rows = NH * SROWS
    head_of_row = lax.broadcasted_iota(jnp.int32, (rows, BW), 0) // SROWS
    head_of_lane = lax.broadcasted_iota(jnp.int32, (rows, BW), 1) // HD
    head_mask = head_of_row == head_of_lane
    caches = (c0_ref, c1_ref, c2_ref)
    kv_refs = (kv0_ref, kv1_ref, kv2_ref)
    stats = []
    for gi in range(3):
        cq, ck, cv = COL_QKV[gi]
        q = z_ref[:, cq * BW:(cq + 1) * BW]
        kn = z_ref[:, ck * BW:(ck + 1) * BW]
        vn = z_ref[:, cv * BW:(cv + 1) * BW]
        kv_refs[gi][:, :BW] = kn
        kv_refs[gi][:, BW:] = vn
        qm = jnp.concatenate([q[:, h * HD:(h + 1) * HD] for h in range(NH)], axis=0).astype(BF16)
        kf = caches[gi][...].reshape(SAMPLE_CACHE_ROWS[gi], HD).astype(BF16)
        s = lax.dot_general(qm, kf, (((1,), (1,)), ((), ())), preferred_element_type=F32) * scale + bias_refs[gi][...]
        qrows = jnp.where(head_mask, jnp.concatenate([q] * NH, axis=0), 0.0)
        bias_n = bnew_ref[gi]
        s_new = []
        for tp in range(DEC_SEQ):
            dotp = jnp.sum(qrows * kn[tp:tp + 1, :], axis=-1, keepdims=True)
            s_new.append(dotp * scale + bias_n[:, tp:tp + 1])
        m = jnp.max(s, axis=-1, keepdims=True)
        for sn in s_new:
            m = jnp.maximum(m, sn)
        p = jnp.exp(s - m)
        den = jnp.sum(p, axis=-1, keepdims=True)
        pv = pltpu.roll(p, NH, axis=1).astype(BF16)
        acc = jnp.dot(pv, kf, preferred_element_type=F32)
        acc_n = jnp.zeros((rows, BW), F32)
        for tp, sn in enumerate(s_new):
            pn = jnp.exp(sn - m)
            den = den + pn
            acc_n = acc_n + pn * vn[tp:tp + 1, :]
        acc = acc + jnp.concatenate([acc_n[h * SROWS:(h + 1) * SROWS, h * HD:(h + 1) * HD] for h in range(NH)], axis=0)
        stats.append((m, den, acc))
    mm = jnp.maximum(jnp.maximum(stats[0][0], stats[1][0]), stats[2][0])
    den = jnp.zeros((rows, 1), F32)
    acc = jnp.zeros((rows, HD), F32)
    for m, d, a in stats:
        w = jnp.exp(m - mm)
        den = den + w * d
        acc = acc + w * a
    o = acc / den
    for h in range(NH):
        oa_ref[:, h * HD:(h + 1) * HD] = o[h * SROWS:(h + 1) * SROWS, :].astype(BF16)


def _attn_sample(z, caches, rel_bias, layer):
    tables = _sample_bucket_tables()
    rows = DEC_BATCH * SROWS
    qrows = NH * SROWS
    n0, n1, n2 = SAMPLE_CACHE_ROWS
    c0 = caches[0].reshape(DEPTH, DEC_BATCH, n0, HD)
    c1 = caches[1].reshape(DEPTH, DEC_BATCH, n1, HD)
    c2 = caches[2].reshape(DEPTH, DEC_BATCH, CH, 16 * 8, HD)
    new_tbl = jnp.asarray(np.stack([t[1] for t in tables]))
    kv_spec = pl.BlockSpec((SROWS, 2 * BW), lambda b: (b, 0))
    kv_shape = jax.ShapeDtypeStruct((rows, 2 * BW), F32)
    const2 = lambda b: (0, 0)
    return pl.pallas_call(
        _attn_sample_body,
        grid=(DEC_BATCH,),
        in_specs=[pl.BlockSpec(memory_space=pltpu.SMEM),
                  pl.BlockSpec((qrows, n0), const2), pl.BlockSpec((qrows, n1), const2),
                  pl.BlockSpec((qrows, n2), const2),
                  pl.BlockSpec((3, qrows, CH), lambda b: (0, 0, 0)),
                  pl.BlockSpec((SROWS, IN_WIDTH), lambda b: (b, 0)),
                  pl.BlockSpec((None, None, n0, HD), lambda b: (layer, b, 0, 0)),
                  pl.BlockSpec((None, None, n1, HD), lambda b: (layer, b, 0, 0)),
                  pl.BlockSpec((None, None, CH, 8 * DEC_SEQ, HD), lambda b: (layer, b, 0, 0, 0))],
        out_specs=[pl.BlockSpec((SROWS, BW), lambda b: (b, 0)), kv_spec, kv_spec, kv_spec],
        out_shape=[jax.ShapeDtypeStruct((rows, BW), BF16), kv_shape, kv_shape, kv_shape],
        scratch_shapes=[pltpu.VMEM((qrows, n0), F32), pltpu.VMEM((qrows, n1), F32), pltpu.VMEM((qrows, n2), F32),
                        pltpu.VMEM((3, qrows, CH), F32)],
        compiler_params=_cparams(1),
        name="attn_sample",
    )(rel_bias, jnp.asarray(tables[0][0]), jnp.asarray(tables[1][0]), jnp.asarray(tables[2][0]), new_tbl,
      z, c0, c1, c2)


def _gmlp_body(bu_ref, bv_ref, g_ref, ws_ref, bs_ref, *refs, rows, emit_vn):
    refs = list(refs)
    o_ref = refs.pop(0)
    vn_ref = refs.pop(0) if emit_vn else None
    tril = lax.broadcasted_iota(jnp.int32, (CH, CH), 0) >= lax.broadcasted_iota(jnp.int32, (CH, CH), 1)
    w = [jnp.where(tril, ws_ref[g], 0.0).astype(BF16) for g in range(NH)]
    if rows < CH:
        pad_u, pad_v = refs
        pad_u[...] = jnp.zeros_like(pad_u)
        pad_v[...] = jnp.zeros_like(pad_v)
        pad_u[0:rows, :] = bu_ref[...]
        pad_v[0:rows, :] = bv_ref[...]
        bu_ref, bv_ref = pad_u, pad_v
    for c in range(max(rows // CH, 1)):
        rs = slice(c * CH, (c + 1) * CH)
        u = jax.nn.gelu(bu_ref[rs, :])
        vn = _rms(jax.nn.gelu(bv_ref[rs, :]), g_ref[...])
        if emit_vn:
            vn_ref[...] = vn[:rows]
        for g in range(NH):
            sl = slice(g * HD, (g + 1) * HD)
            mixed = jnp.dot(w[g], vn[:, sl].astype(BF16), preferred_element_type=F32) + bs_ref[:, sl]
            res = (u[:, sl] * mixed).astype(BF16)
            if rows < CH:
                o_ref[:, sl] = res[:rows]
            else:
                o_ref[rs, sl] = res


def _gmlp(z, g_gmlp, w_spatial, b_spatial, *, rows, emit_vn):
    m = z.shape[0]
    bs_full = jnp.repeat(b_spatial.T, HD, axis=1)
    spec = pl.BlockSpec((rows, BW), lambda i: (i, 0))
    out_shape = [jax.ShapeDtypeStruct((m, BW), BF16)]
    out_specs = [spec]
    if emit_vn:
        out_shape.append(jax.ShapeDtypeStruct((m, BW), F32))
        out_specs.append(spec)
    scratch = [] if rows >= CH else [pltpu.VMEM((CH, BW), F32)] * 2
    return pl.pallas_call(
        functools.partial(_gmlp_body, rows=rows, emit_vn=emit_vn),
        grid=(m // rows,),
        in_specs=[pl.BlockSpec((rows, BW), lambda i: (i, COL_BU)),
                  pl.BlockSpec((rows, BW), lambda i: (i, COL_BV)),
                  pl.BlockSpec((1, BW), lambda i: (0, 0)),
                  pl.BlockSpec((NH, CH, CH), lambda i: (0, 0, 0)),
                  pl.BlockSpec((CH, BW), lambda i: (0, 0))],
        out_specs=out_specs,
        out_shape=out_shape,
        scratch_shapes=scratch,
        compiler_params=_cparams(1),
        name="gmlp",
    )(z, z, g_gmlp.reshape(1, -1), w_spatial, bs_full)


def _pool_body(x_ref, prev_ref, wp_ref, sc_ref, o_ref, st_ref, ext_ref, *, rows, n_new, start, zero_first_prev):
    ib = pl.program_id(1)
    prev = prev_ref[...]
    if zero_first_prev:
        prev = jnp.where(ib == 0, 0.0, prev)
    x = x_ref[...]
    ext_ref[0:16, :] = prev
    ext_ref[16:16 + rows, :] = x
    st_ref[...] = ext_ref[pl.ds(n_new + 1, POOL_STATE), :]
    ext = ext_ref[...]
    pos = start + ib * rows + lax.broadcasted_iota(jnp.int32, (rows, 1), 0)
    for gi, win in enumerate(POOL_WINDOWS):
        sl = slice(gi * HD, (gi + 1) * HD)
        s = ext[:, sl]
        k = 1
        while k < win:
            s = s + pltpu.roll(s, k, axis=0)
            k *= 2
        cnt = jnp.minimum(pos + 1, win).astype(F32)
        diff = s[16:] / cnt - x[:, sl]
        y = jnp.dot(diff.astype(BF16), wp_ref[gi].astype(BF16), preferred_element_type=F32)
        o_ref[:, sl] = (y * sc_ref[:, sl]).astype(BF16)


def _pool(z, prev, w_pool, pool_scale, *, nseq, rows, n_new, start, layer=None):
    m = z.shape[0]
    nblk = m // nseq // rows
    if prev is None:
        per16 = rows // 16
        prev_arr = z
        prev_spec = pl.BlockSpec((16, BW), lambda b, i: (jnp.maximum((b * nblk + i) * per16 - 1, 0), COL_CIN))
    else:
        prev_arr = prev
        prev_spec = pl.BlockSpec((None, None, 16, BW), lambda b, i: (layer, b, 0, 0))
    return pl.pallas_call(
        functools.partial(_pool_body, rows=rows, n_new=n_new, start=start, zero_first_prev=prev is None),
        grid=(nseq, nblk),
        in_specs=[pl.BlockSpec((rows, BW), lambda b, i: (b * nblk + i, COL_CIN)),
                  prev_spec,
                  pl.BlockSpec((NH, HD, HD), lambda b, i: (0, 0, 0)),
                  pl.BlockSpec((1, BW), lambda b, i: (0, 0))],
        out_specs=[pl.BlockSpec((rows, BW), lambda b, i: (b * nblk + i, 0)),
                   pl.BlockSpec((None, POOL_STATE, BW), lambda b, i: (b, 0, 0))],
        out_shape=[jax.ShapeDtypeStruct((m, BW), BF16),
                   jax.ShapeDtypeStruct((nseq, POOL_STATE, BW), F32)],
        scratch_shapes=[pltpu.VMEM((16 + rows, BW), F32)],
        compiler_params=_cparams(2),
        name="pool",
    )(z, prev_arr, w_pool, pool_scale.reshape(1, -1))


def _ret_tables(c_eff, positions):
    lg = np.log1p(-np.power(2.0, -5.0 - np.arange(NH, dtype=np.float64)))
    i = np.arange(CH, dtype=np.float64)
    live = (i < c_eff)
    diff = i[:, None] - i[None, :]
    inner = np.where((diff >= 0) & live[:, None] & live[None, :], np.exp(np.maximum(diff, 0.0)[None] * lg[:, None, None]), 0.0)
    qd = np.where(live[None, :], np.exp((i + 1.0)[None, :] * lg[:, None]), 0.0)
    kd = np.where(live[None, :], np.exp((c_eff - 1.0 - i)[None, :] * lg[:, None]), 0.0)
    chunk = tuple(float(v) for v in np.exp(c_eff * lg))
    qd_full = np.repeat(qd.T, HD, axis=1)
    kd_full = np.repeat(kd.T, HD, axis=1) * (HD ** -0.5)
    half = HD // 2
    inv = ROPE_BASE ** (-np.arange(half, dtype=np.float64) / half)
    ang = np.asarray(positions, np.float64)[:, None] * inv[None, :]
    cosf = np.concatenate([np.cos(ang), np.cos(ang)], axis=1)
    sinf = np.concatenate([-np.sin(ang), np.sin(ang)], axis=1)
    to32 = lambda a: jnp.asarray(a.astype(np.float32))
    return to32(inner), to32(qd_full), to32(kd_full), chunk, to32(cosf), to32(sinf)


def _ret_body(q_ref, k_ref, v_ref, g_ref, cos_ref, sin_ref, inner_ref, qd_ref, kd_ref, gr_ref, *refs,
              nseq, rows, chunk_decay, has_state):
    refs = list(refs)
    s0_ref = refs.pop(0) if has_state else None
    o_ref, sn_ref, s_ref = refs[:3]
    pad_ref = refs[3] if rows != CH else None
    ic = pl.program_id(1)

    @pl.when(ic == 0)
    def _():
        if has_state:
            s_ref[...] = s0_ref[...]
        else:
            s_ref[...] = jnp.zeros_like(s_ref)

    def chunk(ref, b, k):
        if rows == CH:
            return ref[b]
        pad_ref[k] = jnp.zeros((CH, BW), F32)
        pad_ref[k, 0:rows, :] = ref[b]
        return pad_ref[k]

    cosf = cos_ref[...]
    sinf = sin_ref[...]
    for b in range(nseq):
        q = chunk(q_ref, b, 0)
        k = chunk(k_ref, b, 1)
        v = chunk(v_ref, b, 2)
        gate = chunk(g_ref, b, 3)
        for h in range(NH):
            sl = slice(h * HD, (h + 1) * HD)
            qh = q[:, sl]
            kh = k[:, sl]
            rq = qh * cosf + pltpu.roll(qh, HD // 2, axis=1) * sinf
            rk = kh * cosf + pltpu.roll(kh, HD // 2, axis=1) * sinf
            vb = v[:, sl].astype(BF16)
            rqb = rq.astype(BF16)
            state = s_ref[b, h]
            att = lax.dot_general(rqb, (rk * (HD ** -0.5)).astype(BF16), (((1,), (1,)), ((), ())),
                                  preferred_element_type=F32) * inner_ref[h]
            o = (jnp.dot(att.astype(BF16), vb, preferred_element_type=F32)
                 + jnp.dot(rqb, state.astype(BF16), preferred_element_type=F32) * qd_ref[:, sl])
            kdec = (rk * kd_ref[:, sl]).T.astype(BF16)
            new_state = state * chunk_decay[h] + jnp.dot(kdec, vb, preferred_element_type=F32)
            s_ref[b, h] = new_state
            sn_ref[b, h] = new_state
            o = o * lax.rsqrt(jnp.mean(o * o, axis=-1, keepdims=True) + EPS)
            gt = gate[:, sl]
            o_ref[b, :, sl] = (o * gr_ref[:, sl] * (gt * jax.nn.sigmoid(gt)))[:rows].astype(BF16)


def _retention(z, g_ret, state, *, nseq, per_step, rows, c_eff, positions, layer=None):
    m, zw = z.shape
    seq_rows = m // nseq
    nchunk = seq_rows // rows
    inner, qd, kd, chunk_decay, cosf, sinf = _ret_tables(c_eff, positions)
    has_state = state is not None
    z3 = z.reshape(nseq, seq_rows, zw)

    def zspec(col):
        return pl.BlockSpec((per_step, rows, BW), lambda g, i: (g, i, col))

    const2 = lambda g, i: (0, 0)
    in_specs = [zspec(COL_DQ), zspec(COL_DK), zspec(COL_DV), zspec(COL_DG),
                pl.BlockSpec((CH, HD), lambda g, i: (i, 0)), pl.BlockSpec((CH, HD), lambda g, i: (i, 0)),
                pl.BlockSpec((NH, CH, CH), lambda g, i: (0, 0, 0)),
                pl.BlockSpec((CH, BW), const2), pl.BlockSpec((CH, BW), const2), pl.BlockSpec((1, BW), const2)]
    args = [z3, z3, z3, z3, cosf, sinf, inner, qd, kd, g_ret.reshape(1, -1)]
    if has_state:
        in_specs.append(pl.BlockSpec((None, per_step, NH, HD, HD), lambda g, i: (layer, g, 0, 0, 0)))
        args.append(state)
    scratch = [pltpu.VMEM((per_step, NH, HD, HD), F32)]
    if rows != CH:
        scratch.append(pltpu.VMEM((4, CH, BW), F32))
    o, sn = pl.pallas_call(
        functools.partial(_ret_body, nseq=per_step, rows=rows, chunk_decay=chunk_decay, has_state=has_state),
        grid=(nseq // per_step, nchunk),
        in_specs=in_specs,
        out_specs=[pl.BlockSpec((per_step, rows, BW), lambda g, i: (g, i, 0)),
                   pl.BlockSpec((per_step, NH, HD, HD), lambda g, i: (g, 0, 0, 0))],
        out_shape=[jax.ShapeDtypeStruct((nseq, seq_rows, BW), BF16),
                   jax.ShapeDtypeStruct((nseq, NH, HD, HD), F32)],
        scratch_shapes=scratch,
        compiler_params=_cparams(2),
        name="retention",
    )(*args)
    return o.reshape(m, BW), sn


def kernel(x_prompt, x_sample, cache_attn_kv_w128, cache_attn_kv_w512, cache_attn_kv_w2048, state_pool, state_ret, rel_bias, g_ffn1, w_ffn1_gate, w_ffn1_up, w_ffn1_down, g_mix, w_in, g_gmlp, w_spatial, b_spatial, w_pool, pool_scale, g_ret, w_branch, w_out, g_ffn2, w_ffn2_gate, w_ffn2_up, w_ffn2_down, g_final):
    caches = (cache_attn_kv_w128, cache_attn_kv_w512, cache_attn_kv_w2048)
    xp = x_prompt.reshape(BATCH * SEQ, D_MODEL)
    xs = jnp.pad(x_sample, ((0, 0), (0, SROWS - DEC_SEQ), (0, 0))).reshape(DEC_BATCH * SROWS, D_MODEL)
    pool_state = jnp.pad(state_pool, ((0, 0), (0, 0), (1, 0), (0, 0)))
    tm_p, tm_s = 512, DEC_BATCH * SROWS
    sample_pos = PAST_LEN + np.arange(CH)
    wb = w_branch.astype(BF16)
    wo = w_out.astype(BF16)
    gate_col0 = N_PLAIN * BW // 1024
    qkv1, qkv2 = COL_QKV[1][0], COL_QKV[2][0]

    kv_p = [None, None, None]
    kv_s = [[], [], []]
    pool_p, pool_s, ret_p, ret_s, gv_s = [], [], [], [], []
    yp = ys = None
    for l in range(DEPTH):
        last = l == DEPTH - 1

        xs, hs, *w1 = _ffn(xs, g_ffn1[l], w_ffn1_gate, w_ffn1_up, w_ffn1_down, g_mix[l], layer=l,
                           emit_x=True, post_dtype=BF16, tm=tm_s)
        zs, gs, win = _in_proj_convert(hs, w_in, layer=l)
        oa, k0, k1, k2 = _attn_sample(zs, caches, rel_bias, l)
        for gi, kv in enumerate((k0, k1, k2)):
            kv_s[gi].append(kv.reshape(DEC_BATCH, SROWS, 2, NH, HD)[:, :DEC_SEQ])
        ob, vn = _gmlp(zs, g_gmlp[l], w_spatial[l], b_spatial[l], rows=SROWS, emit_vn=True)
        gv_s.append(vn.reshape(DEC_BATCH, SROWS, BW)[:, :DEC_SEQ])
        oc, pn = _pool(zs, pool_state, w_pool[l], pool_scale[l], nseq=DEC_BATCH, rows=SROWS, n_new=DEC_SEQ,
                      start=PAST_LEN, layer=l)
        od, rn = _retention(zs, g_ret[l], state_ret, nseq=DEC_BATCH, per_step=1, rows=SROWS, c_eff=DEC_SEQ,
                            positions=sample_pos, layer=l)
        pool_s.append(pn)
        ret_s.append(rn)
        mix = _branch_merge(oa, ob, oc, od, gs, wb, layer=l, tm=tm_s)
        xs = _matmul(mix, wo, xs, layer=l, tm=tm_s, tn=512, name="out_proj")
        if last:
            ys, *w2 = _ffn(xs, g_ffn2[l], w_ffn2_gate, w_ffn2_up, w_ffn2_down, g_final, layer=l,
                           emit_x=False, post_dtype=F32, tm=tm_s)
        else:
            xs, *w2 = _ffn(xs, g_ffn2[l], w_ffn2_gate, w_ffn2_up, w_ffn2_down, g_final, layer=l,
                           emit_x=True, post_dtype=None, tm=tm_s)

        xp, hp = _ffn(xp, g_ffn1[l], *w1, g_mix[l], emit_x=True, post_dtype=BF16, tm=tm_p)
        zp = _matmul(hp, win, n=N_PLAIN * BW, tm=1024, tn=1024, name="in_proj")
        gp = _matmul(hp, win, col0=gate_col0, n=GATE_W, gate=True, tm=1024, tn=1024, name="in_proj_gate")
        zq = (zp,
              _matmul_deint(hp, win, col0=qkv1, dil=ATT_DILATIONS[1], name="in_proj_g1"),
              _matmul_deint(hp, win, col0=qkv2, dil=ATT_DILATIONS[2], name="in_proj_g2"))
        parts = []
        for gi in range(3):
            o, lse, kv_p[gi] = _attn_prompt(zq[gi], rel_bias, gi, l, kv_p[gi])
            parts += [o, lse]
        oa = _attn_combine(parts)
        ob, = _gmlp(zp, g_gmlp[l], w_spatial[l], b_spatial[l], rows=4 * CH, emit_vn=False)
        oc, pn = _pool(zp, None, w_pool[l], pool_scale[l], nseq=BATCH, rows=512, n_new=512, start=0)
        od, rn = _retention(zp, g_ret[l], None, nseq=BATCH, per_step=BATCH, rows=CH, c_eff=CH,
                            positions=np.arange(SEQ))
        pool_p.append(pn)
        ret_p.append(rn)
        mix = _branch_merge(oa, ob, oc, od, gp, wb, layer=l, tm=256)
        xp = _matmul(mix, wo, xp, layer=l, tm=1024, tn=1024, name="out_proj")
        if last:
            yp, = _ffn(xp, g_ffn2[l], *w2, g_final, emit_x=False, post_dtype=F32, tm=tm_p)
        else:
            xp, = _ffn(xp, g_ffn2[l], *w2, g_final, emit_x=True, post_dtype=None, tm=tm_p)

    y_prompt = yp.reshape(BATCH, SEQ, D_MODEL)
    y_sample = ys.reshape(DEC_BATCH, SROWS, D_MODEL)[:, :DEC_SEQ]
    kv_p = [kv.reshape(DEPTH, BATCH, ATT_WINDOWS[gi], 2, NH, HD) for gi, kv in enumerate(kv_p)]
    return (y_prompt, y_sample,
            kv_p[0], kv_p[1], kv_p[2],
            jnp.stack(kv_s[0]), jnp.stack(kv_s[1]), jnp.stack(kv_s[2]),
            jnp.stack(pool_p), jnp.stack(pool_s),
            jnp.stack(ret_p), jnp.stack(ret_s),
            jnp.stack(gv_s))
```

```python
import functools

import numpy as np
import jax
import jax.numpy as jnp
from jax import lax
from jax.experimental import pallas as pl
from jax.experimental.pallas import tpu as pltpu

F32 = jnp.float32
BF16 = jnp.bfloat16

D_MODEL = 2048
BATCH = 4
SEQ = 2048
DEPTH = 2
DEC_BATCH = 32
DEC_SEQ = 4
PAST_LEN = 8192
D_FF = 5632
IN_WIDTH = 16384
EPS = 1e-6
BW = 512
HD = 128
NH = 4
ATT_WINDOWS = (128, 512, 2048)
ATT_DILATIONS = (1, 4, 16)
NKEY = 129
REL_BUCKETS = 32
REL_MAX_DISTANCE = 2048
POOL_WINDOWS = (2, 4, 8, 16)
POOL_STATE = 15
ROPE_BASE = 10000.0
SROWS = 8
CH = 128
NEG = -1e30
LSE_W = 32

W_IN_ORDER = (0, 3, 6) + tuple(range(9, 32)) + (1, 4, 7) + (2, 5, 8)
COL_QKV = ((0, 1, 2), (26, 27, 28), (29, 30, 31))
COL_BU, COL_BV, COL_CIN = 3, 4, 5
COL_DQ, COL_DK, COL_DV, COL_DG = 6, 7, 8, 9
N_PLAIN = 10
N_GATE = 16
GATE_W = N_GATE * BW
ZBLK = IN_WIDTH // BW

VMEM_BIG = 56 * 1024 * 1024


def _cparams(n_axes, vmem=None):
    return pltpu.CompilerParams(dimension_semantics=("arbitrary",) * n_axes, vmem_limit_bytes=vmem)


def _rms(x, g):
    return x * lax.rsqrt(jnp.mean(x * x, axis=-1, keepdims=True) + EPS) * g


def _ffn_body(x_ref, gpre_ref, *refs, emit_x, emit_post, convert, nf, tf, row_split):
    refs = list(refs)
    if convert:
        wg_ref, wu_ref, wd_ref, gpost_ref = refs[:4]
        refs = refs[4:]
    else:
        wgu_ref, wd_ref, gpost_ref = refs[:3]
        refs = refs[3:]
    n_act = int(emit_x) + int(emit_post)
    outs = refs[:n_act]
    scratch = refs[n_act + (2 if convert else 0):]
    xn_ref = scratch[0]
    acc_ref = outs[0] if emit_x else scratch[1]
    f = pl.program_id(1)

    @pl.when(f == 0)
    def _():
        x = x_ref[...]
        xn_ref[...] = _rms(x, gpre_ref[...]).astype(BF16)
        acc_ref[...] = x

    if convert:
        wgu = jnp.concatenate([wg_ref[...].astype(BF16), wu_ref[...].astype(BF16)], axis=1)
        wd = wd_ref[...].astype(BF16)
        refs[n_act][...] = wgu
        refs[n_act + 1][...] = wd
    else:
        wgu, wd = wgu_ref[...], wd_ref[...]
    rows = xn_ref.shape[0] // row_split
    for part in range(row_split):
        rs = slice(part * rows, (part + 1) * rows)
        r = jnp.dot(xn_ref[rs, :], wgu, preferred_element_type=F32)
        g, u = r[:, :tf], r[:, tf:]
        h = (g * jax.nn.sigmoid(g) * (0.5 * u)).astype(BF16)
        acc_ref[rs, :] += jnp.dot(h, wd, preferred_element_type=F32)

    if emit_post:
        @pl.when(f == nf - 1)
        def _():
            outs[-1][...] = _rms(acc_ref[...], gpost_ref[...]).astype(outs[-1].dtype)


def _ffn(x, g_pre, weights, g_post, *, layer=None, emit_x, post_dtype, tm, tf=512):
    m = x.shape[0]
    nf = D_FF // tf
    emit_post = post_dtype is not None
    convert = layer is not None
    once = dict(pipeline_mode=pl.Buffered(1)) if tm > 512 else {}
    row_spec = pl.BlockSpec((tm, D_MODEL), lambda i, f: (i, 0), **once)
    vec_spec = pl.BlockSpec((1, D_MODEL), lambda i, f: (0, 0))
    out_shape, out_specs = [], []
    if emit_x:
        out_shape.append(jax.ShapeDtypeStruct((m, D_MODEL), F32))
        out_specs.append(row_spec)
    if emit_post:
        out_shape.append(jax.ShapeDtypeStruct((m, D_MODEL), post_dtype))
        out_specs.append(row_spec)
    gu_spec = pl.BlockSpec((D_MODEL, 2 * tf), lambda i, f: (0, f))
    down_spec = pl.BlockSpec((tf, D_MODEL), lambda i, f: (f, 0))
    if convert:
        assert m == tm, "the bf16 weights are written once, by a single row tile"
        w_specs = [pl.BlockSpec((None, D_MODEL, tf), lambda i, f: (layer, 0, f)),
                   pl.BlockSpec((None, D_MODEL, tf), lambda i, f: (layer, 0, f)),
                   pl.BlockSpec((None, tf, D_MODEL), lambda i, f: (layer, f, 0))]
        out_shape += [jax.ShapeDtypeStruct((D_MODEL, 2 * D_FF), BF16), jax.ShapeDtypeStruct((D_FF, D_MODEL), BF16)]
        out_specs += [gu_spec, down_spec]
    else:
        w_specs = [gu_spec, down_spec]
    scratch = [pltpu.VMEM((tm, D_MODEL), BF16)]
    if not emit_x:
        scratch.append(pltpu.VMEM((tm, D_MODEL), F32))
    return pl.pallas_call(
        functools.partial(_ffn_body, emit_x=emit_x, emit_post=emit_post, convert=convert, nf=nf, tf=tf,
                          row_split=max(tm // 512, 1)),
        grid=(m // tm, nf),
        in_specs=[row_spec, vec_spec, *w_specs, vec_spec],
        out_specs=out_specs,
        out_shape=out_shape,
        scratch_shapes=scratch,
        compiler_params=_cparams(2, VMEM_BIG),
        name="ffn",
    )(x, g_pre.reshape(1, -1), *weights, g_post.reshape(1, -1))


def _mm_body(a_ref, b_ref, o_ref):
    o_ref[...] = jnp.dot(a_ref[...], b_ref[...], preferred_element_type=F32)


def _mm_res_body(a_ref, b_ref, r_ref, o_ref):
    o_ref[...] = r_ref[...] + jnp.dot(a_ref[...], b_ref[...], preferred_element_type=F32)


def _mm_gate_body(a_ref, b_ref, o_ref):
    o_ref[...] = jax.nn.sigmoid(jnp.dot(a_ref[...], b_ref[...], preferred_element_type=F32)).astype(BF16)


def _matmul(a, b, res=None, *, layer=None, col0=0, n=None, gate=False, tm, tn, name):
    m, k = a.shape
    n = b.shape[-1] if n is None else n
    if layer is None:
        b_spec = pl.BlockSpec((k, tn), lambda i, j: (0, col0 + j))
    else:
        b_spec = pl.BlockSpec((None, k, tn), lambda i, j: (layer, 0, col0 + j))
    in_specs = [pl.BlockSpec((tm, k), lambda i, j: (i, 0)), b_spec]
    args = [a, b]
    body = _mm_gate_body if gate else _mm_body
    if res is not None:
        in_specs.append(pl.BlockSpec((tm, tn), lambda i, j: (i, j)))
        args.append(res)
        body = _mm_res_body
    return pl.pallas_call(
        body,
        grid=(m // tm, n // tn),
        in_specs=in_specs,
        out_specs=pl.BlockSpec((tm, tn), lambda i, j: (i, j)),
        out_shape=jax.ShapeDtypeStruct((m, n), BF16 if gate else F32),
        compiler_params=_cparams(2, VMEM_BIG),
        name=name,
    )(*args)


def _mm_deint_body(a_ref, b_ref, o_ref, s_ref, *, dil):
    r = jnp.dot(a_ref[...], b_ref[...], preferred_element_type=F32)
    for h in range(NH):
        s_ref[h] = r[:, h * HD:(h + 1) * HD]
    n = SEQ // dil
    for h in range(NH):
        for res in range(dil):
            o_ref[h, pl.ds(res * n, n), :] = s_ref[h, pl.ds(res, n, stride=dil), :]


def _w_in_source_block(j, where=jnp.where):
    return where(j < 3, 3 * j, where(j < 26, j + 6, where(j < 29, 3 * (j - 26) + 1, 3 * (j - 29) + 2)))


assert tuple(_w_in_source_block(np.arange(ZBLK), np.where)) == W_IN_ORDER


def _in_proj_convert_body(a_ref, b_ref, z_ref, gate_ref, w_ref):
    j = pl.program_id(0)
    w = b_ref[...].astype(BF16)
    w_ref[...] = w
    r = jnp.dot(a_ref[...], w, preferred_element_type=F32)
    z_ref[...] = r

    @pl.when((j >= N_PLAIN) & (j < N_PLAIN + N_GATE))
    def _():
        gate_ref[...] = jax.nn.sigmoid(r).astype(BF16)


def _in_proj_convert(a, w_in, *, layer):
    m, k = a.shape
    return pl.pallas_call(
        _in_proj_convert_body,
        grid=(ZBLK,),
        in_specs=[pl.BlockSpec((m, k), lambda j: (0, 0)),
                  pl.BlockSpec((None, k, BW), lambda j: (layer, 0, _w_in_source_block(j)))],
        out_specs=[pl.BlockSpec((m, BW), lambda j: (0, j)),
                   pl.BlockSpec((m, BW), lambda j: (0, jnp.clip(j - N_PLAIN, 0, N_GATE - 1))),
                   pl.BlockSpec((k, BW), lambda j: (0, j))],
        out_shape=[jax.ShapeDtypeStruct((m, IN_WIDTH), F32),
                   jax.ShapeDtypeStruct((m, GATE_W), BF16),
                   jax.ShapeDtypeStruct((k, IN_WIDTH), BF16)],
        compiler_params=_cparams(1, VMEM_BIG),
        name="in_proj_convert",
    )(a, w_in)


def _matmul_deint(a, b, *, col0, dil, name):
    m, k = a.shape
    return pl.pallas_call(
        functools.partial(_mm_deint_body, dil=dil),
        grid=(BATCH, 3),
        in_specs=[pl.BlockSpec((SEQ, k), lambda b, j: (b, 0)),
                  pl.BlockSpec((k, BW), lambda b, j: (0, col0 + j))],
        out_specs=pl.BlockSpec((None, NH, SEQ, HD), lambda b, j: (j, 0, b, 0)),
        out_shape=jax.ShapeDtypeStruct((3, NH, m, HD), F32),
        scratch_shapes=[pltpu.VMEM((NH, SEQ, HD), F32)],
        compiler_params=_cparams(2, VMEM_BIG),
        name=name,
    )(a, b)


def _branch_body(oa_ref, ob_ref, oc_ref, od_ref, gate_ref, wb_ref, o_ref):
    acc = None
    for n, br in enumerate((oa_ref, ob_ref, oc_ref, od_ref)):
        proj = jnp.dot(br[...], wb_ref[n], preferred_element_type=F32)
        t = gate_ref[:, n * D_MODEL:(n + 1) * D_MODEL].astype(F32) * proj
        acc = t if acc is None else acc + t
    o_ref[...] = acc.astype(BF16)


def _branch_merge(oa, ob, oc, od, gates, wb, *, layer, tm):
    m = oa.shape[0]
    br_spec = pl.BlockSpec((tm, BW), lambda i: (i, 0))
    return pl.pallas_call(
        _branch_body,
        grid=(m // tm,),
        in_specs=[br_spec] * 4 + [pl.BlockSpec((tm, GATE_W), lambda i: (i, 0)),
                                  pl.BlockSpec((None, 4, BW, D_MODEL), lambda i: (layer, 0, 0, 0))],
        out_specs=pl.BlockSpec((tm, D_MODEL), lambda i: (i, 0)),
        out_shape=jax.ShapeDtypeStruct((m, D_MODEL), BF16),
        compiler_params=_cparams(1, VMEM_BIG),
        name="branch_merge",
    )(oa, ob, oc, od, gates, wb)


def _t5_buckets(dist):
    max_exact = REL_BUCKETS // 2
    d = np.maximum(dist, 1).astype(np.float32)
    large = max_exact + (np.log(d / max_exact) / np.log(REL_MAX_DISTANCE / max_exact)
                         * (REL_BUCKETS - max_exact)).astype(np.int32)
    large = np.minimum(large, REL_BUCKETS - 1)
    return np.where(dist < max_exact, dist, large).astype(np.int32)


def _bucket_of_step(dil):
    return _t5_buckets(dil * np.arange(NKEY))


def _prompt_bucket_matrix(dil):
    iq = np.arange(CH)[:, None]
    col = np.arange(2 * CH)[None, :]
    j = iq + CH - col
    valid = (j >= 0) & (j < NKEY)
    return np.where(valid, _bucket_of_step(dil)[np.clip(j, 0, NKEY - 1)], -1).astype(np.int32)


SAMPLE_CACHE_ROWS = (8 * ATT_WINDOWS[0], 8 * ATT_WINDOWS[1], 8 * DEC_SEQ * CH)


def _sample_bucket_tables():
    h = np.repeat(np.arange(NH), SROWS)[:, None]
    t = np.tile(np.arange(SROWS), NH)[:, None]
    lane = np.arange(CH)[None, :]
    tables = []
    for gi, dil in enumerate(ATT_DILATIONS):
        bos = _bucket_of_step(dil)
        flat = np.arange(SAMPLE_CACHE_ROWS[gi])[None, :]
        head, is_k = flat % NH, (flat // NH) % 2 == 0
        if gi < 2:
            w = flat // 8
            steps = ATT_WINDOWS[gi] + t - w
            valid = (steps % dil == 0) & (steps // dil < NKEY)
            j = steps // dil
        else:
            res, i = (flat // 8) % DEC_SEQ, flat // (8 * DEC_SEQ)
            valid = res == t
            j = np.broadcast_to(CH - i, valid.shape)
        valid = valid & is_k & (head == h) & (t < DEC_SEQ)
        cache_tbl = np.where(valid, bos[np.clip(j, 0, NKEY - 1)], -1)
        jn = t - lane
        validn = (t < DEC_SEQ) & (lane < DEC_SEQ) & (jn >= 0) & ((dil == 1) | (jn == 0))
        new_tbl = np.where(validn, bos[np.clip(jn, 0, NKEY - 1)], -1)
        tables.append((cache_tbl.astype(np.int32), new_tbl.astype(np.int32)))
    return tables


def _bias_from_buckets(bk, rb_ref, col):
    out = jnp.full(bk.shape, NEG, F32)
    for b in range(REL_BUCKETS):
        out = jnp.where(bk == b, rb_ref[b, col], out)
    return out


def _attn_prompt_body(rb_ref, bk_ref, q_ref, kc_ref, vc_ref, *refs, gi, has_prev, headed):
    dil = ATT_DILATIONS[gi]
    if has_prev:
        kp_ref, vp_ref = refs[:2]
        refs = refs[2:]
    o_ref, lse_ref, kvo_ref, bias_ref = refs[-4:]
    first = (pl.program_id(0) == 0) & (pl.program_id(1) == 0) & (pl.program_id(2) == 0)

    @pl.when(first)
    def _():
        bk = bk_ref[...]
        for h in range(NH):
            bias_ref[h] = _bias_from_buckets(bk, rb_ref, gi * NH + h)

    ib = pl.program_id(1)
    res = pl.program_id(2)
    rows = pl.ds(res, CH, stride=dil) if dil > 1 else slice(None)
    head = (lambda ref, h: ref[h]) if headed else (lambda ref, h: ref[:, h * HD:(h + 1) * HD])
    scale = HD ** -0.5
    lane_head = lax.broadcasted_iota(jnp.int32, (CH, NH * LSE_W), 1) // LSE_W
    lse_tile = None
    if has_prev:
        col = lax.broadcasted_iota(jnp.int32, (CH, 2 * CH), 1)
        no_prev = (ib == 0) & (col < CH)
    for h in range(NH):
        sl = slice(h * HD, (h + 1) * HD)
        kc = head(kc_ref, h)
        vc = head(vc_ref, h)
        kvo_ref[:, sl] = kc
        kvo_ref[:, BW + h * HD:BW + (h + 1) * HD] = vc
        qh = head(q_ref, h).astype(BF16)
        if has_prev:
            kh = jnp.concatenate([head(kp_ref, h), kc], axis=0).astype(BF16)
            vh = jnp.concatenate([head(vp_ref, h), vc], axis=0).astype(BF16)
            bias = jnp.where(no_prev, NEG, bias_ref[h])
        else:
            kh = kc.astype(BF16)
            vh = vc.astype(BF16)
            bias = bias_ref[h][:, CH:]
        s = lax.dot_general(qh, kh, (((1,), (1,)), ((), ())), preferred_element_type=F32) * scale + bias
        m = jnp.max(s, axis=-1, keepdims=True)
        p = jnp.exp(s - m)
        den = jnp.sum(p, axis=-1, keepdims=True)
        acc = jnp.dot(p.astype(BF16), vh, preferred_element_type=F32)
        o_ref[h, rows, :] = acc / den
        lse = m + jnp.log(den)
        lse_tile = lse if h == 0 else jnp.where(lane_head == h, lse, lse_tile)
    lse_ref[rows, :] = lse_tile


def _attn_prompt(zq, rel_bias, gi, layer, kv_all):
    dil = ATT_DILATIONS[gi]
    rows = BATCH * SEQ
    nb = SEQ // dil // CH
    has_prev = nb > 1
    headed = gi > 0
    bk = jnp.asarray(_prompt_bucket_matrix(dil))

    def rowblk(b, i, r, prev):
        return (b * dil + r) * nb + (jnp.maximum(i - 1, 0) if prev else i)

    def zspec(c, prev=False):
        if headed:
            return pl.BlockSpec((None, NH, CH, HD), lambda b, i, r: (c, 0, rowblk(b, i, r, prev), 0))
        return pl.BlockSpec((CH, BW), lambda b, i, r: (rowblk(b, i, r, prev), COL_QKV[0][c]))

    in_specs = [pl.BlockSpec(memory_space=pltpu.SMEM),
                pl.BlockSpec((CH, 2 * CH), lambda b, i, r: (0, 0)),
                zspec(0), zspec(1), zspec(2)]
    args = [rel_bias, bk, zq, zq, zq]
    if has_prev:
        in_specs += [zspec(1, True), zspec(2, True)]
        args += [zq, zq]
    aliases = {}
    if kv_all is not None:
        in_specs.append(pl.BlockSpec(memory_space=pl.ANY))
        args.append(kv_all)
        aliases = {len(args) - 1: 2}
    return pl.pallas_call(
        functools.partial(_attn_prompt_body, gi=gi, has_prev=has_prev, headed=headed),
        grid=(BATCH, nb, dil),
        in_specs=in_specs,
        out_specs=[pl.BlockSpec((NH, CH * dil, HD), lambda b, i, r: (0, b * nb + i, 0)),
                   pl.BlockSpec((CH * dil, NH * LSE_W), lambda b, i, r: (b * nb + i, 0)),
                   pl.BlockSpec((None, CH, 2 * BW), lambda b, i, r: (layer, b, jnp.where(i == nb - 1, r, 0)))],
        out_shape=[jax.ShapeDtypeStruct((NH, rows, HD), F32),
                   jax.ShapeDtypeStruct((rows, NH * LSE_W), F32),
                   jax.ShapeDtypeStruct((DEPTH, BATCH * CH, dil * 2 * BW), F32)],
        scratch_shapes=[pltpu.VMEM((NH, CH, 2 * CH), F32)],
        input_output_aliases=aliases,
        compiler_params=_cparams(3),
        name=f"attn_prompt_g{gi}",
    )(*args)


def _combine_body(o0, l0, o1, l1, o2, l2, out_ref):
    for h in range(NH):
        a0, a1, a2 = (l[:, h * LSE_W:h * LSE_W + 1] for l in (l0, l1, l2))
        m = jnp.maximum(jnp.maximum(a0, a1), a2)
        w0, w1, w2 = jnp.exp(a0 - m), jnp.exp(a1 - m), jnp.exp(a2 - m)
        out_ref[:, h * HD:(h + 1) * HD] = ((w0 * o0[h] + w1 * o1[h] + w2 * o2[h]) / (w0 + w1 + w2)).astype(BF16)


def _attn_combine(parts, *, tm=1024):
    rows = parts[1].shape[0]
    ospec = pl.BlockSpec((NH, tm, HD), lambda i: (0, i, 0))
    lspec = pl.BlockSpec((tm, NH * LSE_W), lambda i: (i, 0))
    return pl.pallas_call(
        _combine_body,
        grid=(rows // tm,),
        in_specs=[ospec, lspec] * 3,
        out_specs=pl.BlockSpec((tm, BW), lambda i: (i, 0)),
        out_shape=jax.ShapeDtypeStruct((rows, BW), BF16),
        compiler_params=_cparams(1),
        name="attn_combine",
    )(*parts)


def _attn_sample_body(rb_ref, bc0_ref, bc1_ref, bc2_ref, bn_ref, z_ref, c0_ref, c1_ref, c2_ref,
                      oa_ref, kv0_ref, kv1_ref, kv2_ref, b0_ref, b1_ref, b2_ref, bnew_ref):
    bucket_refs = (bc0_ref, bc1_ref, bc2_ref)
    bias_refs = (b0_ref, b1_ref, b2_ref)

    @pl.when(pl.program_id(0) == 0)
    def _():
        for gi in range(3):
            for h in range(NH):
                rs = slice(h * SROWS, (h + 1) * SROWS)
                bias_refs[gi][rs, :] = _bias_from_buckets(bucket_refs[gi][rs, :], rb_ref, gi * NH + h)
                bnew_ref[gi, rs, :] = _bias_from_buckets(bn_ref[gi, rs, :], rb_ref, gi * NH + h)

    scale = HD ** -0.5
    rows = NH * SROWS
    head_of_row = lax.broadcasted_iota(jnp.int32, (rows, BW), 0) // SROWS
    head_of_lane = lax.broadcasted_iota(jnp.int32, (rows, BW), 1) // HD
    head_mask = head_of_row == head_of_lane
    caches = (c0_ref, c1_ref, c2_ref)
    kv_refs = (kv0_ref, kv1_ref, kv2_ref)
    stats = []
    for gi in range(3):
        cq, ck, cv = COL_QKV[gi]
        q = z_ref[:, cq * BW:(cq + 1) * BW]
        kn = z_ref[:, ck * BW:(ck + 1) * BW]
        vn = z_ref[:, cv * BW:(cv + 1) * BW]
        kv_refs[gi][:, :BW] = kn
        kv_refs[gi][:, BW:] = vn
        qm = jnp.concatenate([q[:, h * HD:(h + 1) * HD] for h in range(NH)], axis=0).astype(BF16)
        kf = caches[gi][...].reshape(SAMPLE_CACHE_ROWS[gi], HD).astype(BF16)
        s = lax.dot_general(qm, kf, (((1,), (1,)), ((), ())), preferred_element_type=F32) * scale + bias_refs[gi][...]
        qrows = jnp.where(head_mask, jnp.concatenate([q] * NH, axis=0), 0.0)
        bias_n = bnew_ref[gi]
        s_new = []
        for tp in range(DEC_SEQ):
            dotp = jnp.sum(qrows * kn[tp:tp + 1, :], axis=-1, keepdims=True)
            s_new.append(dotp * scale + bias_n[:, tp:tp + 1])
        m = jnp.max(s, axis=-1, keepdims=True)
        for sn in s_new:
            m = jnp.maximum(m, sn)
        p = jnp.exp(s - m)
        den = jnp.sum(p, axis=-1, keepdims=True)
        pv = pltpu.roll(p, NH, axis=1).astype(BF16)
        acc = jnp.dot(pv, kf, preferred_element_type=F32)
        acc_n = jnp.zeros((rows, BW), F32)
        for tp, sn in enumerate(s_new):
            pn = jnp.exp(sn - m)
            den = den + pn
            acc_n = acc_n + pn * vn[tp:tp + 1, :]
        acc = acc + jnp.concatenate([acc_n[h * SROWS:(h + 1) * SROWS, h * HD:(h + 1) * HD] for h in range(NH)], axis=0)
        stats.append((m, den, acc))
    mm = jnp.maximum(jnp.maximum(stats[0][0], stats[1][0]), stats[2][0])
    den = jnp.zeros((rows, 1), F32)
    acc = jnp.zeros((rows, HD), F32)
    for m, d, a in stats:
        w = jnp.exp(m - mm)
        den = den + w * d
        acc = acc + w * a
    o = acc / den
    for h in range(NH):
        oa_ref[:, h * HD:(h + 1) * HD] = o[h * SROWS:(h + 1) * SROWS, :].astype(BF16)


def _attn_sample(z, caches, rel_bias, layer):
    tables = _sample_bucket_tables()
    rows = DEC_BATCH * SROWS
    qrows = NH * SROWS
    n0, n1, n2 = SAMPLE_CACHE_ROWS
    c0 = caches[0].reshape(DEPTH, DEC_BATCH, n0, HD)
    c1 = caches[1].reshape(DEPTH, DEC_BATCH, n1, HD)
    c2 = caches[2].reshape(DEPTH, DEC_BATCH, CH, 16 * 8, HD)
    new_tbl = jnp.asarray(np.stack([t[1] for t in tables]))
    kv_spec = pl.BlockSpec((SROWS, 2 * BW), lambda b: (b, 0))
    kv_shape = jax.ShapeDtypeStruct((rows, 2 * BW), F32)
    const2 = lambda b: (0, 0)
    return pl.pallas_call(
        _attn_sample_body,
        grid=(DEC_BATCH,),
        in_specs=[pl.BlockSpec(memory_space=pltpu.SMEM),
                  pl.BlockSpec((qrows, n0), const2), pl.BlockSpec((qrows, n1), const2),
                  pl.BlockSpec((qrows, n2), const2),
                  pl.BlockSpec((3, qrows, CH), lambda b: (0, 0, 0)),
                  pl.BlockSpec((SROWS, IN_WIDTH), lambda b: (b, 0)),
                  pl.BlockSpec((None, None, n0, HD), lambda b: (layer, b, 0, 0)),
                  pl.BlockSpec((None, None, n1, HD), lambda b: (layer, b, 0, 0)),
                  pl.BlockSpec((None, None, CH, 8 * DEC_SEQ, HD), lambda b: (layer, b, 0, 0, 0))],
        out_specs=[pl.BlockSpec((SROWS, BW), lambda b: (b, 0)), kv_spec, kv_spec, kv_spec],
        out_shape=[jax.ShapeDtypeStruct((rows, BW), BF16), kv_shape, kv_shape, kv_shape],
        scratch_shapes=[pltpu.VMEM((qrows, n0), F32), pltpu.VMEM((qrows, n1), F32), pltpu.VMEM((qrows, n2), F32),
                        pltpu.VMEM((3, qrows, CH), F32)],
        compiler_params=_cparams(1),
        name="attn_sample",
    )(rel_bias, jnp.asarray(tables[0][0]), jnp.asarray(tables[1][0]), jnp.asarray(tables[2][0]), new_tbl,
      z, c0, c1, c2)


def _gmlp_body(bu_ref, bv_ref, g_ref, ws_ref, bs_ref, *refs, rows, emit_vn):
    refs = list(refs)
    o_ref = refs.pop(0)
    vn_ref = refs.pop(0) if emit_vn else None
    tril = lax.broadcasted_iota(jnp.int32, (CH, CH), 0) >= lax.broadcasted_iota(jnp.int32, (CH, CH), 1)
    w = [jnp.where(tril, ws_ref[g], 0.0).astype(BF16) for g in range(NH)]
    if rows < CH:
        pad_u, pad_v = refs
        pad_u[...] = jnp.zeros_like(pad_u)
        pad_v[...] = jnp.zeros_like(pad_v)
        pad_u[0:rows, :] = bu_ref[...]
        pad_v[0:rows, :] = bv_ref[...]
        bu_ref, bv_ref = pad_u, pad_v
    for c in range(max(rows // CH, 1)):
        rs = slice(c * CH, (c + 1) * CH)
        u = jax.nn.gelu(bu_ref[rs, :])
        vn = _rms(jax.nn.gelu(bv_ref[rs, :]), g_ref[...])
        if emit_vn:
            vn_ref[...] = vn[:rows]
        for g in range(NH):
            sl = slice(g * HD, (g + 1) * HD)
            mixed = jnp.dot(w[g], vn[:, sl].astype(BF16), preferred_element_type=F32) + bs_ref[:, sl]
            res = (u[:, sl] * mixed).astype(BF16)
            if rows < CH:
                o_ref[:, sl] = res[:rows]
            else:
                o_ref[rs, sl] = res


def _gmlp(z, g_gmlp, w_spatial, b_spatial, *, rows, emit_vn):
    m = z.shape[0]
    bs_full = jnp.repeat(b_spatial.T, HD, axis=1)
    spec = pl.BlockSpec((rows, BW), lambda i: (i, 0))
    out_shape = [jax.ShapeDtypeStruct((m, BW), BF16)]
    out_specs = [spec]
    if emit_vn:
        out_shape.append(jax.ShapeDtypeStruct((m, BW), F32))
        out_specs.append(spec)
    scratch = [] if rows >= CH else [pltpu.VMEM((CH, BW), F32)] * 2
    return pl.pallas_call(
        functools.partial(_gmlp_body, rows=rows, emit_vn=emit_vn),
        grid=(m // rows,),
        in_specs=[pl.BlockSpec((rows, BW), lambda i: (i, COL_BU)),
                  pl.BlockSpec((rows, BW), lambda i: (i, COL_BV)),
                  pl.BlockSpec((1, BW), lambda i: (0, 0)),
                  pl.BlockSpec((NH, CH, CH), lambda i: (0, 0, 0)),
                  pl.BlockSpec((CH, BW), lambda i: (0, 0))],
        out_specs=out_specs,
        out_shape=out_shape,
        scratch_shapes=scratch,
        compiler_params=_cparams(1),
        name="gmlp",
    )(z, z, g_gmlp.reshape(1, -1), w_spatial, bs_full)


def _pool_body(x_ref, prev_ref, wp_ref, sc_ref, o_ref, st_ref, ext_ref, *, rows, n_new, start, zero_first_prev):
    ib = pl.program_id(1)
    prev = prev_ref[...]
    if zero_first_prev:
        prev = jnp.where(ib == 0, 0.0, prev)
    x = x_ref[...]
    ext_ref[0:16, :] = prev
    ext_ref[16:16 + rows, :] = x
    st_ref[...] = ext_ref[pl.ds(n_new + 1, POOL_STATE), :]
    ext = ext_ref[...]
    pos = start + ib * rows + lax.broadcasted_iota(jnp.int32, (rows, 1), 0)
    for gi, win in enumerate(POOL_WINDOWS):
        sl = slice(gi * HD, (gi + 1) * HD)
        s = ext[:, sl]
        k = 1
        while k < win:
            s = s + pltpu.roll(s, k, axis=0)
            k *= 2
        cnt = jnp.minimum(pos + 1, win).astype(F32)
        diff = s[16:] / cnt - x[:, sl]
        y = jnp.dot(diff.astype(BF16), wp_ref[gi].astype(BF16), preferred_element_type=F32)
        o_ref[:, sl] = (y * sc_ref[:, sl]).astype(BF16)


def _pool(z, prev, w_pool, pool_scale, *, nseq, rows, n_new, start, layer=None):
    m = z.shape[0]
    nblk = m // nseq // rows
    if prev is None:
        per16 = rows // 16
        prev_arr = z
        prev_spec = pl.BlockSpec((16, BW), lambda b, i: (jnp.maximum((b * nblk + i) * per16 - 1, 0), COL_CIN))
    else:
        prev_arr = prev
        prev_spec = pl.BlockSpec((None, None, 16, BW), lambda b, i: (layer, b, 0, 0))
    return pl.pallas_call(
        functools.partial(_pool_body, rows=rows, n_new=n_new, start=start, zero_first_prev=prev is None),
        grid=(nseq, nblk),
        in_specs=[pl.BlockSpec((rows, BW), lambda b, i: (b * nblk + i, COL_CIN)),
                  prev_spec,
                  pl.BlockSpec((NH, HD, HD), lambda b, i: (0, 0, 0)),
                  pl.BlockSpec((1, BW), lambda b, i: (0, 0))],
        out_specs=[pl.BlockSpec((rows, BW), lambda b, i: (b * nblk + i, 0)),
                   pl.BlockSpec((None, POOL_STATE, BW), lambda b, i: (b, 0, 0))],
        out_shape=[jax.ShapeDtypeStruct((m, BW), BF16),
                   jax.ShapeDtypeStruct((nseq, POOL_STATE, BW), F32)],
        scratch_shapes=[pltpu.VMEM((16 + rows, BW), F32)],
        compiler_params=_cparams(2),
        name="pool",
    )(z, prev_arr, w_pool, pool_scale.reshape(1, -1))


def _ret_tables(c_eff, positions):
    lg = np.log1p(-np.power(2.0, -5.0 - np.arange(NH, dtype=np.float64)))
    i = np.arange(CH, dtype=np.float64)
    live = (i < c_eff)
    diff = i[:, None] - i[None, :]
    inner = np.where((diff >= 0) & live[:, None] & live[None, :], np.exp(np.maximum(diff, 0.0)[None] * lg[:, None, None]), 0.0)
    qd = np.where(live[None, :], np.exp((i + 1.0)[None, :] * lg[:, None]), 0.0)
    kd = np.where(live[None, :], np.exp((c_eff - 1.0 - i)[None, :] * lg[:, None]), 0.0)
    chunk = tuple(float(v) for v in np.exp(c_eff * lg))
    qd_full = np.repeat(qd.T, HD, axis=1)
    kd_full = np.repeat(kd.T, HD, axis=1) * (HD ** -0.5)
    half = HD // 2
    inv = ROPE_BASE ** (-np.arange(half, dtype=np.float64) / half)
    ang = np.asarray(positions, np.float64)[:, None] * inv[None, :]
    cosf = np.concatenate([np.cos(ang), np.cos(ang)], axis=1)
    sinf = np.concatenate([-np.sin(ang), np.sin(ang)], axis=1)
    to32 = lambda a: jnp.asarray(a.astype(np.float32))
    return to32(inner), to32(qd_full), to32(kd_full), chunk, to32(cosf), to32(sinf)


def _ret_body(q_ref, k_ref, v_ref, g_ref, cos_ref, sin_ref, inner_ref, qd_ref, kd_ref, gr_ref, *refs,
              nseq, rows, chunk_decay, has_state):
    refs = list(refs)
    s0_ref = refs.pop(0) if has_state else None
    o_ref, sn_ref, s_ref = refs[:3]
    pad_ref = refs[3] if rows != CH else None
    ic = pl.program_id(1)

    @pl.when(ic == 0)
    def _():
        if has_state:
            s_ref[...] = s0_ref[...]
        else:
            s_ref[...] = jnp.zeros_like(s_ref)

    def chunk(ref, b, k):
        if rows == CH:
            return ref[b]
        pad_ref[k] = jnp.zeros((CH, BW), F32)
        pad_ref[k, 0:rows, :] = ref[b]
        return pad_ref[k]

    cosf = cos_ref[...]
    sinf = sin_ref[...]
    for b in range(nseq):
        q = chunk(q_ref, b, 0)
        k = chunk(k_ref, b, 1)
        v = chunk(v_ref, b, 2)
        gate = chunk(g_ref, b, 3)
        for h in range(NH):
            sl = slice(h * HD, (h + 1) * HD)
            qh = q[:, sl]
            kh = k[:, sl]
            rq = qh * cosf + pltpu.roll(qh, HD // 2, axis=1) * sinf
            rk = kh * cosf + pltpu.roll(kh, HD // 2, axis=1) * sinf
            vb = v[:, sl].astype(BF16)
            rqb = rq.astype(BF16)
            state = s_ref[b, h]
            att = lax.dot_general(rqb, (rk * (HD ** -0.5)).astype(BF16), (((1,), (1,)), ((), ())),
                                  preferred_element_type=F32) * inner_ref[h]
            o = (jnp.dot(att.astype(BF16), vb, preferred_element_type=F32)
                 + jnp.dot(rqb, state.astype(BF16), preferred_element_type=F32) * qd_ref[:, sl])
            kdec = (rk * kd_ref[:, sl]).T.astype(BF16)
            new_state = state * chunk_decay[h] + jnp.dot(kdec, vb, preferred_element_type=F32)
            s_ref[b, h] = new_state
            sn_ref[b, h] = new_state
            o = o * lax.rsqrt(jnp.mean(o * o, axis=-1, keepdims=True) + EPS)
            gt = gate[:, sl]
            o_ref[b, :, sl] = (o * gr_ref[:, sl] * (gt * jax.nn.sigmoid(gt)))[:rows].astype(BF16)


def _retention(z, g_ret, state, *, nseq, per_step, rows, c_eff, positions, layer=None):
    m, zw = z.shape
    seq_rows = m // nseq
    nchunk = seq_rows // rows
    inner, qd, kd, chunk_decay, cosf, sinf = _ret_tables(c_eff, positions)
    has_state = state is not None
    z3 = z.reshape(nseq, seq_rows, zw)

    def zspec(col):
        return pl.BlockSpec((per_step, rows, BW), lambda g, i: (g, i, col))

    const2 = lambda g, i: (0, 0)
    in_specs = [zspec(COL_DQ), zspec(COL_DK), zspec(COL_DV), zspec(COL_DG),
                pl.BlockSpec((CH, HD), lambda g, i: (i, 0)), pl.BlockSpec((CH, HD), lambda g, i: (i, 0)),
                pl.BlockSpec((NH, CH, CH), lambda g, i: (0, 0, 0)),
                pl.BlockSpec((CH, BW), const2), pl.BlockSpec((CH, BW), const2), pl.BlockSpec((1, BW), const2)]
    args = [z3, z3, z3, z3, cosf, sinf, inner, qd, kd, g_ret.reshape(1, -1)]
    if has_state:
        in_specs.append(pl.BlockSpec((None, per_step, NH, HD, HD), lambda g, i: (layer, g, 0, 0, 0)))
        args.append(state)
    scratch = [pltpu.VMEM((per_step, NH, HD, HD), F32)]
    if rows != CH:
        scratch.append(pltpu.VMEM((4, CH, BW), F32))
    o, sn = pl.pallas_call(
        functools.partial(_ret_body, nseq=per_step, rows=rows, chunk_decay=chunk_decay, has_state=has_state),
        grid=(nseq // per_step, nchunk),
        in_specs=in_specs,
        out_specs=[pl.BlockSpec((per_step, rows, BW), lambda g, i: (g, i, 0)),
                   pl.BlockSpec((per_step, NH, HD, HD), lambda g, i: (g, 0, 0, 0))],
        out_shape=[jax.ShapeDtypeStruct((nseq, seq_rows, BW), BF16),
                   jax.ShapeDtypeStruct((nseq, NH, HD, HD), F32)],
        scratch_shapes=scratch,
        compiler_params=_cparams(2),
        name="retention",
    )(*args)
    return o.reshape(m, BW), sn


def kernel(x_prompt, x_sample, cache_attn_kv_w128, cache_attn_kv_w512, cache_attn_kv_w2048, state_pool, state_ret, rel_bias, g_ffn1, w_ffn1_gate, w_ffn1_up, w_ffn1_down, g_mix, w_in, g_gmlp, w_spatial, b_spatial, w_pool, pool_scale, g_ret, w_branch, w_out, g_ffn2, w_ffn2_gate, w_ffn2_up, w_ffn2_down, g_final):
    caches = (cache_attn_kv_w128, cache_attn_kv_w512, cache_attn_kv_w2048)
    xp = x_prompt.reshape(BATCH * SEQ, D_MODEL)
    xs = jnp.pad(x_sample, ((0, 0), (0, SROWS - DEC_SEQ), (0, 0))).reshape(DEC_BATCH * SROWS, D_MODEL)
    pool_state = jnp.pad(state_pool, ((0, 0), (0, 0), (1, 0), (0, 0)))
    tm_p, tm_s = 1024, DEC_BATCH * SROWS
    sample_pos = PAST_LEN + np.arange(CH)
    wb = w_branch.astype(BF16)
    wo = w_out.astype(BF16)
    gate_col0 = N_PLAIN * BW // 1024
    qkv1, qkv2 = COL_QKV[1][0], COL_QKV[2][0]

    kv_p = [None, None, None]
    kv_s = [[], [], []]
    pool_p, pool_s, ret_p, ret_s, gv_s = [], [], [], [], []
    yp = ys = None
    for l in range(DEPTH):
        last = l == DEPTH - 1

        xs, hs, *w1 = _ffn(xs, g_ffn1[l], (w_ffn1_gate, w_ffn1_up, w_ffn1_down), g_mix[l], layer=l,
                           emit_x=True, post_dtype=BF16, tm=tm_s)
        zs, gs, win = _in_proj_convert(hs, w_in, layer=l)
        oa, k0, k1, k2 = _attn_sample(zs, caches, rel_bias, l)
        for gi, kv in enumerate((k0, k1, k2)):
            kv_s[gi].append(kv.reshape(DEC_BATCH, SROWS, 2, NH, HD)[:, :DEC_SEQ])
        ob, vn = _gmlp(zs, g_gmlp[l], w_spatial[l], b_spatial[l], rows=SROWS, emit_vn=True)
        gv_s.append(vn.reshape(DEC_BATCH, SROWS, BW)[:, :DEC_SEQ])
        oc, pn = _pool(zs, pool_state, w_pool[l], pool_scale[l], nseq=DEC_BATCH, rows=SROWS, n_new=DEC_SEQ,
                      start=PAST_LEN, layer=l)
        od, rn = _retention(zs, g_ret[l], state_ret, nseq=DEC_BATCH, per_step=1, rows=SROWS, c_eff=DEC_SEQ,
                            positions=sample_pos, layer=l)
        pool_s.append(pn)
        ret_s.append(rn)
        mix = _branch_merge(oa, ob, oc, od, gs, wb, layer=l, tm=tm_s)
        xs = _matmul(mix, wo, xs, layer=l, tm=tm_s, tn=512, name="out_proj")
        if last:
            ys, *w2 = _ffn(xs, g_ffn2[l], (w_ffn2_gate, w_ffn2_up, w_ffn2_down), g_final, layer=l,
                           emit_x=False, post_dtype=F32, tm=tm_s)
        else:
            xs, *w2 = _ffn(xs, g_ffn2[l], (w_ffn2_gate, w_ffn2_up, w_ffn2_down), g_final, layer=l,
                           emit_x=True, post_dtype=None, tm=tm_s)

        xp, hp = _ffn(xp, g_ffn1[l], w1, g_mix[l], emit_x=True, post_dtype=BF16, tm=tm_p)
        zp = _matmul(hp, win, n=N_PLAIN * BW, tm=1024, tn=1024, name="in_proj")
        gp = _matmul(hp, win, col0=gate_col0, n=GATE_W, gate=True, tm=1024, tn=1024, name="in_proj_gate")
        zq = (zp,
              _matmul_deint(hp, win, col0=qkv1, dil=ATT_DILATIONS[1], name="in_proj_g1"),
              _matmul_deint(hp, win, col0=qkv2, dil=ATT_DILATIONS[2], name="in_proj_g2"))
        parts = []
        for gi in range(3):
            o, lse, kv_p[gi] = _attn_prompt(zq[gi], rel_bias, gi, l, kv_p[gi])
            parts += [o, lse]
        oa = _attn_combine(parts)
        ob, = _gmlp(zp, g_gmlp[l], w_spatial[l], b_spatial[l], rows=4 * CH, emit_vn=False)
        oc, pn = _pool(zp, None, w_pool[l], pool_scale[l], nseq=BATCH, rows=512, n_new=512, start=0)
        od, rn = _retention(zp, g_ret[l], None, nseq=BATCH, per_step=BATCH, rows=CH, c_eff=CH,
                            positions=np.arange(SEQ))
        pool_p.append(pn)
        ret_p.append(rn)
        mix = _branch_merge(oa, ob, oc, od, gp, wb, layer=l, tm=256)
        xp = _matmul(mix, wo, xp, layer=l, tm=1024, tn=1024, name="out_proj")
        if last:
            yp, = _ffn(xp, g_ffn2[l], w2, g_final, emit_x=False, post_dtype=F32, tm=tm_p)
        else:
            xp, = _ffn(xp, g_ffn2[l], w2, g_final, emit_x=True, post_dtype=None, tm=tm_p)

    y_prompt = yp.reshape(BATCH, SEQ, D_MODEL)
    y_sample = ys.reshape(DEC_BATCH, SROWS, D_MODEL)[:, :DEC_SEQ]
    kv_p = [kv.reshape(DEPTH, BATCH, ATT_WINDOWS[gi], 2, NH, HD) for gi, kv in enumerate(kv_p)]
    return (y_prompt, y_sample,
            kv_p[0], kv_p[1], kv_p[2],
            jnp.stack(kv_s[0]), jnp.stack(kv_s[1]), jnp.stack(kv_s[2]),
            jnp.stack(pool_p), jnp.stack(pool_s),
            jnp.stack(ret_p), jnp.stack(ret_s),
            jnp.stack(gv_s))
```

```python
import functools

import numpy as np
import jax
import jax.numpy as jnp
from jax import lax
from jax.experimental import pallas as pl
from jax.experimental.pallas import tpu as pltpu

F32 = jnp.float32
BF16 = jnp.bfloat16

D_MODEL = 2048
BATCH = 4
SEQ = 2048
DEPTH = 2
DEC_BATCH = 32
DEC_SEQ = 4
PAST_LEN = 8192
D_FF = 5632
IN_WIDTH = 16384
EPS = 1e-6
BW = 512
HD = 128
NH = 4
ATT_WINDOWS = (128, 512, 2048)
ATT_DILATIONS = (1, 4, 16)
NKEY = 129
REL_BUCKETS = 32
REL_MAX_DISTANCE = 2048
POOL_WINDOWS = (2, 4, 8, 16)
POOL_STATE = 15
ROPE_BASE = 10000.0
SROWS = 8
CH = 128
NEG = -1e30
LSE_W = 32

W_IN_ORDER = (0, 3, 6) + tuple(range(9, 32)) + (1, 4, 7) + (2, 5, 8)
COL_QKV = ((0, 1, 2), (26, 27, 28), (29, 30, 31))
COL_BU, COL_BV, COL_CIN = 3, 4, 5
COL_DQ, COL_DK, COL_DV, COL_DG = 6, 7, 8, 9
N_PLAIN = 10
N_GATE = 16
GATE_W = N_GATE * BW
ZBLK = IN_WIDTH // BW

VMEM_BIG = 56 * 1024 * 1024


def _cparams(n_axes, vmem=None):
    return pltpu.CompilerParams(dimension_semantics=("arbitrary",) * n_axes, vmem_limit_bytes=vmem)


def _rms(x, g):
    return x * lax.rsqrt(jnp.mean(x * x, axis=-1, keepdims=True) + EPS) * g


def _ffn_body(x_ref, gpre_ref, *refs, emit_x, emit_post, convert, nf, tf, row_split):
    refs = list(refs)
    if convert:
        wg_ref, wu_ref, wd_ref, gpost_ref = refs[:4]
        refs = refs[4:]
    else:
        wgu_ref, wd_ref, gpost_ref = refs[:3]
        refs = refs[3:]
    n_act = int(emit_x) + int(emit_post)
    outs = refs[:n_act]
    scratch = refs[n_act + (2 if convert else 0):]
    xn_ref = scratch[0]
    acc_ref = outs[0]
    f = pl.program_id(1)

    @pl.when(f == 0)
    def _():
        x = x_ref[...]
        xn_ref[...] = _rms(x, gpre_ref[...]).astype(BF16)
        acc_ref[...] = x

    if convert:
        wgu = jnp.concatenate([wg_ref[...].astype(BF16), wu_ref[...].astype(BF16)], axis=1)
        wd = wd_ref[...].astype(BF16)
        refs[n_act][...] = wgu
        refs[n_act + 1][...] = wd
    else:
        wgu, wd = wgu_ref[...], wd_ref[...]
    rows = xn_ref.shape[0] // row_split
    for part in range(row_split):
        rs = slice(part * rows, (part + 1) * rows)
        r = jnp.dot(xn_ref[rs, :], wgu, preferred_element_type=F32)
        g, u = r[:, :tf], r[:, tf:]
        h = (g * jax.nn.sigmoid(g) * (0.5 * u)).astype(BF16)
        acc_ref[rs, :] += jnp.dot(h, wd, preferred_element_type=F32)

    if emit_post:
        @pl.when(f == nf - 1)
        def _():
            outs[-1][...] = _rms(acc_ref[...], gpost_ref[...]).astype(outs[-1].dtype)


def _ffn(x, g_pre, weights, g_post, *, layer=None, emit_x, post_dtype, tm, tf=512):
    m = x.shape[0]
    nf = D_FF // tf
    emit_post = post_dtype is not None
    convert = layer is not None
    assert emit_x or post_dtype == F32, "the first output block is the f32 accumulator"
    once = dict(pipeline_mode=pl.Buffered(1)) if tm > 512 else {}
    row_spec = pl.BlockSpec((tm, D_MODEL), lambda i, f: (i, 0))
    row_out_spec = pl.BlockSpec((tm, D_MODEL), lambda i, f: (i, 0), **once)
    vec_spec = pl.BlockSpec((1, D_MODEL), lambda i, f: (0, 0))
    out_shape, out_specs = [], []
    if emit_x:
        out_shape.append(jax.ShapeDtypeStruct((m, D_MODEL), F32))
        out_specs.append(row_out_spec)
    if emit_post:
        out_shape.append(jax.ShapeDtypeStruct((m, D_MODEL), post_dtype))
        out_specs.append(row_out_spec)
    gu_spec = pl.BlockSpec((D_MODEL, 2 * tf), lambda i, f: (0, f))
    down_spec = pl.BlockSpec((tf, D_MODEL), lambda i, f: (f, 0))
    if convert:
        assert m == tm, "the bf16 weights are written once, by a single row tile"
        w_specs = [pl.BlockSpec((None, D_MODEL, tf), lambda i, f: (layer, 0, f)),
                   pl.BlockSpec((None, D_MODEL, tf), lambda i, f: (layer, 0, f)),
                   pl.BlockSpec((None, tf, D_MODEL), lambda i, f: (layer, f, 0))]
        out_shape += [jax.ShapeDtypeStruct((D_MODEL, 2 * D_FF), BF16), jax.ShapeDtypeStruct((D_FF, D_MODEL), BF16)]
        out_specs += [gu_spec, down_spec]
    else:
        w_specs = [gu_spec, down_spec]
    scratch = [pltpu.VMEM((tm, D_MODEL), BF16)]
    return pl.pallas_call(
        functools.partial(_ffn_body, emit_x=emit_x, emit_post=emit_post, convert=convert, nf=nf, tf=tf,
                          row_split=max(tm // 512, 1)),
        grid=(m // tm, nf),
        in_specs=[row_spec, vec_spec, *w_specs, vec_spec],
        out_specs=out_specs,
        out_shape=out_shape,
        scratch_shapes=scratch,
        compiler_params=_cparams(2, VMEM_BIG),
        name="ffn",
    )(x, g_pre.reshape(1, -1), *weights, g_post.reshape(1, -1))


def _mm_body(a_ref, b_ref, o_ref):
    o_ref[...] = jnp.dot(a_ref[...], b_ref[...], preferred_element_type=F32)


def _mm_res_body(a_ref, b_ref, r_ref, o_ref):
    o_ref[...] = r_ref[...] + jnp.dot(a_ref[...], b_ref[...], preferred_element_type=F32)


def _mm_gate_body(a_ref, b_ref, o_ref):
    o_ref[...] = jax.nn.sigmoid(jnp.dot(a_ref[...], b_ref[...], preferred_element_type=F32)).astype(BF16)


def _matmul(a, b, res=None, *, layer=None, col0=0, n=None, gate=False, tm, tn, name):
    m, k = a.shape
    n = b.shape[-1] if n is None else n
    if layer is None:
        b_spec = pl.BlockSpec((k, tn), lambda i, j: (0, col0 + j))
    else:
        b_spec = pl.BlockSpec((None, k, tn), lambda i, j: (layer, 0, col0 + j))
    in_specs = [pl.BlockSpec((tm, k), lambda i, j: (i, 0)), b_spec]
    args = [a, b]
    body = _mm_gate_body if gate else _mm_body
    if res is not None:
        in_specs.append(pl.BlockSpec((tm, tn), lambda i, j: (i, j)))
        args.append(res)
        body = _mm_res_body
    return pl.pallas_call(
        body,
        grid=(m // tm, n // tn),
        in_specs=in_specs,
        out_specs=pl.BlockSpec((tm, tn), lambda i, j: (i, j)),
        out_shape=jax.ShapeDtypeStruct((m, n), BF16 if gate else F32),
        compiler_params=_cparams(2, VMEM_BIG),
        name=name,
    )(*args)


def _mm_deint_body(a_ref, b_ref, o_ref, s_ref, *, dil):
    r = jnp.dot(a_ref[...], b_ref[...], preferred_element_type=F32)
    for h in range(NH):
        s_ref[h] = r[:, h * HD:(h + 1) * HD]
    n = SEQ // dil
    for h in range(NH):
        for res in range(dil):
            o_ref[h, pl.ds(res * n, n), :] = s_ref[h, pl.ds(res, n, stride=dil), :]


def _w_in_source_block(j, where=jnp.where):
    return where(j < 3, 3 * j, where(j < 26, j + 6, where(j < 29, 3 * (j - 26) + 1, 3 * (j - 29) + 2)))


assert tuple(_w_in_source_block(np.arange(ZBLK), np.where)) == W_IN_ORDER


def _in_proj_convert_body(a_ref, b_ref, z_ref, gate_ref, w_ref):
    j = pl.program_id(0)
    w = b_ref[...].astype(BF16)
    w_ref[...] = w
    r = jnp.dot(a_ref[...], w, preferred_element_type=F32)
    z_ref[...] = r

    @pl.when((j >= N_PLAIN) & (j < N_PLAIN + N_GATE))
    def _():
        gate_ref[...] = jax.nn.sigmoid(r).astype(BF16)


def _in_proj_convert(a, w_in, *, layer):
    m, k = a.shape
    return pl.pallas_call(
        _in_proj_convert_body,
        grid=(ZBLK,),
        in_specs=[pl.BlockSpec((m, k), lambda j: (0, 0)),
                  pl.BlockSpec((None, k, BW), lambda j: (layer, 0, _w_in_source_block(j)))],
        out_specs=[pl.BlockSpec((m, BW), lambda j: (0, j)),
                   pl.BlockSpec((m, BW), lambda j: (0, jnp.clip(j - N_PLAIN, 0, N_GATE - 1))),
                   pl.BlockSpec((k, BW), lambda j: (0, j))],
        out_shape=[jax.ShapeDtypeStruct((m, IN_WIDTH), F32),
                   jax.ShapeDtypeStruct((m, GATE_W), BF16),
                   jax.ShapeDtypeStruct((k, IN_WIDTH), BF16)],
        compiler_params=_cparams(1, VMEM_BIG),
        name="in_proj_convert",
    )(a, w_in)


def _matmul_deint(a, b, *, col0, dil, name):
    m, k = a.shape
    return pl.pallas_call(
        functools.partial(_mm_deint_body, dil=dil),
        grid=(BATCH, 3),
        in_specs=[pl.BlockSpec((SEQ, k), lambda b, j: (b, 0)),
                  pl.BlockSpec((k, BW), lambda b, j: (0, col0 + j))],
        out_specs=pl.BlockSpec((None, NH, SEQ, HD), lambda b, j: (j, 0, b, 0)),
        out_shape=jax.ShapeDtypeStruct((3, NH, m, HD), F32),
        scratch_shapes=[pltpu.VMEM((NH, SEQ, HD), F32)],
        compiler_params=_cparams(2, VMEM_BIG),
        name=name,
    )(a, b)


def _branch_body(oa_ref, ob_ref, oc_ref, od_ref, gate_ref, wb_ref, o_ref):
    acc = None
    for n, br in enumerate((oa_ref, ob_ref, oc_ref, od_ref)):
        proj = jnp.dot(br[...], wb_ref[n], preferred_element_type=F32)
        t = gate_ref[:, n * D_MODEL:(n + 1) * D_MODEL].astype(F32) * proj
        acc = t if acc is None else acc + t
    o_ref[...] = acc.astype(BF16)


def _branch_merge(oa, ob, oc, od, gates, wb, *, layer, tm):
    m = oa.shape[0]
    br_spec = pl.BlockSpec((tm, BW), lambda i: (i, 0))
    return pl.pallas_call(
        _branch_body,
        grid=(m // tm,),
        in_specs=[br_spec] * 4 + [pl.BlockSpec((tm, GATE_W), lambda i: (i, 0)),
                                  pl.BlockSpec((None, 4, BW, D_MODEL), lambda i: (layer, 0, 0, 0))],
        out_specs=pl.BlockSpec((tm, D_MODEL), lambda i: (i, 0)),
        out_shape=jax.ShapeDtypeStruct((m, D_MODEL), BF16),
        compiler_params=_cparams(1, VMEM_BIG),
        name="branch_merge",
    )(oa, ob, oc, od, gates, wb)


def _t5_buckets(dist):
    max_exact = REL_BUCKETS // 2
    d = np.maximum(dist, 1).astype(np.float32)
    large = max_exact + (np.log(d / max_exact) / np.log(REL_MAX_DISTANCE / max_exact)
                         * (REL_BUCKETS - max_exact)).astype(np.int32)
    large = np.minimum(large, REL_BUCKETS - 1)
    return np.where(dist < max_exact, dist, large).astype(np.int32)


def _bucket_of_step(dil):
    return _t5_buckets(dil * np.arange(NKEY))


def _prompt_bucket_matrix(dil):
    iq = np.arange(CH)[:, None]
    col = np.arange(2 * CH)[None, :]
    j = iq + CH - col
    valid = (j >= 0) & (j < NKEY)
    return np.where(valid, _bucket_of_step(dil)[np.clip(j, 0, NKEY - 1)], -1).astype(np.int32)


SAMPLE_CACHE_ROWS = (8 * ATT_WINDOWS[0], 8 * ATT_WINDOWS[1], 8 * DEC_SEQ * CH)


def _sample_bucket_tables():
    h = np.repeat(np.arange(NH), SROWS)[:, None]
    t = np.tile(np.arange(SROWS), NH)[:, None]
    lane = np.arange(CH)[None, :]
    tables = []
    for gi, dil in enumerate(ATT_DILATIONS):
        bos = _bucket_of_step(dil)
        flat = np.arange(SAMPLE_CACHE_ROWS[gi])[None, :]
        head, is_k = flat % NH, (flat // NH) % 2 == 0
        if gi < 2:
            w = flat // 8
            steps = ATT_WINDOWS[gi] + t - w
            valid = (steps % dil == 0) & (steps // dil < NKEY)
            j = steps // dil
        else:
            res, i = (flat // 8) % DEC_SEQ, flat // (8 * DEC_SEQ)
            valid = res == t
            j = np.broadcast_to(CH - i, valid.shape)
        valid = valid & is_k & (head == h) & (t < DEC_SEQ)
        cache_tbl = np.where(valid, bos[np.clip(j, 0, NKEY - 1)], -1)
        jn = t - lane
        validn = (t < DEC_SEQ) & (lane < DEC_SEQ) & (jn >= 0) & ((dil == 1) | (jn == 0))
        new_tbl = np.where(validn, bos[np.clip(jn, 0, NKEY - 1)], -1)
        tables.append((cache_tbl.astype(np.int32), new_tbl.astype(np.int32)))
    return tables


def _bias_from_buckets(bk, rb_ref, col):
    out = jnp.full(bk.shape, NEG, F32)
    for b in range(REL_BUCKETS):
        out = jnp.where(bk == b, rb_ref[b, col], out)
    return out


def _attn_prompt_body(rb_ref, bk_ref, q_ref, kc_ref, vc_ref, *refs, gi, has_prev, headed):
    dil = ATT_DILATIONS[gi]
    if has_prev:
        kp_ref, vp_ref = refs[:2]
        refs = refs[2:]
    o_ref, lse_ref, kvo_ref, bias_ref = refs[-4:]
    if len(refs) == 5:
        kvo_ref[0] = refs[0][...]
        kvo_ref = kvo_ref.at[1]
    first = (pl.program_id(0) == 0) & (pl.program_id(1) == 0) & (pl.program_id(2) == 0)

    @pl.when(first)
    def _():
        bk = bk_ref[...]
        for h in range(NH):
            bias_ref[h] = _bias_from_buckets(bk, rb_ref, gi * NH + h)

    ib = pl.program_id(1)
    res = pl.program_id(2)
    rows = pl.ds(res, CH, stride=dil) if dil > 1 else slice(None)
    head = (lambda ref, h: ref[h]) if headed else (lambda ref, h: ref[:, h * HD:(h + 1) * HD])
    scale = HD ** -0.5
    lane_head = lax.broadcasted_iota(jnp.int32, (CH, NH * LSE_W), 1) // LSE_W
    lse_tile = None
    if has_prev:
        col = lax.broadcasted_iota(jnp.int32, (CH, 2 * CH), 1)
        no_prev = (ib == 0) & (col < CH)
    for h in range(NH):
        sl = slice(h * HD, (h + 1) * HD)
        kc = head(kc_ref, h)
        vc = head(vc_ref, h)
        kvo_ref[:, sl] = kc
        kvo_ref[:, BW + h * HD:BW + (h + 1) * HD] = vc
        qh = head(q_ref, h).astype(BF16)
        if has_prev:
            kh = jnp.concatenate([head(kp_ref, h), kc], axis=0).astype(BF16)
            vh = jnp.concatenate([head(vp_ref, h), vc], axis=0).astype(BF16)
            bias = jnp.where(no_prev, NEG, bias_ref[h])
        else:
            kh = kc.astype(BF16)
            vh = vc.astype(BF16)
            bias = bias_ref[h][:, CH:]
        s = lax.dot_general(qh, kh, (((1,), (1,)), ((), ())), preferred_element_type=F32) * scale + bias
        m = jnp.max(s, axis=-1, keepdims=True)
        p = jnp.exp(s - m)
        den = jnp.sum(p, axis=-1, keepdims=True)
        acc = jnp.dot(p.astype(BF16), vh, preferred_element_type=F32)
        o_ref[h, rows, :] = acc / den
        lse = m + jnp.log(den)
        lse_tile = lse if h == 0 else jnp.where(lane_head == h, lse, lse_tile)
    lse_ref[rows, :] = lse_tile


def _attn_prompt(zq, rel_bias, gi, kv_prev):
    dil = ATT_DILATIONS[gi]
    rows = BATCH * SEQ
    nb = SEQ // dil // CH
    has_prev = nb > 1
    headed = gi > 0
    bk = jnp.asarray(_prompt_bucket_matrix(dil))

    def rowblk(b, i, r, prev):
        return (b * dil + r) * nb + (jnp.maximum(i - 1, 0) if prev else i)

    def zspec(c, prev=False):
        if headed:
            return pl.BlockSpec((None, NH, CH, HD), lambda b, i, r: (c, 0, rowblk(b, i, r, prev), 0))
        return pl.BlockSpec((CH, BW), lambda b, i, r: (rowblk(b, i, r, prev), COL_QKV[0][c]))

    in_specs = [pl.BlockSpec(memory_space=pltpu.SMEM),
                pl.BlockSpec((CH, 2 * CH), lambda b, i, r: (0, 0)),
                zspec(0), zspec(1), zspec(2)]
    args = [rel_bias, bk, zq, zq, zq]
    if has_prev:
        in_specs += [zspec(1, True), zspec(2, True)]
        args += [zq, zq]
    kv_index = lambda b, i, r: (b, jnp.where(i == nb - 1, r, 0))
    kv_shape = (BATCH * CH, dil * 2 * BW)
    if kv_prev is None:
        kv_spec = pl.BlockSpec((CH, 2 * BW), kv_index)
    else:
        in_specs.append(pl.BlockSpec((CH, 2 * BW), kv_index))
        args.append(kv_prev)
        kv_spec = pl.BlockSpec((2, CH, 2 * BW), lambda b, i, r: (0, *kv_index(b, i, r)))
        kv_shape = (2, *kv_shape)
    return pl.pallas_call(
        functools.partial(_attn_prompt_body, gi=gi, has_prev=has_prev, headed=headed),
        grid=(BATCH, nb, dil),
        in_specs=in_specs,
        out_specs=[pl.BlockSpec((NH, CH * dil, HD), lambda b, i, r: (0, b * nb + i, 0)),
                   pl.BlockSpec((CH * dil, NH * LSE_W), lambda b, i, r: (b * nb + i, 0)),
                   kv_spec],
        out_shape=[jax.ShapeDtypeStruct((NH, rows, HD), F32),
                   jax.ShapeDtypeStruct((rows, NH * LSE_W), F32),
                   jax.ShapeDtypeStruct(kv_shape, F32)],
        scratch_shapes=[pltpu.VMEM((NH, CH, 2 * CH), F32)],
        compiler_params=_cparams(3),
        name=f"attn_prompt_g{gi}",
    )(*args)


def _combine_body(o0, l0, o1, l1, o2, l2, out_ref):
    for h in range(NH):
        a0, a1, a2 = (l[:, h * LSE_W:h * LSE_W + 1] for l in (l0, l1, l2))
        m = jnp.maximum(jnp.maximum(a0, a1), a2)
        w0, w1, w2 = jnp.exp(a0 - m), jnp.exp(a1 - m), jnp.exp(a2 - m)
        out_ref[:, h * HD:(h + 1) * HD] = ((w0 * o0[h] + w1 * o1[h] + w2 * o2[h]) / (w0 + w1 + w2)).astype(BF16)


def _attn_combine(parts, *, tm=1024):
    rows = parts[1].shape[0]
    ospec = pl.BlockSpec((NH, tm, HD), lambda i: (0, i, 0))
    lspec = pl.BlockSpec((tm, NH * LSE_W), lambda i: (i, 0))
    return pl.pallas_call(
        _combine_body,
        grid=(rows // tm,),
        in_specs=[ospec, lspec] * 3,
        out_specs=pl.BlockSpec((tm, BW), lambda i: (i, 0)),
        out_shape=jax.ShapeDtypeStruct((rows, BW), BF16),
        compiler_params=_cparams(1),
        name="attn_combine",
    )(*parts)


def _attn_sample_body(rb_ref, bc0_ref, bc1_ref, bc2_ref, bn_ref, z_ref, c0_ref, c1_ref, c2_ref,
                      oa_ref, kv0_ref, kv1_ref, kv2_ref, b0_ref, b1_ref, b2_ref, bnew_ref):
    bucket_refs = (bc0_ref, bc1_ref, bc2_ref)
    bias_refs = (b0_ref, b1_ref, b2_ref)

    @pl.when(pl.program_id(0) == 0)
    def _():
        for gi in range(3):
            for h in range(NH):
                rs = slice(h * SROWS, (h + 1) * SROWS)
                bias_refs[gi][rs, :] = _bias_from_buckets(bucket_refs[gi][rs, :], rb_ref, gi * NH + h)
                bnew_ref[gi, rs, :] = _bias_from_buckets(bn_ref[gi, rs, :], rb_ref, gi * NH + h)

    scale = HD ** -0.5
    rows = NH * SROWS
    head_of_row = lax.broadcasted_iota(jnp.int32, (rows, BW), 0) // SROWS
    head_of_lane = lax.broadcasted_iota(jnp.int32, (rows, BW), 1) // HD
    head_mask = head_of_row == head_of_lane
    caches = (c0_ref, c1_ref, c2_ref)
    kv_refs = (kv0_ref, kv1_ref, kv2_ref)
    stats = []
    for gi in range(3):
        cq, ck, cv = COL_QKV[gi]
        q = z_ref[:, cq * BW:(cq + 1) * BW]
        kn = z_ref[:, ck * BW:(ck + 1) * BW]
        vn = z_ref[:, cv * BW:(cv + 1) * BW]
        kv_refs[gi][:, :BW] = kn
        kv_refs[gi][:, BW:] = vn
        qm = jnp.concatenate([q[:, h * HD:(h + 1) * HD] for h in range(NH)], axis=0).astype(BF16)
        kf = caches[gi][...].reshape(SAMPLE_CACHE_ROWS[gi], HD).astype(BF16)
        s = lax.dot_general(qm, kf, (((1,), (1,)), ((), ())), preferred_element_type=F32) * scale + bias_refs[gi][...]
        qrows = jnp.where(head_mask, jnp.concatenate([q] * NH, axis=0), 0.0)
        bias_n = bnew_ref[gi]
        s_new = []
        for tp in range(DEC_SEQ):
            dotp = jnp.sum(qrows * kn[tp:tp + 1, :], axis=-1, keepdims=True)
            s_new.append(dotp * scale + bias_n[:, tp:tp + 1])
        m = jnp.max(s, axis=-1, keepdims=True)
        for sn in s_new:
            m = jnp.maximum(m, sn)
        p = jnp.exp(s - m)
        den = jnp.sum(p, axis=-1, keepdims=True)
        pv = pltpu.roll(p, NH, axis=1).astype(BF16)
        acc = jnp.dot(pv, kf, preferred_element_type=F32)
        acc_n = jnp.zeros((rows, BW), F32)
        for tp, sn in enumerate(s_new):
            pn = jnp.exp(sn - m)
            den = den + pn
            acc_n = acc_n + pn * vn[tp:tp + 1, :]
        acc = acc + jnp.concatenate([acc_n[h * SROWS:(h + 1) * SROWS, h * HD:(h + 1) * HD] for h in range(NH)], axis=0)
        stats.append((m, den, acc))
    mm = jnp.maximum(jnp.maximum(stats[0][0], stats[1][0]), stats[2][0])
    den = jnp.zeros((rows, 1), F32)
    acc = jnp.zeros((rows, HD), F32)
    for m, d, a in stats:
        w = jnp.exp(m - mm)
        den = den + w * d
        acc = acc + w * a
    o = acc / den
    for h in range(NH):
        oa_ref[:, h * HD:(h + 1) * HD] = o[h * SROWS:(h + 1) * SROWS, :].astype(BF16)


def _attn_sample(z, caches, rel_bias, layer):
    tables = _sample_bucket_tables()
    rows = DEC_BATCH * SROWS
    qrows = NH * SROWS
    n0, n1, n2 = SAMPLE_CACHE_ROWS
    c0 = caches[0].reshape(DEPTH, DEC_BATCH, n0, HD)
    c1 = caches[1].reshape(DEPTH, DEC_BATCH, n1, HD)
    c2 = caches[2].reshape(DEPTH, DEC_BATCH, CH, 16 * 8, HD)
    new_tbl = jnp.asarray(np.stack([t[1] for t in tables]))
    kv_spec = pl.BlockSpec((SROWS, 2 * BW), lambda b: (b, 0))
    kv_shape = jax.ShapeDtypeStruct((rows, 2 * BW), F32)
    const2 = lambda b: (0, 0)
    return pl.pallas_call(
        _attn_sample_body,
        grid=(DEC_BATCH,),
        in_specs=[pl.BlockSpec(memory_space=pltpu.SMEM),
                  pl.BlockSpec((qrows, n0), const2), pl.BlockSpec((qrows, n1), const2),
                  pl.BlockSpec((qrows, n2), const2),
                  pl.BlockSpec((3, qrows, CH), lambda b: (0, 0, 0)),
                  pl.BlockSpec((SROWS, IN_WIDTH), lambda b: (b, 0)),
                  pl.BlockSpec((None, None, n0, HD), lambda b: (layer, b, 0, 0)),
                  pl.BlockSpec((None, None, n1, HD), lambda b: (layer, b, 0, 0)),
                  pl.BlockSpec((None, None, CH, 8 * DEC_SEQ, HD), lambda b: (layer, b, 0, 0, 0))],
        out_specs=[pl.BlockSpec((SROWS, BW), lambda b: (b, 0)), kv_spec, kv_spec, kv_spec],
        out_shape=[jax.ShapeDtypeStruct((rows, BW), BF16), kv_shape, kv_shape, kv_shape],
        scratch_shapes=[pltpu.VMEM((qrows, n0), F32), pltpu.VMEM((qrows, n1), F32), pltpu.VMEM((qrows, n2), F32),
                        pltpu.VMEM((3, qrows, CH), F32)],
        compiler_params=_cparams(1),
        name="attn_sample",
    )(rel_bias, jnp.asarray(tables[0][0]), jnp.asarray(tables[1][0]), jnp.asarray(tables[2][0]), new_tbl,
      z, c0, c1, c2)


def _gmlp_body(bu_ref, bv_ref, g_ref, ws_ref, bs_ref, *refs, rows, emit_vn):
    refs = list(refs)
    o_ref = refs.pop(0)
    vn_ref = refs.pop(0) if emit_vn else None
    tril = lax.broadcasted_iota(jnp.int32, (CH, CH), 0) >= lax.broadcasted_iota(jnp.int32, (CH, CH), 1)
    w = [jnp.where(tril, ws_ref[g], 0.0).astype(BF16) for g in range(NH)]
    if rows < CH:
        pad_u, pad_v = refs
        pad_u[...] = jnp.zeros_like(pad_u)
        pad_v[...] = jnp.zeros_like(pad_v)
        pad_u[0:rows, :] = bu_ref[...]
        pad_v[0:rows, :] = bv_ref[...]
        bu_ref, bv_ref = pad_u, pad_v
    for c in range(max(rows // CH, 1)):
        rs = slice(c * CH, (c + 1) * CH)
        u = jax.nn.gelu(bu_ref[rs, :])
        vn = _rms(jax.nn.gelu(bv_ref[rs, :]), g_ref[...])
        if emit_vn:
            vn_ref[...] = vn[:rows]
        for g in range(NH):
            sl = slice(g * HD, (g + 1) * HD)
            mixed = jnp.dot(w[g], vn[:, sl].astype(BF16), preferred_element_type=F32) + bs_ref[:, sl]
            res = (u[:, sl] * mixed).astype(BF16)
            if rows < CH:
                o_ref[:, sl] = res[:rows]
            else:
                o_ref[rs, sl] = res


def _gmlp(z, g_gmlp, w_spatial, b_spatial, *, rows, emit_vn):
    m = z.shape[0]
    bs_full = jnp.repeat(b_spatial.T, HD, axis=1)
    spec = pl.BlockSpec((rows, BW), lambda i: (i, 0))
    out_shape = [jax.ShapeDtypeStruct((m, BW), BF16)]
    out_specs = [spec]
    if emit_vn:
        out_shape.append(jax.ShapeDtypeStruct((m, BW), F32))
        out_specs.append(spec)
    scratch = [] if rows >= CH else [pltpu.VMEM((CH, BW), F32)] * 2
    return pl.pallas_call(
        functools.partial(_gmlp_body, rows=rows, emit_vn=emit_vn),
        grid=(m // rows,),
        in_specs=[pl.BlockSpec((rows, BW), lambda i: (i, COL_BU)),
                  pl.BlockSpec((rows, BW), lambda i: (i, COL_BV)),
                  pl.BlockSpec((1, BW), lambda i: (0, 0)),
                  pl.BlockSpec((NH, CH, CH), lambda i: (0, 0, 0)),
                  pl.BlockSpec((CH, BW), lambda i: (0, 0))],
        out_specs=out_specs,
        out_shape=out_shape,
        scratch_shapes=scratch,
        compiler_params=_cparams(1),
        name="gmlp",
    )(z, z, g_gmlp.reshape(1, -1), w_spatial, bs_full)


def _pool_body(x_ref, prev_ref, wp_ref, sc_ref, o_ref, st_ref, ext_ref, *, rows, n_new, start, zero_first_prev):
    ib = pl.program_id(1)
    prev = prev_ref[...]
    if zero_first_prev:
        prev = jnp.where(ib == 0, 0.0, prev)
    x = x_ref[...]
    ext_ref[0:16, :] = prev
    ext_ref[16:16 + rows, :] = x
    st_ref[...] = ext_ref[pl.ds(n_new + 1, POOL_STATE), :]
    ext = ext_ref[...]
    pos = start + ib * rows + lax.broadcasted_iota(jnp.int32, (rows, 1), 0)
    for gi, win in enumerate(POOL_WINDOWS):
        sl = slice(gi * HD, (gi + 1) * HD)
        s = ext[:, sl]
        k = 1
        while k < win:
            s = s + pltpu.roll(s, k, axis=0)
            k *= 2
        cnt = jnp.minimum(pos + 1, win).astype(F32)
        diff = s[16:] / cnt - x[:, sl]
        y = jnp.dot(diff.astype(BF16), wp_ref[gi].astype(BF16), preferred_element_type=F32)
        o_ref[:, sl] = (y * sc_ref[:, sl]).astype(BF16)


def _pool(z, prev, w_pool, pool_scale, *, nseq, rows, n_new, start, layer=None):
    m = z.shape[0]
    nblk = m // nseq // rows
    if prev is None:
        per16 = rows // 16
        prev_arr = z
        prev_spec = pl.BlockSpec((16, BW), lambda b, i: (jnp.maximum((b * nblk + i) * per16 - 1, 0), COL_CIN))
    else:
        prev_arr = prev
        prev_spec = pl.BlockSpec((None, None, 16, BW), lambda b, i: (layer, b, 0, 0))
    return pl.pallas_call(
        functools.partial(_pool_body, rows=rows, n_new=n_new, start=start, zero_first_prev=prev is None),
        grid=(nseq, nblk),
        in_specs=[pl.BlockSpec((rows, BW), lambda b, i: (b * nblk + i, COL_CIN)),
                  prev_spec,
                  pl.BlockSpec((NH, HD, HD), lambda b, i: (0, 0, 0)),
                  pl.BlockSpec((1, BW), lambda b, i: (0, 0))],
        out_specs=[pl.BlockSpec((rows, BW), lambda b, i: (b * nblk + i, 0)),
                   pl.BlockSpec((None, POOL_STATE, BW), lambda b, i: (b, 0, 0))],
        out_shape=[jax.ShapeDtypeStruct((m, BW), BF16),
                   jax.ShapeDtypeStruct((nseq, POOL_STATE, BW), F32)],
        scratch_shapes=[pltpu.VMEM((16 + rows, BW), F32)],
        compiler_params=_cparams(2),
        name="pool",
    )(z, prev_arr, w_pool, pool_scale.reshape(1, -1))


def _ret_tables(c_eff, positions):
    lg = np.log1p(-np.power(2.0, -5.0 - np.arange(NH, dtype=np.float64)))
    i = np.arange(CH, dtype=np.float64)
    live = (i < c_eff)
    diff = i[:, None] - i[None, :]
    inner = np.where((diff >= 0) & live[:, None] & live[None, :], np.exp(np.maximum(diff, 0.0)[None] * lg[:, None, None]), 0.0)
    qd = np.where(live[None, :], np.exp((i + 1.0)[None, :] * lg[:, None]), 0.0)
    kd = np.where(live[None, :], np.exp((c_eff - 1.0 - i)[None, :] * lg[:, None]), 0.0)
    chunk = tuple(float(v) for v in np.exp(c_eff * lg))
    qd_full = np.repeat(qd.T, HD, axis=1)
    kd_full = np.repeat(kd.T, HD, axis=1) * (HD ** -0.5)
    half = HD // 2
    inv = ROPE_BASE ** (-np.arange(half, dtype=np.float64) / half)
    ang = np.asarray(positions, np.float64)[:, None] * inv[None, :]
    cosf = np.concatenate([np.cos(ang), np.cos(ang)], axis=1)
    sinf = np.concatenate([-np.sin(ang), np.sin(ang)], axis=1)
    to32 = lambda a: jnp.asarray(a.astype(np.float32))
    return to32(inner), to32(qd_full), to32(kd_full), chunk, to32(cosf), to32(sinf)


def _ret_body(q_ref, k_ref, v_ref, g_ref, cos_ref, sin_ref, inner_ref, qd_ref, kd_ref, gr_ref, *refs,
              nseq, rows, chunk_decay, has_state):
    refs = list(refs)
    s0_ref = refs.pop(0) if has_state else None
    o_ref, sn_ref, s_ref = refs[:3]
    pad_ref = refs[3] if rows != CH else None
    ic = pl.program_id(1)

    @pl.when(ic == 0)
    def _():
        if has_state:
            s_ref[...] = s0_ref[...]
        else:
            s_ref[...] = jnp.zeros_like(s_ref)

    def chunk(ref, b, k):
        if rows == CH:
            return ref[b]
        pad_ref[k] = jnp.zeros((CH, BW), F32)
        pad_ref[k, 0:rows, :] = ref[b]
        return pad_ref[k]

    cosf = cos_ref[...]
    sinf = sin_ref[...]
    for b in range(nseq):
        q = chunk(q_ref, b, 0)
        k = chunk(k_ref, b, 1)
        v = chunk(v_ref, b, 2)
        gate = chunk(g_ref, b, 3)
        for h in range(NH):
            sl = slice(h * HD, (h + 1) * HD)
            qh = q[:, sl]
            kh = k[:, sl]
            rq = qh * cosf + pltpu.roll(qh, HD // 2, axis=1) * sinf
            rk = kh * cosf + pltpu.roll(kh, HD // 2, axis=1) * sinf
            vb = v[:, sl].astype(BF16)
            rqb = rq.astype(BF16)
            state = s_ref[b, h]
            att = lax.dot_general(rqb, (rk * (HD ** -0.5)).astype(BF16), (((1,), (1,)), ((), ())),
                                  preferred_element_type=F32) * inner_ref[h]
            o = (jnp.dot(att.astype(BF16), vb, preferred_element_type=F32)
                 + jnp.dot(rqb, state.astype(BF16), preferred_element_type=F32) * qd_ref[:, sl])
            kdec = (rk * kd_ref[:, sl]).T.astype(BF16)
            new_state = state * chunk_decay[h] + jnp.dot(kdec, vb, preferred_element_type=F32)
            s_ref[b, h] = new_state
            sn_ref[b, h] = new_state
            o = o * lax.rsqrt(jnp.mean(o * o, axis=-1, keepdims=True) + EPS)
            gt = gate[:, sl]
            o_ref[b, :, sl] = (o * gr_ref[:, sl] * (gt * jax.nn.sigmoid(gt)))[:rows].astype(BF16)


def _retention(z, g_ret, state, *, nseq, per_step, rows, c_eff, positions, layer=None):
    m, zw = z.shape
    seq_rows = m // nseq
    nchunk = seq_rows // rows
    inner, qd, kd, chunk_decay, cosf, sinf = _ret_tables(c_eff, positions)
    has_state = state is not None
    z3 = z.reshape(nseq, seq_rows, zw)

    def zspec(col):
        return pl.BlockSpec((per_step, rows, BW), lambda g, i: (g, i, col))

    const2 = lambda g, i: (0, 0)
    in_specs = [zspec(COL_DQ), zspec(COL_DK), zspec(COL_DV), zspec(COL_DG),
                pl.BlockSpec((CH, HD), lambda g, i: (i, 0)), pl.BlockSpec((CH, HD), lambda g, i: (i, 0)),
                pl.BlockSpec((NH, CH, CH), lambda g, i: (0, 0, 0)),
                pl.BlockSpec((CH, BW), const2), pl.BlockSpec((CH, BW), const2), pl.BlockSpec((1, BW), const2)]
    args = [z3, z3, z3, z3, cosf, sinf, inner, qd, kd, g_ret.reshape(1, -1)]
    if has_state:
        in_specs.append(pl.BlockSpec((None, per_step, NH, HD, HD), lambda g, i: (layer, g, 0, 0, 0)))
        args.append(state)
    scratch = [pltpu.VMEM((per_step, NH, HD, HD), F32)]
    if rows != CH:
        scratch.append(pltpu.VMEM((4, CH, BW), F32))
    o, sn = pl.pallas_call(
        functools.partial(_ret_body, nseq=per_step, rows=rows, chunk_decay=chunk_decay, has_state=has_state),
        grid=(nseq // per_step, nchunk),
        in_specs=in_specs,
        out_specs=[pl.BlockSpec((per_step, rows, BW), lambda g, i: (g, i, 0)),
                   pl.BlockSpec((per_step, NH, HD, HD), lambda g, i: (g, 0, 0, 0))],
        out_shape=[jax.ShapeDtypeStruct((nseq, seq_rows, BW), BF16),
                   jax.ShapeDtypeStruct((nseq, NH, HD, HD), F32)],
        scratch_shapes=scratch,
        compiler_params=_cparams(2),
        name="retention",
    )(*args)
    return o.reshape(m, BW), sn


def kernel(x_prompt, x_sample, cache_attn_kv_w128, cache_attn_kv_w512, cache_attn_kv_w2048, state_pool, state_ret, rel_bias, g_ffn1, w_ffn1_gate, w_ffn1_up, w_ffn1_down, g_mix, w_in, g_gmlp, w_spatial, b_spatial, w_pool, pool_scale, g_ret, w_branch, w_out, g_ffn2, w_ffn2_gate, w_ffn2_up, w_ffn2_down, g_final):
    caches = (cache_attn_kv_w128, cache_attn_kv_w512, cache_attn_kv_w2048)
    xp = x_prompt.reshape(BATCH * SEQ, D_MODEL)
    xs = jnp.pad(x_sample, ((0, 0), (0, SROWS - DEC_SEQ), (0, 0))).reshape(DEC_BATCH * SROWS, D_MODEL)
    pool_state = jnp.pad(state_pool, ((0, 0), (0, 0), (1, 0), (0, 0)))
    tm_p, tm_s = 1024, DEC_BATCH * SROWS
    sample_pos = PAST_LEN + np.arange(CH)
    wb = w_branch.astype(BF16)
    wo = w_out.astype(BF16)
    gate_col0 = N_PLAIN * BW // 1024
    qkv1, qkv2 = COL_QKV[1][0], COL_QKV[2][0]

    kv_p = [None, None, None]
    kv_s = [[], [], []]
    pool_p, pool_s, ret_p, ret_s, gv_s = [], [], [], [], []
    yp = ys = None
    for l in range(DEPTH):
        last = l == DEPTH - 1

        xs, hs, *w1 = _ffn(xs, g_ffn1[l], (w_ffn1_gate, w_ffn1_up, w_ffn1_down), g_mix[l], layer=l,
                           emit_x=True, post_dtype=BF16, tm=tm_s)
        zs, gs, win = _in_proj_convert(hs, w_in, layer=l)
        oa, k0, k1, k2 = _attn_sample(zs, caches, rel_bias, l)
        for gi, kv in enumerate((k0, k1, k2)):
            kv_s[gi].append(kv.reshape(DEC_BATCH, SROWS, 2, NH, HD)[:, :DEC_SEQ])
        ob, vn = _gmlp(zs, g_gmlp[l], w_spatial[l], b_spatial[l], rows=SROWS, emit_vn=True)
        gv_s.append(vn.reshape(DEC_BATCH, SROWS, BW)[:, :DEC_SEQ])
        oc, pn = _pool(zs, pool_state, w_pool[l], pool_scale[l], nseq=DEC_BATCH, rows=SROWS, n_new=DEC_SEQ,
                      start=PAST_LEN, layer=l)
        od, rn = _retention(zs, g_ret[l], state_ret, nseq=DEC_BATCH, per_step=1, rows=SROWS, c_eff=DEC_SEQ,
                            positions=sample_pos, layer=l)
        pool_s.append(pn)
        ret_s.append(rn)
        mix = _branch_merge(oa, ob, oc, od, gs, wb, layer=l, tm=tm_s)
        xs = _matmul(mix, wo, xs, layer=l, tm=tm_s, tn=512, name="out_proj")
        if last:
            ys, *w2 = _ffn(xs, g_ffn2[l], (w_ffn2_gate, w_ffn2_up, w_ffn2_down), g_final, layer=l,
                           emit_x=False, post_dtype=F32, tm=tm_s)
        else:
            xs, *w2 = _ffn(xs, g_ffn2[l], (w_ffn2_gate, w_ffn2_up, w_ffn2_down), g_final, layer=l,
                           emit_x=True, post_dtype=None, tm=tm_s)

        xp, hp = _ffn(xp, g_ffn1[l], w1, g_mix[l], emit_x=True, post_dtype=BF16, tm=tm_p)
        zp = _matmul(hp, win, n=N_PLAIN * BW, tm=1024, tn=1024, name="in_proj")
        gp = _matmul(hp, win, col0=gate_col0, n=GATE_W, gate=True, tm=1024, tn=1024, name="in_proj_gate")
        zq = (zp,
              _matmul_deint(hp, win, col0=qkv1, dil=ATT_DILATIONS[1], name="in_proj_g1"),
              _matmul_deint(hp, win, col0=qkv2, dil=ATT_DILATIONS[2], name="in_proj_g2"))
        parts = []
        for gi in range(3):
            o, lse, kv_p[gi] = _attn_prompt(zq[gi], rel_bias, gi, kv_p[gi])
            parts += [o, lse]
        oa = _attn_combine(parts)
        ob, = _gmlp(zp, g_gmlp[l], w_spatial[l], b_spatial[l], rows=4 * CH, emit_vn=False)
        oc, pn = _pool(zp, None, w_pool[l], pool_scale[l], nseq=BATCH, rows=512, n_new=512, start=0)
        od, rn = _retention(zp, g_ret[l], None, nseq=BATCH, per_step=BATCH, rows=CH, c_eff=CH,
                            positions=np.arange(SEQ))
        pool_p.append(pn)
        ret_p.append(rn)
        mix = _branch_merge(oa, ob, oc, od, gp, wb, layer=l, tm=256)
        xp = _matmul(mix, wo, xp, layer=l, tm=1024, tn=1024, name="out_proj")
        if last:
            yp, = _ffn(xp, g_ffn2[l], w2, g_final, emit_x=False, post_dtype=F32, tm=tm_p)
        else:
            xp, = _ffn(xp, g_ffn2[l], w2, g_final, emit_x=True, post_dtype=None, tm=tm_p)

    y_prompt = yp.reshape(BATCH, SEQ, D_MODEL)
    y_sample = ys.reshape(DEC_BATCH, SROWS, D_MODEL)[:, :DEC_SEQ]
    kv_p = [kv.reshape(DEPTH, BATCH, ATT_WINDOWS[gi], 2, NH, HD) for gi, kv in enumerate(kv_p)]
    return (y_prompt, y_sample,
            kv_p[0], kv_p[1], kv_p[2],
            jnp.stack(kv_s[0]), jnp.stack(kv_s[1]), jnp.stack(kv_s[2]),
            jnp.stack(pool_p), jnp.stack(pool_s),
            jnp.stack(ret_p), jnp.stack(ret_s),
            jnp.stack(gv_s))
```

```python
import functools

import numpy as np
import jax
import jax.numpy as jnp
from jax import lax
from jax.experimental import pallas as pl
from jax.experimental.pallas import tpu as pltpu

F32 = jnp.float32
BF16 = jnp.bfloat16

D_MODEL = 2048
BATCH = 4
SEQ = 2048
DEPTH = 2
DEC_BATCH = 32
DEC_SEQ = 4
PAST_LEN = 8192
D_FF = 5632
IN_WIDTH = 16384
EPS = 1e-6
BW = 512
HD = 128
NH = 4
ATT_WINDOWS = (128, 512, 2048)
ATT_DILATIONS = (1, 4, 16)
NKEY = 129
REL_BUCKETS = 32
REL_MAX_DISTANCE = 2048
POOL_WINDOWS = (2, 4, 8, 16)
POOL_STATE = 15
ROPE_BASE = 10000.0
SROWS = 8
CH = 128
NEG = -1e30
LSE_W = 32

W_IN_ORDER = (0, 3, 6) + tuple(range(9, 32)) + (1, 4, 7) + (2, 5, 8)
COL_QKV = ((0, 1, 2), (26, 27, 28), (29, 30, 31))
COL_BU, COL_BV, COL_CIN = 3, 4, 5
COL_DQ, COL_DK, COL_DV, COL_DG = 6, 7, 8, 9
N_PLAIN = 10
N_GATE = 16
GATE_W = N_GATE * BW
ZBLK = IN_WIDTH // BW

VMEM_BIG = 56 * 1024 * 1024


def _cparams(n_axes, vmem=None):
    return pltpu.CompilerParams(dimension_semantics=("arbitrary",) * n_axes, vmem_limit_bytes=vmem)


def _rms(x, g):
    return x * lax.rsqrt(jnp.mean(x * x, axis=-1, keepdims=True) + EPS) * g


def _ffn_body(x_ref, gpre_ref, *refs, emit_x, emit_post, convert, nf, tf, row_split):
    refs = list(refs)
    if convert:
        wg_ref, wu_ref, wd_ref, gpost_ref = refs[:4]
        refs = refs[4:]
    else:
        wgu_ref, wd_ref, gpost_ref = refs[:3]
        refs = refs[3:]
    n_act = int(emit_x) + int(emit_post)
    outs = refs[:n_act]
    scratch = refs[n_act + (2 if convert else 0):]
    xn_ref = scratch[0]
    acc_ref = outs[0]
    f = pl.program_id(1)

    @pl.when(f == 0)
    def _():
        x = x_ref[...]
        xn_ref[...] = _rms(x, gpre_ref[...]).astype(BF16)
        acc_ref[...] = x

    if convert:
        wgu = jnp.concatenate([wg_ref[...].astype(BF16), wu_ref[...].astype(BF16)], axis=1)
        wd = wd_ref[...].astype(BF16)
        refs[n_act][...] = wgu
        refs[n_act + 1][...] = wd
    else:
        wgu, wd = wgu_ref[...], wd_ref[...]
    rows = xn_ref.shape[0] // row_split
    for part in range(row_split):
        rs = slice(part * rows, (part + 1) * rows)
        r = jnp.dot(xn_ref[rs, :], wgu, preferred_element_type=F32)
        g, u = r[:, :tf], r[:, tf:]
        h = (g * jax.nn.sigmoid(g) * (0.5 * u)).astype(BF16)
        acc_ref[rs, :] += jnp.dot(h, wd, preferred_element_type=F32)

    if emit_post:
        @pl.when(f == nf - 1)
        def _():
            outs[-1][...] = _rms(acc_ref[...], gpost_ref[...]).astype(outs[-1].dtype)


def _ffn(x, g_pre, weights, g_post, *, layer=None, emit_x, post_dtype, tm, tf=512):
    m = x.shape[0]
    nf = D_FF // tf
    emit_post = post_dtype is not None
    convert = layer is not None
    assert emit_x or post_dtype == F32, "the first output block is the f32 accumulator"
    once = dict(pipeline_mode=pl.Buffered(1)) if tm > 512 else {}
    row_spec = pl.BlockSpec((tm, D_MODEL), lambda i, f: (i, 0))
    row_out_spec = pl.BlockSpec((tm, D_MODEL), lambda i, f: (i, 0), **once)
    vec_spec = pl.BlockSpec((1, D_MODEL), lambda i, f: (0, 0))
    out_shape, out_specs = [], []
    if emit_x:
        out_shape.append(jax.ShapeDtypeStruct((m, D_MODEL), F32))
        out_specs.append(row_out_spec)
    if emit_post:
        out_shape.append(jax.ShapeDtypeStruct((m, D_MODEL), post_dtype))
        out_specs.append(row_out_spec)
    gu_spec = pl.BlockSpec((D_MODEL, 2 * tf), lambda i, f: (0, f))
    down_spec = pl.BlockSpec((tf, D_MODEL), lambda i, f: (f, 0))
    if convert:
        assert m == tm, "the bf16 weights are written once, by a single row tile"
        w_specs = [pl.BlockSpec((None, D_MODEL, tf), lambda i, f: (layer, 0, f)),
                   pl.BlockSpec((None, D_MODEL, tf), lambda i, f: (layer, 0, f)),
                   pl.BlockSpec((None, tf, D_MODEL), lambda i, f: (layer, f, 0))]
        out_shape += [jax.ShapeDtypeStruct((D_MODEL, 2 * D_FF), BF16), jax.ShapeDtypeStruct((D_FF, D_MODEL), BF16)]
        out_specs += [gu_spec, down_spec]
    else:
        w_specs = [gu_spec, down_spec]
    scratch = [pltpu.VMEM((tm, D_MODEL), BF16)]
    return pl.pallas_call(
        functools.partial(_ffn_body, emit_x=emit_x, emit_post=emit_post, convert=convert, nf=nf, tf=tf,
                          row_split=max(tm // 512, 1)),
        grid=(m // tm, nf),
        in_specs=[row_spec, vec_spec, *w_specs, vec_spec],
        out_specs=out_specs,
        out_shape=out_shape,
        scratch_shapes=scratch,
        compiler_params=_cparams(2, VMEM_BIG),
        name="ffn",
    )(x, g_pre.reshape(1, -1), *weights, g_post.reshape(1, -1))


def _mm_body(a_ref, b_ref, o_ref):
    o_ref[...] = jnp.dot(a_ref[...], b_ref[...], preferred_element_type=F32)


def _mm_res_body(a_ref, b_ref, r_ref, o_ref):
    o_ref[...] = r_ref[...] + jnp.dot(a_ref[...], b_ref[...], preferred_element_type=F32)


def _mm_gate_body(a_ref, b_ref, o_ref):
    o_ref[...] = jax.nn.sigmoid(jnp.dot(a_ref[...], b_ref[...], preferred_element_type=F32)).astype(BF16)


def _matmul(a, b, res=None, *, layer=None, col0=0, n=None, gate=False, tm, tn, name):
    m, k = a.shape
    n = b.shape[-1] if n is None else n
    if layer is None:
        b_spec = pl.BlockSpec((k, tn), lambda i, j: (0, col0 + j))
    else:
        b_spec = pl.BlockSpec((None, k, tn), lambda i, j: (layer, 0, col0 + j))
    in_specs = [pl.BlockSpec((tm, k), lambda i, j: (i, 0)), b_spec]
    args = [a, b]
    body = _mm_gate_body if gate else _mm_body
    if res is not None:
        in_specs.append(pl.BlockSpec((tm, tn), lambda i, j: (i, j)))
        args.append(res)
        body = _mm_res_body
    return pl.pallas_call(
        body,
        grid=(m // tm, n // tn),
        in_specs=in_specs,
        out_specs=pl.BlockSpec((tm, tn), lambda i, j: (i, j)),
        out_shape=jax.ShapeDtypeStruct((m, n), BF16 if gate else F32),
        compiler_params=_cparams(2, VMEM_BIG),
        name=name,
    )(*args)


def _mm_deint_body(a_ref, b_ref, o_ref, s_ref, *, dil):
    r = jnp.dot(a_ref[...], b_ref[...], preferred_element_type=F32)
    for h in range(NH):
        s_ref[h] = r[:, h * HD:(h + 1) * HD]
    n = SEQ // dil
    for h in range(NH):
        for res in range(dil):
            o_ref[h, pl.ds(res * n, n), :] = s_ref[h, pl.ds(res, n, stride=dil), :]


def _w_in_source_block(j, where=jnp.where):
    return where(j < 3, 3 * j, where(j < 26, j + 6, where(j < 29, 3 * (j - 26) + 1, 3 * (j - 29) + 2)))


assert tuple(_w_in_source_block(np.arange(ZBLK), np.where)) == W_IN_ORDER


def _in_proj_convert_body(a_ref, b_ref, z_ref, gate_ref, w_ref):
    j = pl.program_id(0)
    w = b_ref[...].astype(BF16)
    w_ref[...] = w
    r = jnp.dot(a_ref[...], w, preferred_element_type=F32)
    z_ref[...] = r

    @pl.when((j >= N_PLAIN) & (j < N_PLAIN + N_GATE))
    def _():
        gate_ref[...] = jax.nn.sigmoid(r).astype(BF16)


def _in_proj_convert(a, w_in, *, layer):
    m, k = a.shape
    return pl.pallas_call(
        _in_proj_convert_body,
        grid=(ZBLK,),
        in_specs=[pl.BlockSpec((m, k), lambda j: (0, 0)),
                  pl.BlockSpec((None, k, BW), lambda j: (layer, 0, _w_in_source_block(j)))],
        out_specs=[pl.BlockSpec((m, BW), lambda j: (0, j)),
                   pl.BlockSpec((m, BW), lambda j: (0, jnp.clip(j - N_PLAIN, 0, N_GATE - 1))),
                   pl.BlockSpec((k, BW), lambda j: (0, j))],
        out_shape=[jax.ShapeDtypeStruct((m, IN_WIDTH), F32),
                   jax.ShapeDtypeStruct((m, GATE_W), BF16),
                   jax.ShapeDtypeStruct((k, IN_WIDTH), BF16)],
        compiler_params=_cparams(1, VMEM_BIG),
        name="in_proj_convert",
    )(a, w_in)


def _matmul_deint(a, b, *, col0, dil, name):
    m, k = a.shape
    return pl.pallas_call(
        functools.partial(_mm_deint_body, dil=dil),
        grid=(BATCH, 3),
        in_specs=[pl.BlockSpec((SEQ, k), lambda b, j: (b, 0)),
                  pl.BlockSpec((k, BW), lambda b, j: (0, col0 + j))],
        out_specs=pl.BlockSpec((None, NH, SEQ, HD), lambda b, j: (j, 0, b, 0)),
        out_shape=jax.ShapeDtypeStruct((3, NH, m, HD), F32),
        scratch_shapes=[pltpu.VMEM((NH, SEQ, HD), F32)],
        compiler_params=_cparams(2, VMEM_BIG),
        name=name,
    )(a, b)


def _branch_body(oa_ref, ob_ref, oc_ref, od_ref, gate_ref, wb_ref, o_ref):
    acc = None
    for n, br in enumerate((oa_ref, ob_ref, oc_ref, od_ref)):
        proj = jnp.dot(br[...], wb_ref[n], preferred_element_type=F32)
        t = gate_ref[:, n * D_MODEL:(n + 1) * D_MODEL].astype(F32) * proj
        acc = t if acc is None else acc + t
    o_ref[...] = acc.astype(BF16)


def _branch_merge(oa, ob, oc, od, gates, wb, *, layer, tm):
    m = oa.shape[0]
    br_spec = pl.BlockSpec((tm, BW), lambda i: (i, 0))
    return pl.pallas_call(
        _branch_body,
        grid=(m // tm,),
        in_specs=[br_spec] * 4 + [pl.BlockSpec((tm, GATE_W), lambda i: (i, 0)),
                                  pl.BlockSpec((None, 4, BW, D_MODEL), lambda i: (layer, 0, 0, 0))],
        out_specs=pl.BlockSpec((tm, D_MODEL), lambda i: (i, 0)),
        out_shape=jax.ShapeDtypeStruct((m, D_MODEL), BF16),
        compiler_params=_cparams(1, VMEM_BIG),
        name="branch_merge",
    )(oa, ob, oc, od, gates, wb)


def _t5_buckets(dist):
    max_exact = REL_BUCKETS // 2
    d = np.maximum(dist, 1).astype(np.float32)
    large = max_exact + (np.log(d / max_exact) / np.log(REL_MAX_DISTANCE / max_exact)
                         * (REL_BUCKETS - max_exact)).astype(np.int32)
    large = np.minimum(large, REL_BUCKETS - 1)
    return np.where(dist < max_exact, dist, large).astype(np.int32)


def _bucket_of_step(dil):
    return _t5_buckets(dil * np.arange(NKEY))


def _prompt_bucket_matrix(dil):
    iq = np.arange(CH)[:, None]
    col = np.arange(2 * CH)[None, :]
    j = iq + CH - col
    valid = (j >= 0) & (j < NKEY)
    return np.where(valid, _bucket_of_step(dil)[np.clip(j, 0, NKEY - 1)], -1).astype(np.int32)


SAMPLE_CACHE_ROWS = (8 * ATT_WINDOWS[0], 8 * ATT_WINDOWS[1], 8 * DEC_SEQ * CH)


def _sample_bucket_tables():
    h = np.repeat(np.arange(NH), SROWS)[:, None]
    t = np.tile(np.arange(SROWS), NH)[:, None]
    lane = np.arange(CH)[None, :]
    tables = []
    for gi, dil in enumerate(ATT_DILATIONS):
        bos = _bucket_of_step(dil)
        flat = np.arange(SAMPLE_CACHE_ROWS[gi])[None, :]
        head, is_k = flat % NH, (flat // NH) % 2 == 0
        if gi < 2:
            w = flat // 8
            steps = ATT_WINDOWS[gi] + t - w
            valid = (steps % dil == 0) & (steps // dil < NKEY)
            j = steps // dil
        else:
            res, i = (flat // 8) % DEC_SEQ, flat // (8 * DEC_SEQ)
            valid = res == t
            j = np.broadcast_to(CH - i, valid.shape)
        valid = valid & is_k & (head == h) & (t < DEC_SEQ)
        cache_tbl = np.where(valid, bos[np.clip(j, 0, NKEY - 1)], -1)
        jn = t - lane
        validn = (t < DEC_SEQ) & (lane < DEC_SEQ) & (jn >= 0) & ((dil == 1) | (jn == 0))
        new_tbl = np.where(validn, bos[np.clip(jn, 0, NKEY - 1)], -1)
        tables.append((cache_tbl.astype(np.int32), new_tbl.astype(np.int32)))
    return tables


def _bias_from_buckets(bk, rb_ref, col):
    out = jnp.full(bk.shape, NEG, F32)
    for b in range(REL_BUCKETS):
        out = jnp.where(bk == b, rb_ref[b, col], out)
    return out


def _attn_prompt_body(rb_ref, bk_ref, q_ref, kc_ref, vc_ref, *refs, gi, has_prev, headed):
    dil = ATT_DILATIONS[gi]
    if has_prev:
        kp_ref, vp_ref = refs[:2]
        refs = refs[2:]
    o_ref, lse_ref, kvo_ref, bias_ref = refs[-4:]
    if len(refs) == 5:
        kvo_ref[0] = refs[0][...]
        kvo_ref = kvo_ref.at[1]
    first = (pl.program_id(0) == 0) & (pl.program_id(1) == 0) & (pl.program_id(2) == 0)

    @pl.when(first)
    def _():
        bk = bk_ref[...]
        for h in range(NH):
            bias_ref[h] = _bias_from_buckets(bk, rb_ref, gi * NH + h)

    ib = pl.program_id(1)
    res = pl.program_id(2)
    rows = pl.ds(res, CH, stride=dil) if dil > 1 else slice(None)
    head = (lambda ref, h: ref[h]) if headed else (lambda ref, h: ref[:, h * HD:(h + 1) * HD])
    scale = HD ** -0.5
    lane_head = lax.broadcasted_iota(jnp.int32, (CH, NH * LSE_W), 1) // LSE_W
    lse_tile = None
    if has_prev:
        col = lax.broadcasted_iota(jnp.int32, (CH, 2 * CH), 1)
        no_prev = (ib == 0) & (col < CH)
    for h in range(NH):
        sl = slice(h * HD, (h + 1) * HD)
        kc = head(kc_ref, h)
        vc = head(vc_ref, h)
        kvo_ref[:, sl] = kc
        kvo_ref[:, BW + h * HD:BW + (h + 1) * HD] = vc
        qh = head(q_ref, h).astype(BF16)
        if has_prev:
            kh = jnp.concatenate([head(kp_ref, h), kc], axis=0).astype(BF16)
            vh = jnp.concatenate([head(vp_ref, h), vc], axis=0).astype(BF16)
            bias = jnp.where(no_prev, NEG, bias_ref[h])
        else:
            kh = kc.astype(BF16)
            vh = vc.astype(BF16)
            bias = bias_ref[h][:, CH:]
        s = lax.dot_general(qh, kh, (((1,), (1,)), ((), ())), preferred_element_type=F32) * scale + bias
        m = jnp.max(s, axis=-1, keepdims=True)
        p = jnp.exp(s - m)
        den = jnp.sum(p, axis=-1, keepdims=True)
        acc = jnp.dot(p.astype(BF16), vh, preferred_element_type=F32)
        o_ref[h, rows, :] = acc / den
        lse = m + jnp.log(den)
        lse_tile = lse if h == 0 else jnp.where(lane_head == h, lse, lse_tile)
    lse_ref[rows, :] = lse_tile


def _attn_prompt(zq, rel_bias, gi, kv_prev):
    dil = ATT_DILATIONS[gi]
    rows = BATCH * SEQ
    nb = SEQ // dil // CH
    has_prev = nb > 1
    headed = gi > 0
    bk = jnp.asarray(_prompt_bucket_matrix(dil))

    def rowblk(b, i, r, prev):
        return (b * dil + r) * nb + (jnp.maximum(i - 1, 0) if prev else i)

    def zspec(c, prev=False):
        if headed:
            return pl.BlockSpec((None, NH, CH, HD), lambda b, i, r: (c, 0, rowblk(b, i, r, prev), 0))
        return pl.BlockSpec((CH, BW), lambda b, i, r: (rowblk(b, i, r, prev), COL_QKV[0][c]))

    in_specs = [pl.BlockSpec(memory_space=pltpu.SMEM),
                pl.BlockSpec((CH, 2 * CH), lambda b, i, r: (0, 0)),
                zspec(0), zspec(1), zspec(2)]
    args = [rel_bias, bk, zq, zq, zq]
    if has_prev:
        in_specs += [zspec(1, True), zspec(2, True)]
        args += [zq, zq]
    kv_index = lambda b, i, r: (b, jnp.where(i == nb - 1, r, 0))
    kv_shape = (BATCH * CH, dil * 2 * BW)
    if kv_prev is None:
        kv_spec = pl.BlockSpec((CH, 2 * BW), kv_index)
    else:
        in_specs.append(pl.BlockSpec((CH, 2 * BW), kv_index))
        args.append(kv_prev)
        kv_spec = pl.BlockSpec((2, CH, 2 * BW), lambda b, i, r: (0, *kv_index(b, i, r)))
        kv_shape = (2, *kv_shape)
    return pl.pallas_call(
        functools.partial(_attn_prompt_body, gi=gi, has_prev=has_prev, headed=headed),
        grid=(BATCH, nb, dil),
        in_specs=in_specs,
        out_specs=[pl.BlockSpec((NH, CH * dil, HD), lambda b, i, r: (0, b * nb + i, 0)),
                   pl.BlockSpec((CH * dil, NH * LSE_W), lambda b, i, r: (b * nb + i, 0)),
                   kv_spec],
        out_shape=[jax.ShapeDtypeStruct((NH, rows, HD), F32),
                   jax.ShapeDtypeStruct((rows, NH * LSE_W), F32),
                   jax.ShapeDtypeStruct(kv_shape, F32)],
        scratch_shapes=[pltpu.VMEM((NH, CH, 2 * CH), F32)],
        compiler_params=_cparams(3),
        name=f"attn_prompt_g{gi}",
    )(*args)


def _merge_body(o0, l0, o1, l1, o2, l2, ob_ref, oc_ref, od_ref, gate_ref, wb_ref, wo_ref, x_ref, out_ref):
    heads = []
    for h in range(NH):
        a0, a1, a2 = (l[:, h * LSE_W:h * LSE_W + 1] for l in (l0, l1, l2))
        m = jnp.maximum(jnp.maximum(a0, a1), a2)
        w0, w1, w2 = jnp.exp(a0 - m), jnp.exp(a1 - m), jnp.exp(a2 - m)
        heads.append(((w0 * o0[h] + w1 * o1[h] + w2 * o2[h]) / (w0 + w1 + w2)).astype(BF16))
    oa = jnp.concatenate(heads, axis=1)
    acc = None
    for n, br in enumerate((oa, ob_ref[...], oc_ref[...], od_ref[...])):
        proj = jnp.dot(br, wb_ref[n], preferred_element_type=F32)
        t = gate_ref[:, n * D_MODEL:(n + 1) * D_MODEL].astype(F32) * proj
        acc = t if acc is None else acc + t
    out_ref[...] = x_ref[...] + jnp.dot(acc.astype(BF16), wo_ref[...], preferred_element_type=F32)


def _merge(parts, ob, oc, od, gates, wb, wo, x, *, layer, tm=256):
    m = x.shape[0]
    once = dict(pipeline_mode=pl.Buffered(1))
    ospec = pl.BlockSpec((NH, tm, HD), lambda i: (0, i, 0))
    lspec = pl.BlockSpec((tm, NH * LSE_W), lambda i: (i, 0))
    br_spec = pl.BlockSpec((tm, BW), lambda i: (i, 0))
    row_spec = pl.BlockSpec((tm, D_MODEL), lambda i: (i, 0))
    return pl.pallas_call(
        _merge_body,
        grid=(m // tm,),
        in_specs=[ospec, lspec] * 3 + [br_spec] * 3 + [
            pl.BlockSpec((tm, GATE_W), lambda i: (i, 0)),
            pl.BlockSpec((None, 4, BW, D_MODEL), lambda i: (layer, 0, 0, 0), **once),
            pl.BlockSpec((None, D_MODEL, D_MODEL), lambda i: (layer, 0, 0), **once),
            row_spec],
        out_specs=row_spec,
        out_shape=jax.ShapeDtypeStruct((m, D_MODEL), F32),
        compiler_params=_cparams(1, VMEM_BIG),
        name="merge",
    )(*parts, ob, oc, od, gates, wb, wo, x)


def _attn_sample_body(rb_ref, bc0_ref, bc1_ref, bc2_ref, bn_ref, z_ref, c0_ref, c1_ref, c2_ref,
                      oa_ref, kv0_ref, kv1_ref, kv2_ref, b0_ref, b1_ref, b2_ref, bnew_ref):
    bucket_refs = (bc0_ref, bc1_ref, bc2_ref)
    bias_refs = (b0_ref, b1_ref, b2_ref)

    @pl.when(pl.program_id(0) == 0)
    def _():
        for gi in range(3):
            for h in range(NH):
                rs = slice(h * SROWS, (h + 1) * SROWS)
                bias_refs[gi][rs, :] = _bias_from_buckets(bucket_refs[gi][rs, :], rb_ref, gi * NH + h)
                bnew_ref[gi, rs, :] = _bias_from_buckets(bn_ref[gi, rs, :], rb_ref, gi * NH + h)

    scale = HD ** -0.5
    rows = NH * SROWS
    head_of_row = lax.broadcasted_iota(jnp.int32, (rows, BW), 0) // SROWS
    head_of_lane = lax.broadcasted_iota(jnp.int32, (rows, BW), 1) // HD
    head_mask = head_of_row == head_of_lane
    caches = (c0_ref, c1_ref, c2_ref)
    kv_refs = (kv0_ref, kv1_ref, kv2_ref)
    stats = []
    for gi in range(3):
        cq, ck, cv = COL_QKV[gi]
        q = z_ref[:, cq * BW:(cq + 1) * BW]
        kn = z_ref[:, ck * BW:(ck + 1) * BW]
        vn = z_ref[:, cv * BW:(cv + 1) * BW]
        kv_refs[gi][:, :BW] = kn
        kv_refs[gi][:, BW:] = vn
        qm = jnp.concatenate([q[:, h * HD:(h + 1) * HD] for h in range(NH)], axis=0).astype(BF16)
        kf = caches[gi][...].reshape(SAMPLE_CACHE_ROWS[gi], HD).astype(BF16)
        s = lax.dot_general(qm, kf, (((1,), (1,)), ((), ())), preferred_element_type=F32) * scale + bias_refs[gi][...]
        qrows = jnp.where(head_mask, jnp.concatenate([q] * NH, axis=0), 0.0)
        bias_n = bnew_ref[gi]
        s_new = []
        for tp in range(DEC_SEQ):
            dotp = jnp.sum(qrows * kn[tp:tp + 1, :], axis=-1, keepdims=True)
            s_new.append(dotp * scale + bias_n[:, tp:tp + 1])
        m = jnp.max(s, axis=-1, keepdims=True)
        for sn in s_new:
            m = jnp.maximum(m, sn)
        p = jnp.exp(s - m)
        den = jnp.sum(p, axis=-1, keepdims=True)
        pv = pltpu.roll(p, NH, axis=1).astype(BF16)
        acc = jnp.dot(pv, kf, preferred_element_type=F32)
        acc_n = jnp.zeros((rows, BW), F32)
        for tp, sn in enumerate(s_new):
            pn = jnp.exp(sn - m)
            den = den + pn
            acc_n = acc_n + pn * vn[tp:tp + 1, :]
        acc = acc + jnp.concatenate([acc_n[h * SROWS:(h + 1) * SROWS, h * HD:(h + 1) * HD] for h in range(NH)], axis=0)
        stats.append((m, den, acc))
    mm = jnp.maximum(jnp.maximum(stats[0][0], stats[1][0]), stats[2][0])
    den = jnp.zeros((rows, 1), F32)
    acc = jnp.zeros((rows, HD), F32)
    for m, d, a in stats:
        w = jnp.exp(m - mm)
        den = den + w * d
        acc = acc + w * a
    o = acc / den
    for h in range(NH):
        oa_ref[:, h * HD:(h + 1) * HD] = o[h * SROWS:(h + 1) * SROWS, :].astype(BF16)


def _attn_sample(z, caches, rel_bias, layer):
    tables = _sample_bucket_tables()
    rows = DEC_BATCH * SROWS
    qrows = NH * SROWS
    n0, n1, n2 = SAMPLE_CACHE_ROWS
    c0 = caches[0].reshape(DEPTH, DEC_BATCH, n0, HD)
    c1 = caches[1].reshape(DEPTH, DEC_BATCH, n1, HD)
    c2 = caches[2].reshape(DEPTH, DEC_BATCH, CH, 16 * 8, HD)
    new_tbl = jnp.asarray(np.stack([t[1] for t in tables]))
    kv_spec = pl.BlockSpec((SROWS, 2 * BW), lambda b: (b, 0))
    kv_shape = jax.ShapeDtypeStruct((rows, 2 * BW), F32)
    const2 = lambda b: (0, 0)
    return pl.pallas_call(
        _attn_sample_body,
        grid=(DEC_BATCH,),
        in_specs=[pl.BlockSpec(memory_space=pltpu.SMEM),
                  pl.BlockSpec((qrows, n0), const2), pl.BlockSpec((qrows, n1), const2),
                  pl.BlockSpec((qrows, n2), const2),
                  pl.BlockSpec((3, qrows, CH), lambda b: (0, 0, 0)),
                  pl.BlockSpec((SROWS, IN_WIDTH), lambda b: (b, 0)),
                  pl.BlockSpec((None, None, n0, HD), lambda b: (layer, b, 0, 0)),
                  pl.BlockSpec((None, None, n1, HD), lambda b: (layer, b, 0, 0)),
                  pl.BlockSpec((None, None, CH, 8 * DEC_SEQ, HD), lambda b: (layer, b, 0, 0, 0))],
        out_specs=[pl.BlockSpec((SROWS, BW), lambda b: (b, 0)), kv_spec, kv_spec, kv_spec],
        out_shape=[jax.ShapeDtypeStruct((rows, BW), BF16), kv_shape, kv_shape, kv_shape],
        scratch_shapes=[pltpu.VMEM((qrows, n0), F32), pltpu.VMEM((qrows, n1), F32), pltpu.VMEM((qrows, n2), F32),
                        pltpu.VMEM((3, qrows, CH), F32)],
        compiler_params=_cparams(1),
        name="attn_sample",
    )(rel_bias, jnp.asarray(tables[0][0]), jnp.asarray(tables[1][0]), jnp.asarray(tables[2][0]), new_tbl,
      z, c0, c1, c2)


def _gmlp_body(bu_ref, bv_ref, g_ref, ws_ref, bs_ref, *refs, rows, emit_vn):
    refs = list(refs)
    o_ref = refs.pop(0)
    vn_ref = refs.pop(0) if emit_vn else None
    tril = lax.broadcasted_iota(jnp.int32, (CH, CH), 0) >= lax.broadcasted_iota(jnp.int32, (CH, CH), 1)
    w = [jnp.where(tril, ws_ref[g], 0.0).astype(BF16) for g in range(NH)]
    if rows < CH:
        pad_u, pad_v = refs
        pad_u[...] = jnp.zeros_like(pad_u)
        pad_v[...] = jnp.zeros_like(pad_v)
        pad_u[0:rows, :] = bu_ref[...]
        pad_v[0:rows, :] = bv_ref[...]
        bu_ref, bv_ref = pad_u, pad_v
    for c in range(max(rows // CH, 1)):
        rs = slice(c * CH, (c + 1) * CH)
        u = jax.nn.gelu(bu_ref[rs, :])
        vn = _rms(jax.nn.gelu(bv_ref[rs, :]), g_ref[...])
        if emit_vn:
            vn_ref[...] = vn[:rows]
        for g in range(NH):
            sl = slice(g * HD, (g + 1) * HD)
            mixed = jnp.dot(w[g], vn[:, sl].astype(BF16), preferred_element_type=F32) + bs_ref[:, sl]
            res = (u[:, sl] * mixed).astype(BF16)
            if rows < CH:
                o_ref[:, sl] = res[:rows]
            else:
                o_ref[rs, sl] = res


def _gmlp(z, g_gmlp, w_spatial, b_spatial, *, rows, emit_vn):
    m = z.shape[0]
    bs_full = jnp.repeat(b_spatial.T, HD, axis=1)
    spec = pl.BlockSpec((rows, BW), lambda i: (i, 0))
    out_shape = [jax.ShapeDtypeStruct((m, BW), BF16)]
    out_specs = [spec]
    if emit_vn:
        out_shape.append(jax.ShapeDtypeStruct((m, BW), F32))
        out_specs.append(spec)
    scratch = [] if rows >= CH else [pltpu.VMEM((CH, BW), F32)] * 2
    return pl.pallas_call(
        functools.partial(_gmlp_body, rows=rows, emit_vn=emit_vn),
        grid=(m // rows,),
        in_specs=[pl.BlockSpec((rows, BW), lambda i: (i, COL_BU)),
                  pl.BlockSpec((rows, BW), lambda i: (i, COL_BV)),
                  pl.BlockSpec((1, BW), lambda i: (0, 0)),
                  pl.BlockSpec((NH, CH, CH), lambda i: (0, 0, 0)),
                  pl.BlockSpec((CH, BW), lambda i: (0, 0))],
        out_specs=out_specs,
        out_shape=out_shape,
        scratch_shapes=scratch,
        compiler_params=_cparams(1),
        name="gmlp",
    )(z, z, g_gmlp.reshape(1, -1), w_spatial, bs_full)


def _pool_body(x_ref, prev_ref, wp_ref, sc_ref, o_ref, st_ref, ext_ref, *, rows, n_new, start, zero_first_prev):
    ib = pl.program_id(1)
    prev = prev_ref[...]
    if zero_first_prev:
        prev = jnp.where(ib == 0, 0.0, prev)
    x = x_ref[...]
    ext_ref[0:16, :] = prev
    ext_ref[16:16 + rows, :] = x
    st_ref[...] = ext_ref[pl.ds(n_new + 1, POOL_STATE), :]
    ext = ext_ref[...]
    pos = start + ib * rows + lax.broadcasted_iota(jnp.int32, (rows, 1), 0)
    for gi, win in enumerate(POOL_WINDOWS):
        sl = slice(gi * HD, (gi + 1) * HD)
        s = ext[:, sl]
        k = 1
        while k < win:
            s = s + pltpu.roll(s, k, axis=0)
            k *= 2
        cnt = jnp.minimum(pos + 1, win).astype(F32)
        diff = s[16:] / cnt - x[:, sl]
        y = jnp.dot(diff.astype(BF16), wp_ref[gi].astype(BF16), preferred_element_type=F32)
        o_ref[:, sl] = (y * sc_ref[:, sl]).astype(BF16)


def _pool(z, prev, w_pool, pool_scale, *, nseq, rows, n_new, start, layer=None):
    m = z.shape[0]
    nblk = m // nseq // rows
    if prev is None:
        per16 = rows // 16
        prev_arr = z
        prev_spec = pl.BlockSpec((16, BW), lambda b, i: (jnp.maximum((b * nblk + i) * per16 - 1, 0), COL_CIN))
    else:
        prev_arr = prev
        prev_spec = pl.BlockSpec((None, None, 16, BW), lambda b, i: (layer, b, 0, 0))
    return pl.pallas_call(
        functools.partial(_pool_body, rows=rows, n_new=n_new, start=start, zero_first_prev=prev is None),
        grid=(nseq, nblk),
        in_specs=[pl.BlockSpec((rows, BW), lambda b, i: (b * nblk + i, COL_CIN)),
                  prev_spec,
                  pl.BlockSpec((NH, HD, HD), lambda b, i: (0, 0, 0)),
                  pl.BlockSpec((1, BW), lambda b, i: (0, 0))],
        out_specs=[pl.BlockSpec((rows, BW), lambda b, i: (b * nblk + i, 0)),
                   pl.BlockSpec((None, POOL_STATE, BW), lambda b, i: (b, 0, 0))],
        out_shape=[jax.ShapeDtypeStruct((m, BW), BF16),
                   jax.ShapeDtypeStruct((nseq, POOL_STATE, BW), F32)],
        scratch_shapes=[pltpu.VMEM((16 + rows, BW), F32)],
        compiler_params=_cparams(2),
        name="pool",
    )(z, prev_arr, w_pool, pool_scale.reshape(1, -1))


def _gmlp_sample_body(bu_ref, bv_ref, g_ref, wk_ref, bs_ref, o_ref, vn_ref):
    u = jax.nn.gelu(bu_ref[...])
    vn = _rms(jax.nn.gelu(bv_ref[...]), g_ref[...])
    vn_ref[...] = vn
    mixed = jnp.tile(bs_ref[...], (DEC_BATCH, 1))
    for k in range(DEC_SEQ):
        shifted = vn if k == 0 else pltpu.roll(vn, k, axis=0)
        mixed = mixed + jnp.tile(wk_ref[k], (DEC_BATCH, 1)) * shifted
    o_ref[...] = (u * mixed).astype(BF16)


def _gmlp_sample(z, g_gmlp, w_spatial, b_spatial):
    m = z.shape[0]
    t = np.arange(SROWS)
    live = t < DEC_SEQ
    wk = []
    for k in range(DEC_SEQ):
        ok = live & (t - k >= 0)
        diag = w_spatial[:, np.where(ok, t, 0), np.where(ok, t - k, 0)] * jnp.asarray(ok, F32)
        wk.append(jnp.repeat(diag.T, HD, axis=1))
    bs = jnp.repeat((b_spatial[:, :SROWS] * jnp.asarray(live, F32)).T, HD, axis=1)
    return pl.pallas_call(
        _gmlp_sample_body,
        grid=(1,),
        in_specs=[pl.BlockSpec((m, BW), lambda i: (0, COL_BU)),
                  pl.BlockSpec((m, BW), lambda i: (0, COL_BV)),
                  pl.BlockSpec((1, BW), lambda i: (0, 0)),
                  pl.BlockSpec((DEC_SEQ, SROWS, BW), lambda i: (0, 0, 0)),
                  pl.BlockSpec((SROWS, BW), lambda i: (0, 0))],
        out_specs=[pl.BlockSpec((m, BW), lambda i: (0, 0))] * 2,
        out_shape=[jax.ShapeDtypeStruct((m, BW), BF16), jax.ShapeDtypeStruct((m, BW), F32)],
        compiler_params=_cparams(1),
        name="gmlp_sample",
    )(z, z, g_gmlp.reshape(1, -1), jnp.stack(wk), bs)


POOL_PAD = 16


def _pool_sample_body(x_ref, prev_ref, wp_ref, sc_ref, o_ref, st_ref, ext_ref, *, start):
    per = POOL_PAD + SROWS
    ext_ref[:, 0:POOL_PAD, :] = prev_ref[...]
    ext_ref[:, POOL_PAD:per, :] = x_ref[...]
    st_ref[...] = ext_ref[:, pl.ds(DEC_SEQ + 1, POOL_STATE), :]
    ext = ext_ref[...].reshape(DEC_BATCH * per, BW)
    pos = start + (lax.broadcasted_iota(jnp.int32, (DEC_BATCH * per, 1), 0) % per - POOL_PAD)
    for gi, win in enumerate(POOL_WINDOWS):
        sl = slice(gi * HD, (gi + 1) * HD)
        x = ext[:, sl]
        s = x
        k = 1
        while k < win:
            s = s + pltpu.roll(s, k, axis=0)
            k *= 2
        cnt = jnp.minimum(jnp.maximum(pos, 0) + 1, win).astype(F32)
        y = jnp.dot((s / cnt - x).astype(BF16), wp_ref[gi].astype(BF16), preferred_element_type=F32)
        o_ref[:, :, sl] = (y * sc_ref[:, sl]).reshape(DEC_BATCH, per, HD)[:, POOL_PAD:, :].astype(BF16)


def _pool_sample(z, prev, w_pool, pool_scale, *, start, layer):
    m, zw = z.shape
    z3 = z.reshape(DEC_BATCH, SROWS, zw)
    o, st = pl.pallas_call(
        functools.partial(_pool_sample_body, start=start),
        grid=(1,),
        in_specs=[pl.BlockSpec((DEC_BATCH, SROWS, BW), lambda i: (0, 0, COL_CIN)),
                  pl.BlockSpec((None, DEC_BATCH, POOL_PAD, BW), lambda i: (layer, 0, 0, 0)),
                  pl.BlockSpec((NH, HD, HD), lambda i: (0, 0, 0)),
                  pl.BlockSpec((1, BW), lambda i: (0, 0))],
        out_specs=[pl.BlockSpec((DEC_BATCH, SROWS, BW), lambda i: (0, 0, 0)),
                   pl.BlockSpec((DEC_BATCH, POOL_STATE, BW), lambda i: (0, 0, 0))],
        out_shape=[jax.ShapeDtypeStruct((DEC_BATCH, SROWS, BW), BF16),
                   jax.ShapeDtypeStruct((DEC_BATCH, POOL_STATE, BW), F32)],
        scratch_shapes=[pltpu.VMEM((DEC_BATCH, POOL_PAD + SROWS, BW), F32)],
        compiler_params=_cparams(1),
        name="pool_sample",
    )(z3, prev, w_pool, pool_scale.reshape(1, -1))
    return o.reshape(m, BW), st


def _ret_tables(c_eff, positions):
    lg = np.log1p(-np.power(2.0, -5.0 - np.arange(NH, dtype=np.float64)))
    i = np.arange(CH, dtype=np.float64)
    live = (i < c_eff)
    diff = i[:, None] - i[None, :]
    inner = np.where((diff >= 0) & live[:, None] & live[None, :], np.exp(np.maximum(diff, 0.0)[None] * lg[:, None, None]), 0.0)
    qd = np.where(live[None, :], np.exp((i + 1.0)[None, :] * lg[:, None]), 0.0)
    kd = np.where(live[None, :], np.exp((c_eff - 1.0 - i)[None, :] * lg[:, None]), 0.0)
    chunk = tuple(float(v) for v in np.exp(c_eff * lg))
    qd_full = np.repeat(qd.T, HD, axis=1)
    kd_full = np.repeat(kd.T, HD, axis=1) * (HD ** -0.5)
    half = HD // 2
    inv = ROPE_BASE ** (-np.arange(half, dtype=np.float64) / half)
    ang = np.asarray(positions, np.float64)[:, None] * inv[None, :]
    cosf = np.concatenate([np.cos(ang), np.cos(ang)], axis=1)
    sinf = np.concatenate([-np.sin(ang), np.sin(ang)], axis=1)
    to32 = lambda a: jnp.asarray(a.astype(np.float32))
    return to32(inner), to32(qd_full), to32(kd_full), chunk, to32(cosf), to32(sinf)


def _ret_body(q_ref, k_ref, v_ref, g_ref, cos_ref, sin_ref, inner_ref, qd_ref, kd_ref, gr_ref, *refs,
              nseq, rows, chunk_decay, has_state):
    refs = list(refs)
    s0_ref = refs.pop(0) if has_state else None
    o_ref, sn_ref, s_ref = refs[:3]
    pad_ref = refs[3] if rows != CH else None
    ic = pl.program_id(1)

    @pl.when(ic == 0)
    def _():
        if has_state:
            s_ref[...] = s0_ref[...]
        else:
            s_ref[...] = jnp.zeros_like(s_ref)

    def chunk(ref, b, k):
        if rows == CH:
            return ref[b]
        pad_ref[k] = jnp.zeros((CH, BW), F32)
        pad_ref[k, 0:rows, :] = ref[b]
        return pad_ref[k]

    cosf = cos_ref[...]
    sinf = sin_ref[...]
    for b in range(nseq):
        q = chunk(q_ref, b, 0)
        k = chunk(k_ref, b, 1)
        v = chunk(v_ref, b, 2)
        gate = chunk(g_ref, b, 3)
        for h in range(NH):
            sl = slice(h * HD, (h + 1) * HD)
            qh = q[:, sl]
            kh = k[:, sl]
            rq = qh * cosf + pltpu.roll(qh, HD // 2, axis=1) * sinf
            rk = kh * cosf + pltpu.roll(kh, HD // 2, axis=1) * sinf
            vb = v[:, sl].astype(BF16)
            rqb = rq.astype(BF16)
            state = s_ref[b, h]
            att = lax.dot_general(rqb, (rk * (HD ** -0.5)).astype(BF16), (((1,), (1,)), ((), ())),
                                  preferred_element_type=F32) * inner_ref[h]
            o = (jnp.dot(att.astype(BF16), vb, preferred_element_type=F32)
                 + jnp.dot(rqb, state.astype(BF16), preferred_element_type=F32) * qd_ref[:, sl])
            kdec = (rk * kd_ref[:, sl]).T.astype(BF16)
            new_state = state * chunk_decay[h] + jnp.dot(kdec, vb, preferred_element_type=F32)
            s_ref[b, h] = new_state
            sn_ref[b, h] = new_state
            o = o * lax.rsqrt(jnp.mean(o * o, axis=-1, keepdims=True) + EPS)
            gt = gate[:, sl]
            o_ref[b, :, sl] = (o * gr_ref[:, sl] * (gt * jax.nn.sigmoid(gt)))[:rows].astype(BF16)


def _retention(z, g_ret, state, *, nseq, per_step, rows, c_eff, positions, layer=None):
    m, zw = z.shape
    seq_rows = m // nseq
    nchunk = seq_rows // rows
    inner, qd, kd, chunk_decay, cosf, sinf = _ret_tables(c_eff, positions)
    has_state = state is not None
    z3 = z.reshape(nseq, seq_rows, zw)

    def zspec(col):
        return pl.BlockSpec((per_step, rows, BW), lambda g, i: (g, i, col))

    const2 = lambda g, i: (0, 0)
    in_specs = [zspec(COL_DQ), zspec(COL_DK), zspec(COL_DV), zspec(COL_DG),
                pl.BlockSpec((CH, HD), lambda g, i: (i, 0)), pl.BlockSpec((CH, HD), lambda g, i: (i, 0)),
                pl.BlockSpec((NH, CH, CH), lambda g, i: (0, 0, 0)),
                pl.BlockSpec((CH, BW), const2), pl.BlockSpec((CH, BW), const2), pl.BlockSpec((1, BW), const2)]
    args = [z3, z3, z3, z3, cosf, sinf, inner, qd, kd, g_ret.reshape(1, -1)]
    if has_state:
        in_specs.append(pl.BlockSpec((None, per_step, NH, HD, HD), lambda g, i: (layer, g, 0, 0, 0)))
        args.append(state)
    scratch = [pltpu.VMEM((per_step, NH, HD, HD), F32)]
    if rows != CH:
        scratch.append(pltpu.VMEM((4, CH, BW), F32))
    o, sn = pl.pallas_call(
        functools.partial(_ret_body, nseq=per_step, rows=rows, chunk_decay=chunk_decay, has_state=has_state),
        grid=(nseq // per_step, nchunk),
        in_specs=in_specs,
        out_specs=[pl.BlockSpec((per_step, rows, BW), lambda g, i: (g, i, 0)),
                   pl.BlockSpec((per_step, NH, HD, HD), lambda g, i: (g, 0, 0, 0))],
        out_shape=[jax.ShapeDtypeStruct((nseq, seq_rows, BW), BF16),
                   jax.ShapeDtypeStruct((nseq, NH, HD, HD), F32)],
        scratch_shapes=scratch,
        compiler_params=_cparams(2),
        name="retention",
    )(*args)
    return o.reshape(m, BW), sn


def kernel(x_prompt, x_sample, cache_attn_kv_w128, cache_attn_kv_w512, cache_attn_kv_w2048, state_pool, state_ret, rel_bias, g_ffn1, w_ffn1_gate, w_ffn1_up, w_ffn1_down, g_mix, w_in, g_gmlp, w_spatial, b_spatial, w_pool, pool_scale, g_ret, w_branch, w_out, g_ffn2, w_ffn2_gate, w_ffn2_up, w_ffn2_down, g_final):
    caches = (cache_attn_kv_w128, cache_attn_kv_w512, cache_attn_kv_w2048)
    xp = x_prompt.reshape(BATCH * SEQ, D_MODEL)
    xs = jnp.pad(x_sample, ((0, 0), (0, SROWS - DEC_SEQ), (0, 0))).reshape(DEC_BATCH * SROWS, D_MODEL)
    pool_state = jnp.pad(state_pool, ((0, 0), (0, 0), (1, 0), (0, 0)))
    tm_p, tm_s = 1024, DEC_BATCH * SROWS
    sample_pos = PAST_LEN + np.arange(CH)
    wb = w_branch.astype(BF16)
    wo = w_out.astype(BF16)
    gate_col0 = N_PLAIN * BW // 1024
    qkv1, qkv2 = COL_QKV[1][0], COL_QKV[2][0]

    kv_p = [None, None, None]
    kv_s = [[], [], []]
    pool_p, pool_s, ret_p, ret_s, gv_s = [], [], [], [], []
    yp = ys = None
    for l in range(DEPTH):
        last = l == DEPTH - 1

        xs, hs, *w1 = _ffn(xs, g_ffn1[l], (w_ffn1_gate, w_ffn1_up, w_ffn1_down), g_mix[l], layer=l,
                           emit_x=True, post_dtype=BF16, tm=tm_s)
        zs, gs, win = _in_proj_convert(hs, w_in, layer=l)
        oa, k0, k1, k2 = _attn_sample(zs, caches, rel_bias, l)
        for gi, kv in enumerate((k0, k1, k2)):
            kv_s[gi].append(kv.reshape(DEC_BATCH, SROWS, 2, NH, HD)[:, :DEC_SEQ])
        ob, vn = _gmlp_sample(zs, g_gmlp[l], w_spatial[l], b_spatial[l])
        gv_s.append(vn.reshape(DEC_BATCH, SROWS, BW)[:, :DEC_SEQ])
        oc, pn = _pool_sample(zs, pool_state, w_pool[l], pool_scale[l], start=PAST_LEN, layer=l)
        od, rn = _retention(zs, g_ret[l], state_ret, nseq=DEC_BATCH, per_step=4, rows=SROWS, c_eff=DEC_SEQ,
                            positions=sample_pos, layer=l)
        pool_s.append(pn)
        ret_s.append(rn)
        mix = _branch_merge(oa, ob, oc, od, gs, wb, layer=l, tm=tm_s)
        xs = _matmul(mix, wo, xs, layer=l, tm=tm_s, tn=512, name="out_proj")
        if last:
            ys, *w2 = _ffn(xs, g_ffn2[l], (w_ffn2_gate, w_ffn2_up, w_ffn2_down), g_final, layer=l,
                           emit_x=False, post_dtype=F32, tm=tm_s)
        else:
            xs, *w2 = _ffn(xs, g_ffn2[l], (w_ffn2_gate, w_ffn2_up, w_ffn2_down), g_final, layer=l,
                           emit_x=True, post_dtype=None, tm=tm_s)

        xp, hp = _ffn(xp, g_ffn1[l], w1, g_mix[l], emit_x=True, post_dtype=BF16, tm=tm_p)
        zp = _matmul(hp, win, n=N_PLAIN * BW, tm=2048, tn=1024, name="in_proj")
        gp = _matmul(hp, win, col0=gate_col0, n=GATE_W, gate=True, tm=2048, tn=1024, name="in_proj_gate")
        zq = (zp,
              _matmul_deint(hp, win, col0=qkv1, dil=ATT_DILATIONS[1], name="in_proj_g1"),
              _matmul_deint(hp, win, col0=qkv2, dil=ATT_DILATIONS[2], name="in_proj_g2"))
        parts = []
        for gi in range(3):
            o, lse, kv_p[gi] = _attn_prompt(zq[gi], rel_bias, gi, kv_p[gi])
            parts += [o, lse]
        ob, = _gmlp(zp, g_gmlp[l], w_spatial[l], b_spatial[l], rows=4 * CH, emit_vn=False)
        oc, pn = _pool(zp, None, w_pool[l], pool_scale[l], nseq=BATCH, rows=512, n_new=512, start=0)
        od, rn = _retention(zp, g_ret[l], None, nseq=BATCH, per_step=BATCH, rows=CH, c_eff=CH,
                            positions=np.arange(SEQ))
        pool_p.append(pn)
        ret_p.append(rn)
        xp = _merge(parts, ob, oc, od, gp, wb, wo, xp, layer=l)
        if last:
            yp, = _ffn(xp, g_ffn2[l], w2, g_final, emit_x=False, post_dtype=F32, tm=tm_p)
        else:
            xp, = _ffn(xp, g_ffn2[l], w2, g_final, emit_x=True, post_dtype=None, tm=tm_p)

    y_prompt = yp.reshape(BATCH, SEQ, D_MODEL)
    y_sample = ys.reshape(DEC_BATCH, SROWS, D_MODEL)[:, :DEC_SEQ]
    kv_p = [kv.reshape(DEPTH, BATCH, ATT_WINDOWS[gi], 2, NH, HD) for gi, kv in enumerate(kv_p)]
    return (y_prompt, y_sample,
            kv_p[0], kv_p[1], kv_p[2],
            jnp.stack(kv_s[0]), jnp.stack(kv_s[1]), jnp.stack(kv_s[2]),
            jnp.stack(pool_p), jnp.stack(pool_s),
            jnp.stack(ret_p), jnp.stack(ret_s),
            jnp.stack(gv_s))
```

```python
import functools

import numpy as np
import jax
import jax.numpy as jnp
from jax import lax
from jax.experimental import pallas as pl
from jax.experimental.pallas import tpu as pltpu

F32 = jnp.float32
BF16 = jnp.bfloat16

D_MODEL = 2048
BATCH = 4
SEQ = 2048
DEPTH = 2
DEC_BATCH = 32
DEC_SEQ = 4
PAST_LEN = 8192
D_FF = 5632
IN_WIDTH = 16384
EPS = 1e-6
BW = 512
HD = 128
NH = 4
ATT_WINDOWS = (128, 512, 2048)
ATT_DILATIONS = (1, 4, 16)
NKEY = 129
REL_BUCKETS = 32
REL_MAX_DISTANCE = 2048
POOL_WINDOWS = (2, 4, 8, 16)
POOL_STATE = 15
ROPE_BASE = 10000.0
SROWS = 8
CH = 128
NEG = -1e30
LSE_W = 32

W_IN_ORDER = (0, 3, 6) + tuple(range(9, 32)) + (1, 4, 7) + (2, 5, 8)
COL_QKV = ((0, 1, 2), (26, 27, 28), (29, 30, 31))
COL_BU, COL_BV, COL_CIN = 3, 4, 5
COL_DQ, COL_DK, COL_DV, COL_DG = 6, 7, 8, 9
N_PLAIN = 10
N_GATE = 16
GATE_W = N_GATE * BW
ZBLK = IN_WIDTH // BW

VMEM_BIG = 56 * 1024 * 1024


def _cparams(n_axes, vmem=None):
    return pltpu.CompilerParams(dimension_semantics=("arbitrary",) * n_axes, vmem_limit_bytes=vmem)


def _rms(x, g):
    return x * lax.rsqrt(jnp.mean(x * x, axis=-1, keepdims=True) + EPS) * g


def _ffn_body(x_ref, gpre_ref, *refs, emit_x, emit_post, convert, nf, tf, row_split):
    refs = list(refs)
    if convert:
        wg_ref, wu_ref, wd_ref, gpost_ref = refs[:4]
        refs = refs[4:]
    else:
        wgu_ref, wd_ref, gpost_ref = refs[:3]
        refs = refs[3:]
    n_act = int(emit_x) + int(emit_post)
    outs = refs[:n_act]
    scratch = refs[n_act + (2 if convert else 0):]
    xn_ref = scratch[0]
    acc_ref = outs[0]
    f = pl.program_id(1)

    @pl.when(f == 0)
    def _():
        x = x_ref[...]
        xn_ref[...] = _rms(x, gpre_ref[...]).astype(BF16)
        acc_ref[...] = x

    if convert:
        wgu = jnp.concatenate([wg_ref[...].astype(BF16), wu_ref[...].astype(BF16)], axis=1)
        wd = wd_ref[...].astype(BF16)
        refs[n_act][...] = wgu
        refs[n_act + 1][...] = wd
    else:
        wgu, wd = wgu_ref[...], wd_ref[...]
    rows = xn_ref.shape[0] // row_split
    for part in range(row_split):
        rs = slice(part * rows, (part + 1) * rows)
        r = jnp.dot(xn_ref[rs, :], wgu, preferred_element_type=F32)
        g, u = r[:, :tf], r[:, tf:]
        h = (g * jax.nn.sigmoid(g) * (0.5 * u)).astype(BF16)
        acc_ref[rs, :] += jnp.dot(h, wd, preferred_element_type=F32)

    if emit_post:
        @pl.when(f == nf - 1)
        def _():
            outs[-1][...] = _rms(acc_ref[...], gpost_ref[...]).astype(outs[-1].dtype)


def _ffn(x, g_pre, weights, g_post, *, layer=None, emit_x, post_dtype, tm, tf=512):
    m = x.shape[0]
    nf = D_FF // tf
    emit_post = post_dtype is not None
    convert = layer is not None
    assert emit_x or post_dtype == F32, "the first output block is the f32 accumulator"
    once = dict(pipeline_mode=pl.Buffered(1)) if tm > 512 else {}
    row_spec = pl.BlockSpec((tm, D_MODEL), lambda i, f: (i, 0))
    row_out_spec = pl.BlockSpec((tm, D_MODEL), lambda i, f: (i, 0), **once)
    vec_spec = pl.BlockSpec((1, D_MODEL), lambda i, f: (0, 0))
    out_shape, out_specs = [], []
    if emit_x:
        out_shape.append(jax.ShapeDtypeStruct((m, D_MODEL), F32))
        out_specs.append(row_out_spec)
    if emit_post:
        out_shape.append(jax.ShapeDtypeStruct((m, D_MODEL), post_dtype))
        out_specs.append(row_out_spec)
    gu_spec = pl.BlockSpec((D_MODEL, 2 * tf), lambda i, f: (0, f))
    down_spec = pl.BlockSpec((tf, D_MODEL), lambda i, f: (f, 0))
    if convert:
        assert m == tm, "the bf16 weights are written once, by a single row tile"
        w_specs = [pl.BlockSpec((None, D_MODEL, tf), lambda i, f: (layer, 0, f)),
                   pl.BlockSpec((None, D_MODEL, tf), lambda i, f: (layer, 0, f)),
                   pl.BlockSpec((None, tf, D_MODEL), lambda i, f: (layer, f, 0))]
        out_shape += [jax.ShapeDtypeStruct((D_MODEL, 2 * D_FF), BF16), jax.ShapeDtypeStruct((D_FF, D_MODEL), BF16)]
        out_specs += [gu_spec, down_spec]
    else:
        w_specs = [gu_spec, down_spec]
    scratch = [pltpu.VMEM((tm, D_MODEL), BF16)]
    return pl.pallas_call(
        functools.partial(_ffn_body, emit_x=emit_x, emit_post=emit_post, convert=convert, nf=nf, tf=tf,
                          row_split=max(tm // 512, 1)),
        grid=(m // tm, nf),
        in_specs=[row_spec, vec_spec, *w_specs, vec_spec],
        out_specs=out_specs,
        out_shape=out_shape,
        scratch_shapes=scratch,
        compiler_params=_cparams(2, VMEM_BIG),
        name="ffn",
    )(x, g_pre.reshape(1, -1), *weights, g_post.reshape(1, -1))


def _mm_body(a_ref, b_ref, o_ref):
    o_ref[...] = jnp.dot(a_ref[...], b_ref[...], preferred_element_type=F32)


def _mm_res_body(a_ref, b_ref, r_ref, o_ref):
    o_ref[...] = r_ref[...] + jnp.dot(a_ref[...], b_ref[...], preferred_element_type=F32)


def _mm_gate_body(a_ref, b_ref, o_ref):
    o_ref[...] = jax.nn.sigmoid(jnp.dot(a_ref[...], b_ref[...], preferred_element_type=F32)).astype(BF16)


def _matmul(a, b, res=None, *, layer=None, col0=0, n=None, gate=False, tm, tn, name):
    m, k = a.shape
    n = b.shape[-1] if n is None else n
    if layer is None:
        b_spec = pl.BlockSpec((k, tn), lambda i, j: (0, col0 + j))
    else:
        b_spec = pl.BlockSpec((None, k, tn), lambda i, j: (layer, 0, col0 + j))
    in_specs = [pl.BlockSpec((tm, k), lambda i, j: (i, 0)), b_spec]
    args = [a, b]
    body = _mm_gate_body if gate else _mm_body
    if res is not None:
        in_specs.append(pl.BlockSpec((tm, tn), lambda i, j: (i, j)))
        args.append(res)
        body = _mm_res_body
    return pl.pallas_call(
        body,
        grid=(m // tm, n // tn),
        in_specs=in_specs,
        out_specs=pl.BlockSpec((tm, tn), lambda i, j: (i, j)),
        out_shape=jax.ShapeDtypeStruct((m, n), BF16 if gate else F32),
        compiler_params=_cparams(2, VMEM_BIG),
        name=name,
    )(*args)


def _mm_deint_body(a_ref, b_ref, o_ref, s_ref, *, dil):
    r = jnp.dot(a_ref[...], b_ref[...], preferred_element_type=F32)
    for h in range(NH):
        s_ref[h] = r[:, h * HD:(h + 1) * HD]
    n = SEQ // dil
    for h in range(NH):
        for res in range(dil):
            o_ref[h, pl.ds(res * n, n), :] = s_ref[h, pl.ds(res, n, stride=dil), :]


def _w_in_source_block(j, where=jnp.where):
    return where(j < 3, 3 * j, where(j < 26, j + 6, where(j < 29, 3 * (j - 26) + 1, 3 * (j - 29) + 2)))


assert tuple(_w_in_source_block(np.arange(ZBLK), np.where)) == W_IN_ORDER


def _in_proj_convert_body(a_ref, b_ref, z_ref, gate_ref, w_ref):
    j = pl.program_id(0)
    w = b_ref[...].astype(BF16)
    w_ref[...] = w
    r = jnp.dot(a_ref[...], w, preferred_element_type=F32)
    z_ref[...] = r

    @pl.when((j >= N_PLAIN) & (j < N_PLAIN + N_GATE))
    def _():
        gate_ref[...] = jax.nn.sigmoid(r).astype(BF16)


def _in_proj_convert(a, w_in, *, layer):
    m, k = a.shape
    return pl.pallas_call(
        _in_proj_convert_body,
        grid=(ZBLK,),
        in_specs=[pl.BlockSpec((m, k), lambda j: (0, 0)),
                  pl.BlockSpec((None, k, BW), lambda j: (layer, 0, _w_in_source_block(j)))],
        out_specs=[pl.BlockSpec((m, BW), lambda j: (0, j)),
                   pl.BlockSpec((m, BW), lambda j: (0, jnp.clip(j - N_PLAIN, 0, N_GATE - 1))),
                   pl.BlockSpec((k, BW), lambda j: (0, j))],
        out_shape=[jax.ShapeDtypeStruct((m, IN_WIDTH), F32),
                   jax.ShapeDtypeStruct((m, GATE_W), BF16),
                   jax.ShapeDtypeStruct((k, IN_WIDTH), BF16)],
        compiler_params=_cparams(1, VMEM_BIG),
        name="in_proj_convert",
    )(a, w_in)


def _matmul_deint(a, b, *, col0, dil, name):
    m, k = a.shape
    return pl.pallas_call(
        functools.partial(_mm_deint_body, dil=dil),
        grid=(BATCH, 3),
        in_specs=[pl.BlockSpec((SEQ, k), lambda b, j: (b, 0)),
                  pl.BlockSpec((k, BW), lambda b, j: (0, col0 + j))],
        out_specs=pl.BlockSpec((None, NH, SEQ, HD), lambda b, j: (j, 0, b, 0)),
        out_shape=jax.ShapeDtypeStruct((3, NH, m, HD), F32),
        scratch_shapes=[pltpu.VMEM((NH, SEQ, HD), F32)],
        compiler_params=_cparams(2, VMEM_BIG),
        name=name,
    )(a, b)


def _branch_body(oa_ref, ob_ref, oc_ref, od_ref, gate_ref, wb_ref, o_ref):
    acc = None
    for n, br in enumerate((oa_ref, ob_ref, oc_ref, od_ref)):
        proj = jnp.dot(br[...], wb_ref[n], preferred_element_type=F32)
        t = gate_ref[:, n * D_MODEL:(n + 1) * D_MODEL].astype(F32) * proj
        acc = t if acc is None else acc + t
    o_ref[...] = acc.astype(BF16)


def _branch_merge(oa, ob, oc, od, gates, wb, *, layer, tm):
    m = oa.shape[0]
    br_spec = pl.BlockSpec((tm, BW), lambda i: (i, 0))
    return pl.pallas_call(
        _branch_body,
        grid=(m // tm,),
        in_specs=[br_spec] * 4 + [pl.BlockSpec((tm, GATE_W), lambda i: (i, 0)),
                                  pl.BlockSpec((None, 4, BW, D_MODEL), lambda i: (layer, 0, 0, 0))],
        out_specs=pl.BlockSpec((tm, D_MODEL), lambda i: (i, 0)),
        out_shape=jax.ShapeDtypeStruct((m, D_MODEL), BF16),
        compiler_params=_cparams(1, VMEM_BIG),
        name="branch_merge",
    )(oa, ob, oc, od, gates, wb)


def _t5_buckets(dist):
    max_exact = REL_BUCKETS // 2
    d = np.maximum(dist, 1).astype(np.float32)
    large = max_exact + (np.log(d / max_exact) / np.log(REL_MAX_DISTANCE / max_exact)
                         * (REL_BUCKETS - max_exact)).astype(np.int32)
    large = np.minimum(large, REL_BUCKETS - 1)
    return np.where(dist < max_exact, dist, large).astype(np.int32)


def _bucket_of_step(dil):
    return _t5_buckets(dil * np.arange(NKEY))


def _prompt_bucket_matrix(dil):
    iq = np.arange(CH)[:, None]
    col = np.arange(2 * CH)[None, :]
    j = iq + CH - col
    valid = (j >= 0) & (j < NKEY)
    return np.where(valid, _bucket_of_step(dil)[np.clip(j, 0, NKEY - 1)], -1).astype(np.int32)


SAMPLE_CACHE_ROWS = (8 * ATT_WINDOWS[0], 8 * ATT_WINDOWS[1], 8 * DEC_SEQ * CH)


def _sample_bucket_tables():
    h = np.repeat(np.arange(NH), SROWS)[:, None]
    t = np.tile(np.arange(SROWS), NH)[:, None]
    lane = np.arange(CH)[None, :]
    tables = []
    for gi, dil in enumerate(ATT_DILATIONS):
        bos = _bucket_of_step(dil)
        flat = np.arange(SAMPLE_CACHE_ROWS[gi])[None, :]
        head, is_k = flat % NH, (flat // NH) % 2 == 0
        if gi < 2:
            w = flat // 8
            steps = ATT_WINDOWS[gi] + t - w
            valid = (steps % dil == 0) & (steps // dil < NKEY)
            j = steps // dil
        else:
            res, i = (flat // 8) % DEC_SEQ, flat // (8 * DEC_SEQ)
            valid = res == t
            j = np.broadcast_to(CH - i, valid.shape)
        valid = valid & is_k & (head == h) & (t < DEC_SEQ)
        cache_tbl = np.where(valid, bos[np.clip(j, 0, NKEY - 1)], -1)
        jn = t - lane
        validn = (t < DEC_SEQ) & (lane < DEC_SEQ) & (jn >= 0) & ((dil == 1) | (jn == 0))
        new_tbl = np.where(validn, bos[np.clip(jn, 0, NKEY - 1)], -1)
        tables.append((cache_tbl.astype(np.int32), new_tbl.astype(np.int32)))
    return tables


def _bias_from_buckets(bk, rb_ref, col):
    out = jnp.full(bk.shape, NEG, F32)
    for b in range(REL_BUCKETS):
        out = jnp.where(bk == b, rb_ref[b, col], out)
    return out


ATT_SUB = 4


def _attn_prompt_body(rb_ref, bk_ref, q_ref, k_ref, v_ref, *refs, gi, headed):
    dil = ATT_DILATIONS[gi]
    has_prev = gi == 0
    if has_prev:
        kp_ref, vp_ref = refs[:2]
        refs = refs[2:]
    o_ref, lse_ref, kvo_ref, bias_ref = refs[-4:]
    if len(refs) == 5:
        kvo_ref[0] = refs[0][...]
        kvo_ref = kvo_ref.at[1]
    step = pl.program_id(1)

    @pl.when((pl.program_id(0) == 0) & (step == 0))
    def _():
        bk = bk_ref[...]
        for h in range(NH):
            bias_ref[h] = _bias_from_buckets(bk, rb_ref, gi * NH + h)

    def tile(ref, h, t):
        rs = slice(t * CH, (t + 1) * CH)
        return ref[h, rs, :] if headed else ref[rs, h * HD:(h + 1) * HD]

    scale = HD ** -0.5
    lane_head = lax.broadcasted_iota(jnp.int32, (CH, NH * LSE_W), 1) // LSE_W
    if has_prev:
        col = lax.broadcasted_iota(jnp.int32, (CH, 2 * CH), 1)
        no_prev = (step == 0) & (col < CH)
    for t in range(ATT_SUB):
        if gi == 0:
            dst = slice(t * CH, (t + 1) * CH)
        elif gi == 1:
            dst = pl.ds(t * CH * dil + step, CH, stride=dil)
        else:
            dst = pl.ds(step * ATT_SUB + t, CH, stride=dil)
        keep = gi == 2 or t == ATT_SUB - 1
        kv_row0 = (t if gi == 2 else 0) * 2 * NH
        lse_tile = None
        for h in range(NH):
            kc = tile(k_ref, h, t)
            vc = tile(v_ref, h, t)
            if keep:
                kvo_ref[:, kv_row0 + h, :] = kc
                kvo_ref[:, kv_row0 + NH + h, :] = vc
            qh = tile(q_ref, h, t).astype(BF16)
            if gi < 2 and (t > 0 or has_prev):
                if t > 0:
                    kp, vp, bias = tile(k_ref, h, t - 1), tile(v_ref, h, t - 1), bias_ref[h]
                else:
                    kp, vp = kp_ref[:, h * HD:(h + 1) * HD], vp_ref[:, h * HD:(h + 1) * HD]
                    bias = jnp.where(no_prev, NEG, bias_ref[h])
                kh = jnp.concatenate([kp, kc], axis=0).astype(BF16)
                vh = jnp.concatenate([vp, vc], axis=0).astype(BF16)
            else:
                kh = kc.astype(BF16)
                vh = vc.astype(BF16)
                bias = bias_ref[h][:, CH:]
            s = lax.dot_general(qh, kh, (((1,), (1,)), ((), ())), preferred_element_type=F32) * scale + bias
            m = jnp.max(s, axis=-1, keepdims=True)
            p = jnp.exp(s - m)
            den = jnp.sum(p, axis=-1, keepdims=True)
            acc = jnp.dot(p.astype(BF16), vh, preferred_element_type=F32)
            o_ref[h, dst, :] = acc / den
            lse = m + jnp.log(den)
            lse_tile = lse if h == 0 else jnp.where(lane_head == h, lse, lse_tile)
        lse_ref[dst, :] = lse_tile


def _attn_prompt(zq, rel_bias, gi, kv_prev):
    dil = ATT_DILATIONS[gi]
    rows = BATCH * SEQ
    nsteps = SEQ // (ATT_SUB * CH)
    assert SEQ // dil // CH in (1, ATT_SUB, ATT_SUB * nsteps)
    headed = gi > 0
    bk = jnp.asarray(_prompt_bucket_matrix(dil))
    span = ATT_SUB * CH

    def zspec(c):
        if headed:
            return pl.BlockSpec((None, NH, span, HD), lambda b, s: (c, 0, b * nsteps + s, 0))
        return pl.BlockSpec((span, BW), lambda b, s: (b * nsteps + s, COL_QKV[0][c]))

    in_specs = [pl.BlockSpec(memory_space=pltpu.SMEM),
                pl.BlockSpec((CH, 2 * CH), lambda b, s: (0, 0)),
                zspec(0), zspec(1), zspec(2)]
    args = [rel_bias, bk, zq, zq, zq]
    if gi == 0:
        above = lambda c: pl.BlockSpec(
            (CH, BW), lambda b, s: (jnp.maximum((b * nsteps + s) * ATT_SUB - 1, 0), COL_QKV[0][c]))
        in_specs += [above(1), above(2)]
        args += [zq, zq]
        out_rows, out_index = span, (lambda b, s: b * nsteps + s)
    else:
        out_rows, out_index = SEQ, (lambda b, s: b)
    n_res = ATT_SUB if gi == 2 else 1
    kv_blk = (CH, n_res * 2 * NH, HD)
    kv_index = lambda b, s: (b, 0, s if gi else 0, 0)
    kv_shape = (BATCH, CH, dil * 2 * NH, HD)
    if kv_prev is None:
        kv_spec = pl.BlockSpec((None, *kv_blk), kv_index)
    else:
        in_specs.append(pl.BlockSpec((None, *kv_blk), kv_index))
        args.append(kv_prev)
        kv_spec = pl.BlockSpec((2, None, *kv_blk), lambda b, s: (0, *kv_index(b, s)))
        kv_shape = (2, *kv_shape)
    return pl.pallas_call(
        functools.partial(_attn_prompt_body, gi=gi, headed=headed),
        grid=(BATCH, nsteps),
        in_specs=in_specs,
        out_specs=[pl.BlockSpec((NH, out_rows, HD), lambda b, s: (0, out_index(b, s), 0)),
                   pl.BlockSpec((out_rows, NH * LSE_W), lambda b, s: (out_index(b, s), 0)),
                   kv_spec],
        out_shape=[jax.ShapeDtypeStruct((NH, rows, HD), F32),
                   jax.ShapeDtypeStruct((rows, NH * LSE_W), F32),
                   jax.ShapeDtypeStruct(kv_shape, F32)],
        scratch_shapes=[pltpu.VMEM((NH, CH, 2 * CH), F32)],
        compiler_params=_cparams(2),
        name=f"attn_prompt_g{gi}",
    )(*args)


def _merge_body(o0, l0, o1, l1, o2, l2, ob_ref, oc_ref, od_ref, gate_ref, wb_ref, wo_ref, x_ref, out_ref):
    heads = []
    for h in range(NH):
        a0, a1, a2 = (l[:, h * LSE_W:h * LSE_W + 1] for l in (l0, l1, l2))
        m = jnp.maximum(jnp.maximum(a0, a1), a2)
        w0, w1, w2 = jnp.exp(a0 - m), jnp.exp(a1 - m), jnp.exp(a2 - m)
        heads.append(((w0 * o0[h] + w1 * o1[h] + w2 * o2[h]) / (w0 + w1 + w2)).astype(BF16))
    oa = jnp.concatenate(heads, axis=1)
    acc = None
    for n, br in enumerate((oa, ob_ref[...], oc_ref[...], od_ref[...])):
        proj = jnp.dot(br, wb_ref[n], preferred_element_type=F32)
        t = gate_ref[:, n * D_MODEL:(n + 1) * D_MODEL].astype(F32) * proj
        acc = t if acc is None else acc + t
    out_ref[...] = x_ref[...] + jnp.dot(acc.astype(BF16), wo_ref[...], preferred_element_type=F32)


def _merge(parts, ob, oc, od, gates, wb, wo, x, *, layer, tm=256):
    m = x.shape[0]
    once = dict(pipeline_mode=pl.Buffered(1))
    ospec = pl.BlockSpec((NH, tm, HD), lambda i: (0, i, 0))
    lspec = pl.BlockSpec((tm, NH * LSE_W), lambda i: (i, 0))
    br_spec = pl.BlockSpec((tm, BW), lambda i: (i, 0))
    row_spec = pl.BlockSpec((tm, D_MODEL), lambda i: (i, 0))
    return pl.pallas_call(
        _merge_body,
        grid=(m // tm,),
        in_specs=[ospec, lspec] * 3 + [br_spec] * 3 + [
            pl.BlockSpec((tm, GATE_W), lambda i: (i, 0)),
            pl.BlockSpec((None, 4, BW, D_MODEL), lambda i: (layer, 0, 0, 0), **once),
            pl.BlockSpec((None, D_MODEL, D_MODEL), lambda i: (layer, 0, 0), **once),
            row_spec],
        out_specs=row_spec,
        out_shape=jax.ShapeDtypeStruct((m, D_MODEL), F32),
        compiler_params=_cparams(1, VMEM_BIG),
        name="merge",
    )(*parts, ob, oc, od, gates, wb, wo, x)


def _attn_sample_body(rb_ref, bc0_ref, bc1_ref, bc2_ref, bn_ref, z_ref, c0_ref, c1_ref, c2_ref,
                      oa_ref, kv0_ref, kv1_ref, kv2_ref, b0_ref, b1_ref, b2_ref, bnew_ref):
    bucket_refs = (bc0_ref, bc1_ref, bc2_ref)
    bias_refs = (b0_ref, b1_ref, b2_ref)

    @pl.when(pl.program_id(0) == 0)
    def _():
        for gi in range(3):
            for h in range(NH):
                rs = slice(h * SROWS, (h + 1) * SROWS)
                bias_refs[gi][rs, :] = _bias_from_buckets(bucket_refs[gi][rs, :], rb_ref, gi * NH + h)
                bnew_ref[gi, rs, :] = _bias_from_buckets(bn_ref[gi, rs, :], rb_ref, gi * NH + h)

    scale = HD ** -0.5
    rows = NH * SROWS
    head_of_row = lax.broadcasted_iota(jnp.int32, (rows, BW), 0) // SROWS
    head_of_lane = lax.broadcasted_iota(jnp.int32, (rows, BW), 1) // HD
    head_mask = head_of_row == head_of_lane
    caches = (c0_ref, c1_ref, c2_ref)
    kv_refs = (kv0_ref, kv1_ref, kv2_ref)
    stats = []
    for gi in range(3):
        cq, ck, cv = COL_QKV[gi]
        q = z_ref[:, cq * BW:(cq + 1) * BW]
        kn = z_ref[:, ck * BW:(ck + 1) * BW]
        vn = z_ref[:, cv * BW:(cv + 1) * BW]
        kv_refs[gi][:, :BW] = kn
        kv_refs[gi][:, BW:] = vn
        qm = jnp.concatenate([q[:, h * HD:(h + 1) * HD] for h in range(NH)], axis=0).astype(BF16)
        kf = caches[gi][...].reshape(SAMPLE_CACHE_ROWS[gi], HD).astype(BF16)
        s = lax.dot_general(qm, kf, (((1,), (1,)), ((), ())), preferred_element_type=F32) * scale + bias_refs[gi][...]
        qrows = jnp.where(head_mask, jnp.concatenate([q] * NH, axis=0), 0.0)
        bias_n = bnew_ref[gi]
        s_new = []
        for tp in range(DEC_SEQ):
            dotp = jnp.sum(qrows * kn[tp:tp + 1, :], axis=-1, keepdims=True)
            s_new.append(dotp * scale + bias_n[:, tp:tp + 1])
        m = jnp.max(s, axis=-1, keepdims=True)
        for sn in s_new:
            m = jnp.maximum(m, sn)
        p = jnp.exp(s - m)
        den = jnp.sum(p, axis=-1, keepdims=True)
        pv = pltpu.roll(p, NH, axis=1).astype(BF16)
        acc = jnp.dot(pv, kf, preferred_element_type=F32)
        acc_n = jnp.zeros((rows, BW), F32)
        for tp, sn in enumerate(s_new):
            pn = jnp.exp(sn - m)
            den = den + pn
            acc_n = acc_n + pn * vn[tp:tp + 1, :]
        acc = acc + jnp.concatenate([acc_n[h * SROWS:(h + 1) * SROWS, h * HD:(h + 1) * HD] for h in range(NH)], axis=0)
        stats.append((m, den, acc))
    mm = jnp.maximum(jnp.maximum(stats[0][0], stats[1][0]), stats[2][0])
    den = jnp.zeros((rows, 1), F32)
    acc = jnp.zeros((rows, HD), F32)
    for m, d, a in stats:
        w = jnp.exp(m - mm)
        den = den + w * d
        acc = acc + w * a
    o = acc / den
    for h in range(NH):
        oa_ref[:, h * HD:(h + 1) * HD] = o[h * SROWS:(h + 1) * SROWS, :].astype(BF16)


def _attn_sample(z, caches, rel_bias, layer):
    tables = _sample_bucket_tables()
    rows = DEC_BATCH * SROWS
    qrows = NH * SROWS
    n0, n1, n2 = SAMPLE_CACHE_ROWS
    c0 = caches[0].reshape(DEPTH, DEC_BATCH, n0, HD)
    c1 = caches[1].reshape(DEPTH, DEC_BATCH, n1, HD)
    c2 = caches[2].reshape(DEPTH, DEC_BATCH, CH, 16 * 8, HD)
    new_tbl = jnp.asarray(np.stack([t[1] for t in tables]))
    kv_spec = pl.BlockSpec((SROWS, 2 * BW), lambda b: (b, 0))
    kv_shape = jax.ShapeDtypeStruct((rows, 2 * BW), F32)
    const2 = lambda b: (0, 0)
    return pl.pallas_call(
        _attn_sample_body,
        grid=(DEC_BATCH,),
        in_specs=[pl.BlockSpec(memory_space=pltpu.SMEM),
                  pl.BlockSpec((qrows, n0), const2), pl.BlockSpec((qrows, n1), const2),
                  pl.BlockSpec((qrows, n2), const2),
                  pl.BlockSpec((3, qrows, CH), lambda b: (0, 0, 0)),
                  pl.BlockSpec((SROWS, IN_WIDTH), lambda b: (b, 0)),
                  pl.BlockSpec((None, None, n0, HD), lambda b: (layer, b, 0, 0)),
                  pl.BlockSpec((None, None, n1, HD), lambda b: (layer, b, 0, 0)),
                  pl.BlockSpec((None, None, CH, 8 * DEC_SEQ, HD), lambda b: (layer, b, 0, 0, 0))],
        out_specs=[pl.BlockSpec((SROWS, BW), lambda b: (b, 0)), kv_spec, kv_spec, kv_spec],
        out_shape=[jax.ShapeDtypeStruct((rows, BW), BF16), kv_shape, kv_shape, kv_shape],
        scratch_shapes=[pltpu.VMEM((qrows, n0), F32), pltpu.VMEM((qrows, n1), F32), pltpu.VMEM((qrows, n2), F32),
                        pltpu.VMEM((3, qrows, CH), F32)],
        compiler_params=_cparams(1),
        name="attn_sample",
    )(rel_bias, jnp.asarray(tables[0][0]), jnp.asarray(tables[1][0]), jnp.asarray(tables[2][0]), new_tbl,
      z, c0, c1, c2)


def _gmlp_body(bu_ref, bv_ref, g_ref, ws_ref, bs_ref, *refs, rows, emit_vn):
    refs = list(refs)
    o_ref = refs.pop(0)
    vn_ref = refs.pop(0) if emit_vn else None
    tril = lax.broadcasted_iota(jnp.int32, (CH, CH), 0) >= lax.broadcasted_iota(jnp.int32, (CH, CH), 1)
    w = [jnp.where(tril, ws_ref[g], 0.0).astype(BF16) for g in range(NH)]
    if rows < CH:
        pad_u, pad_v = refs
        pad_u[...] = jnp.zeros_like(pad_u)
        pad_v[...] = jnp.zeros_like(pad_v)
        pad_u[0:rows, :] = bu_ref[...]
        pad_v[0:rows, :] = bv_ref[...]
        bu_ref, bv_ref = pad_u, pad_v
    for c in range(max(rows // CH, 1)):
        rs = slice(c * CH, (c + 1) * CH)
        u = jax.nn.gelu(bu_ref[rs, :])
        vn = _rms(jax.nn.gelu(bv_ref[rs, :]), g_ref[...])
        if emit_vn:
            vn_ref[...] = vn[:rows]
        for g in range(NH):
            sl = slice(g * HD, (g + 1) * HD)
            mixed = jnp.dot(w[g], vn[:, sl].astype(BF16), preferred_element_type=F32) + bs_ref[:, sl]
            res = (u[:, sl] * mixed).astype(BF16)
            if rows < CH:
                o_ref[:, sl] = res[:rows]
            else:
                o_ref[rs, sl] = res


def _gmlp(z, g_gmlp, w_spatial, b_spatial, *, rows, emit_vn):
    m = z.shape[0]
    bs_full = jnp.repeat(b_spatial.T, HD, axis=1)
    spec = pl.BlockSpec((rows, BW), lambda i: (i, 0))
    out_shape = [jax.ShapeDtypeStruct((m, BW), BF16)]
    out_specs = [spec]
    if emit_vn:
        out_shape.append(jax.ShapeDtypeStruct((m, BW), F32))
        out_specs.append(spec)
    scratch = [] if rows >= CH else [pltpu.VMEM((CH, BW), F32)] * 2
    return pl.pallas_call(
        functools.partial(_gmlp_body, rows=rows, emit_vn=emit_vn),
        grid=(m // rows,),
        in_specs=[pl.BlockSpec((rows, BW), lambda i: (i, COL_BU)),
                  pl.BlockSpec((rows, BW), lambda i: (i, COL_BV)),
                  pl.BlockSpec((1, BW), lambda i: (0, 0)),
                  pl.BlockSpec((NH, CH, CH), lambda i: (0, 0, 0)),
                  pl.BlockSpec((CH, BW), lambda i: (0, 0))],
        out_specs=out_specs,
        out_shape=out_shape,
        scratch_shapes=scratch,
        compiler_params=_cparams(1),
        name="gmlp",
    )(z, z, g_gmlp.reshape(1, -1), w_spatial, bs_full)


def _pool_body(x_ref, prev_ref, wp_ref, sc_ref, o_ref, st_ref, ext_ref, *, rows, n_new, start, zero_first_prev):
    ib = pl.program_id(1)
    prev = prev_ref[...]
    if zero_first_prev:
        prev = jnp.where(ib == 0, 0.0, prev)
    x = x_ref[...]
    ext_ref[0:16, :] = prev
    ext_ref[16:16 + rows, :] = x
    st_ref[...] = ext_ref[pl.ds(n_new + 1, POOL_STATE), :]
    ext = ext_ref[...]
    pos = start + ib * rows + lax.broadcasted_iota(jnp.int32, (rows, 1), 0)
    for gi, win in enumerate(POOL_WINDOWS):
        sl = slice(gi * HD, (gi + 1) * HD)
        s = ext[:, sl]
        k = 1
        while k < win:
            s = s + pltpu.roll(s, k, axis=0)
            k *= 2
        cnt = jnp.minimum(pos + 1, win).astype(F32)
        diff = s[16:] / cnt - x[:, sl]
        y = jnp.dot(diff.astype(BF16), wp_ref[gi].astype(BF16), preferred_element_type=F32)
        o_ref[:, sl] = (y * sc_ref[:, sl]).astype(BF16)


def _pool(z, prev, w_pool, pool_scale, *, nseq, rows, n_new, start, layer=None):
    m = z.shape[0]
    nblk = m // nseq // rows
    if prev is None:
        per16 = rows // 16
        prev_arr = z
        prev_spec = pl.BlockSpec((16, BW), lambda b, i: (jnp.maximum((b * nblk + i) * per16 - 1, 0), COL_CIN))
    else:
        prev_arr = prev
        prev_spec = pl.BlockSpec((None, None, 16, BW), lambda b, i: (layer, b, 0, 0))
    return pl.pallas_call(
        functools.partial(_pool_body, rows=rows, n_new=n_new, start=start, zero_first_prev=prev is None),
        grid=(nseq, nblk),
        in_specs=[pl.BlockSpec((rows, BW), lambda b, i: (b * nblk + i, COL_CIN)),
                  prev_spec,
                  pl.BlockSpec((NH, HD, HD), lambda b, i: (0, 0, 0)),
                  pl.BlockSpec((1, BW), lambda b, i: (0, 0))],
        out_specs=[pl.BlockSpec((rows, BW), lambda b, i: (b * nblk + i, 0)),
                   pl.BlockSpec((None, POOL_STATE, BW), lambda b, i: (b, 0, 0))],
        out_shape=[jax.ShapeDtypeStruct((m, BW), BF16),
                   jax.ShapeDtypeStruct((nseq, POOL_STATE, BW), F32)],
        scratch_shapes=[pltpu.VMEM((16 + rows, BW), F32)],
        compiler_params=_cparams(2),
        name="pool",
    )(z, prev_arr, w_pool, pool_scale.reshape(1, -1))


def _gmlp_sample_body(bu_ref, bv_ref, g_ref, wk_ref, bs_ref, o_ref, vn_ref):
    u = jax.nn.gelu(bu_ref[...])
    vn = _rms(jax.nn.gelu(bv_ref[...]), g_ref[...])
    vn_ref[...] = vn
    mixed = jnp.tile(bs_ref[...], (DEC_BATCH, 1))
    for k in range(DEC_SEQ):
        shifted = vn if k == 0 else pltpu.roll(vn, k, axis=0)
        mixed = mixed + jnp.tile(wk_ref[k], (DEC_BATCH, 1)) * shifted
    o_ref[...] = (u * mixed).astype(BF16)


def _gmlp_sample(z, g_gmlp, w_spatial, b_spatial):
    m = z.shape[0]
    t = np.arange(SROWS)
    live = t < DEC_SEQ
    wk = []
    for k in range(DEC_SEQ):
        ok = live & (t - k >= 0)
        diag = w_spatial[:, np.where(ok, t, 0), np.where(ok, t - k, 0)] * jnp.asarray(ok, F32)
        wk.append(jnp.repeat(diag.T, HD, axis=1))
    bs = jnp.repeat((b_spatial[:, :SROWS] * jnp.asarray(live, F32)).T, HD, axis=1)
    return pl.pallas_call(
        _gmlp_sample_body,
        grid=(1,),
        in_specs=[pl.BlockSpec((m, BW), lambda i: (0, COL_BU)),
                  pl.BlockSpec((m, BW), lambda i: (0, COL_BV)),
                  pl.BlockSpec((1, BW), lambda i: (0, 0)),
                  pl.BlockSpec((DEC_SEQ, SROWS, BW), lambda i: (0, 0, 0)),
                  pl.BlockSpec((SROWS, BW), lambda i: (0, 0))],
        out_specs=[pl.BlockSpec((m, BW), lambda i: (0, 0))] * 2,
        out_shape=[jax.ShapeDtypeStruct((m, BW), BF16), jax.ShapeDtypeStruct((m, BW), F32)],
        compiler_params=_cparams(1),
        name="gmlp_sample",
    )(z, z, g_gmlp.reshape(1, -1), jnp.stack(wk), bs)


POOL_PAD = 16


def _pool_sample_body(x_ref, prev_ref, wp_ref, sc_ref, o_ref, st_ref, ext_ref, *, start):
    per = POOL_PAD + SROWS
    ext_ref[:, 0:POOL_PAD, :] = prev_ref[...]
    ext_ref[:, POOL_PAD:per, :] = x_ref[...]
    st_ref[...] = ext_ref[:, pl.ds(DEC_SEQ + 1, POOL_STATE), :]
    ext = ext_ref[...].reshape(DEC_BATCH * per, BW)
    pos = start + (lax.broadcasted_iota(jnp.int32, (DEC_BATCH * per, 1), 0) % per - POOL_PAD)
    for gi, win in enumerate(POOL_WINDOWS):
        sl = slice(gi * HD, (gi + 1) * HD)
        x = ext[:, sl]
        s = x
        k = 1
        while k < win:
            s = s + pltpu.roll(s, k, axis=0)
            k *= 2
        cnt = jnp.minimum(jnp.maximum(pos, 0) + 1, win).astype(F32)
        y = jnp.dot((s / cnt - x).astype(BF16), wp_ref[gi].astype(BF16), preferred_element_type=F32)
        o_ref[:, :, sl] = (y * sc_ref[:, sl]).reshape(DEC_BATCH, per, HD)[:, POOL_PAD:, :].astype(BF16)


def _pool_sample(z, prev, w_pool, pool_scale, *, start, layer):
    m, zw = z.shape
    z3 = z.reshape(DEC_BATCH, SROWS, zw)
    o, st = pl.pallas_call(
        functools.partial(_pool_sample_body, start=start),
        grid=(1,),
        in_specs=[pl.BlockSpec((DEC_BATCH, SROWS, BW), lambda i: (0, 0, COL_CIN)),
                  pl.BlockSpec((None, DEC_BATCH, POOL_PAD, BW), lambda i: (layer, 0, 0, 0)),
                  pl.BlockSpec((NH, HD, HD), lambda i: (0, 0, 0)),
                  pl.BlockSpec((1, BW), lambda i: (0, 0))],
        out_specs=[pl.BlockSpec((DEC_BATCH, SROWS, BW), lambda i: (0, 0, 0)),
                   pl.BlockSpec((DEC_BATCH, POOL_STATE, BW), lambda i: (0, 0, 0))],
        out_shape=[jax.ShapeDtypeStruct((DEC_BATCH, SROWS, BW), BF16),
                   jax.ShapeDtypeStruct((DEC_BATCH, POOL_STATE, BW), F32)],
        scratch_shapes=[pltpu.VMEM((DEC_BATCH, POOL_PAD + SROWS, BW), F32)],
        compiler_params=_cparams(1),
        name="pool_sample",
    )(z3, prev, w_pool, pool_scale.reshape(1, -1))
    return o.reshape(m, BW), st


def _ret_tables(c_eff, positions):
    lg = np.log1p(-np.power(2.0, -5.0 - np.arange(NH, dtype=np.float64)))
    i = np.arange(CH, dtype=np.float64)
    live = (i < c_eff)
    diff = i[:, None] - i[None, :]
    inner = np.where((diff >= 0) & live[:, None] & live[None, :], np.exp(np.maximum(diff, 0.0)[None] * lg[:, None, None]), 0.0)
    qd = np.where(live[None, :], np.exp((i + 1.0)[None, :] * lg[:, None]), 0.0)
    kd = np.where(live[None, :], np.exp((c_eff - 1.0 - i)[None, :] * lg[:, None]), 0.0)
    chunk = tuple(float(v) for v in np.exp(c_eff * lg))
    qd_full = np.repeat(qd.T, HD, axis=1)
    kd_full = np.repeat(kd.T, HD, axis=1) * (HD ** -0.5)
    half = HD // 2
    inv = ROPE_BASE ** (-np.arange(half, dtype=np.float64) / half)
    ang = np.asarray(positions, np.float64)[:, None] * inv[None, :]
    cosf = np.concatenate([np.cos(ang), np.cos(ang)], axis=1)
    sinf = np.concatenate([-np.sin(ang), np.sin(ang)], axis=1)
    to32 = lambda a: jnp.asarray(a.astype(np.float32))
    return to32(inner), to32(qd_full), to32(kd_full), chunk, to32(cosf), to32(sinf)


def _ret_body(q_ref, k_ref, v_ref, g_ref, cos_ref, sin_ref, inner_ref, qd_ref, kd_ref, gr_ref, *refs,
              nseq, rows, chunk_decay, has_state):
    refs = list(refs)
    s0_ref = refs.pop(0) if has_state else None
    o_ref, sn_ref, s_ref = refs[:3]
    pad_ref = refs[3] if rows != CH else None
    ic = pl.program_id(1)

    @pl.when(ic == 0)
    def _():
        if has_state:
            s_ref[...] = s0_ref[...]
        else:
            s_ref[...] = jnp.zeros_like(s_ref)

    def chunk(ref, b, k):
        if rows == CH:
            return ref[b]
        pad_ref[k] = jnp.zeros((CH, BW), F32)
        pad_ref[k, 0:rows, :] = ref[b]
        return pad_ref[k]

    cosf = cos_ref[...]
    sinf = sin_ref[...]
    for b in range(nseq):
        q = chunk(q_ref, b, 0)
        k = chunk(k_ref, b, 1)
        v = chunk(v_ref, b, 2)
        gate = chunk(g_ref, b, 3)
        for h in range(NH):
            sl = slice(h * HD, (h + 1) * HD)
            qh = q[:, sl]
            kh = k[:, sl]
            rq = qh * cosf + pltpu.roll(qh, HD // 2, axis=1) * sinf
            rk = kh * cosf + pltpu.roll(kh, HD // 2, axis=1) * sinf
            vb = v[:, sl].astype(BF16)
            rqb = rq.astype(BF16)
            state = s_ref[b, h]
            att = lax.dot_general(rqb, (rk * (HD ** -0.5)).astype(BF16), (((1,), (1,)), ((), ())),
                                  preferred_element_type=F32) * inner_ref[h]
            o = (jnp.dot(att.astype(BF16), vb, preferred_element_type=F32)
                 + jnp.dot(rqb, state.astype(BF16), preferred_element_type=F32) * qd_ref[:, sl])
            kdec = (rk * kd_ref[:, sl]).T.astype(BF16)
            new_state = state * chunk_decay[h] + jnp.dot(kdec, vb, preferred_element_type=F32)
            s_ref[b, h] = new_state
            sn_ref[b, h] = new_state
            o = o * lax.rsqrt(jnp.mean(o * o, axis=-1, keepdims=True) + EPS)
            gt = gate[:, sl]
            o_ref[b, :, sl] = (o * gr_ref[:, sl] * (gt * jax.nn.sigmoid(gt)))[:rows].astype(BF16)


def _retention(z, g_ret, state, *, nseq, per_step, rows, c_eff, positions, layer=None):
    m, zw = z.shape
    seq_rows = m // nseq
    nchunk = seq_rows // rows
    inner, qd, kd, chunk_decay, cosf, sinf = _ret_tables(c_eff, positions)
    has_state = state is not None
    z3 = z.reshape(nseq, seq_rows, zw)

    def zspec(col):
        return pl.BlockSpec((per_step, rows, BW), lambda g, i: (g, i, col))

    const2 = lambda g, i: (0, 0)
    in_specs = [zspec(COL_DQ), zspec(COL_DK), zspec(COL_DV), zspec(COL_DG),
                pl.BlockSpec((CH, HD), lambda g, i: (i, 0)), pl.BlockSpec((CH, HD), lambda g, i: (i, 0)),
                pl.BlockSpec((NH, CH, CH), lambda g, i: (0, 0, 0)),
                pl.BlockSpec((CH, BW), const2), pl.BlockSpec((CH, BW), const2), pl.BlockSpec((1, BW), const2)]
    args = [z3, z3, z3, z3, cosf, sinf, inner, qd, kd, g_ret.reshape(1, -1)]
    if has_state:
        in_specs.append(pl.BlockSpec((None, per_step, NH, HD, HD), lambda g, i: (layer, g, 0, 0, 0)))
        args.append(state)
    scratch = [pltpu.VMEM((per_step, NH, HD, HD), F32)]
    if rows != CH:
        scratch.append(pltpu.VMEM((4, CH, BW), F32))
    o, sn = pl.pallas_call(
        functools.partial(_ret_body, nseq=per_step, rows=rows, chunk_decay=chunk_decay, has_state=has_state),
        grid=(nseq // per_step, nchunk),
        in_specs=in_specs,
        out_specs=[pl.BlockSpec((per_step, rows, BW), lambda g, i: (g, i, 0)),
                   pl.BlockSpec((per_step, NH, HD, HD), lambda g, i: (g, 0, 0, 0))],
        out_shape=[jax.ShapeDtypeStruct((nseq, seq_rows, BW), BF16),
                   jax.ShapeDtypeStruct((nseq, NH, HD, HD), F32)],
        scratch_shapes=scratch,
        compiler_params=_cparams(2),
        name="retention",
    )(*args)
    return o.reshape(m, BW), sn


def kernel(x_prompt, x_sample, cache_attn_kv_w128, cache_attn_kv_w512, cache_attn_kv_w2048, state_pool, state_ret, rel_bias, g_ffn1, w_ffn1_gate, w_ffn1_up, w_ffn1_down, g_mix, w_in, g_gmlp, w_spatial, b_spatial, w_pool, pool_scale, g_ret, w_branch, w_out, g_ffn2, w_ffn2_gate, w_ffn2_up, w_ffn2_down, g_final):
    caches = (cache_attn_kv_w128, cache_attn_kv_w512, cache_attn_kv_w2048)
    xp = x_prompt.reshape(BATCH * SEQ, D_MODEL)
    xs = jnp.pad(x_sample, ((0, 0), (0, SROWS - DEC_SEQ), (0, 0))).reshape(DEC_BATCH * SROWS, D_MODEL)
    pool_state = jnp.pad(state_pool, ((0, 0), (0, 0), (1, 0), (0, 0)))
    tm_p, tm_s = 1024, DEC_BATCH * SROWS
    sample_pos = PAST_LEN + np.arange(CH)
    wb = w_branch.astype(BF16)
    wo = w_out.astype(BF16)
    gate_col0 = N_PLAIN * BW // 1024
    qkv1, qkv2 = COL_QKV[1][0], COL_QKV[2][0]

    kv_p = [None, None, None]
    kv_s = [[], [], []]
    pool_p, pool_s, ret_p, ret_s, gv_s = [], [], [], [], []
    yp = ys = None
    for l in range(DEPTH):
        last = l == DEPTH - 1

        xs, hs, *w1 = _ffn(xs, g_ffn1[l], (w_ffn1_gate, w_ffn1_up, w_ffn1_down), g_mix[l], layer=l,
                           emit_x=True, post_dtype=BF16, tm=tm_s)
        zs, gs, win = _in_proj_convert(hs, w_in, layer=l)
        oa, k0, k1, k2 = _attn_sample(zs, caches, rel_bias, l)
        for gi, kv in enumerate((k0, k1, k2)):
            kv_s[gi].append(kv.reshape(DEC_BATCH, SROWS, 2, NH, HD)[:, :DEC_SEQ])
        ob, vn = _gmlp_sample(zs, g_gmlp[l], w_spatial[l], b_spatial[l])
        gv_s.append(vn.reshape(DEC_BATCH, SROWS, BW)[:, :DEC_SEQ])
        oc, pn = _pool_sample(zs, pool_state, w_pool[l], pool_scale[l], start=PAST_LEN, layer=l)
        od, rn = _retention(zs, g_ret[l], state_ret, nseq=DEC_BATCH, per_step=4, rows=SROWS, c_eff=DEC_SEQ,
                            positions=sample_pos, layer=l)
        pool_s.append(pn)
        ret_s.append(rn)
        mix = _branch_merge(oa, ob, oc, od, gs, wb, layer=l, tm=tm_s)
        xs = _matmul(mix, wo, xs, layer=l, tm=tm_s, tn=512, name="out_proj")
        if last:
            ys, *w2 = _ffn(xs, g_ffn2[l], (w_ffn2_gate, w_ffn2_up, w_ffn2_down), g_final, layer=l,
                           emit_x=False, post_dtype=F32, tm=tm_s)
        else:
            xs, *w2 = _ffn(xs, g_ffn2[l], (w_ffn2_gate, w_ffn2_up, w_ffn2_down), g_final, layer=l,
                           emit_x=True, post_dtype=None, tm=tm_s)

        xp, hp = _ffn(xp, g_ffn1[l], w1, g_mix[l], emit_x=True, post_dtype=BF16, tm=tm_p)
        zp = _matmul(hp, win, n=N_PLAIN * BW, tm=2048, tn=1024, name="in_proj")
        gp = _matmul(hp, win, col0=gate_col0, n=GATE_W, gate=True, tm=2048, tn=1024, name="in_proj_gate")
        zq = (zp,
              _matmul_deint(hp, win, col0=qkv1, dil=ATT_DILATIONS[1], name="in_proj_g1"),
              _matmul_deint(hp, win, col0=qkv2, dil=ATT_DILATIONS[2], name="in_proj_g2"))
        parts = []
        for gi in range(3):
            o, lse, kv_p[gi] = _attn_prompt(zq[gi], rel_bias, gi, kv_p[gi])
            parts += [o, lse]
        ob, = _gmlp(zp, g_gmlp[l], w_spatial[l], b_spatial[l], rows=4 * CH, emit_vn=False)
        oc, pn = _pool(zp, None, w_pool[l], pool_scale[l], nseq=BATCH, rows=512, n_new=512, start=0)
        od, rn = _retention(zp, g_ret[l], None, nseq=BATCH, per_step=BATCH, rows=CH, c_eff=CH,
                            positions=np.arange(SEQ))
        pool_p.append(pn)
        ret_p.append(rn)
        xp = _merge(parts, ob, oc, od, gp, wb, wo, xp, layer=l)
        if last:
            yp, = _ffn(xp, g_ffn2[l], w2, g_final, emit_x=False, post_dtype=F32, tm=tm_p)
        else:
            xp, = _ffn(xp, g_ffn2[l], w2, g_final, emit_x=True, post_dtype=None, tm=tm_p)

    y_prompt = yp.reshape(BATCH, SEQ, D_MODEL)
    y_sample = ys.reshape(DEC_BATCH, SROWS, D_MODEL)[:, :DEC_SEQ]
    kv_p = [kv.reshape(DEPTH, BATCH, ATT_WINDOWS[gi], 2, NH, HD) for gi, kv in enumerate(kv_p)]
    return (y_prompt, y_sample,
            kv_p[0], kv_p[1], kv_p[2],
            jnp.stack(kv_s[0]), jnp.stack(kv_s[1]), jnp.stack(kv_s[2]),
            jnp.stack(pool_p), jnp.stack(pool_s),
            jnp.stack(ret_p), jnp.stack(ret_s),
            jnp.stack(gv_s))
```

```python
import functools

import numpy as np
import jax
import jax.numpy as jnp
from jax import lax
from jax.experimental import pallas as pl
from jax.experimental.pallas import tpu as pltpu

F32 = jnp.float32
BF16 = jnp.bfloat16

D_MODEL = 2048
BATCH = 4
SEQ = 2048
DEPTH = 2
DEC_BATCH = 32
DEC_SEQ = 4
PAST_LEN = 8192
D_FF = 5632
IN_WIDTH = 16384
EPS = 1e-6
BW = 512
HD = 128
NH = 4
ATT_WINDOWS = (128, 512, 2048)
ATT_DILATIONS = (1, 4, 16)
NKEY = 129
REL_BUCKETS = 32
REL_MAX_DISTANCE = 2048
POOL_WINDOWS = (2, 4, 8, 16)
POOL_STATE = 15
ROPE_BASE = 10000.0
SROWS = 8
CH = 128
NEG = -1e30
LSE_W = 32

W_IN_ORDER = (0, 3, 6) + tuple(range(9, 32)) + (1, 4, 7) + (2, 5, 8)
COL_QKV = ((0, 1, 2), (26, 27, 28), (29, 30, 31))
COL_BU, COL_BV, COL_CIN = 3, 4, 5
COL_DQ, COL_DK, COL_DV, COL_DG = 6, 7, 8, 9
N_PLAIN = 10
N_GATE = 16
GATE_W = N_GATE * BW
ZBLK = IN_WIDTH // BW

VMEM_BIG = 56 * 1024 * 1024


def _cparams(n_axes, vmem=None):
    return pltpu.CompilerParams(dimension_semantics=("arbitrary",) * n_axes, vmem_limit_bytes=vmem)


def _rms(x, g):
    return x * lax.rsqrt(jnp.mean(x * x, axis=-1, keepdims=True) + EPS) * g


def _ffn_body(x_ref, gpre_ref, *refs, emit_x, emit_post, convert, nf, tf, row_split):
    refs = list(refs)
    if convert:
        wg_ref, wu_ref, wd_ref, gpost_ref = refs[:4]
        refs = refs[4:]
    else:
        wgu_ref, wd_ref, gpost_ref = refs[:3]
        refs = refs[3:]
    n_act = int(emit_x) + int(emit_post)
    outs = refs[:n_act]
    scratch = refs[n_act + (2 if convert else 0):]
    xn_ref = scratch[0]
    acc_ref = outs[0]
    f = pl.program_id(1)

    @pl.when(f == 0)
    def _():
        x = x_ref[...]
        xn_ref[...] = _rms(x, gpre_ref[...]).astype(BF16)
        acc_ref[...] = x

    if convert:
        wgu = jnp.concatenate([wg_ref[...].astype(BF16), wu_ref[...].astype(BF16)], axis=1)
        wd = wd_ref[...].astype(BF16)
        refs[n_act][...] = wgu
        refs[n_act + 1][...] = wd
    else:
        wgu, wd = wgu_ref[...], wd_ref[...]
    rows = xn_ref.shape[0] // row_split
    for part in range(row_split):
        rs = slice(part * rows, (part + 1) * rows)
        r = jnp.dot(xn_ref[rs, :], wgu, preferred_element_type=F32)
        g, u = r[:, :tf], r[:, tf:]
        h = (g * jax.nn.sigmoid(g) * (0.5 * u)).astype(BF16)
        acc_ref[rs, :] += jnp.dot(h, wd, preferred_element_type=F32)

    if emit_post:
        @pl.when(f == nf - 1)
        def _():
            outs[-1][...] = _rms(acc_ref[...], gpost_ref[...]).astype(outs[-1].dtype)


def _ffn(x, g_pre, weights, g_post, *, layer=None, emit_x, post_dtype, tm, tf=512):
    m = x.shape[0]
    nf = D_FF // tf
    emit_post = post_dtype is not None
    convert = layer is not None
    assert emit_x or post_dtype == F32, "the first output block is the f32 accumulator"
    once = dict(pipeline_mode=pl.Buffered(1)) if tm > 512 else {}
    row_spec = pl.BlockSpec((tm, D_MODEL), lambda i, f: (i, 0))
    row_out_spec = pl.BlockSpec((tm, D_MODEL), lambda i, f: (i, 0), **once)
    vec_spec = pl.BlockSpec((1, D_MODEL), lambda i, f: (0, 0))
    out_shape, out_specs = [], []
    if emit_x:
        out_shape.append(jax.ShapeDtypeStruct((m, D_MODEL), F32))
        out_specs.append(row_out_spec)
    if emit_post:
        out_shape.append(jax.ShapeDtypeStruct((m, D_MODEL), post_dtype))
        out_specs.append(row_out_spec)
    gu_spec = pl.BlockSpec((D_MODEL, 2 * tf), lambda i, f: (0, f))
    down_spec = pl.BlockSpec((tf, D_MODEL), lambda i, f: (f, 0))
    if convert:
        assert m == tm, "the bf16 weights are written once, by a single row tile"
        w_specs = [pl.BlockSpec((None, D_MODEL, tf), lambda i, f: (layer, 0, f)),
                   pl.BlockSpec((None, D_MODEL, tf), lambda i, f: (layer, 0, f)),
                   pl.BlockSpec((None, tf, D_MODEL), lambda i, f: (layer, f, 0))]
        out_shape += [jax.ShapeDtypeStruct((D_MODEL, 2 * D_FF), BF16), jax.ShapeDtypeStruct((D_FF, D_MODEL), BF16)]
        out_specs += [gu_spec, down_spec]
    else:
        w_specs = [gu_spec, down_spec]
    scratch = [pltpu.VMEM((tm, D_MODEL), BF16)]
    return pl.pallas_call(
        functools.partial(_ffn_body, emit_x=emit_x, emit_post=emit_post, convert=convert, nf=nf, tf=tf,
                          row_split=max(tm // 512, 1)),
        grid=(m // tm, nf),
        in_specs=[row_spec, vec_spec, *w_specs, vec_spec],
        out_specs=out_specs,
        out_shape=out_shape,
        scratch_shapes=scratch,
        compiler_params=_cparams(2, VMEM_BIG),
        name="ffn",
    )(x, g_pre.reshape(1, -1), *weights, g_post.reshape(1, -1))


def _mm_body(a_ref, b_ref, o_ref):
    o_ref[...] = jnp.dot(a_ref[...], b_ref[...], preferred_element_type=F32)


def _mm_res_body(a_ref, b_ref, r_ref, o_ref):
    o_ref[...] = r_ref[...] + jnp.dot(a_ref[...], b_ref[...], preferred_element_type=F32)


def _sigmoid(x):
    return 0.5 * jnp.tanh(0.5 * x) + 0.5


def _mm_gate_body(a_ref, b_ref, o_ref):
    o_ref[...] = _sigmoid(jnp.dot(a_ref[...], b_ref[...], preferred_element_type=F32)).astype(BF16)


def _matmul(a, b, res=None, *, layer=None, col0=0, n=None, gate=False, tm, tn, name):
    m, k = a.shape
    n = b.shape[-1] if n is None else n
    if layer is None:
        b_spec = pl.BlockSpec((k, tn), lambda i, j: (0, col0 + j))
    else:
        b_spec = pl.BlockSpec((None, k, tn), lambda i, j: (layer, 0, col0 + j))
    in_specs = [pl.BlockSpec((tm, k), lambda i, j: (i, 0)), b_spec]
    args = [a, b]
    body = _mm_gate_body if gate else _mm_body
    if res is not None:
        in_specs.append(pl.BlockSpec((tm, tn), lambda i, j: (i, j)))
        args.append(res)
        body = _mm_res_body
    return pl.pallas_call(
        body,
        grid=(m // tm, n // tn),
        in_specs=in_specs,
        out_specs=pl.BlockSpec((tm, tn), lambda i, j: (i, j)),
        out_shape=jax.ShapeDtypeStruct((m, n), BF16 if gate else F32),
        compiler_params=_cparams(2, VMEM_BIG),
        name=name,
    )(*args)


def _mm_deint_body(a_ref, b_ref, o_ref, s_ref, *, dil):
    r = jnp.dot(a_ref[...], b_ref[...], preferred_element_type=F32)
    for h in range(NH):
        s_ref[h] = r[:, h * HD:(h + 1) * HD]
    n = SEQ // dil
    for h in range(NH):
        for res in range(dil):
            o_ref[h, pl.ds(res * n, n), :] = s_ref[h, pl.ds(res, n, stride=dil), :]


def _w_in_source_block(j, where=jnp.where):
    return where(j < 3, 3 * j, where(j < 26, j + 6, where(j < 29, 3 * (j - 26) + 1, 3 * (j - 29) + 2)))


assert tuple(_w_in_source_block(np.arange(ZBLK), np.where)) == W_IN_ORDER


def _in_proj_convert_body(a_ref, b_ref, z_ref, gate_ref, w_ref):
    j = pl.program_id(0)
    w = b_ref[...].astype(BF16)
    w_ref[...] = w
    r = jnp.dot(a_ref[...], w, preferred_element_type=F32)
    z_ref[...] = r

    @pl.when((j >= N_PLAIN) & (j < N_PLAIN + N_GATE))
    def _():
        gate_ref[...] = _sigmoid(r).astype(BF16)


def _in_proj_convert(a, w_in, *, layer):
    m, k = a.shape
    return pl.pallas_call(
        _in_proj_convert_body,
        grid=(ZBLK,),
        in_specs=[pl.BlockSpec((m, k), lambda j: (0, 0)),
                  pl.BlockSpec((None, k, BW), lambda j: (layer, 0, _w_in_source_block(j)))],
        out_specs=[pl.BlockSpec((m, BW), lambda j: (0, j)),
                   pl.BlockSpec((m, BW), lambda j: (0, jnp.clip(j - N_PLAIN, 0, N_GATE - 1))),
                   pl.BlockSpec((k, BW), lambda j: (0, j))],
        out_shape=[jax.ShapeDtypeStruct((m, IN_WIDTH), F32),
                   jax.ShapeDtypeStruct((m, GATE_W), BF16),
                   jax.ShapeDtypeStruct((k, IN_WIDTH), BF16)],
        compiler_params=_cparams(1, VMEM_BIG),
        name="in_proj_convert",
    )(a, w_in)


def _matmul_deint(a, b, *, col0, dil, name):
    m, k = a.shape
    return pl.pallas_call(
        functools.partial(_mm_deint_body, dil=dil),
        grid=(BATCH, 3),
        in_specs=[pl.BlockSpec((SEQ, k), lambda b, j: (b, 0)),
                  pl.BlockSpec((k, BW), lambda b, j: (0, col0 + j))],
        out_specs=pl.BlockSpec((None, NH, SEQ, HD), lambda b, j: (j, 0, b, 0)),
        out_shape=jax.ShapeDtypeStruct((3, NH, m, HD), F32),
        scratch_shapes=[pltpu.VMEM((NH, SEQ, HD), F32)],
        compiler_params=_cparams(2, VMEM_BIG),
        name=name,
    )(a, b)


def _branch_body(oa_ref, ob_ref, oc_ref, od_ref, gate_ref, wb_ref, o_ref):
    acc = None
    for n, br in enumerate((oa_ref, ob_ref, oc_ref, od_ref)):
        proj = jnp.dot(br[...], wb_ref[n], preferred_element_type=F32)
        t = gate_ref[:, n * D_MODEL:(n + 1) * D_MODEL].astype(F32) * proj
        acc = t if acc is None else acc + t
    o_ref[...] = acc.astype(BF16)


def _branch_merge(oa, ob, oc, od, gates, wb, *, layer, tm):
    m = oa.shape[0]
    br_spec = pl.BlockSpec((tm, BW), lambda i: (i, 0))
    return pl.pallas_call(
        _branch_body,
        grid=(m // tm,),
        in_specs=[br_spec] * 4 + [pl.BlockSpec((tm, GATE_W), lambda i: (i, 0)),
                                  pl.BlockSpec((None, 4, BW, D_MODEL), lambda i: (layer, 0, 0, 0))],
        out_specs=pl.BlockSpec((tm, D_MODEL), lambda i: (i, 0)),
        out_shape=jax.ShapeDtypeStruct((m, D_MODEL), BF16),
        compiler_params=_cparams(1, VMEM_BIG),
        name="branch_merge",
    )(oa, ob, oc, od, gates, wb)


def _t5_buckets(dist):
    max_exact = REL_BUCKETS // 2
    d = np.maximum(dist, 1).astype(np.float32)
    large = max_exact + (np.log(d / max_exact) / np.log(REL_MAX_DISTANCE / max_exact)
                         * (REL_BUCKETS - max_exact)).astype(np.int32)
    large = np.minimum(large, REL_BUCKETS - 1)
    return np.where(dist < max_exact, dist, large).astype(np.int32)


def _bucket_of_step(dil):
    return _t5_buckets(dil * np.arange(NKEY))


def _prompt_bucket_matrix(dil):
    iq = np.arange(CH)[:, None]
    col = np.arange(2 * CH)[None, :]
    j = iq + CH - col
    valid = (j >= 0) & (j < NKEY)
    return np.where(valid, _bucket_of_step(dil)[np.clip(j, 0, NKEY - 1)], -1).astype(np.int32)


SAMPLE_CACHE_ROWS = (8 * ATT_WINDOWS[0], 8 * ATT_WINDOWS[1], 8 * DEC_SEQ * CH)


def _sample_bucket_tables():
    h = np.repeat(np.arange(NH), SROWS)[:, None]
    t = np.tile(np.arange(SROWS), NH)[:, None]
    lane = np.arange(CH)[None, :]
    tables = []
    for gi, dil in enumerate(ATT_DILATIONS):
        bos = _bucket_of_step(dil)
        flat = np.arange(SAMPLE_CACHE_ROWS[gi])[None, :]
        head, is_k = flat % NH, (flat // NH) % 2 == 0
        if gi < 2:
            w = flat // 8
            steps = ATT_WINDOWS[gi] + t - w
            valid = (steps % dil == 0) & (steps // dil < NKEY)
            j = steps // dil
        else:
            res, i = (flat // 8) % DEC_SEQ, flat // (8 * DEC_SEQ)
            valid = res == t
            j = np.broadcast_to(CH - i, valid.shape)
        valid = valid & is_k & (head == h) & (t < DEC_SEQ)
        cache_tbl = np.where(valid, bos[np.clip(j, 0, NKEY - 1)], -1)
        jn = t - lane
        validn = (t < DEC_SEQ) & (lane < DEC_SEQ) & (jn >= 0) & ((dil == 1) | (jn == 0))
        new_tbl = np.where(validn, bos[np.clip(jn, 0, NKEY - 1)], -1)
        tables.append((cache_tbl.astype(np.int32), new_tbl.astype(np.int32)))
    return tables


def _bias_from_buckets(bk, rb_ref, col):
    out = jnp.full(bk.shape, NEG, F32)
    for b in range(REL_BUCKETS):
        out = jnp.where(bk == b, rb_ref[b, col], out)
    return out


ATT_SUB = 4


def _attn_prompt_body(rb_ref, bk_ref, q_ref, k_ref, v_ref, *refs, gi, headed):
    dil = ATT_DILATIONS[gi]
    has_prev = gi == 0
    if has_prev:
        kp_ref, vp_ref = refs[:2]
        refs = refs[2:]
    o_ref, lse_ref, kvo_ref, bias_ref = refs[-4:]
    if len(refs) == 5:
        kvo_ref[0] = refs[0][...]
        kvo_ref = kvo_ref.at[1]
    step = pl.program_id(1)

    @pl.when((pl.program_id(0) == 0) & (step == 0))
    def _():
        bk = bk_ref[...]
        for h in range(NH):
            bias_ref[h] = _bias_from_buckets(bk, rb_ref, gi * NH + h)

    def tile(ref, h, t):
        rs = slice(t * CH, (t + 1) * CH)
        return ref[h, rs, :] if headed else ref[rs, h * HD:(h + 1) * HD]

    scale = HD ** -0.5
    lane_head = lax.broadcasted_iota(jnp.int32, (CH, NH * LSE_W), 1) // LSE_W
    if has_prev:
        col = lax.broadcasted_iota(jnp.int32, (CH, 2 * CH), 1)
        no_prev = (step == 0) & (col < CH)
    for t in range(ATT_SUB):
        if gi == 0:
            dst = slice(t * CH, (t + 1) * CH)
        elif gi == 1:
            dst = pl.ds(t * CH * dil + step, CH, stride=dil)
        else:
            dst = pl.ds(step * ATT_SUB + t, CH, stride=dil)
        keep = gi == 2 or t == ATT_SUB - 1
        kv_row0 = (t if gi == 2 else 0) * 2 * NH
        lse_tile = None
        for h in range(NH):
            kc = tile(k_ref, h, t)
            vc = tile(v_ref, h, t)
            if keep:
                kvo_ref[:, kv_row0 + h, :] = kc
                kvo_ref[:, kv_row0 + NH + h, :] = vc
            qh = tile(q_ref, h, t).astype(BF16)
            if gi < 2 and (t > 0 or has_prev):
                if t > 0:
                    kp, vp, bias = tile(k_ref, h, t - 1), tile(v_ref, h, t - 1), bias_ref[h]
                else:
                    kp, vp = kp_ref[:, h * HD:(h + 1) * HD], vp_ref[:, h * HD:(h + 1) * HD]
                    bias = jnp.where(no_prev, NEG, bias_ref[h])
                kh = jnp.concatenate([kp, kc], axis=0).astype(BF16)
                vh = jnp.concatenate([vp, vc], axis=0).astype(BF16)
            else:
                kh = kc.astype(BF16)
                vh = vc.astype(BF16)
                bias = bias_ref[h][:, CH:]
            s = lax.dot_general(qh, kh, (((1,), (1,)), ((), ())), preferred_element_type=F32) * scale + bias
            m = jnp.max(s, axis=-1, keepdims=True)
            p = jnp.exp(s - m)
            den = jnp.sum(p, axis=-1, keepdims=True)
            acc = jnp.dot(p.astype(BF16), vh, preferred_element_type=F32)
            o_ref[h, dst, :] = acc / den
            lse = m + jnp.log(den)
            lse_tile = lse if h == 0 else jnp.where(lane_head == h, lse, lse_tile)
        lse_ref[dst, :] = lse_tile


def _attn_prompt(zq, rel_bias, gi, kv_prev):
    dil = ATT_DILATIONS[gi]
    rows = BATCH * SEQ
    nsteps = SEQ // (ATT_SUB * CH)
    assert SEQ // dil // CH in (1, ATT_SUB, ATT_SUB * nsteps)
    headed = gi > 0
    bk = jnp.asarray(_prompt_bucket_matrix(dil))
    span = ATT_SUB * CH

    def zspec(c):
        if headed:
            return pl.BlockSpec((None, NH, span, HD), lambda b, s: (c, 0, b * nsteps + s, 0))
        return pl.BlockSpec((span, BW), lambda b, s: (b * nsteps + s, COL_QKV[0][c]))

    in_specs = [pl.BlockSpec(memory_space=pltpu.SMEM),
                pl.BlockSpec((CH, 2 * CH), lambda b, s: (0, 0)),
                zspec(0), zspec(1), zspec(2)]
    args = [rel_bias, bk, zq, zq, zq]
    if gi == 0:
        above = lambda c: pl.BlockSpec(
            (CH, BW), lambda b, s: (jnp.maximum((b * nsteps + s) * ATT_SUB - 1, 0), COL_QKV[0][c]))
        in_specs += [above(1), above(2)]
        args += [zq, zq]
        out_rows, out_index = span, (lambda b, s: b * nsteps + s)
    else:
        out_rows, out_index = SEQ, (lambda b, s: b)
    n_res = ATT_SUB if gi == 2 else 1
    kv_blk = (CH, n_res * 2 * NH, HD)
    kv_index = lambda b, s: (b, 0, s if gi else 0, 0)
    kv_shape = (BATCH, CH, dil * 2 * NH, HD)
    if kv_prev is None:
        kv_spec = pl.BlockSpec((None, *kv_blk), kv_index)
    else:
        in_specs.append(pl.BlockSpec((None, *kv_blk), kv_index))
        args.append(kv_prev)
        kv_spec = pl.BlockSpec((2, None, *kv_blk), lambda b, s: (0, *kv_index(b, s)))
        kv_shape = (2, *kv_shape)
    return pl.pallas_call(
        functools.partial(_attn_prompt_body, gi=gi, headed=headed),
        grid=(BATCH, nsteps),
        in_specs=in_specs,
        out_specs=[pl.BlockSpec((NH, out_rows, HD), lambda b, s: (0, out_index(b, s), 0)),
                   pl.BlockSpec((out_rows, NH * LSE_W), lambda b, s: (out_index(b, s), 0)),
                   kv_spec],
        out_shape=[jax.ShapeDtypeStruct((NH, rows, HD), F32),
                   jax.ShapeDtypeStruct((rows, NH * LSE_W), F32),
                   jax.ShapeDtypeStruct(kv_shape, F32)],
        scratch_shapes=[pltpu.VMEM((NH, CH, 2 * CH), F32)],
        compiler_params=_cparams(2),
        name=f"attn_prompt_g{gi}",
    )(*args)


def _merge_body(o0, l0, o1, l1, o2, l2, ob_ref, oc_ref, od_ref, gate_ref, wb_ref, wo_ref, x_ref, out_ref):
    heads = []
    for h in range(NH):
        a0, a1, a2 = (l[:, h * LSE_W:h * LSE_W + 1] for l in (l0, l1, l2))
        m = jnp.maximum(jnp.maximum(a0, a1), a2)
        w0, w1, w2 = jnp.exp(a0 - m), jnp.exp(a1 - m), jnp.exp(a2 - m)
        heads.append(((w0 * o0[h] + w1 * o1[h] + w2 * o2[h]) / (w0 + w1 + w2)).astype(BF16))
    oa = jnp.concatenate(heads, axis=1)
    acc = None
    for n, br in enumerate((oa, ob_ref[...], oc_ref[...], od_ref[...])):
        proj = jnp.dot(br, wb_ref[n], preferred_element_type=F32)
        t = gate_ref[:, n * D_MODEL:(n + 1) * D_MODEL].astype(F32) * proj
        acc = t if acc is None else acc + t
    out_ref[...] = x_ref[...] + jnp.dot(acc.astype(BF16), wo_ref[...], preferred_element_type=F32)


def _merge(parts, ob, oc, od, gates, wb, wo, x, *, layer, tm=256):
    m = x.shape[0]
    once = dict(pipeline_mode=pl.Buffered(1))
    ospec = pl.BlockSpec((NH, tm, HD), lambda i: (0, i, 0))
    lspec = pl.BlockSpec((tm, NH * LSE_W), lambda i: (i, 0))
    br_spec = pl.BlockSpec((tm, BW), lambda i: (i, 0))
    row_spec = pl.BlockSpec((tm, D_MODEL), lambda i: (i, 0))
    return pl.pallas_call(
        _merge_body,
        grid=(m // tm,),
        in_specs=[ospec, lspec] * 3 + [br_spec] * 3 + [
            pl.BlockSpec((tm, GATE_W), lambda i: (i, 0)),
            pl.BlockSpec((None, 4, BW, D_MODEL), lambda i: (layer, 0, 0, 0), **once),
            pl.BlockSpec((None, D_MODEL, D_MODEL), lambda i: (layer, 0, 0), **once),
            row_spec],
        out_specs=row_spec,
        out_shape=jax.ShapeDtypeStruct((m, D_MODEL), F32),
        compiler_params=_cparams(1, VMEM_BIG),
        name="merge",
    )(*parts, ob, oc, od, gates, wb, wo, x)


def _attn_sample_body(rb_ref, bc0_ref, bc1_ref, bc2_ref, bn_ref, z_ref, c0_ref, c1_ref, c2_ref,
                      oa_ref, kv0_ref, kv1_ref, kv2_ref, b0_ref, b1_ref, b2_ref, bnew_ref):
    bucket_refs = (bc0_ref, bc1_ref, bc2_ref)
    bias_refs = (b0_ref, b1_ref, b2_ref)

    @pl.when(pl.program_id(0) == 0)
    def _():
        for gi in range(3):
            for h in range(NH):
                rs = slice(h * SROWS, (h + 1) * SROWS)
                bias_refs[gi][rs, :] = _bias_from_buckets(bucket_refs[gi][rs, :], rb_ref, gi * NH + h)
                bnew_ref[gi, rs, :] = _bias_from_buckets(bn_ref[gi, rs, :], rb_ref, gi * NH + h)

    scale = HD ** -0.5
    rows = NH * SROWS
    head_of_row = lax.broadcasted_iota(jnp.int32, (rows, BW), 0) // SROWS
    head_of_lane = lax.broadcasted_iota(jnp.int32, (rows, BW), 1) // HD
    head_mask = head_of_row == head_of_lane
    caches = (c0_ref, c1_ref, c2_ref)
    kv_refs = (kv0_ref, kv1_ref, kv2_ref)
    stats = []
    for gi in range(3):
        cq, ck, cv = COL_QKV[gi]
        q = z_ref[:, cq * BW:(cq + 1) * BW]
        kn = z_ref[:, ck * BW:(ck + 1) * BW]
        vn = z_ref[:, cv * BW:(cv + 1) * BW]
        kv_refs[gi][:, :BW] = kn
        kv_refs[gi][:, BW:] = vn
        qm = jnp.concatenate([q[:, h * HD:(h + 1) * HD] for h in range(NH)], axis=0).astype(BF16)
        kf = caches[gi][...].reshape(SAMPLE_CACHE_ROWS[gi], HD).astype(BF16)
        s = lax.dot_general(qm, kf, (((1,), (1,)), ((), ())), preferred_element_type=F32) * scale + bias_refs[gi][...]
        qrows = jnp.where(head_mask, jnp.concatenate([q] * NH, axis=0), 0.0)
        bias_n = bnew_ref[gi]
        s_new = []
        for tp in range(DEC_SEQ):
            dotp = jnp.sum(qrows * kn[tp:tp + 1, :], axis=-1, keepdims=True)
            s_new.append(dotp * scale + bias_n[:, tp:tp + 1])
        m = jnp.max(s, axis=-1, keepdims=True)
        for sn in s_new:
            m = jnp.maximum(m, sn)
        p = jnp.exp(s - m)
        den = jnp.sum(p, axis=-1, keepdims=True)
        pv = pltpu.roll(p, NH, axis=1).astype(BF16)
        acc = jnp.dot(pv, kf, preferred_element_type=F32)
        acc_n = jnp.zeros((rows, BW), F32)
        for tp, sn in enumerate(s_new):
            pn = jnp.exp(sn - m)
            den = den + pn
            acc_n = acc_n + pn * vn[tp:tp + 1, :]
        acc = acc + jnp.concatenate([acc_n[h * SROWS:(h + 1) * SROWS, h * HD:(h + 1) * HD] for h in range(NH)], axis=0)
        stats.append((m, den, acc))
    mm = jnp.maximum(jnp.maximum(stats[0][0], stats[1][0]), stats[2][0])
    den = jnp.zeros((rows, 1), F32)
    acc = jnp.zeros((rows, HD), F32)
    for m, d, a in stats:
        w = jnp.exp(m - mm)
        den = den + w * d
        acc = acc + w * a
    o = acc / den
    for h in range(NH):
        oa_ref[:, h * HD:(h + 1) * HD] = o[h * SROWS:(h + 1) * SROWS, :].astype(BF16)


def _attn_sample(z, caches, rel_bias, layer):
    tables = _sample_bucket_tables()
    rows = DEC_BATCH * SROWS
    qrows = NH * SROWS
    n0, n1, n2 = SAMPLE_CACHE_ROWS
    c0 = caches[0].reshape(DEPTH, DEC_BATCH, n0, HD)
    c1 = caches[1].reshape(DEPTH, DEC_BATCH, n1, HD)
    c2 = caches[2].reshape(DEPTH, DEC_BATCH, CH, 16 * 8, HD)
    new_tbl = jnp.asarray(np.stack([t[1] for t in tables]))
    kv_spec = pl.BlockSpec((SROWS, 2 * BW), lambda b: (b, 0))
    kv_shape = jax.ShapeDtypeStruct((rows, 2 * BW), F32)
    const2 = lambda b: (0, 0)
    return pl.pallas_call(
        _attn_sample_body,
        grid=(DEC_BATCH,),
        in_specs=[pl.BlockSpec(memory_space=pltpu.SMEM),
                  pl.BlockSpec((qrows, n0), const2), pl.BlockSpec((qrows, n1), const2),
                  pl.BlockSpec((qrows, n2), const2),
                  pl.BlockSpec((3, qrows, CH), lambda b: (0, 0, 0)),
                  pl.BlockSpec((SROWS, IN_WIDTH), lambda b: (b, 0)),
                  pl.BlockSpec((None, None, n0, HD), lambda b: (layer, b, 0, 0)),
                  pl.BlockSpec((None, None, n1, HD), lambda b: (layer, b, 0, 0)),
                  pl.BlockSpec((None, None, CH, 8 * DEC_SEQ, HD), lambda b: (layer, b, 0, 0, 0))],
        out_specs=[pl.BlockSpec((SROWS, BW), lambda b: (b, 0)), kv_spec, kv_spec, kv_spec],
        out_shape=[jax.ShapeDtypeStruct((rows, BW), BF16), kv_shape, kv_shape, kv_shape],
        scratch_shapes=[pltpu.VMEM((qrows, n0), F32), pltpu.VMEM((qrows, n1), F32), pltpu.VMEM((qrows, n2), F32),
                        pltpu.VMEM((3, qrows, CH), F32)],
        compiler_params=_cparams(1),
        name="attn_sample",
    )(rel_bias, jnp.asarray(tables[0][0]), jnp.asarray(tables[1][0]), jnp.asarray(tables[2][0]), new_tbl,
      z, c0, c1, c2)


def _gmlp_body(bu_ref, bv_ref, g_ref, ws_ref, bs_ref, *refs, rows, emit_vn):
    refs = list(refs)
    o_ref = refs.pop(0)
    vn_ref = refs.pop(0) if emit_vn else None
    tril = lax.broadcasted_iota(jnp.int32, (CH, CH), 0) >= lax.broadcasted_iota(jnp.int32, (CH, CH), 1)
    w = [jnp.where(tril, ws_ref[g], 0.0).astype(BF16) for g in range(NH)]
    if rows < CH:
        pad_u, pad_v = refs
        pad_u[...] = jnp.zeros_like(pad_u)
        pad_v[...] = jnp.zeros_like(pad_v)
        pad_u[0:rows, :] = bu_ref[...]
        pad_v[0:rows, :] = bv_ref[...]
        bu_ref, bv_ref = pad_u, pad_v
    for c in range(max(rows // CH, 1)):
        rs = slice(c * CH, (c + 1) * CH)
        u = jax.nn.gelu(bu_ref[rs, :])
        vn = _rms(jax.nn.gelu(bv_ref[rs, :]), g_ref[...])
        if emit_vn:
            vn_ref[...] = vn[:rows]
        for g in range(NH):
            sl = slice(g * HD, (g + 1) * HD)
            mixed = jnp.dot(w[g], vn[:, sl].astype(BF16), preferred_element_type=F32) + bs_ref[:, sl]
            res = (u[:, sl] * mixed).astype(BF16)
            if rows < CH:
                o_ref[:, sl] = res[:rows]
            else:
                o_ref[rs, sl] = res


def _gmlp(z, g_gmlp, w_spatial, b_spatial, *, rows, emit_vn):
    m = z.shape[0]
    bs_full = jnp.repeat(b_spatial.T, HD, axis=1)
    spec = pl.BlockSpec((rows, BW), lambda i: (i, 0))
    out_shape = [jax.ShapeDtypeStruct((m, BW), BF16)]
    out_specs = [spec]
    if emit_vn:
        out_shape.append(jax.ShapeDtypeStruct((m, BW), F32))
        out_specs.append(spec)
    scratch = [] if rows >= CH else [pltpu.VMEM((CH, BW), F32)] * 2
    return pl.pallas_call(
        functools.partial(_gmlp_body, rows=rows, emit_vn=emit_vn),
        grid=(m // rows,),
        in_specs=[pl.BlockSpec((rows, BW), lambda i: (i, COL_BU)),
                  pl.BlockSpec((rows, BW), lambda i: (i, COL_BV)),
                  pl.BlockSpec((1, BW), lambda i: (0, 0)),
                  pl.BlockSpec((NH, CH, CH), lambda i: (0, 0, 0)),
                  pl.BlockSpec((CH, BW), lambda i: (0, 0))],
        out_specs=out_specs,
        out_shape=out_shape,
        scratch_shapes=scratch,
        compiler_params=_cparams(1),
        name="gmlp",
    )(z, z, g_gmlp.reshape(1, -1), w_spatial, bs_full)


def _pool_body(x_ref, prev_ref, wp_ref, sc_ref, o_ref, st_ref, ext_ref, *, rows, n_new, start, zero_first_prev):
    ib = pl.program_id(1)
    prev = prev_ref[...]
    if zero_first_prev:
        prev = jnp.where(ib == 0, 0.0, prev)
    x = x_ref[...]
    ext_ref[0:16, :] = prev
    ext_ref[16:16 + rows, :] = x
    st_ref[...] = ext_ref[pl.ds(n_new + 1, POOL_STATE), :]
    ext = ext_ref[...]
    pos = start + ib * rows + lax.broadcasted_iota(jnp.int32, (rows, 1), 0)
    for gi, win in enumerate(POOL_WINDOWS):
        sl = slice(gi * HD, (gi + 1) * HD)
        s = ext[:, sl]
        k = 1
        while k < win:
            s = s + pltpu.roll(s, k, axis=0)
            k *= 2
        cnt = jnp.minimum(pos + 1, win).astype(F32)
        diff = s[16:] / cnt - x[:, sl]
        y = jnp.dot(diff.astype(BF16), wp_ref[gi].astype(BF16), preferred_element_type=F32)
        o_ref[:, sl] = (y * sc_ref[:, sl]).astype(BF16)


def _pool(z, prev, w_pool, pool_scale, *, nseq, rows, n_new, start, layer=None):
    m = z.shape[0]
    nblk = m // nseq // rows
    if prev is None:
        per16 = rows // 16
        prev_arr = z
        prev_spec = pl.BlockSpec((16, BW), lambda b, i: (jnp.maximum((b * nblk + i) * per16 - 1, 0), COL_CIN))
    else:
        prev_arr = prev
        prev_spec = pl.BlockSpec((None, None, 16, BW), lambda b, i: (layer, b, 0, 0))
    return pl.pallas_call(
        functools.partial(_pool_body, rows=rows, n_new=n_new, start=start, zero_first_prev=prev is None),
        grid=(nseq, nblk),
        in_specs=[pl.BlockSpec((rows, BW), lambda b, i: (b * nblk + i, COL_CIN)),
                  prev_spec,
                  pl.BlockSpec((NH, HD, HD), lambda b, i: (0, 0, 0)),
                  pl.BlockSpec((1, BW), lambda b, i: (0, 0))],
        out_specs=[pl.BlockSpec((rows, BW), lambda b, i: (b * nblk + i, 0)),
                   pl.BlockSpec((None, POOL_STATE, BW), lambda b, i: (b, 0, 0))],
        out_shape=[jax.ShapeDtypeStruct((m, BW), BF16),
                   jax.ShapeDtypeStruct((nseq, POOL_STATE, BW), F32)],
        scratch_shapes=[pltpu.VMEM((16 + rows, BW), F32)],
        compiler_params=_cparams(2),
        name="pool",
    )(z, prev_arr, w_pool, pool_scale.reshape(1, -1))


def _gmlp_sample_body(bu_ref, bv_ref, g_ref, wk_ref, bs_ref, o_ref, vn_ref):
    u = jax.nn.gelu(bu_ref[...])
    vn = _rms(jax.nn.gelu(bv_ref[...]), g_ref[...])
    vn_ref[...] = vn
    mixed = jnp.tile(bs_ref[...], (DEC_BATCH, 1))
    for k in range(DEC_SEQ):
        shifted = vn if k == 0 else pltpu.roll(vn, k, axis=0)
        mixed = mixed + jnp.tile(wk_ref[k], (DEC_BATCH, 1)) * shifted
    o_ref[...] = (u * mixed).astype(BF16)


def _gmlp_sample(z, g_gmlp, w_spatial, b_spatial):
    m = z.shape[0]
    t = np.arange(SROWS)
    live = t < DEC_SEQ
    wk = []
    for k in range(DEC_SEQ):
        ok = live & (t - k >= 0)
        diag = w_spatial[:, np.where(ok, t, 0), np.where(ok, t - k, 0)] * jnp.asarray(ok, F32)
        wk.append(jnp.repeat(diag.T, HD, axis=1))
    bs = jnp.repeat((b_spatial[:, :SROWS] * jnp.asarray(live, F32)).T, HD, axis=1)
    return pl.pallas_call(
        _gmlp_sample_body,
        grid=(1,),
        in_specs=[pl.BlockSpec((m, BW), lambda i: (0, COL_BU)),
                  pl.BlockSpec((m, BW), lambda i: (0, COL_BV)),
                  pl.BlockSpec((1, BW), lambda i: (0, 0)),
                  pl.BlockSpec((DEC_SEQ, SROWS, BW), lambda i: (0, 0, 0)),
                  pl.BlockSpec((SROWS, BW), lambda i: (0, 0))],
        out_specs=[pl.BlockSpec((m, BW), lambda i: (0, 0))] * 2,
        out_shape=[jax.ShapeDtypeStruct((m, BW), BF16), jax.ShapeDtypeStruct((m, BW), F32)],
        compiler_params=_cparams(1),
        name="gmlp_sample",
    )(z, z, g_gmlp.reshape(1, -1), jnp.stack(wk), bs)


POOL_PAD = 16


def _pool_sample_body(x_ref, prev_ref, wp_ref, sc_ref, o_ref, st_ref, ext_ref, *, start):
    per = POOL_PAD + SROWS
    ext_ref[:, 0:POOL_PAD, :] = prev_ref[...]
    ext_ref[:, POOL_PAD:per, :] = x_ref[...]
    st_ref[...] = ext_ref[:, pl.ds(DEC_SEQ + 1, POOL_STATE), :]
    ext = ext_ref[...].reshape(DEC_BATCH * per, BW)
    pos = start + (lax.broadcasted_iota(jnp.int32, (DEC_BATCH * per, 1), 0) % per - POOL_PAD)
    for gi, win in enumerate(POOL_WINDOWS):
        sl = slice(gi * HD, (gi + 1) * HD)
        x = ext[:, sl]
        s = x
        k = 1
        while k < win:
            s = s + pltpu.roll(s, k, axis=0)
            k *= 2
        cnt = jnp.minimum(jnp.maximum(pos, 0) + 1, win).astype(F32)
        y = jnp.dot((s / cnt - x).astype(BF16), wp_ref[gi].astype(BF16), preferred_element_type=F32)
        o_ref[:, :, sl] = (y * sc_ref[:, sl]).reshape(DEC_BATCH, per, HD)[:, POOL_PAD:, :].astype(BF16)


def _pool_sample(z, prev, w_pool, pool_scale, *, start, layer):
    m, zw = z.shape
    z3 = z.reshape(DEC_BATCH, SROWS, zw)
    o, st = pl.pallas_call(
        functools.partial(_pool_sample_body, start=start),
        grid=(1,),
        in_specs=[pl.BlockSpec((DEC_BATCH, SROWS, BW), lambda i: (0, 0, COL_CIN)),
                  pl.BlockSpec((None, DEC_BATCH, POOL_PAD, BW), lambda i: (layer, 0, 0, 0)),
                  pl.BlockSpec((NH, HD, HD), lambda i: (0, 0, 0)),
                  pl.BlockSpec((1, BW), lambda i: (0, 0))],
        out_specs=[pl.BlockSpec((DEC_BATCH, SROWS, BW), lambda i: (0, 0, 0)),
                   pl.BlockSpec((DEC_BATCH, POOL_STATE, BW), lambda i: (0, 0, 0))],
        out_shape=[jax.ShapeDtypeStruct((DEC_BATCH, SROWS, BW), BF16),
                   jax.ShapeDtypeStruct((DEC_BATCH, POOL_STATE, BW), F32)],
        scratch_shapes=[pltpu.VMEM((DEC_BATCH, POOL_PAD + SROWS, BW), F32)],
        compiler_params=_cparams(1),
        name="pool_sample",
    )(z3, prev, w_pool, pool_scale.reshape(1, -1))
    return o.reshape(m, BW), st


def _ret_tables(c_eff, positions, size):
    lg = np.log1p(-np.power(2.0, -5.0 - np.arange(NH, dtype=np.float64)))
    i = np.arange(size, dtype=np.float64)
    live = (i < c_eff)
    diff = i[:, None] - i[None, :]
    inner = np.where((diff >= 0) & live[:, None] & live[None, :], np.exp(np.maximum(diff, 0.0)[None] * lg[:, None, None]), 0.0)
    qd = np.where(live[None, :], np.exp((i + 1.0)[None, :] * lg[:, None]), 0.0)
    kd = np.where(live[None, :], np.exp((c_eff - 1.0 - i)[None, :] * lg[:, None]), 0.0)
    chunk = tuple(float(v) for v in np.exp(c_eff * lg))
    qd_full = np.repeat(qd.T, HD, axis=1)
    kd_full = np.repeat(kd.T, HD, axis=1) * (HD ** -0.5)
    half = HD // 2
    inv = ROPE_BASE ** (-np.arange(half, dtype=np.float64) / half)
    ang = np.asarray(positions, np.float64)[:, None] * inv[None, :]
    cosf = np.concatenate([np.cos(ang), np.cos(ang)], axis=1)
    sinf = np.concatenate([-np.sin(ang), np.sin(ang)], axis=1)
    to32 = lambda a: jnp.asarray(a.astype(np.float32))
    return to32(inner), to32(qd_full), to32(kd_full), chunk, to32(cosf), to32(sinf)


def _ret_body(q_ref, k_ref, v_ref, g_ref, cos_ref, sin_ref, inner_ref, qd_ref, kd_ref, gr_ref, *refs,
              nseq, rows, chunk_decay, has_state):
    refs = list(refs)
    s0_ref = refs.pop(0) if has_state else None
    o_ref, sn_ref, s_ref = refs[:3]
    pad_ref = refs[3] if rows < CH else None
    ic = pl.program_id(1)

    @pl.when(ic == 0)
    def _():
        if has_state:
            s_ref[...] = s0_ref[...]
        else:
            s_ref[...] = jnp.zeros_like(s_ref)

    def chunk(ref, b, k):
        if rows >= CH:
            return ref[b]
        pad_ref[k] = jnp.zeros((CH, BW), F32)
        pad_ref[k, 0:rows, :] = ref[b]
        return pad_ref[k]

    cosf = cos_ref[...]
    sinf = sin_ref[...]
    for b in range(nseq):
        q = chunk(q_ref, b, 0)
        k = chunk(k_ref, b, 1)
        v = chunk(v_ref, b, 2)
        gate = chunk(g_ref, b, 3)
        for h in range(NH):
            sl = slice(h * HD, (h + 1) * HD)
            qh = q[:, sl]
            kh = k[:, sl]
            rq = qh * cosf + pltpu.roll(qh, HD // 2, axis=1) * sinf
            rk = kh * cosf + pltpu.roll(kh, HD // 2, axis=1) * sinf
            vb = v[:, sl].astype(BF16)
            rqb = rq.astype(BF16)
            state = s_ref[b, h]
            att = lax.dot_general(rqb, (rk * (HD ** -0.5)).astype(BF16), (((1,), (1,)), ((), ())),
                                  preferred_element_type=F32) * inner_ref[h]
            o = (jnp.dot(att.astype(BF16), vb, preferred_element_type=F32)
                 + jnp.dot(rqb, state.astype(BF16), preferred_element_type=F32) * qd_ref[:, sl])
            kdec = (rk * kd_ref[:, sl]).astype(BF16)
            new_state = state * chunk_decay[h] + lax.dot_general(
                kdec, vb, (((0,), (0,)), ((), ())), preferred_element_type=F32)
            s_ref[b, h] = new_state
            sn_ref[b, h] = new_state
            o = o * lax.rsqrt(jnp.mean(o * o, axis=-1, keepdims=True) + EPS)
            gt = gate[:, sl]
            o_ref[b, :, sl] = (o * gr_ref[:, sl] * (gt * jax.nn.sigmoid(gt)))[:rows].astype(BF16)


def _retention(z, g_ret, state, *, nseq, per_step, rows, c_eff, positions, layer=None):
    m, zw = z.shape
    seq_rows = m // nseq
    nchunk = seq_rows // rows
    size = max(rows, CH)
    inner, qd, kd, chunk_decay, cosf, sinf = _ret_tables(c_eff, positions, size)
    has_state = state is not None
    z3 = z.reshape(nseq, seq_rows, zw)

    def zspec(col):
        return pl.BlockSpec((per_step, rows, BW), lambda g, i: (g, i, col))

    const2 = lambda g, i: (0, 0)
    in_specs = [zspec(COL_DQ), zspec(COL_DK), zspec(COL_DV), zspec(COL_DG),
                pl.BlockSpec((size, HD), lambda g, i: (i, 0)), pl.BlockSpec((size, HD), lambda g, i: (i, 0)),
                pl.BlockSpec((NH, size, size), lambda g, i: (0, 0, 0)),
                pl.BlockSpec((size, BW), const2), pl.BlockSpec((size, BW), const2), pl.BlockSpec((1, BW), const2)]
    args = [z3, z3, z3, z3, cosf, sinf, inner, qd, kd, g_ret.reshape(1, -1)]
    if has_state:
        in_specs.append(pl.BlockSpec((None, per_step, NH, HD, HD), lambda g, i: (layer, g, 0, 0, 0)))
        args.append(state)
    scratch = [pltpu.VMEM((per_step, NH, HD, HD), F32)]
    if rows < CH:
        scratch.append(pltpu.VMEM((4, CH, BW), F32))
    o, sn = pl.pallas_call(
        functools.partial(_ret_body, nseq=per_step, rows=rows, chunk_decay=chunk_decay, has_state=has_state),
        grid=(nseq // per_step, nchunk),
        in_specs=in_specs,
        out_specs=[pl.BlockSpec((per_step, rows, BW), lambda g, i: (g, i, 0)),
                   pl.BlockSpec((per_step, NH, HD, HD), lambda g, i: (g, 0, 0, 0))],
        out_shape=[jax.ShapeDtypeStruct((nseq, seq_rows, BW), BF16),
                   jax.ShapeDtypeStruct((nseq, NH, HD, HD), F32)],
        scratch_shapes=scratch,
        compiler_params=_cparams(2),
        name="retention",
    )(*args)
    return o.reshape(m, BW), sn


def kernel(x_prompt, x_sample, cache_attn_kv_w128, cache_attn_kv_w512, cache_attn_kv_w2048, state_pool, state_ret, rel_bias, g_ffn1, w_ffn1_gate, w_ffn1_up, w_ffn1_down, g_mix, w_in, g_gmlp, w_spatial, b_spatial, w_pool, pool_scale, g_ret, w_branch, w_out, g_ffn2, w_ffn2_gate, w_ffn2_up, w_ffn2_down, g_final):
    caches = (cache_attn_kv_w128, cache_attn_kv_w512, cache_attn_kv_w2048)
    xp = x_prompt.reshape(BATCH * SEQ, D_MODEL)
    xs = jnp.pad(x_sample, ((0, 0), (0, SROWS - DEC_SEQ), (0, 0))).reshape(DEC_BATCH * SROWS, D_MODEL)
    pool_state = jnp.pad(state_pool, ((0, 0), (0, 0), (1, 0), (0, 0)))
    tm_p, tm_s = 1024, DEC_BATCH * SROWS
    sample_pos = PAST_LEN + np.arange(CH)
    wb = w_branch.astype(BF16)
    wo = w_out.astype(BF16)
    gate_col0 = N_PLAIN * BW // 1024
    qkv1, qkv2 = COL_QKV[1][0], COL_QKV[2][0]

    kv_p = [None, None, None]
    kv_s = [[], [], []]
    pool_p, pool_s, ret_p, ret_s, gv_s = [], [], [], [], []
    yp = ys = None
    for l in range(DEPTH):
        last = l == DEPTH - 1

        xs, hs, *w1 = _ffn(xs, g_ffn1[l], (w_ffn1_gate, w_ffn1_up, w_ffn1_down), g_mix[l], layer=l,
                           emit_x=True, post_dtype=BF16, tm=tm_s)
        zs, gs, win = _in_proj_convert(hs, w_in, layer=l)
        oa, k0, k1, k2 = _attn_sample(zs, caches, rel_bias, l)
        for gi, kv in enumerate((k0, k1, k2)):
            kv_s[gi].append(kv.reshape(DEC_BATCH, SROWS, 2, NH, HD)[:, :DEC_SEQ])
        ob, vn = _gmlp_sample(zs, g_gmlp[l], w_spatial[l], b_spatial[l])
        gv_s.append(vn.reshape(DEC_BATCH, SROWS, BW)[:, :DEC_SEQ])
        oc, pn = _pool_sample(zs, pool_state, w_pool[l], pool_scale[l], start=PAST_LEN, layer=l)
        od, rn = _retention(zs, g_ret[l], state_ret, nseq=DEC_BATCH, per_step=4, rows=SROWS, c_eff=DEC_SEQ,
                            positions=sample_pos, layer=l)
        pool_s.append(pn)
        ret_s.append(rn)
        mix = _branch_merge(oa, ob, oc, od, gs, wb, layer=l, tm=tm_s)
        xs = _matmul(mix, wo, xs, layer=l, tm=tm_s, tn=512, name="out_proj")
        if last:
            ys, *w2 = _ffn(xs, g_ffn2[l], (w_ffn2_gate, w_ffn2_up, w_ffn2_down), g_final, layer=l,
                           emit_x=False, post_dtype=F32, tm=tm_s)
        else:
            xs, *w2 = _ffn(xs, g_ffn2[l], (w_ffn2_gate, w_ffn2_up, w_ffn2_down), g_final, layer=l,
                           emit_x=True, post_dtype=None, tm=tm_s)

        xp, hp = _ffn(xp, g_ffn1[l], w1, g_mix[l], emit_x=True, post_dtype=BF16, tm=tm_p)
        zp = _matmul(hp, win, n=N_PLAIN * BW, tm=2048, tn=1024, name="in_proj")
        gp = _matmul(hp, win, col0=gate_col0, n=GATE_W, gate=True, tm=2048, tn=1024, name="in_proj_gate")
        zq = (zp,
              _matmul_deint(hp, win, col0=qkv1, dil=ATT_DILATIONS[1], name="in_proj_g1"),
              _matmul_deint(hp, win, col0=qkv2, dil=ATT_DILATIONS[2], name="in_proj_g2"))
        parts = []
        for gi in range(3):
            o, lse, kv_p[gi] = _attn_prompt(zq[gi], rel_bias, gi, kv_p[gi])
            parts += [o, lse]
        ob, = _gmlp(zp, g_gmlp[l], w_spatial[l], b_spatial[l], rows=4 * CH, emit_vn=False)
        oc, pn = _pool(zp, None, w_pool[l], pool_scale[l], nseq=BATCH, rows=512, n_new=512, start=0)
        od, rn = _retention(zp, g_ret[l], None, nseq=BATCH, per_step=BATCH, rows=CH, c_eff=CH,
                            positions=np.arange(SEQ))
        pool_p.append(pn)
        ret_p.append(rn)
        xp = _merge(parts, ob, oc, od, gp, wb, wo, xp, layer=l)
        if last:
            yp, = _ffn(xp, g_ffn2[l], w2, g_final, emit_x=False, post_dtype=F32, tm=tm_p)
        else:
            xp, = _ffn(xp, g_ffn2[l], w2, g_final, emit_x=True, post_dtype=None, tm=tm_p)

    y_prompt = yp.reshape(BATCH, SEQ, D_MODEL)
    y_sample = ys.reshape(DEC_BATCH, SROWS, D_MODEL)[:, :DEC_SEQ]
    kv_p = [kv.reshape(DEPTH, BATCH, ATT_WINDOWS[gi], 2, NH, HD) for gi, kv in enumerate(kv_p)]
    return (y_prompt, y_sample,
            kv_p[0], kv_p[1], kv_p[2],
            jnp.stack(kv_s[0]), jnp.stack(kv_s[1]), jnp.stack(kv_s[2]),
            jnp.stack(pool_p), jnp.stack(pool_s),
            jnp.stack(ret_p), jnp.stack(ret_s),
            jnp.stack(gv_s))
```

```python
import functools

import numpy as np
import jax
import jax.numpy as jnp
from jax import lax
from jax.experimental import pallas as pl
from jax.experimental.pallas import tpu as pltpu

F32 = jnp.float32
BF16 = jnp.bfloat16

D_MODEL = 2048
BATCH = 4
SEQ = 2048
DEPTH = 2
DEC_BATCH = 32
DEC_SEQ = 4
PAST_LEN = 8192
D_FF = 5632
IN_WIDTH = 16384
EPS = 1e-6
BW = 512
HD = 128
NH = 4
ATT_WINDOWS = (128, 512, 2048)
ATT_DILATIONS = (1, 4, 16)
NKEY = 129
REL_BUCKETS = 32
REL_MAX_DISTANCE = 2048
POOL_WINDOWS = (2, 4, 8, 16)
POOL_STATE = 15
ROPE_BASE = 10000.0
SROWS = 8
CH = 128
NEG = -1e30
LSE_W = 32

W_IN_ORDER = (0, 3, 6) + tuple(range(9, 32)) + (1, 4, 7) + (2, 5, 8)
COL_QKV = ((0, 1, 2), (26, 27, 28), (29, 30, 31))
COL_BU, COL_BV, COL_CIN = 3, 4, 5
COL_DQ, COL_DK, COL_DV, COL_DG = 6, 7, 8, 9
N_PLAIN = 10
N_GATE = 16
GATE_W = N_GATE * BW
ZBLK = IN_WIDTH // BW

VMEM_BIG = 56 * 1024 * 1024
VMEM_FFN = 62 * 1024 * 1024


def _cparams(n_axes, vmem=None):
    return pltpu.CompilerParams(dimension_semantics=("arbitrary",) * n_axes, vmem_limit_bytes=vmem)


def _rms(x, g):
    return x * lax.rsqrt(jnp.mean(x * x, axis=-1, keepdims=True) + EPS) * g


def _ffn_body(x_ref, gpre_ref, *refs, emit_x, emit_post, convert, nf, tf, row_split):
    refs = list(refs)
    if convert:
        wg_ref, wu_ref, wd_ref, gpost_ref = refs[:4]
        refs = refs[4:]
    else:
        wgu_ref, wd_ref, gpost_ref = refs[:3]
        refs = refs[3:]
    n_act = int(emit_x) + int(emit_post)
    outs = refs[:n_act]
    scratch = refs[n_act + (2 if convert else 0):]
    xn_ref = scratch[0]
    acc_ref = outs[0]
    f = pl.program_id(1)

    @pl.when(f == 0)
    def _():
        x = x_ref[...]
        xn_ref[...] = _rms(x, gpre_ref[...]).astype(BF16)
        acc_ref[...] = x

    if convert:
        wgu = jnp.concatenate([wg_ref[...].astype(BF16), wu_ref[...].astype(BF16)], axis=1)
        wd = wd_ref[...].astype(BF16)
        refs[n_act][...] = wgu
        refs[n_act + 1][...] = wd
    else:
        wgu, wd = wgu_ref[...], wd_ref[...]
    rows = xn_ref.shape[0] // row_split
    for part in range(row_split):
        rs = slice(part * rows, (part + 1) * rows)
        r = jnp.dot(xn_ref[rs, :], wgu, preferred_element_type=F32)
        g, u = r[:, :tf], r[:, tf:]
        h = (g * jax.nn.sigmoid(g) * (0.5 * u)).astype(BF16)
        acc_ref[rs, :] += jnp.dot(h, wd, preferred_element_type=F32)

    if emit_post:
        @pl.when(f == nf - 1)
        def _():
            outs[-1][...] = _rms(acc_ref[...], gpost_ref[...]).astype(outs[-1].dtype)


def _ffn(x, g_pre, weights, g_post, *, layer=None, emit_x, post_dtype, tm, tf=512):
    m = x.shape[0]
    nf = D_FF // tf
    emit_post = post_dtype is not None
    convert = layer is not None
    assert emit_x or post_dtype == F32, "the first output block is the f32 accumulator"
    once = dict(pipeline_mode=pl.Buffered(1)) if tm > 512 else {}
    row_spec = pl.BlockSpec((tm, D_MODEL), lambda i, f: (i, 0))
    row_out_spec = pl.BlockSpec((tm, D_MODEL), lambda i, f: (i, 0), **once)
    vec_spec = pl.BlockSpec((1, D_MODEL), lambda i, f: (0, 0))
    out_shape, out_specs = [], []
    if emit_x:
        out_shape.append(jax.ShapeDtypeStruct((m, D_MODEL), F32))
        out_specs.append(row_spec)
    if emit_post:
        out_shape.append(jax.ShapeDtypeStruct((m, D_MODEL), post_dtype))
        out_specs.append(row_out_spec if emit_x else row_spec)
    gu_spec = pl.BlockSpec((D_MODEL, 2 * tf), lambda i, f: (0, f))
    down_spec = pl.BlockSpec((tf, D_MODEL), lambda i, f: (f, 0))
    if convert:
        assert m == tm, "the bf16 weights are written once, by a single row tile"
        w_specs = [pl.BlockSpec((None, D_MODEL, tf), lambda i, f: (layer, 0, f)),
                   pl.BlockSpec((None, D_MODEL, tf), lambda i, f: (layer, 0, f)),
                   pl.BlockSpec((None, tf, D_MODEL), lambda i, f: (layer, f, 0))]
        out_shape += [jax.ShapeDtypeStruct((D_MODEL, 2 * D_FF), BF16), jax.ShapeDtypeStruct((D_FF, D_MODEL), BF16)]
        out_specs += [gu_spec, down_spec]
    else:
        w_specs = [gu_spec, down_spec]
    scratch = [pltpu.VMEM((tm, D_MODEL), BF16)]
    return pl.pallas_call(
        functools.partial(_ffn_body, emit_x=emit_x, emit_post=emit_post, convert=convert, nf=nf, tf=tf,
                          row_split=max(tm // 512, 1)),
        grid=(m // tm, nf),
        in_specs=[row_spec, vec_spec, *w_specs, vec_spec],
        out_specs=out_specs,
        out_shape=out_shape,
        scratch_shapes=scratch,
        compiler_params=_cparams(2, VMEM_FFN),
        name="ffn",
    )(x, g_pre.reshape(1, -1), *weights, g_post.reshape(1, -1))


def _mm_body(a_ref, b_ref, o_ref):
    o_ref[...] = jnp.dot(a_ref[...], b_ref[...], preferred_element_type=F32)


def _mm_res_body(a_ref, b_ref, r_ref, o_ref):
    o_ref[...] = r_ref[...] + jnp.dot(a_ref[...], b_ref[...], preferred_element_type=F32)


def _sigmoid(x):
    return 0.5 * jnp.tanh(0.5 * x) + 0.5


def _mm_gate_body(a_ref, b_ref, o_ref):
    o_ref[...] = _sigmoid(jnp.dot(a_ref[...], b_ref[...], preferred_element_type=F32)).astype(BF16)


def _matmul(a, b, res=None, *, layer=None, col0=0, n=None, gate=False, tm, tn, name):
    m, k = a.shape
    n = b.shape[-1] if n is None else n
    if layer is None:
        b_spec = pl.BlockSpec((k, tn), lambda i, j: (0, col0 + j))
    else:
        b_spec = pl.BlockSpec((None, k, tn), lambda i, j: (layer, 0, col0 + j))
    in_specs = [pl.BlockSpec((tm, k), lambda i, j: (i, 0)), b_spec]
    args = [a, b]
    body = _mm_gate_body if gate else _mm_body
    if res is not None:
        in_specs.append(pl.BlockSpec((tm, tn), lambda i, j: (i, j)))
        args.append(res)
        body = _mm_res_body
    return pl.pallas_call(
        body,
        grid=(m // tm, n // tn),
        in_specs=in_specs,
        out_specs=pl.BlockSpec((tm, tn), lambda i, j: (i, j)),
        out_shape=jax.ShapeDtypeStruct((m, n), BF16 if gate else F32),
        compiler_params=_cparams(2, VMEM_BIG),
        name=name,
    )(*args)


def _mm_deint_body(a_ref, b_ref, o_ref, s_ref, *, dil):
    r = jnp.dot(a_ref[...], b_ref[...], preferred_element_type=F32)
    for h in range(NH):
        s_ref[h] = r[:, h * HD:(h + 1) * HD]
    n = SEQ // dil
    for h in range(NH):
        for res in range(dil):
            o_ref[h, pl.ds(res * n, n), :] = s_ref[h, pl.ds(res, n, stride=dil), :]


def _w_in_source_block(j, where=jnp.where):
    return where(j < 3, 3 * j, where(j < 26, j + 6, where(j < 29, 3 * (j - 26) + 1, 3 * (j - 29) + 2)))


assert tuple(_w_in_source_block(np.arange(ZBLK), np.where)) == W_IN_ORDER


def _in_proj_convert_body(a_ref, b_ref, z_ref, gate_ref, w_ref):
    j = pl.program_id(0)
    w = b_ref[...].astype(BF16)
    w_ref[...] = w
    r = jnp.dot(a_ref[...], w, preferred_element_type=F32)
    z_ref[...] = r

    @pl.when((j >= N_PLAIN) & (j < N_PLAIN + N_GATE))
    def _():
        gate_ref[...] = _sigmoid(r).astype(BF16)


def _in_proj_convert(a, w_in, *, layer):
    m, k = a.shape
    return pl.pallas_call(
        _in_proj_convert_body,
        grid=(ZBLK,),
        in_specs=[pl.BlockSpec((m, k), lambda j: (0, 0)),
                  pl.BlockSpec((None, k, BW), lambda j: (layer, 0, _w_in_source_block(j)))],
        out_specs=[pl.BlockSpec((m, BW), lambda j: (0, j)),
                   pl.BlockSpec((m, BW), lambda j: (0, jnp.clip(j - N_PLAIN, 0, N_GATE - 1))),
                   pl.BlockSpec((k, BW), lambda j: (0, j))],
        out_shape=[jax.ShapeDtypeStruct((m, IN_WIDTH), F32),
                   jax.ShapeDtypeStruct((m, GATE_W), BF16),
                   jax.ShapeDtypeStruct((k, IN_WIDTH), BF16)],
        compiler_params=_cparams(1, VMEM_BIG),
        name="in_proj_convert",
    )(a, w_in)


def _matmul_deint(a, b, *, col0, dil, name):
    m, k = a.shape
    return pl.pallas_call(
        functools.partial(_mm_deint_body, dil=dil),
        grid=(BATCH, 3),
        in_specs=[pl.BlockSpec((SEQ, k), lambda b, j: (b, 0)),
                  pl.BlockSpec((k, BW), lambda b, j: (0, col0 + j))],
        out_specs=pl.BlockSpec((None, NH, SEQ, HD), lambda b, j: (j, 0, b, 0)),
        out_shape=jax.ShapeDtypeStruct((3, NH, m, HD), F32),
        scratch_shapes=[pltpu.VMEM((NH, SEQ, HD), F32)],
        compiler_params=_cparams(2, VMEM_BIG),
        name=name,
    )(a, b)


def _branch_body(oa_ref, ob_ref, oc_ref, od_ref, gate_ref, wb_ref, o_ref):
    acc = None
    for n, br in enumerate((oa_ref, ob_ref, oc_ref, od_ref)):
        proj = jnp.dot(br[...], wb_ref[n], preferred_element_type=F32)
        t = gate_ref[:, n * D_MODEL:(n + 1) * D_MODEL].astype(F32) * proj
        acc = t if acc is None else acc + t
    o_ref[...] = acc.astype(BF16)


def _branch_merge(oa, ob, oc, od, gates, wb, *, layer, tm):
    m = oa.shape[0]
    br_spec = pl.BlockSpec((tm, BW), lambda i: (i, 0))
    return pl.pallas_call(
        _branch_body,
        grid=(m // tm,),
        in_specs=[br_spec] * 4 + [pl.BlockSpec((tm, GATE_W), lambda i: (i, 0)),
                                  pl.BlockSpec((None, 4, BW, D_MODEL), lambda i: (layer, 0, 0, 0))],
        out_specs=pl.BlockSpec((tm, D_MODEL), lambda i: (i, 0)),
        out_shape=jax.ShapeDtypeStruct((m, D_MODEL), BF16),
        compiler_params=_cparams(1, VMEM_BIG),
        name="branch_merge",
    )(oa, ob, oc, od, gates, wb)


def _t5_buckets(dist):
    max_exact = REL_BUCKETS // 2
    d = np.maximum(dist, 1).astype(np.float32)
    large = max_exact + (np.log(d / max_exact) / np.log(REL_MAX_DISTANCE / max_exact)
                         * (REL_BUCKETS - max_exact)).astype(np.int32)
    large = np.minimum(large, REL_BUCKETS - 1)
    return np.where(dist < max_exact, dist, large).astype(np.int32)


def _bucket_of_step(dil):
    return _t5_buckets(dil * np.arange(NKEY))


def _prompt_bucket_matrix(dil):
    iq = np.arange(CH)[:, None]
    col = np.arange(2 * CH)[None, :]
    j = iq + CH - col
    valid = (j >= 0) & (j < NKEY)
    return np.where(valid, _bucket_of_step(dil)[np.clip(j, 0, NKEY - 1)], -1).astype(np.int32)


SAMPLE_CACHE_ROWS = (8 * ATT_WINDOWS[0], 8 * ATT_WINDOWS[1], 8 * DEC_SEQ * CH)


def _sample_bucket_tables():
    h = np.repeat(np.arange(NH), SROWS)[:, None]
    t = np.tile(np.arange(SROWS), NH)[:, None]
    lane = np.arange(CH)[None, :]
    tables = []
    for gi, dil in enumerate(ATT_DILATIONS):
        bos = _bucket_of_step(dil)
        flat = np.arange(SAMPLE_CACHE_ROWS[gi])[None, :]
        head, is_k = flat % NH, (flat // NH) % 2 == 0
        if gi < 2:
            w = flat // 8
            steps = ATT_WINDOWS[gi] + t - w
            valid = (steps % dil == 0) & (steps // dil < NKEY)
            j = steps // dil
        else:
            res, i = (flat // 8) % DEC_SEQ, flat // (8 * DEC_SEQ)
            valid = res == t
            j = np.broadcast_to(CH - i, valid.shape)
        valid = valid & is_k & (head == h) & (t < DEC_SEQ)
        cache_tbl = np.where(valid, bos[np.clip(j, 0, NKEY - 1)], -1)
        jn = t - lane
        validn = (t < DEC_SEQ) & (lane < DEC_SEQ) & (jn >= 0) & ((dil == 1) | (jn == 0))
        new_tbl = np.where(validn, bos[np.clip(jn, 0, NKEY - 1)], -1)
        tables.append((cache_tbl.astype(np.int32), new_tbl.astype(np.int32)))
    return tables


def _bias_from_buckets(bk, rb_ref, col):
    out = jnp.full(bk.shape, NEG, F32)
    for b in range(REL_BUCKETS):
        out = jnp.where(bk == b, rb_ref[b, col], out)
    return out


ATT_SUB = 4


def _attn_prompt_body(rb_ref, bk_ref, q_ref, k_ref, v_ref, *refs, gi, headed):
    dil = ATT_DILATIONS[gi]
    has_prev = gi == 0
    if has_prev:
        kp_ref, vp_ref = refs[:2]
        refs = refs[2:]
    o_ref, lse_ref, kvo_ref, bias_ref = refs[-4:]
    if len(refs) == 5:
        kvo_ref[0] = refs[0][...]
        kvo_ref = kvo_ref.at[1]
    step = pl.program_id(1)

    @pl.when((pl.program_id(0) == 0) & (step == 0))
    def _():
        bk = bk_ref[...]
        for h in range(NH):
            bias_ref[h] = _bias_from_buckets(bk, rb_ref, gi * NH + h)

    def tile(ref, h, t):
        rs = slice(t * CH, (t + 1) * CH)
        return ref[h, rs, :] if headed else ref[rs, h * HD:(h + 1) * HD]

    scale = HD ** -0.5
    lane_head = lax.broadcasted_iota(jnp.int32, (CH, NH * LSE_W), 1) // LSE_W
    if has_prev:
        col = lax.broadcasted_iota(jnp.int32, (CH, 2 * CH), 1)
        no_prev = (step == 0) & (col < CH)
    for t in range(ATT_SUB):
        if gi == 0:
            dst = slice(t * CH, (t + 1) * CH)
        elif gi == 1:
            dst = pl.ds(t * CH * dil + step, CH, stride=dil)
        else:
            dst = pl.ds(step * ATT_SUB + t, CH, stride=dil)
        keep = gi == 2 or t == ATT_SUB - 1
        kv_row0 = (t if gi == 2 else 0) * 2 * NH
        lse_tile = None
        for h in range(NH):
            kc = tile(k_ref, h, t)
            vc = tile(v_ref, h, t)
            if keep:
                kvo_ref[:, kv_row0 + h, :] = kc
                kvo_ref[:, kv_row0 + NH + h, :] = vc
            qh = tile(q_ref, h, t).astype(BF16)
            if gi < 2 and (t > 0 or has_prev):
                if t > 0:
                    kp, vp, bias = tile(k_ref, h, t - 1), tile(v_ref, h, t - 1), bias_ref[h]
                else:
                    kp, vp = kp_ref[:, h * HD:(h + 1) * HD], vp_ref[:, h * HD:(h + 1) * HD]
                    bias = jnp.where(no_prev, NEG, bias_ref[h])
                kh = jnp.concatenate([kp, kc], axis=0).astype(BF16)
                vh = jnp.concatenate([vp, vc], axis=0).astype(BF16)
            else:
                kh = kc.astype(BF16)
                vh = vc.astype(BF16)
                bias = bias_ref[h][:, CH:]
            s = lax.dot_general(qh, kh, (((1,), (1,)), ((), ())), preferred_element_type=F32) * scale + bias
            m = jnp.max(s, axis=-1, keepdims=True)
            p = jnp.exp(s - m)
            den = jnp.sum(p, axis=-1, keepdims=True)
            acc = jnp.dot(p.astype(BF16), vh, preferred_element_type=F32)
            o_ref[h, dst, :] = acc / den
            lse = m + jnp.log(den)
            lse_tile = lse if h == 0 else jnp.where(lane_head == h, lse, lse_tile)
        lse_ref[dst, :] = lse_tile


def _attn_prompt(zq, rel_bias, gi, kv_prev):
    dil = ATT_DILATIONS[gi]
    rows = BATCH * SEQ
    nsteps = SEQ // (ATT_SUB * CH)
    assert SEQ // dil // CH in (1, ATT_SUB, ATT_SUB * nsteps)
    headed = gi > 0
    bk = jnp.asarray(_prompt_bucket_matrix(dil))
    span = ATT_SUB * CH

    def zspec(c):
        if headed:
            return pl.BlockSpec((None, NH, span, HD), lambda b, s: (c, 0, b * nsteps + s, 0))
        return pl.BlockSpec((span, BW), lambda b, s: (b * nsteps + s, COL_QKV[0][c]))

    in_specs = [pl.BlockSpec(memory_space=pltpu.SMEM),
                pl.BlockSpec((CH, 2 * CH), lambda b, s: (0, 0)),
                zspec(0), zspec(1), zspec(2)]
    args = [rel_bias, bk, zq, zq, zq]
    if gi == 0:
        above = lambda c: pl.BlockSpec(
            (CH, BW), lambda b, s: (jnp.maximum((b * nsteps + s) * ATT_SUB - 1, 0), COL_QKV[0][c]))
        in_specs += [above(1), above(2)]
        args += [zq, zq]
        out_rows, out_index = span, (lambda b, s: b * nsteps + s)
    else:
        out_rows, out_index = SEQ, (lambda b, s: b)
    n_res = ATT_SUB if gi == 2 else 1
    kv_blk = (CH, n_res * 2 * NH, HD)
    kv_index = lambda b, s: (b, 0, s if gi else 0, 0)
    kv_shape = (BATCH, CH, dil * 2 * NH, HD)
    if kv_prev is None:
        kv_spec = pl.BlockSpec((None, *kv_blk), kv_index)
    else:
        in_specs.append(pl.BlockSpec((None, *kv_blk), kv_index))
        args.append(kv_prev)
        kv_spec = pl.BlockSpec((2, None, *kv_blk), lambda b, s: (0, *kv_index(b, s)))
        kv_shape = (2, *kv_shape)
    return pl.pallas_call(
        functools.partial(_attn_prompt_body, gi=gi, headed=headed),
        grid=(BATCH, nsteps),
        in_specs=in_specs,
        out_specs=[pl.BlockSpec((NH, out_rows, HD), lambda b, s: (0, out_index(b, s), 0)),
                   pl.BlockSpec((out_rows, NH * LSE_W), lambda b, s: (out_index(b, s), 0)),
                   kv_spec],
        out_shape=[jax.ShapeDtypeStruct((NH, rows, HD), F32),
                   jax.ShapeDtypeStruct((rows, NH * LSE_W), F32),
                   jax.ShapeDtypeStruct(kv_shape, F32)],
        scratch_shapes=[pltpu.VMEM((NH, CH, 2 * CH), F32)],
        compiler_params=_cparams(2),
        name=f"attn_prompt_g{gi}",
    )(*args)


def _merge_body(o0, l0, o1, l1, o2, l2, ob_ref, oc_ref, od_ref, gate_ref, wb_ref, wo_ref, x_ref, out_ref):
    heads = []
    for h in range(NH):
        a0, a1, a2 = (l[:, h * LSE_W:h * LSE_W + 1] for l in (l0, l1, l2))
        m = jnp.maximum(jnp.maximum(a0, a1), a2)
        w0, w1, w2 = jnp.exp(a0 - m), jnp.exp(a1 - m), jnp.exp(a2 - m)
        heads.append(((w0 * o0[h] + w1 * o1[h] + w2 * o2[h]) / (w0 + w1 + w2)).astype(BF16))
    oa = jnp.concatenate(heads, axis=1)
    acc = None
    for n, br in enumerate((oa, ob_ref[...], oc_ref[...], od_ref[...])):
        proj = jnp.dot(br, wb_ref[n], preferred_element_type=F32)
        t = gate_ref[:, n * D_MODEL:(n + 1) * D_MODEL].astype(F32) * proj
        acc = t if acc is None else acc + t
    out_ref[...] = x_ref[...] + jnp.dot(acc.astype(BF16), wo_ref[...], preferred_element_type=F32)


def _merge(parts, ob, oc, od, gates, wb, wo, x, *, layer, tm=256):
    m = x.shape[0]
    once = dict(pipeline_mode=pl.Buffered(1))
    ospec = pl.BlockSpec((NH, tm, HD), lambda i: (0, i, 0))
    lspec = pl.BlockSpec((tm, NH * LSE_W), lambda i: (i, 0))
    br_spec = pl.BlockSpec((tm, BW), lambda i: (i, 0))
    row_spec = pl.BlockSpec((tm, D_MODEL), lambda i: (i, 0))
    return pl.pallas_call(
        _merge_body,
        grid=(m // tm,),
        in_specs=[ospec, lspec] * 3 + [br_spec] * 3 + [
            pl.BlockSpec((tm, GATE_W), lambda i: (i, 0)),
            pl.BlockSpec((None, 4, BW, D_MODEL), lambda i: (layer, 0, 0, 0), **once),
            pl.BlockSpec((None, D_MODEL, D_MODEL), lambda i: (layer, 0, 0), **once),
            row_spec],
        out_specs=row_spec,
        out_shape=jax.ShapeDtypeStruct((m, D_MODEL), F32),
        compiler_params=_cparams(1, VMEM_BIG),
        name="merge",
    )(*parts, ob, oc, od, gates, wb, wo, x)


def _attn_sample_body(rb_ref, bc0_ref, bc1_ref, bc2_ref, bn_ref, z_ref, c0_ref, c1_ref, c2_ref,
                      oa_ref, kv0_ref, kv1_ref, kv2_ref, b0_ref, b1_ref, b2_ref, bnew_ref):
    bucket_refs = (bc0_ref, bc1_ref, bc2_ref)
    bias_refs = (b0_ref, b1_ref, b2_ref)

    @pl.when(pl.program_id(0) == 0)
    def _():
        for gi in range(3):
            for h in range(NH):
                rs = slice(h * SROWS, (h + 1) * SROWS)
                bias_refs[gi][rs, :] = _bias_from_buckets(bucket_refs[gi][rs, :], rb_ref, gi * NH + h)
                bnew_ref[gi, rs, :] = _bias_from_buckets(bn_ref[gi, rs, :], rb_ref, gi * NH + h)

    scale = HD ** -0.5
    rows = NH * SROWS
    head_of_row = lax.broadcasted_iota(jnp.int32, (rows, BW), 0) // SROWS
    head_of_lane = lax.broadcasted_iota(jnp.int32, (rows, BW), 1) // HD
    head_mask = head_of_row == head_of_lane
    caches = (c0_ref, c1_ref, c2_ref)
    kv_refs = (kv0_ref, kv1_ref, kv2_ref)
    stats = []
    for gi in range(3):
        cq, ck, cv = COL_QKV[gi]
        q = z_ref[:, cq * BW:(cq + 1) * BW]
        kn = z_ref[:, ck * BW:(ck + 1) * BW]
        vn = z_ref[:, cv * BW:(cv + 1) * BW]
        kv_refs[gi][:, :BW] = kn
        kv_refs[gi][:, BW:] = vn
        qm = jnp.concatenate([q[:, h * HD:(h + 1) * HD] for h in range(NH)], axis=0).astype(BF16)
        kf = caches[gi][...].reshape(SAMPLE_CACHE_ROWS[gi], HD).astype(BF16)
        s = lax.dot_general(qm, kf, (((1,), (1,)), ((), ())), preferred_element_type=F32) * scale + bias_refs[gi][...]
        qrows = jnp.where(head_mask, jnp.concatenate([q] * NH, axis=0), 0.0)
        bias_n = bnew_ref[gi]
        s_new = []
        for tp in range(DEC_SEQ):
            dotp = jnp.sum(qrows * kn[tp:tp + 1, :], axis=-1, keepdims=True)
            s_new.append(dotp * scale + bias_n[:, tp:tp + 1])
        m = jnp.max(s, axis=-1, keepdims=True)
        for sn in s_new:
            m = jnp.maximum(m, sn)
        p = jnp.exp(s - m)
        den = jnp.sum(p, axis=-1, keepdims=True)
        pv = pltpu.roll(p, NH, axis=1).astype(BF16)
        acc = jnp.dot(pv, kf, preferred_element_type=F32)
        acc_n = jnp.zeros((rows, BW), F32)
        for tp, sn in enumerate(s_new):
            pn = jnp.exp(sn - m)
            den = den + pn
            acc_n = acc_n + pn * vn[tp:tp + 1, :]
        acc = acc + jnp.concatenate([acc_n[h * SROWS:(h + 1) * SROWS, h * HD:(h + 1) * HD] for h in range(NH)], axis=0)
        stats.append((m, den, acc))
    mm = jnp.maximum(jnp.maximum(stats[0][0], stats[1][0]), stats[2][0])
    den = jnp.zeros((rows, 1), F32)
    acc = jnp.zeros((rows, HD), F32)
    for m, d, a in stats:
        w = jnp.exp(m - mm)
        den = den + w * d
        acc = acc + w * a
    o = acc / den
    for h in range(NH):
        oa_ref[:, h * HD:(h + 1) * HD] = o[h * SROWS:(h + 1) * SROWS, :].astype(BF16)


def _attn_sample(z, caches, rel_bias, layer):
    tables = _sample_bucket_tables()
    rows = DEC_BATCH * SROWS
    qrows = NH * SROWS
    n0, n1, n2 = SAMPLE_CACHE_ROWS
    c0 = caches[0].reshape(DEPTH, DEC_BATCH, n0, HD)
    c1 = caches[1].reshape(DEPTH, DEC_BATCH, n1, HD)
    c2 = caches[2].reshape(DEPTH, DEC_BATCH, CH, 16 * 8, HD)
    new_tbl = jnp.asarray(np.stack([t[1] for t in tables]))
    kv_spec = pl.BlockSpec((SROWS, 2 * BW), lambda b: (b, 0))
    kv_shape = jax.ShapeDtypeStruct((rows, 2 * BW), F32)
    const2 = lambda b: (0, 0)
    return pl.pallas_call(
        _attn_sample_body,
        grid=(DEC_BATCH,),
        in_specs=[pl.BlockSpec(memory_space=pltpu.SMEM),
                  pl.BlockSpec((qrows, n0), const2), pl.BlockSpec((qrows, n1), const2),
                  pl.BlockSpec((qrows, n2), const2),
                  pl.BlockSpec((3, qrows, CH), lambda b: (0, 0, 0)),
                  pl.BlockSpec((SROWS, IN_WIDTH), lambda b: (b, 0)),
                  pl.BlockSpec((None, None, n0, HD), lambda b: (layer, b, 0, 0)),
                  pl.BlockSpec((None, None, n1, HD), lambda b: (layer, b, 0, 0)),
                  pl.BlockSpec((None, None, CH, 8 * DEC_SEQ, HD), lambda b: (layer, b, 0, 0, 0))],
        out_specs=[pl.BlockSpec((SROWS, BW), lambda b: (b, 0)), kv_spec, kv_spec, kv_spec],
        out_shape=[jax.ShapeDtypeStruct((rows, BW), BF16), kv_shape, kv_shape, kv_shape],
        scratch_shapes=[pltpu.VMEM((qrows, n0), F32), pltpu.VMEM((qrows, n1), F32), pltpu.VMEM((qrows, n2), F32),
                        pltpu.VMEM((3, qrows, CH), F32)],
        compiler_params=_cparams(1),
        name="attn_sample",
    )(rel_bias, jnp.asarray(tables[0][0]), jnp.asarray(tables[1][0]), jnp.asarray(tables[2][0]), new_tbl,
      z, c0, c1, c2)


def _gmlp_body(bu_ref, bv_ref, g_ref, ws_ref, bs_ref, *refs, rows, emit_vn):
    refs = list(refs)
    o_ref = refs.pop(0)
    vn_ref = refs.pop(0) if emit_vn else None
    tril = lax.broadcasted_iota(jnp.int32, (CH, CH), 0) >= lax.broadcasted_iota(jnp.int32, (CH, CH), 1)
    w = [jnp.where(tril, ws_ref[g], 0.0).astype(BF16) for g in range(NH)]
    if rows < CH:
        pad_u, pad_v = refs
        pad_u[...] = jnp.zeros_like(pad_u)
        pad_v[...] = jnp.zeros_like(pad_v)
        pad_u[0:rows, :] = bu_ref[...]
        pad_v[0:rows, :] = bv_ref[...]
        bu_ref, bv_ref = pad_u, pad_v
    for c in range(max(rows // CH, 1)):
        rs = slice(c * CH, (c + 1) * CH)
        u = jax.nn.gelu(bu_ref[rs, :])
        vn = _rms(jax.nn.gelu(bv_ref[rs, :]), g_ref[...])
        if emit_vn:
            vn_ref[...] = vn[:rows]
        for g in range(NH):
            sl = slice(g * HD, (g + 1) * HD)
            mixed = jnp.dot(w[g], vn[:, sl].astype(BF16), preferred_element_type=F32) + bs_ref[:, sl]
            res = (u[:, sl] * mixed).astype(BF16)
            if rows < CH:
                o_ref[:, sl] = res[:rows]
            else:
                o_ref[rs, sl] = res


def _gmlp(z, g_gmlp, w_spatial, b_spatial, *, rows, emit_vn):
    m = z.shape[0]
    bs_full = jnp.repeat(b_spatial.T, HD, axis=1)
    spec = pl.BlockSpec((rows, BW), lambda i: (i, 0))
    out_shape = [jax.ShapeDtypeStruct((m, BW), BF16)]
    out_specs = [spec]
    if emit_vn:
        out_shape.append(jax.ShapeDtypeStruct((m, BW), F32))
        out_specs.append(spec)
    scratch = [] if rows >= CH else [pltpu.VMEM((CH, BW), F32)] * 2
    return pl.pallas_call(
        functools.partial(_gmlp_body, rows=rows, emit_vn=emit_vn),
        grid=(m // rows,),
        in_specs=[pl.BlockSpec((rows, BW), lambda i: (i, COL_BU)),
                  pl.BlockSpec((rows, BW), lambda i: (i, COL_BV)),
                  pl.BlockSpec((1, BW), lambda i: (0, 0)),
                  pl.BlockSpec((NH, CH, CH), lambda i: (0, 0, 0)),
                  pl.BlockSpec((CH, BW), lambda i: (0, 0))],
        out_specs=out_specs,
        out_shape=out_shape,
        scratch_shapes=scratch,
        compiler_params=_cparams(1),
        name="gmlp",
    )(z, z, g_gmlp.reshape(1, -1), w_spatial, bs_full)


def _pool_body(x_ref, prev_ref, wp_ref, sc_ref, o_ref, st_ref, ext_ref, *, rows, n_new, start, zero_first_prev):
    ib = pl.program_id(1)
    prev = prev_ref[...]
    if zero_first_prev:
        prev = jnp.where(ib == 0, 0.0, prev)
    x = x_ref[...]
    ext_ref[0:16, :] = prev
    ext_ref[16:16 + rows, :] = x
    st_ref[...] = ext_ref[pl.ds(n_new + 1, POOL_STATE), :]
    ext = ext_ref[...]
    pos = start + ib * rows + lax.broadcasted_iota(jnp.int32, (rows, 1), 0)
    for gi, win in enumerate(POOL_WINDOWS):
        sl = slice(gi * HD, (gi + 1) * HD)
        s = ext[:, sl]
        k = 1
        while k < win:
            s = s + pltpu.roll(s, k, axis=0)
            k *= 2
        cnt = jnp.minimum(pos + 1, win).astype(F32)
        diff = s[16:] / cnt - x[:, sl]
        y = jnp.dot(diff.astype(BF16), wp_ref[gi].astype(BF16), preferred_element_type=F32)
        o_ref[:, sl] = (y * sc_ref[:, sl]).astype(BF16)


def _pool(z, prev, w_pool, pool_scale, *, nseq, rows, n_new, start, layer=None):
    m = z.shape[0]
    nblk = m // nseq // rows
    if prev is None:
        per16 = rows // 16
        prev_arr = z
        prev_spec = pl.BlockSpec((16, BW), lambda b, i: (jnp.maximum((b * nblk + i) * per16 - 1, 0), COL_CIN))
    else:
        prev_arr = prev
        prev_spec = pl.BlockSpec((None, None, 16, BW), lambda b, i: (layer, b, 0, 0))
    return pl.pallas_call(
        functools.partial(_pool_body, rows=rows, n_new=n_new, start=start, zero_first_prev=prev is None),
        grid=(nseq, nblk),
        in_specs=[pl.BlockSpec((rows, BW), lambda b, i: (b * nblk + i, COL_CIN)),
                  prev_spec,
                  pl.BlockSpec((NH, HD, HD), lambda b, i: (0, 0, 0)),
                  pl.BlockSpec((1, BW), lambda b, i: (0, 0))],
        out_specs=[pl.BlockSpec((rows, BW), lambda b, i: (b * nblk + i, 0)),
                   pl.BlockSpec((None, POOL_STATE, BW), lambda b, i: (b, 0, 0))],
        out_shape=[jax.ShapeDtypeStruct((m, BW), BF16),
                   jax.ShapeDtypeStruct((nseq, POOL_STATE, BW), F32)],
        scratch_shapes=[pltpu.VMEM((16 + rows, BW), F32)],
        compiler_params=_cparams(2),
        name="pool",
    )(z, prev_arr, w_pool, pool_scale.reshape(1, -1))


def _gmlp_sample_body(bu_ref, bv_ref, g_ref, wk_ref, bs_ref, o_ref, vn_ref):
    u = jax.nn.gelu(bu_ref[...])
    vn = _rms(jax.nn.gelu(bv_ref[...]), g_ref[...])
    vn_ref[...] = vn
    mixed = jnp.tile(bs_ref[...], (DEC_BATCH, 1))
    for k in range(DEC_SEQ):
        shifted = vn if k == 0 else pltpu.roll(vn, k, axis=0)
        mixed = mixed + jnp.tile(wk_ref[k], (DEC_BATCH, 1)) * shifted
    o_ref[...] = (u * mixed).astype(BF16)


def _gmlp_sample(z, g_gmlp, w_spatial, b_spatial):
    m = z.shape[0]
    t = np.arange(SROWS)
    live = t < DEC_SEQ
    wk = []
    for k in range(DEC_SEQ):
        ok = live & (t - k >= 0)
        diag = w_spatial[:, np.where(ok, t, 0), np.where(ok, t - k, 0)] * jnp.asarray(ok, F32)
        wk.append(jnp.repeat(diag.T, HD, axis=1))
    bs = jnp.repeat((b_spatial[:, :SROWS] * jnp.asarray(live, F32)).T, HD, axis=1)
    return pl.pallas_call(
        _gmlp_sample_body,
        grid=(1,),
        in_specs=[pl.BlockSpec((m, BW), lambda i: (0, COL_BU)),
                  pl.BlockSpec((m, BW), lambda i: (0, COL_BV)),
                  pl.BlockSpec((1, BW), lambda i: (0, 0)),
                  pl.BlockSpec((DEC_SEQ, SROWS, BW), lambda i: (0, 0, 0)),
                  pl.BlockSpec((SROWS, BW), lambda i: (0, 0))],
        out_specs=[pl.BlockSpec((m, BW), lambda i: (0, 0))] * 2,
        out_shape=[jax.ShapeDtypeStruct((m, BW), BF16), jax.ShapeDtypeStruct((m, BW), F32)],
        compiler_params=_cparams(1),
        name="gmlp_sample",
    )(z, z, g_gmlp.reshape(1, -1), jnp.stack(wk), bs)


POOL_PAD = 16


def _pool_sample_body(x_ref, prev_ref, wp_ref, sc_ref, o_ref, st_ref, ext_ref, *, start):
    per = POOL_PAD + SROWS
    ext_ref[:, 0:POOL_PAD, :] = prev_ref[...]
    ext_ref[:, POOL_PAD:per, :] = x_ref[...]
    st_ref[...] = ext_ref[:, pl.ds(DEC_SEQ + 1, POOL_STATE), :]
    ext = ext_ref[...].reshape(DEC_BATCH * per, BW)
    pos = start + (lax.broadcasted_iota(jnp.int32, (DEC_BATCH * per, 1), 0) % per - POOL_PAD)
    for gi, win in enumerate(POOL_WINDOWS):
        sl = slice(gi * HD, (gi + 1) * HD)
        x = ext[:, sl]
        s = x
        k = 1
        while k < win:
            s = s + pltpu.roll(s, k, axis=0)
            k *= 2
        cnt = jnp.minimum(jnp.maximum(pos, 0) + 1, win).astype(F32)
        y = jnp.dot((s / cnt - x).astype(BF16), wp_ref[gi].astype(BF16), preferred_element_type=F32)
        o_ref[:, :, sl] = (y * sc_ref[:, sl]).reshape(DEC_BATCH, per, HD)[:, POOL_PAD:, :].astype(BF16)


def _pool_sample(z, prev, w_pool, pool_scale, *, start, layer):
    m, zw = z.shape
    z3 = z.reshape(DEC_BATCH, SROWS, zw)
    o, st = pl.pallas_call(
        functools.partial(_pool_sample_body, start=start),
        grid=(1,),
        in_specs=[pl.BlockSpec((DEC_BATCH, SROWS, BW), lambda i: (0, 0, COL_CIN)),
                  pl.BlockSpec((None, DEC_BATCH, POOL_PAD, BW), lambda i: (layer, 0, 0, 0)),
                  pl.BlockSpec((NH, HD, HD), lambda i: (0, 0, 0)),
                  pl.BlockSpec((1, BW), lambda i: (0, 0))],
        out_specs=[pl.BlockSpec((DEC_BATCH, SROWS, BW), lambda i: (0, 0, 0)),
                   pl.BlockSpec((DEC_BATCH, POOL_STATE, BW), lambda i: (0, 0, 0))],
        out_shape=[jax.ShapeDtypeStruct((DEC_BATCH, SROWS, BW), BF16),
                   jax.ShapeDtypeStruct((DEC_BATCH, POOL_STATE, BW), F32)],
        scratch_shapes=[pltpu.VMEM((DEC_BATCH, POOL_PAD + SROWS, BW), F32)],
        compiler_params=_cparams(1),
        name="pool_sample",
    )(z3, prev, w_pool, pool_scale.reshape(1, -1))
    return o.reshape(m, BW), st


def _ret_tables(c_eff, positions, size):
    lg = np.log1p(-np.power(2.0, -5.0 - np.arange(NH, dtype=np.float64)))
    i = np.arange(size, dtype=np.float64)
    live = (i < c_eff)
    diff = i[:, None] - i[None, :]
    inner = np.where((diff >= 0) & live[:, None] & live[None, :], np.exp(np.maximum(diff, 0.0)[None] * lg[:, None, None]), 0.0)
    qd = np.where(live[None, :], np.exp((i + 1.0)[None, :] * lg[:, None]), 0.0)
    kd = np.where(live[None, :], np.exp((c_eff - 1.0 - i)[None, :] * lg[:, None]), 0.0)
    chunk = tuple(float(v) for v in np.exp(c_eff * lg))
    qd_full = np.repeat(qd.T, HD, axis=1)
    kd_full = np.repeat(kd.T, HD, axis=1) * (HD ** -0.5)
    half = HD // 2
    inv = ROPE_BASE ** (-np.arange(half, dtype=np.float64) / half)
    ang = np.asarray(positions, np.float64)[:, None] * inv[None, :]
    cosf = np.concatenate([np.cos(ang), np.cos(ang)], axis=1)
    sinf = np.concatenate([-np.sin(ang), np.sin(ang)], axis=1)
    to32 = lambda a: jnp.asarray(a.astype(np.float32))
    return to32(inner), to32(qd_full), to32(kd_full), chunk, to32(cosf), to32(sinf)


def _ret_body(q_ref, k_ref, v_ref, g_ref, cos_ref, sin_ref, inner_ref, qd_ref, kd_ref, gr_ref, *refs,
              nseq, rows, chunk_decay, has_state):
    refs = list(refs)
    s0_ref = refs.pop(0) if has_state else None
    o_ref, sn_ref, s_ref = refs[:3]
    pad_ref = refs[3] if rows < CH else None
    ic = pl.program_id(1)

    @pl.when(ic == 0)
    def _():
        if has_state:
            s_ref[...] = s0_ref[...]
        else:
            s_ref[...] = jnp.zeros_like(s_ref)

    def chunk(ref, b, k):
        if rows >= CH:
            return ref[b]
        pad_ref[k] = jnp.zeros((CH, BW), F32)
        pad_ref[k, 0:rows, :] = ref[b]
        return pad_ref[k]

    cosf = cos_ref[...]
    sinf = sin_ref[...]
    for b in range(nseq):
        q = chunk(q_ref, b, 0)
        k = chunk(k_ref, b, 1)
        v = chunk(v_ref, b, 2)
        gate = chunk(g_ref, b, 3)
        for h in range(NH):
            sl = slice(h * HD, (h + 1) * HD)
            qh = q[:, sl]
            kh = k[:, sl]
            rq = qh * cosf + pltpu.roll(qh, HD // 2, axis=1) * sinf
            rk = kh * cosf + pltpu.roll(kh, HD // 2, axis=1) * sinf
            vb = v[:, sl].astype(BF16)
            rqb = rq.astype(BF16)
            state = s_ref[b, h]
            att = lax.dot_general(rqb, (rk * (HD ** -0.5)).astype(BF16), (((1,), (1,)), ((), ())),
                                  preferred_element_type=F32) * inner_ref[h]
            o = (jnp.dot(att.astype(BF16), vb, preferred_element_type=F32)
                 + jnp.dot(rqb, state.astype(BF16), preferred_element_type=F32) * qd_ref[:, sl])
            kdec = (rk * kd_ref[:, sl]).astype(BF16)
            new_state = state * chunk_decay[h] + lax.dot_general(
                kdec, vb, (((0,), (0,)), ((), ())), preferred_element_type=F32)
            s_ref[b, h] = new_state
            sn_ref[b, h] = new_state
            o = o * lax.rsqrt(jnp.mean(o * o, axis=-1, keepdims=True) + EPS)
            gt = gate[:, sl]
            o_ref[b, :, sl] = (o * gr_ref[:, sl] * (gt * jax.nn.sigmoid(gt)))[:rows].astype(BF16)


def _retention(z, g_ret, state, *, nseq, per_step, rows, c_eff, positions, layer=None):
    m, zw = z.shape
    seq_rows = m // nseq
    nchunk = seq_rows // rows
    size = max(rows, CH)
    inner, qd, kd, chunk_decay, cosf, sinf = _ret_tables(c_eff, positions, size)
    has_state = state is not None
    z3 = z.reshape(nseq, seq_rows, zw)

    def zspec(col):
        return pl.BlockSpec((per_step, rows, BW), lambda g, i: (g, i, col))

    const2 = lambda g, i: (0, 0)
    in_specs = [zspec(COL_DQ), zspec(COL_DK), zspec(COL_DV), zspec(COL_DG),
                pl.BlockSpec((size, HD), lambda g, i: (i, 0)), pl.BlockSpec((size, HD), lambda g, i: (i, 0)),
                pl.BlockSpec((NH, size, size), lambda g, i: (0, 0, 0)),
                pl.BlockSpec((size, BW), const2), pl.BlockSpec((size, BW), const2), pl.BlockSpec((1, BW), const2)]
    args = [z3, z3, z3, z3, cosf, sinf, inner, qd, kd, g_ret.reshape(1, -1)]
    if has_state:
        in_specs.append(pl.BlockSpec((None, per_step, NH, HD, HD), lambda g, i: (layer, g, 0, 0, 0)))
        args.append(state)
    scratch = [pltpu.VMEM((per_step, NH, HD, HD), F32)]
    if rows < CH:
        scratch.append(pltpu.VMEM((4, CH, BW), F32))
    o, sn = pl.pallas_call(
        functools.partial(_ret_body, nseq=per_step, rows=rows, chunk_decay=chunk_decay, has_state=has_state),
        grid=(nseq // per_step, nchunk),
        in_specs=in_specs,
        out_specs=[pl.BlockSpec((per_step, rows, BW), lambda g, i: (g, i, 0)),
                   pl.BlockSpec((per_step, NH, HD, HD), lambda g, i: (g, 0, 0, 0))],
        out_shape=[jax.ShapeDtypeStruct((nseq, seq_rows, BW), BF16),
                   jax.ShapeDtypeStruct((nseq, NH, HD, HD), F32)],
        scratch_shapes=scratch,
        compiler_params=_cparams(2),
        name="retention",
    )(*args)
    return o.reshape(m, BW), sn


def kernel(x_prompt, x_sample, cache_attn_kv_w128, cache_attn_kv_w512, cache_attn_kv_w2048, state_pool, state_ret, rel_bias, g_ffn1, w_ffn1_gate, w_ffn1_up, w_ffn1_down, g_mix, w_in, g_gmlp, w_spatial, b_spatial, w_pool, pool_scale, g_ret, w_branch, w_out, g_ffn2, w_ffn2_gate, w_ffn2_up, w_ffn2_down, g_final):
    caches = (cache_attn_kv_w128, cache_attn_kv_w512, cache_attn_kv_w2048)
    xp = x_prompt.reshape(BATCH * SEQ, D_MODEL)
    xs = jnp.pad(x_sample, ((0, 0), (0, SROWS - DEC_SEQ), (0, 0))).reshape(DEC_BATCH * SROWS, D_MODEL)
    pool_state = jnp.pad(state_pool, ((0, 0), (0, 0), (1, 0), (0, 0)))
    tm_p, tm_s = 1024, DEC_BATCH * SROWS
    sample_pos = PAST_LEN + np.arange(CH)
    wb = w_branch.astype(BF16)
    wo = w_out.astype(BF16)
    gate_col0 = N_PLAIN * BW // 1024
    qkv1, qkv2 = COL_QKV[1][0], COL_QKV[2][0]

    kv_p = [None, None, None]
    kv_s = [[], [], []]
    pool_p, pool_s, ret_p, ret_s, gv_s = [], [], [], [], []
    yp = ys = None
    for l in range(DEPTH):
        last = l == DEPTH - 1

        xs, hs, *w1 = _ffn(xs, g_ffn1[l], (w_ffn1_gate, w_ffn1_up, w_ffn1_down), g_mix[l], layer=l,
                           emit_x=True, post_dtype=BF16, tm=tm_s)
        zs, gs, win = _in_proj_convert(hs, w_in, layer=l)
        oa, k0, k1, k2 = _attn_sample(zs, caches, rel_bias, l)
        for gi, kv in enumerate((k0, k1, k2)):
            kv_s[gi].append(kv.reshape(DEC_BATCH, SROWS, 2, NH, HD)[:, :DEC_SEQ])
        ob, vn = _gmlp_sample(zs, g_gmlp[l], w_spatial[l], b_spatial[l])
        gv_s.append(vn.reshape(DEC_BATCH, SROWS, BW)[:, :DEC_SEQ])
        oc, pn = _pool_sample(zs, pool_state, w_pool[l], pool_scale[l], start=PAST_LEN, layer=l)
        od, rn = _retention(zs, g_ret[l], state_ret, nseq=DEC_BATCH, per_step=4, rows=SROWS, c_eff=DEC_SEQ,
                            positions=sample_pos, layer=l)
        pool_s.append(pn)
        ret_s.append(rn)
        mix = _branch_merge(oa, ob, oc, od, gs, wb, layer=l, tm=tm_s)
        xs = _matmul(mix, wo, xs, layer=l, tm=tm_s, tn=512, name="out_proj")
        if last:
            ys, *w2 = _ffn(xs, g_ffn2[l], (w_ffn2_gate, w_ffn2_up, w_ffn2_down), g_final, layer=l,
                           emit_x=False, post_dtype=F32, tm=tm_s)
        else:
            xs, *w2 = _ffn(xs, g_ffn2[l], (w_ffn2_gate, w_ffn2_up, w_ffn2_down), g_final, layer=l,
                           emit_x=True, post_dtype=None, tm=tm_s)

        xp, hp = _ffn(xp, g_ffn1[l], w1, g_mix[l], emit_x=True, post_dtype=BF16, tm=tm_p)
        zp = _matmul(hp, win, n=N_PLAIN * BW, tm=2048, tn=1024, name="in_proj")
        gp = _matmul(hp, win, col0=gate_col0, n=GATE_W, gate=True, tm=2048, tn=1024, name="in_proj_gate")
        zq = (zp,
              _matmul_deint(hp, win, col0=qkv1, dil=ATT_DILATIONS[1], name="in_proj_g1"),
              _matmul_deint(hp, win, col0=qkv2, dil=ATT_DILATIONS[2], name="in_proj_g2"))
        parts = []
        for gi in range(3):
            o, lse, kv_p[gi] = _attn_prompt(zq[gi], rel_bias, gi, kv_p[gi])
            parts += [o, lse]
        ob, = _gmlp(zp, g_gmlp[l], w_spatial[l], b_spatial[l], rows=4 * CH, emit_vn=False)
        oc, pn = _pool(zp, None, w_pool[l], pool_scale[l], nseq=BATCH, rows=512, n_new=512, start=0)
        od, rn = _retention(zp, g_ret[l], None, nseq=BATCH, per_step=BATCH, rows=CH, c_eff=CH,
                            positions=np.arange(SEQ))
        pool_p.append(pn)
        ret_p.append(rn)
        xp = _merge(parts, ob, oc, od, gp, wb, wo, xp, layer=l)
        if last:
            yp, = _ffn(xp, g_ffn2[l], w2, g_final, emit_x=False, post_dtype=F32, tm=tm_p)
        else:
            xp, = _ffn(xp, g_ffn2[l], w2, g_final, emit_x=True, post_dtype=None, tm=tm_p)

    y_prompt = yp.reshape(BATCH, SEQ, D_MODEL)
    y_sample = ys.reshape(DEC_BATCH, SROWS, D_MODEL)[:, :DEC_SEQ]
    kv_p = [kv.reshape(DEPTH, BATCH, ATT_WINDOWS[gi], 2, NH, HD) for gi, kv in enumerate(kv_p)]
    return (y_prompt, y_sample,
            kv_p[0], kv_p[1], kv_p[2],
            jnp.stack(kv_s[0]), jnp.stack(kv_s[1]), jnp.stack(kv_s[2]),
            jnp.stack(pool_p), jnp.stack(pool_s),
            jnp.stack(ret_p), jnp.stack(ret_s),
            jnp.stack(gv_s))
```

```python
import functools

import numpy as np
import jax
import jax.numpy as jnp
from jax import lax
from jax.experimental import pallas as pl
from jax.experimental.pallas import tpu as pltpu

F32 = jnp.float32
BF16 = jnp.bfloat16

D_MODEL = 2048
BATCH = 4
SEQ = 2048
DEPTH = 2
DEC_BATCH = 32
DEC_SEQ = 4
PAST_LEN = 8192
D_FF = 5632
IN_WIDTH = 16384
EPS = 1e-6
BW = 512
HD = 128
NH = 4
ATT_WINDOWS = (128, 512, 2048)
ATT_DILATIONS = (1, 4, 16)
NKEY = 129
REL_BUCKETS = 32
REL_MAX_DISTANCE = 2048
POOL_WINDOWS = (2, 4, 8, 16)
POOL_STATE = 15
ROPE_BASE = 10000.0
SROWS = 8
CH = 128
NEG = -1e30
LSE_W = 32

W_IN_ORDER = (0, 3, 6) + tuple(range(9, 32)) + (1, 4, 7) + (2, 5, 8)
COL_QKV = ((0, 1, 2), (26, 27, 28), (29, 30, 31))
COL_BU, COL_BV, COL_CIN = 3, 4, 5
COL_DQ, COL_DK, COL_DV, COL_DG = 6, 7, 8, 9
N_PLAIN = 10
N_GATE = 16
GATE_W = N_GATE * BW
ZBLK = IN_WIDTH // BW

VMEM_BIG = 56 * 1024 * 1024
VMEM_FFN = 62 * 1024 * 1024


def _cparams(n_axes, vmem=None):
    return pltpu.CompilerParams(dimension_semantics=("arbitrary",) * n_axes, vmem_limit_bytes=vmem)


def _rms(x, g):
    return x * lax.rsqrt(jnp.mean(x * x, axis=-1, keepdims=True) + EPS) * g


def _ffn_body(x_ref, gpre_ref, *refs, emit_x, emit_post, convert, nf, tf, row_split):
    refs = list(refs)
    if convert:
        wg_ref, wu_ref, wd_ref, gpost_ref = refs[:4]
        refs = refs[4:]
    else:
        wgu_ref, wd_ref, gpost_ref = refs[:3]
        refs = refs[3:]
    n_act = int(emit_x) + int(emit_post)
    outs = refs[:n_act]
    scratch = refs[n_act + (2 if convert else 0):]
    xn_ref = scratch[0] if scratch else outs[1]
    acc_ref = outs[0]
    f = pl.program_id(1)

    @pl.when(f == 0)
    def _():
        x = x_ref[...]
        xn_ref[...] = _rms(x, gpre_ref[...]).astype(BF16)
        acc_ref[...] = x

    if convert:
        wgu = jnp.concatenate([wg_ref[...].astype(BF16), wu_ref[...].astype(BF16)], axis=1)
        wd = wd_ref[...].astype(BF16)
        refs[n_act][...] = wgu
        refs[n_act + 1][...] = wd
    else:
        wgu, wd = wgu_ref[...], wd_ref[...]
    rows = xn_ref.shape[0] // row_split
    for part in range(row_split):
        rs = slice(part * rows, (part + 1) * rows)
        r = jnp.dot(xn_ref[rs, :], wgu, preferred_element_type=F32)
        g, u = r[:, :tf], r[:, tf:]
        h = (g * jax.nn.sigmoid(g) * (0.5 * u)).astype(BF16)
        acc_ref[rs, :] += jnp.dot(h, wd, preferred_element_type=F32)

    if emit_post:
        @pl.when(f == nf - 1)
        def _():
            outs[-1][...] = _rms(acc_ref[...], gpost_ref[...]).astype(outs[-1].dtype)


def _ffn(x, g_pre, weights, g_post, *, layer=None, emit_x, post_dtype, tm, tf=512):
    m = x.shape[0]
    nf = D_FF // tf
    emit_post = post_dtype is not None
    convert = layer is not None
    assert emit_x or post_dtype == F32, "the first output block is the f32 accumulator"
    once = dict(pipeline_mode=pl.Buffered(1)) if tm > 512 else {}
    row_spec = pl.BlockSpec((tm, D_MODEL), lambda i, f: (i, 0))
    row_out_spec = pl.BlockSpec((tm, D_MODEL), lambda i, f: (i, 0), **once)
    vec_spec = pl.BlockSpec((1, D_MODEL), lambda i, f: (0, 0))
    out_shape, out_specs = [], []
    if emit_x:
        out_shape.append(jax.ShapeDtypeStruct((m, D_MODEL), F32))
        out_specs.append(row_spec)
    post_holds_xn = emit_x and post_dtype == BF16 and tm > 512
    if emit_post:
        out_shape.append(jax.ShapeDtypeStruct((m, D_MODEL), post_dtype))
        out_specs.append(row_out_spec if emit_x and not post_holds_xn else row_spec)
    gu_spec = pl.BlockSpec((D_MODEL, 2 * tf), lambda i, f: (0, f))
    down_spec = pl.BlockSpec((tf, D_MODEL), lambda i, f: (f, 0))
    if convert:
        assert m == tm, "the bf16 weights are written once, by a single row tile"
        w_specs = [pl.BlockSpec((None, D_MODEL, tf), lambda i, f: (layer, 0, f)),
                   pl.BlockSpec((None, D_MODEL, tf), lambda i, f: (layer, 0, f)),
                   pl.BlockSpec((None, tf, D_MODEL), lambda i, f: (layer, f, 0))]
        out_shape += [jax.ShapeDtypeStruct((D_MODEL, 2 * D_FF), BF16), jax.ShapeDtypeStruct((D_FF, D_MODEL), BF16)]
        out_specs += [gu_spec, down_spec]
    else:
        w_specs = [gu_spec, down_spec]
    scratch = [] if post_holds_xn else [pltpu.VMEM((tm, D_MODEL), BF16)]
    return pl.pallas_call(
        functools.partial(_ffn_body, emit_x=emit_x, emit_post=emit_post, convert=convert, nf=nf, tf=tf,
                          row_split=max(tm // 512, 1)),
        grid=(m // tm, nf),
        in_specs=[row_spec, vec_spec, *w_specs, vec_spec],
        out_specs=out_specs,
        out_shape=out_shape,
        scratch_shapes=scratch,
        compiler_params=_cparams(2, VMEM_FFN),
        name="ffn",
    )(x, g_pre.reshape(1, -1), *weights, g_post.reshape(1, -1))


def _mm_body(a_ref, b_ref, o_ref):
    o_ref[...] = jnp.dot(a_ref[...], b_ref[...], preferred_element_type=F32)


def _mm_res_body(a_ref, b_ref, r_ref, o_ref):
    o_ref[...] = r_ref[...] + jnp.dot(a_ref[...], b_ref[...], preferred_element_type=F32)


def _sigmoid(x):
    return 0.5 * jnp.tanh(0.5 * x) + 0.5


def _mm_gate_body(a_ref, b_ref, o_ref):
    o_ref[...] = _sigmoid(jnp.dot(a_ref[...], b_ref[...], preferred_element_type=F32)).astype(BF16)


def _matmul(a, b, res=None, *, layer=None, col0=0, n=None, gate=False, tm, tn, name):
    m, k = a.shape
    n = b.shape[-1] if n is None else n
    if layer is None:
        b_spec = pl.BlockSpec((k, tn), lambda i, j: (0, col0 + j))
    else:
        b_spec = pl.BlockSpec((None, k, tn), lambda i, j: (layer, 0, col0 + j))
    in_specs = [pl.BlockSpec((tm, k), lambda i, j: (i, 0)), b_spec]
    args = [a, b]
    body = _mm_gate_body if gate else _mm_body
    if res is not None:
        in_specs.append(pl.BlockSpec((tm, tn), lambda i, j: (i, j)))
        args.append(res)
        body = _mm_res_body
    return pl.pallas_call(
        body,
        grid=(m // tm, n // tn),
        in_specs=in_specs,
        out_specs=pl.BlockSpec((tm, tn), lambda i, j: (i, j)),
        out_shape=jax.ShapeDtypeStruct((m, n), BF16 if gate else F32),
        compiler_params=_cparams(2, VMEM_BIG),
        name=name,
    )(*args)


def _mm_deint_body(a_ref, b_ref, o_ref, s_ref, *, dil):
    r = jnp.dot(a_ref[...], b_ref[...], preferred_element_type=F32)
    for h in range(NH):
        s_ref[h] = r[:, h * HD:(h + 1) * HD]
    n = SEQ // dil
    for h in range(NH):
        for res in range(dil):
            o_ref[h, pl.ds(res * n, n), :] = s_ref[h, pl.ds(res, n, stride=dil), :]


def _w_in_source_block(j, where=jnp.where):
    return where(j < 3, 3 * j, where(j < 26, j + 6, where(j < 29, 3 * (j - 26) + 1, 3 * (j - 29) + 2)))


assert tuple(_w_in_source_block(np.arange(ZBLK), np.where)) == W_IN_ORDER


def _in_proj_convert_body(a_ref, b_ref, z_ref, gate_ref, w_ref):
    j = pl.program_id(0)
    w = b_ref[...].astype(BF16)
    w_ref[...] = w
    r = jnp.dot(a_ref[...], w, preferred_element_type=F32)
    z_ref[...] = r

    @pl.when((j >= N_PLAIN) & (j < N_PLAIN + N_GATE))
    def _():
        gate_ref[...] = _sigmoid(r).astype(BF16)


def _in_proj_convert(a, w_in, *, layer):
    m, k = a.shape
    return pl.pallas_call(
        _in_proj_convert_body,
        grid=(ZBLK,),
        in_specs=[pl.BlockSpec((m, k), lambda j: (0, 0)),
                  pl.BlockSpec((None, k, BW), lambda j: (layer, 0, _w_in_source_block(j)))],
        out_specs=[pl.BlockSpec((m, BW), lambda j: (0, j)),
                   pl.BlockSpec((m, BW), lambda j: (0, jnp.clip(j - N_PLAIN, 0, N_GATE - 1))),
                   pl.BlockSpec((k, BW), lambda j: (0, j))],
        out_shape=[jax.ShapeDtypeStruct((m, IN_WIDTH), F32),
                   jax.ShapeDtypeStruct((m, GATE_W), BF16),
                   jax.ShapeDtypeStruct((k, IN_WIDTH), BF16)],
        compiler_params=_cparams(1, VMEM_BIG),
        name="in_proj_convert",
    )(a, w_in)


def _matmul_deint(a, b, *, col0, dil, name):
    m, k = a.shape
    return pl.pallas_call(
        functools.partial(_mm_deint_body, dil=dil),
        grid=(BATCH, 3),
        in_specs=[pl.BlockSpec((SEQ, k), lambda b, j: (b, 0)),
                  pl.BlockSpec((k, BW), lambda b, j: (0, col0 + j))],
        out_specs=pl.BlockSpec((None, NH, SEQ, HD), lambda b, j: (j, 0, b, 0)),
        out_shape=jax.ShapeDtypeStruct((3, NH, m, HD), F32),
        scratch_shapes=[pltpu.VMEM((NH, SEQ, HD), F32)],
        compiler_params=_cparams(2, VMEM_BIG),
        name=name,
    )(a, b)


def _branch_body(oa_ref, ob_ref, oc_ref, od_ref, gate_ref, wb_ref, o_ref):
    acc = None
    for n, br in enumerate((oa_ref, ob_ref, oc_ref, od_ref)):
        proj = jnp.dot(br[...], wb_ref[n], preferred_element_type=F32)
        t = gate_ref[:, n * D_MODEL:(n + 1) * D_MODEL].astype(F32) * proj
        acc = t if acc is None else acc + t
    o_ref[...] = acc.astype(BF16)


def _branch_merge(oa, ob, oc, od, gates, wb, *, layer, tm):
    m = oa.shape[0]
    br_spec = pl.BlockSpec((tm, BW), lambda i: (i, 0))
    return pl.pallas_call(
        _branch_body,
        grid=(m // tm,),
        in_specs=[br_spec] * 4 + [pl.BlockSpec((tm, GATE_W), lambda i: (i, 0)),
                                  pl.BlockSpec((None, 4, BW, D_MODEL), lambda i: (layer, 0, 0, 0))],
        out_specs=pl.BlockSpec((tm, D_MODEL), lambda i: (i, 0)),
        out_shape=jax.ShapeDtypeStruct((m, D_MODEL), BF16),
        compiler_params=_cparams(1, VMEM_BIG),
        name="branch_merge",
    )(oa, ob, oc, od, gates, wb)


def _t5_buckets(dist):
    max_exact = REL_BUCKETS // 2
    d = np.maximum(dist, 1).astype(np.float32)
    large = max_exact + (np.log(d / max_exact) / np.log(REL_MAX_DISTANCE / max_exact)
                         * (REL_BUCKETS - max_exact)).astype(np.int32)
    large = np.minimum(large, REL_BUCKETS - 1)
    return np.where(dist < max_exact, dist, large).astype(np.int32)


def _bucket_of_step(dil):
    return _t5_buckets(dil * np.arange(NKEY))


def _prompt_bucket_matrix(dil):
    iq = np.arange(CH)[:, None]
    col = np.arange(2 * CH)[None, :]
    j = iq + CH - col
    valid = (j >= 0) & (j < NKEY)
    return np.where(valid, _bucket_of_step(dil)[np.clip(j, 0, NKEY - 1)], -1).astype(np.int32)


SAMPLE_CACHE_ROWS = (8 * ATT_WINDOWS[0], 8 * ATT_WINDOWS[1], 8 * DEC_SEQ * CH)


def _sample_bucket_tables():
    h = np.repeat(np.arange(NH), SROWS)[:, None]
    t = np.tile(np.arange(SROWS), NH)[:, None]
    lane = np.arange(CH)[None, :]
    tables = []
    for gi, dil in enumerate(ATT_DILATIONS):
        bos = _bucket_of_step(dil)
        flat = np.arange(SAMPLE_CACHE_ROWS[gi])[None, :]
        head, is_k = flat % NH, (flat // NH) % 2 == 0
        if gi < 2:
            w = flat // 8
            steps = ATT_WINDOWS[gi] + t - w
            valid = (steps % dil == 0) & (steps // dil < NKEY)
            j = steps // dil
        else:
            res, i = (flat // 8) % DEC_SEQ, flat // (8 * DEC_SEQ)
            valid = res == t
            j = np.broadcast_to(CH - i, valid.shape)
        valid = valid & is_k & (head == h) & (t < DEC_SEQ)
        cache_tbl = np.where(valid, bos[np.clip(j, 0, NKEY - 1)], -1)
        jn = t - lane
        validn = (t < DEC_SEQ) & (lane < DEC_SEQ) & (jn >= 0) & ((dil == 1) | (jn == 0))
        new_tbl = np.where(validn, bos[np.clip(jn, 0, NKEY - 1)], -1)
        tables.append((cache_tbl.astype(np.int32), new_tbl.astype(np.int32)))
    return tables


def _bias_from_buckets(bk, rb_ref, col):
    out = jnp.full(bk.shape, NEG, F32)
    for b in range(REL_BUCKETS):
        out = jnp.where(bk == b, rb_ref[b, col], out)
    return out


ATT_SUB = 4


def _attn_prompt_body(rb_ref, bk_ref, q_ref, k_ref, v_ref, *refs, gi, headed):
    dil = ATT_DILATIONS[gi]
    has_prev = gi == 0
    if has_prev:
        kp_ref, vp_ref = refs[:2]
        refs = refs[2:]
    o_ref, lse_ref, kvo_ref, bias_ref = refs[-4:]
    if len(refs) == 5:
        kvo_ref[0] = refs[0][...]
        kvo_ref = kvo_ref.at[1]
    step = pl.program_id(1)

    @pl.when((pl.program_id(0) == 0) & (step == 0))
    def _():
        bk = bk_ref[...]
        for h in range(NH):
            bias_ref[h] = _bias_from_buckets(bk, rb_ref, gi * NH + h)

    def tile(ref, h, t):
        rs = slice(t * CH, (t + 1) * CH)
        return ref[h, rs, :] if headed else ref[rs, h * HD:(h + 1) * HD]

    scale = HD ** -0.5
    lane_head = lax.broadcasted_iota(jnp.int32, (CH, NH * LSE_W), 1) // LSE_W
    if has_prev:
        col = lax.broadcasted_iota(jnp.int32, (CH, 2 * CH), 1)
        no_prev = (step == 0) & (col < CH)
    for t in range(ATT_SUB):
        if gi == 0:
            dst = slice(t * CH, (t + 1) * CH)
        elif gi == 1:
            dst = pl.ds(t * CH * dil + step, CH, stride=dil)
        else:
            dst = pl.ds(step * ATT_SUB + t, CH, stride=dil)
        keep = gi == 2 or t == ATT_SUB - 1
        kv_row0 = (t if gi == 2 else 0) * 2 * NH
        lse_tile = None
        for h in range(NH):
            kc = tile(k_ref, h, t)
            vc = tile(v_ref, h, t)
            if keep:
                kvo_ref[:, kv_row0 + h, :] = kc
                kvo_ref[:, kv_row0 + NH + h, :] = vc
            qh = tile(q_ref, h, t).astype(BF16)
            if gi < 2 and (t > 0 or has_prev):
                if t > 0:
                    kp, vp, bias = tile(k_ref, h, t - 1), tile(v_ref, h, t - 1), bias_ref[h]
                else:
                    kp, vp = kp_ref[:, h * HD:(h + 1) * HD], vp_ref[:, h * HD:(h + 1) * HD]
                    bias = jnp.where(no_prev, NEG, bias_ref[h])
                kh = jnp.concatenate([kp, kc], axis=0).astype(BF16)
                vh = jnp.concatenate([vp, vc], axis=0).astype(BF16)
            else:
                kh = kc.astype(BF16)
                vh = vc.astype(BF16)
                bias = bias_ref[h][:, CH:]
            s = lax.dot_general(qh, kh, (((1,), (1,)), ((), ())), preferred_element_type=F32) * scale + bias
            m = jnp.max(s, axis=-1, keepdims=True)
            p = jnp.exp(s - m)
            den = jnp.sum(p, axis=-1, keepdims=True)
            acc = jnp.dot(p.astype(BF16), vh, preferred_element_type=F32)
            o_ref[h, dst, :] = acc / den
            lse = m + jnp.log(den)
            lse_tile = lse if h == 0 else jnp.where(lane_head == h, lse, lse_tile)
        lse_ref[dst, :] = lse_tile


def _attn_prompt(zq, rel_bias, gi, kv_prev):
    dil = ATT_DILATIONS[gi]
    rows = BATCH * SEQ
    nsteps = SEQ // (ATT_SUB * CH)
    assert SEQ // dil // CH in (1, ATT_SUB, ATT_SUB * nsteps)
    headed = gi > 0
    bk = jnp.asarray(_prompt_bucket_matrix(dil))
    span = ATT_SUB * CH

    def zspec(c):
        if headed:
            return pl.BlockSpec((None, NH, span, HD), lambda b, s: (c, 0, b * nsteps + s, 0))
        return pl.BlockSpec((span, BW), lambda b, s: (b * nsteps + s, COL_QKV[0][c]))

    in_specs = [pl.BlockSpec(memory_space=pltpu.SMEM),
                pl.BlockSpec((CH, 2 * CH), lambda b, s: (0, 0)),
                zspec(0), zspec(1), zspec(2)]
    args = [rel_bias, bk, zq, zq, zq]
    if gi == 0:
        above = lambda c: pl.BlockSpec(
            (CH, BW), lambda b, s: (jnp.maximum((b * nsteps + s) * ATT_SUB - 1, 0), COL_QKV[0][c]))
        in_specs += [above(1), above(2)]
        args += [zq, zq]
        out_rows, out_index = span, (lambda b, s: b * nsteps + s)
    else:
        out_rows, out_index = SEQ, (lambda b, s: b)
    n_res = ATT_SUB if gi == 2 else 1
    kv_blk = (CH, n_res * 2 * NH, HD)
    kv_index = lambda b, s: (b, 0, s if gi else 0, 0)
    kv_shape = (BATCH, CH, dil * 2 * NH, HD)
    if kv_prev is None:
        kv_spec = pl.BlockSpec((None, *kv_blk), kv_index)
    else:
        in_specs.append(pl.BlockSpec((None, *kv_blk), kv_index))
        args.append(kv_prev)
        kv_spec = pl.BlockSpec((2, None, *kv_blk), lambda b, s: (0, *kv_index(b, s)))
        kv_shape = (2, *kv_shape)
    return pl.pallas_call(
        functools.partial(_attn_prompt_body, gi=gi, headed=headed),
        grid=(BATCH, nsteps),
        in_specs=in_specs,
        out_specs=[pl.BlockSpec((NH, out_rows, HD), lambda b, s: (0, out_index(b, s), 0)),
                   pl.BlockSpec((out_rows, NH * LSE_W), lambda b, s: (out_index(b, s), 0)),
                   kv_spec],
        out_shape=[jax.ShapeDtypeStruct((NH, rows, HD), F32),
                   jax.ShapeDtypeStruct((rows, NH * LSE_W), F32),
                   jax.ShapeDtypeStruct(kv_shape, F32)],
        scratch_shapes=[pltpu.VMEM((NH, CH, 2 * CH), F32)],
        compiler_params=_cparams(2),
        name=f"attn_prompt_g{gi}",
    )(*args)


def _merge_body(o0, l0, o1, l1, o2, l2, ob_ref, oc_ref, od_ref, gate_ref, wb_ref, wo_ref, x_ref, out_ref):
    heads = []
    for h in range(NH):
        a0, a1, a2 = (l[:, h * LSE_W:h * LSE_W + 1] for l in (l0, l1, l2))
        m = jnp.maximum(jnp.maximum(a0, a1), a2)
        w0, w1, w2 = jnp.exp(a0 - m), jnp.exp(a1 - m), jnp.exp(a2 - m)
        heads.append(((w0 * o0[h] + w1 * o1[h] + w2 * o2[h]) / (w0 + w1 + w2)).astype(BF16))
    oa = jnp.concatenate(heads, axis=1)
    acc = None
    for n, br in enumerate((oa, ob_ref[...], oc_ref[...], od_ref[...])):
        proj = jnp.dot(br, wb_ref[n], preferred_element_type=F32)
        t = gate_ref[:, n * D_MODEL:(n + 1) * D_MODEL].astype(F32) * proj
        acc = t if acc is None else acc + t
    out_ref[...] = x_ref[...] + jnp.dot(acc.astype(BF16), wo_ref[...], preferred_element_type=F32)


def _merge(parts, ob, oc, od, gates, wb, wo, x, *, layer, tm=256):
    m = x.shape[0]
    once = dict(pipeline_mode=pl.Buffered(1))
    ospec = pl.BlockSpec((NH, tm, HD), lambda i: (0, i, 0))
    lspec = pl.BlockSpec((tm, NH * LSE_W), lambda i: (i, 0))
    br_spec = pl.BlockSpec((tm, BW), lambda i: (i, 0))
    row_spec = pl.BlockSpec((tm, D_MODEL), lambda i: (i, 0))
    return pl.pallas_call(
        _merge_body,
        grid=(m // tm,),
        in_specs=[ospec, lspec] * 3 + [br_spec] * 3 + [
            pl.BlockSpec((tm, GATE_W), lambda i: (i, 0)),
            pl.BlockSpec((None, 4, BW, D_MODEL), lambda i: (layer, 0, 0, 0), **once),
            pl.BlockSpec((None, D_MODEL, D_MODEL), lambda i: (layer, 0, 0), **once),
            row_spec],
        out_specs=row_spec,
        out_shape=jax.ShapeDtypeStruct((m, D_MODEL), F32),
        compiler_params=_cparams(1, VMEM_BIG),
        name="merge",
    )(*parts, ob, oc, od, gates, wb, wo, x)


def _attn_sample_body(rb_ref, bc0_ref, bc1_ref, bc2_ref, bn_ref, z_ref, c0_ref, c1_ref, c2_ref,
                      oa_ref, kv0_ref, kv1_ref, kv2_ref, b0_ref, b1_ref, b2_ref, bnew_ref):
    bucket_refs = (bc0_ref, bc1_ref, bc2_ref)
    bias_refs = (b0_ref, b1_ref, b2_ref)

    @pl.when(pl.program_id(0) == 0)
    def _():
        for gi in range(3):
            for h in range(NH):
                rs = slice(h * SROWS, (h + 1) * SROWS)
                bias_refs[gi][rs, :] = _bias_from_buckets(bucket_refs[gi][rs, :], rb_ref, gi * NH + h)
                bnew_ref[gi, rs, :] = _bias_from_buckets(bn_ref[gi, rs, :], rb_ref, gi * NH + h)

    scale = HD ** -0.5
    rows = NH * SROWS
    head_of_row = lax.broadcasted_iota(jnp.int32, (rows, BW), 0) // SROWS
    head_of_lane = lax.broadcasted_iota(jnp.int32, (rows, BW), 1) // HD
    head_mask = head_of_row == head_of_lane
    caches = (c0_ref, c1_ref, c2_ref)
    kv_refs = (kv0_ref, kv1_ref, kv2_ref)
    stats = []
    for gi in range(3):
        cq, ck, cv = COL_QKV[gi]
        q = z_ref[:, cq * BW:(cq + 1) * BW]
        kn = z_ref[:, ck * BW:(ck + 1) * BW]
        vn = z_ref[:, cv * BW:(cv + 1) * BW]
        kv_refs[gi][:, :BW] = kn
        kv_refs[gi][:, BW:] = vn
        qm = jnp.concatenate([q[:, h * HD:(h + 1) * HD] for h in range(NH)], axis=0).astype(BF16)
        kf = caches[gi][...].reshape(SAMPLE_CACHE_ROWS[gi], HD).astype(BF16)
        s = lax.dot_general(qm, kf, (((1,), (1,)), ((), ())), preferred_element_type=F32) * scale + bias_refs[gi][...]
        qrows = jnp.where(head_mask, jnp.concatenate([q] * NH, axis=0), 0.0)
        bias_n = bnew_ref[gi]
        s_new = []
        for tp in range(DEC_SEQ):
            dotp = jnp.sum(qrows * kn[tp:tp + 1, :], axis=-1, keepdims=True)
            s_new.append(dotp * scale + bias_n[:, tp:tp + 1])
        m = jnp.max(s, axis=-1, keepdims=True)
        for sn in s_new:
            m = jnp.maximum(m, sn)
        p = jnp.exp(s - m)
        den = jnp.sum(p, axis=-1, keepdims=True)
        pv = pltpu.roll(p, NH, axis=1).astype(BF16)
        acc = jnp.dot(pv, kf, preferred_element_type=F32)
        acc_n = jnp.zeros((rows, BW), F32)
        for tp, sn in enumerate(s_new):
            pn = jnp.exp(sn - m)
            den = den + pn
            acc_n = acc_n + pn * vn[tp:tp + 1, :]
        acc = acc + jnp.concatenate([acc_n[h * SROWS:(h + 1) * SROWS, h * HD:(h + 1) * HD] for h in range(NH)], axis=0)
        stats.append((m, den, acc))
    mm = jnp.maximum(jnp.maximum(stats[0][0], stats[1][0]), stats[2][0])
    den = jnp.zeros((rows, 1), F32)
    acc = jnp.zeros((rows, HD), F32)
    for m, d, a in stats:
        w = jnp.exp(m - mm)
        den = den + w * d
        acc = acc + w * a
    o = acc / den
    for h in range(NH):
        oa_ref[:, h * HD:(h + 1) * HD] = o[h * SROWS:(h + 1) * SROWS, :].astype(BF16)


def _attn_sample(z, caches, rel_bias, layer):
    tables = _sample_bucket_tables()
    rows = DEC_BATCH * SROWS
    qrows = NH * SROWS
    n0, n1, n2 = SAMPLE_CACHE_ROWS
    c0 = caches[0].reshape(DEPTH, DEC_BATCH, n0, HD)
    c1 = caches[1].reshape(DEPTH, DEC_BATCH, n1, HD)
    c2 = caches[2].reshape(DEPTH, DEC_BATCH, CH, 16 * 8, HD)
    new_tbl = jnp.asarray(np.stack([t[1] for t in tables]))
    kv_spec = pl.BlockSpec((SROWS, 2 * BW), lambda b: (b, 0))
    kv_shape = jax.ShapeDtypeStruct((rows, 2 * BW), F32)
    const2 = lambda b: (0, 0)
    return pl.pallas_call(
        _attn_sample_body,
        grid=(DEC_BATCH,),
        in_specs=[pl.BlockSpec(memory_space=pltpu.SMEM),
                  pl.BlockSpec((qrows, n0), const2), pl.BlockSpec((qrows, n1), const2),
                  pl.BlockSpec((qrows, n2), const2),
                  pl.BlockSpec((3, qrows, CH), lambda b: (0, 0, 0)),
                  pl.BlockSpec((SROWS, IN_WIDTH), lambda b: (b, 0)),
                  pl.BlockSpec((None, None, n0, HD), lambda b: (layer, b, 0, 0)),
                  pl.BlockSpec((None, None, n1, HD), lambda b: (layer, b, 0, 0)),
                  pl.BlockSpec((None, None, CH, 8 * DEC_SEQ, HD), lambda b: (layer, b, 0, 0, 0))],
        out_specs=[pl.BlockSpec((SROWS, BW), lambda b: (b, 0)), kv_spec, kv_spec, kv_spec],
        out_shape=[jax.ShapeDtypeStruct((rows, BW), BF16), kv_shape, kv_shape, kv_shape],
        scratch_shapes=[pltpu.VMEM((qrows, n0), F32), pltpu.VMEM((qrows, n1), F32), pltpu.VMEM((qrows, n2), F32),
                        pltpu.VMEM((3, qrows, CH), F32)],
        compiler_params=_cparams(1),
        name="attn_sample",
    )(rel_bias, jnp.asarray(tables[0][0]), jnp.asarray(tables[1][0]), jnp.asarray(tables[2][0]), new_tbl,
      z, c0, c1, c2)


def _gmlp_body(bu_ref, bv_ref, g_ref, ws_ref, bs_ref, *refs, rows, emit_vn):
    refs = list(refs)
    o_ref = refs.pop(0)
    vn_ref = refs.pop(0) if emit_vn else None
    tril = lax.broadcasted_iota(jnp.int32, (CH, CH), 0) >= lax.broadcasted_iota(jnp.int32, (CH, CH), 1)
    w = [jnp.where(tril, ws_ref[g], 0.0).astype(BF16) for g in range(NH)]
    if rows < CH:
        pad_u, pad_v = refs
        pad_u[...] = jnp.zeros_like(pad_u)
        pad_v[...] = jnp.zeros_like(pad_v)
        pad_u[0:rows, :] = bu_ref[...]
        pad_v[0:rows, :] = bv_ref[...]
        bu_ref, bv_ref = pad_u, pad_v
    for c in range(max(rows // CH, 1)):
        rs = slice(c * CH, (c + 1) * CH)
        u = jax.nn.gelu(bu_ref[rs, :])
        vn = _rms(jax.nn.gelu(bv_ref[rs, :]), g_ref[...])
        if emit_vn:
            vn_ref[...] = vn[:rows]
        for g in range(NH):
            sl = slice(g * HD, (g + 1) * HD)
            mixed = jnp.dot(w[g], vn[:, sl].astype(BF16), preferred_element_type=F32) + bs_ref[:, sl]
            res = (u[:, sl] * mixed).astype(BF16)
            if rows < CH:
                o_ref[:, sl] = res[:rows]
            else:
                o_ref[rs, sl] = res


def _gmlp(z, g_gmlp, w_spatial, b_spatial, *, rows, emit_vn):
    m = z.shape[0]
    bs_full = jnp.repeat(b_spatial.T, HD, axis=1)
    spec = pl.BlockSpec((rows, BW), lambda i: (i, 0))
    out_shape = [jax.ShapeDtypeStruct((m, BW), BF16)]
    out_specs = [spec]
    if emit_vn:
        out_shape.append(jax.ShapeDtypeStruct((m, BW), F32))
        out_specs.append(spec)
    scratch = [] if rows >= CH else [pltpu.VMEM((CH, BW), F32)] * 2
    return pl.pallas_call(
        functools.partial(_gmlp_body, rows=rows, emit_vn=emit_vn),
        grid=(m // rows,),
        in_specs=[pl.BlockSpec((rows, BW), lambda i: (i, COL_BU)),
                  pl.BlockSpec((rows, BW), lambda i: (i, COL_BV)),
                  pl.BlockSpec((1, BW), lambda i: (0, 0)),
                  pl.BlockSpec((NH, CH, CH), lambda i: (0, 0, 0)),
                  pl.BlockSpec((CH, BW), lambda i: (0, 0))],
        out_specs=out_specs,
        out_shape=out_shape,
        scratch_shapes=scratch,
        compiler_params=_cparams(1),
        name="gmlp",
    )(z, z, g_gmlp.reshape(1, -1), w_spatial, bs_full)


def _pool_body(x_ref, prev_ref, wp_ref, sc_ref, o_ref, st_ref, ext_ref, *, rows, n_new, start, zero_first_prev):
    ib = pl.program_id(1)
    prev = prev_ref[...]
    if zero_first_prev:
        prev = jnp.where(ib == 0, 0.0, prev)
    x = x_ref[...]
    ext_ref[0:16, :] = prev
    ext_ref[16:16 + rows, :] = x
    st_ref[...] = ext_ref[pl.ds(n_new + 1, POOL_STATE), :]
    ext = ext_ref[...]
    pos = start + ib * rows + lax.broadcasted_iota(jnp.int32, (rows, 1), 0)
    for gi, win in enumerate(POOL_WINDOWS):
        sl = slice(gi * HD, (gi + 1) * HD)
        s = ext[:, sl]
        k = 1
        while k < win:
            s = s + pltpu.roll(s, k, axis=0)
            k *= 2
        cnt = jnp.minimum(pos + 1, win).astype(F32)
        diff = s[16:] / cnt - x[:, sl]
        y = jnp.dot(diff.astype(BF16), wp_ref[gi].astype(BF16), preferred_element_type=F32)
        o_ref[:, sl] = (y * sc_ref[:, sl]).astype(BF16)


def _pool(z, prev, w_pool, pool_scale, *, nseq, rows, n_new, start, layer=None):
    m = z.shape[0]
    nblk = m // nseq // rows
    if prev is None:
        per16 = rows // 16
        prev_arr = z
        prev_spec = pl.BlockSpec((16, BW), lambda b, i: (jnp.maximum((b * nblk + i) * per16 - 1, 0), COL_CIN))
    else:
        prev_arr = prev
        prev_spec = pl.BlockSpec((None, None, 16, BW), lambda b, i: (layer, b, 0, 0))
    return pl.pallas_call(
        functools.partial(_pool_body, rows=rows, n_new=n_new, start=start, zero_first_prev=prev is None),
        grid=(nseq, nblk),
        in_specs=[pl.BlockSpec((rows, BW), lambda b, i: (b * nblk + i, COL_CIN)),
                  prev_spec,
                  pl.BlockSpec((NH, HD, HD), lambda b, i: (0, 0, 0)),
                  pl.BlockSpec((1, BW), lambda b, i: (0, 0))],
        out_specs=[pl.BlockSpec((rows, BW), lambda b, i: (b * nblk + i, 0)),
                   pl.BlockSpec((None, POOL_STATE, BW), lambda b, i: (b, 0, 0))],
        out_shape=[jax.ShapeDtypeStruct((m, BW), BF16),
                   jax.ShapeDtypeStruct((nseq, POOL_STATE, BW), F32)],
        scratch_shapes=[pltpu.VMEM((16 + rows, BW), F32)],
        compiler_params=_cparams(2),
        name="pool",
    )(z, prev_arr, w_pool, pool_scale.reshape(1, -1))


def _gmlp_sample_body(bu_ref, bv_ref, g_ref, wk_ref, bs_ref, o_ref, vn_ref):
    u = jax.nn.gelu(bu_ref[...])
    vn = _rms(jax.nn.gelu(bv_ref[...]), g_ref[...])
    vn_ref[...] = vn
    mixed = jnp.tile(bs_ref[...], (DEC_BATCH, 1))
    for k in range(DEC_SEQ):
        shifted = vn if k == 0 else pltpu.roll(vn, k, axis=0)
        mixed = mixed + jnp.tile(wk_ref[k], (DEC_BATCH, 1)) * shifted
    o_ref[...] = (u * mixed).astype(BF16)


def _gmlp_sample(z, g_gmlp, w_spatial, b_spatial):
    m = z.shape[0]
    t = np.arange(SROWS)
    live = t < DEC_SEQ
    pick = np.zeros((DEC_SEQ, SROWS, SROWS), np.float32)
    for k in range(DEC_SEQ):
        pick[k, t[live & (t >= k)], t[live & (t >= k)] - k] = 1.0
    diag = jnp.sum(w_spatial[None, :, :SROWS, :SROWS] * pick[:, None], axis=-1)
    wk = jnp.repeat(diag.transpose(0, 2, 1), HD, axis=2)
    bs = jnp.repeat((b_spatial[:, :SROWS] * jnp.asarray(live, F32)).T, HD, axis=1)
    return pl.pallas_call(
        _gmlp_sample_body,
        grid=(1,),
        in_specs=[pl.BlockSpec((m, BW), lambda i: (0, COL_BU)),
                  pl.BlockSpec((m, BW), lambda i: (0, COL_BV)),
                  pl.BlockSpec((1, BW), lambda i: (0, 0)),
                  pl.BlockSpec((DEC_SEQ, SROWS, BW), lambda i: (0, 0, 0)),
                  pl.BlockSpec((SROWS, BW), lambda i: (0, 0))],
        out_specs=[pl.BlockSpec((m, BW), lambda i: (0, 0))] * 2,
        out_shape=[jax.ShapeDtypeStruct((m, BW), BF16), jax.ShapeDtypeStruct((m, BW), F32)],
        compiler_params=_cparams(1),
        name="gmlp_sample",
    )(z, z, g_gmlp.reshape(1, -1), wk, bs)


POOL_PAD = 16


def _pool_sample_body(x_ref, prev_ref, wp_ref, sc_ref, o_ref, st_ref, ext_ref, *, start):
    per = POOL_PAD + SROWS
    ext_ref[:, 0:POOL_PAD, :] = prev_ref[...]
    ext_ref[:, POOL_PAD:per, :] = x_ref[...]
    st_ref[...] = ext_ref[:, pl.ds(DEC_SEQ + 1, POOL_STATE), :]
    ext = ext_ref[...].reshape(DEC_BATCH * per, BW)
    pos = start + (lax.broadcasted_iota(jnp.int32, (DEC_BATCH * per, 1), 0) % per - POOL_PAD)
    for gi, win in enumerate(POOL_WINDOWS):
        sl = slice(gi * HD, (gi + 1) * HD)
        x = ext[:, sl]
        s = x
        k = 1
        while k < win:
            s = s + pltpu.roll(s, k, axis=0)
            k *= 2
        cnt = jnp.minimum(jnp.maximum(pos, 0) + 1, win).astype(F32)
        y = jnp.dot((s / cnt - x).astype(BF16), wp_ref[gi].astype(BF16), preferred_element_type=F32)
        o_ref[:, :, sl] = (y * sc_ref[:, sl]).reshape(DEC_BATCH, per, HD)[:, POOL_PAD:, :].astype(BF16)


def _pool_sample(z, prev, w_pool, pool_scale, *, start, layer):
    m, zw = z.shape
    z3 = z.reshape(DEC_BATCH, SROWS, zw)
    o, st = pl.pallas_call(
        functools.partial(_pool_sample_body, start=start),
        grid=(1,),
        in_specs=[pl.BlockSpec((DEC_BATCH, SROWS, BW), lambda i: (0, 0, COL_CIN)),
                  pl.BlockSpec((None, DEC_BATCH, POOL_PAD, BW), lambda i: (layer, 0, 0, 0)),
                  pl.BlockSpec((NH, HD, HD), lambda i: (0, 0, 0)),
                  pl.BlockSpec((1, BW), lambda i: (0, 0))],
        out_specs=[pl.BlockSpec((DEC_BATCH, SROWS, BW), lambda i: (0, 0, 0)),
                   pl.BlockSpec((DEC_BATCH, POOL_STATE, BW), lambda i: (0, 0, 0))],
        out_shape=[jax.ShapeDtypeStruct((DEC_BATCH, SROWS, BW), BF16),
                   jax.ShapeDtypeStruct((DEC_BATCH, POOL_STATE, BW), F32)],
        scratch_shapes=[pltpu.VMEM((DEC_BATCH, POOL_PAD + SROWS, BW), F32)],
        compiler_params=_cparams(1),
        name="pool_sample",
    )(z3, prev, w_pool, pool_scale.reshape(1, -1))
    return o.reshape(m, BW), st


def _ret_tables(c_eff, positions, size):
    lg = np.log1p(-np.power(2.0, -5.0 - np.arange(NH, dtype=np.float64)))
    i = np.arange(size, dtype=np.float64)
    live = (i < c_eff)
    diff = i[:, None] - i[None, :]
    inner = np.where((diff >= 0) & live[:, None] & live[None, :], np.exp(np.maximum(diff, 0.0)[None] * lg[:, None, None]), 0.0)
    qd = np.where(live[None, :], np.exp((i + 1.0)[None, :] * lg[:, None]), 0.0)
    kd = np.where(live[None, :], np.exp((c_eff - 1.0 - i)[None, :] * lg[:, None]), 0.0)
    chunk = tuple(float(v) for v in np.exp(c_eff * lg))
    qd_full = np.repeat(qd.T, HD, axis=1)
    kd_full = np.repeat(kd.T, HD, axis=1) * (HD ** -0.5)
    half = HD // 2
    inv = ROPE_BASE ** (-np.arange(half, dtype=np.float64) / half)
    ang = np.asarray(positions, np.float64)[:, None] * inv[None, :]
    cosf = np.concatenate([np.cos(ang), np.cos(ang)], axis=1)
    sinf = np.concatenate([-np.sin(ang), np.sin(ang)], axis=1)
    to32 = lambda a: jnp.asarray(a.astype(np.float32))
    return to32(inner), to32(qd_full), to32(kd_full), chunk, to32(cosf), to32(sinf)


def _ret_body(q_ref, k_ref, v_ref, g_ref, cos_ref, sin_ref, inner_ref, qd_ref, kd_ref, gr_ref, *refs,
              nseq, rows, chunk_decay, has_state):
    refs = list(refs)
    s0_ref = refs.pop(0) if has_state else None
    o_ref, sn_ref, s_ref = refs[:3]
    pad_ref = refs[3] if rows < CH else None
    ic = pl.program_id(1)

    @pl.when(ic == 0)
    def _():
        if has_state:
            s_ref[...] = s0_ref[...]
        else:
            s_ref[...] = jnp.zeros_like(s_ref)

    def chunk(ref, b, k):
        if rows >= CH:
            return ref[b]
        pad_ref[k] = jnp.zeros((CH, BW), F32)
        pad_ref[k, 0:rows, :] = ref[b]
        return pad_ref[k]

    cosf = cos_ref[...]
    sinf = sin_ref[...]
    for b in range(nseq):
        q = chunk(q_ref, b, 0)
        k = chunk(k_ref, b, 1)
        v = chunk(v_ref, b, 2)
        gate = chunk(g_ref, b, 3)
        for h in range(NH):
            sl = slice(h * HD, (h + 1) * HD)
            qh = q[:, sl]
            kh = k[:, sl]
            rq = qh * cosf + pltpu.roll(qh, HD // 2, axis=1) * sinf
            rk = kh * cosf + pltpu.roll(kh, HD // 2, axis=1) * sinf
            vb = v[:, sl].astype(BF16)
            rqb = rq.astype(BF16)
            state = s_ref[b, h]
            att = lax.dot_general(rqb, (rk * (HD ** -0.5)).astype(BF16), (((1,), (1,)), ((), ())),
                                  preferred_element_type=F32) * inner_ref[h]
            o = (jnp.dot(att.astype(BF16), vb, preferred_element_type=F32)
                 + jnp.dot(rqb, state.astype(BF16), preferred_element_type=F32) * qd_ref[:, sl])
            kdec = (rk * kd_ref[:, sl]).astype(BF16)
            new_state = state * chunk_decay[h] + lax.dot_general(
                kdec, vb, (((0,), (0,)), ((), ())), preferred_element_type=F32)
            s_ref[b, h] = new_state
            sn_ref[b, h] = new_state
            o = o * lax.rsqrt(jnp.mean(o * o, axis=-1, keepdims=True) + EPS)
            gt = gate[:, sl]
            o_ref[b, :, sl] = (o * gr_ref[:, sl] * (gt * jax.nn.sigmoid(gt)))[:rows].astype(BF16)


def _retention(z, g_ret, state, *, nseq, per_step, rows, c_eff, positions, layer=None):
    m, zw = z.shape
    seq_rows = m // nseq
    nchunk = seq_rows // rows
    size = max(rows, CH)
    inner, qd, kd, chunk_decay, cosf, sinf = _ret_tables(c_eff, positions, size)
    has_state = state is not None
    z3 = z.reshape(nseq, seq_rows, zw)

    def zspec(col):
        return pl.BlockSpec((per_step, rows, BW), lambda g, i: (g, i, col))

    const2 = lambda g, i: (0, 0)
    in_specs = [zspec(COL_DQ), zspec(COL_DK), zspec(COL_DV), zspec(COL_DG),
                pl.BlockSpec((size, HD), lambda g, i: (i, 0)), pl.BlockSpec((size, HD), lambda g, i: (i, 0)),
                pl.BlockSpec((NH, size, size), lambda g, i: (0, 0, 0)),
                pl.BlockSpec((size, BW), const2), pl.BlockSpec((size, BW), const2), pl.BlockSpec((1, BW), const2)]
    args = [z3, z3, z3, z3, cosf, sinf, inner, qd, kd, g_ret.reshape(1, -1)]
    if has_state:
        in_specs.append(pl.BlockSpec((None, per_step, NH, HD, HD), lambda g, i: (layer, g, 0, 0, 0)))
        args.append(state)
    scratch = [pltpu.VMEM((per_step, NH, HD, HD), F32)]
    if rows < CH:
        scratch.append(pltpu.VMEM((4, CH, BW), F32))
    o, sn = pl.pallas_call(
        functools.partial(_ret_body, nseq=per_step, rows=rows, chunk_decay=chunk_decay, has_state=has_state),
        grid=(nseq // per_step, nchunk),
        in_specs=in_specs,
        out_specs=[pl.BlockSpec((per_step, rows, BW), lambda g, i: (g, i, 0)),
                   pl.BlockSpec((per_step, NH, HD, HD), lambda g, i: (g, 0, 0, 0))],
        out_shape=[jax.ShapeDtypeStruct((nseq, seq_rows, BW), BF16),
                   jax.ShapeDtypeStruct((nseq, NH, HD, HD), F32)],
        scratch_shapes=scratch,
        compiler_params=_cparams(2),
        name="retention",
    )(*args)
    return o.reshape(m, BW), sn


def kernel(x_prompt, x_sample, cache_attn_kv_w128, cache_attn_kv_w512, cache_attn_kv_w2048, state_pool, state_ret, rel_bias, g_ffn1, w_ffn1_gate, w_ffn1_up, w_ffn1_down, g_mix, w_in, g_gmlp, w_spatial, b_spatial, w_pool, pool_scale, g_ret, w_branch, w_out, g_ffn2, w_ffn2_gate, w_ffn2_up, w_ffn2_down, g_final):
    caches = (cache_attn_kv_w128, cache_attn_kv_w512, cache_attn_kv_w2048)
    xp = x_prompt.reshape(BATCH * SEQ, D_MODEL)
    xs = jnp.pad(x_sample, ((0, 0), (0, SROWS - DEC_SEQ), (0, 0))).reshape(DEC_BATCH * SROWS, D_MODEL)
    pool_state = jnp.pad(state_pool, ((0, 0), (0, 0), (1, 0), (0, 0)))
    tm_p, tm_s = 1024, DEC_BATCH * SROWS
    sample_pos = PAST_LEN + np.arange(CH)
    wb = w_branch.astype(BF16)
    wo = w_out.astype(BF16)
    gate_col0 = N_PLAIN * BW // 1024
    qkv1, qkv2 = COL_QKV[1][0], COL_QKV[2][0]

    kv_p = [None, None, None]
    kv_s = [[], [], []]
    pool_p, pool_s, ret_p, ret_s, gv_s = [], [], [], [], []
    yp = ys = None
    for l in range(DEPTH):
        last = l == DEPTH - 1

        xs, hs, *w1 = _ffn(xs, g_ffn1[l], (w_ffn1_gate, w_ffn1_up, w_ffn1_down), g_mix[l], layer=l,
                           emit_x=True, post_dtype=BF16, tm=tm_s)
        zs, gs, win = _in_proj_convert(hs, w_in, layer=l)
        oa, k0, k1, k2 = _attn_sample(zs, caches, rel_bias, l)
        for gi, kv in enumerate((k0, k1, k2)):
            kv_s[gi].append(kv.reshape(DEC_BATCH, SROWS, 2, NH, HD)[:, :DEC_SEQ])
        ob, vn = _gmlp_sample(zs, g_gmlp[l], w_spatial[l], b_spatial[l])
        gv_s.append(vn.reshape(DEC_BATCH, SROWS, BW)[:, :DEC_SEQ])
        oc, pn = _pool_sample(zs, pool_state, w_pool[l], pool_scale[l], start=PAST_LEN, layer=l)
        od, rn = _retention(zs, g_ret[l], state_ret, nseq=DEC_BATCH, per_step=4, rows=SROWS, c_eff=DEC_SEQ,
                            positions=sample_pos, layer=l)
        pool_s.append(pn)
        ret_s.append(rn)
        mix = _branch_merge(oa, ob, oc, od, gs, wb, layer=l, tm=tm_s)
        xs = _matmul(mix, wo, xs, layer=l, tm=tm_s, tn=512, name="out_proj")
        if last:
            ys, *w2 = _ffn(xs, g_ffn2[l], (w_ffn2_gate, w_ffn2_up, w_ffn2_down), g_final, layer=l,
                           emit_x=False, post_dtype=F32, tm=tm_s)
        else:
            xs, *w2 = _ffn(xs, g_ffn2[l], (w_ffn2_gate, w_ffn2_up, w_ffn2_down), g_final, layer=l,
                           emit_x=True, post_dtype=None, tm=tm_s)

        xp, hp = _ffn(xp, g_ffn1[l], w1, g_mix[l], emit_x=True, post_dtype=BF16, tm=tm_p)
        zp = _matmul(hp, win, n=N_PLAIN * BW, tm=2048, tn=1024, name="in_proj")
        gp = _matmul(hp, win, col0=gate_col0, n=GATE_W, gate=True, tm=2048, tn=1024, name="in_proj_gate")
        zq = (zp,
              _matmul_deint(hp, win, col0=qkv1, dil=ATT_DILATIONS[1], name="in_proj_g1"),
              _matmul_deint(hp, win, col0=qkv2, dil=ATT_DILATIONS[2], name="in_proj_g2"))
        parts = []
        for gi in range(3):
            o, lse, kv_p[gi] = _attn_prompt(zq[gi], rel_bias, gi, kv_p[gi])
            parts += [o, lse]
        ob, = _gmlp(zp, g_gmlp[l], w_spatial[l], b_spatial[l], rows=4 * CH, emit_vn=False)
        oc, pn = _pool(zp, None, w_pool[l], pool_scale[l], nseq=BATCH, rows=512, n_new=512, start=0)
        od, rn = _retention(zp, g_ret[l], None, nseq=BATCH, per_step=BATCH, rows=CH, c_eff=CH,
                            positions=np.arange(SEQ))
        pool_p.append(pn)
        ret_p.append(rn)
        xp = _merge(parts, ob, oc, od, gp, wb, wo, xp, layer=l)
        if last:
            yp, = _ffn(xp, g_ffn2[l], w2, g_final, emit_x=False, post_dtype=F32, tm=tm_p)
        else:
            xp, = _ffn(xp, g_ffn2[l], w2, g_final, emit_x=True, post_dtype=None, tm=tm_p)

    y_prompt = yp.reshape(BATCH, SEQ, D_MODEL)
    y_sample = ys.reshape(DEC_BATCH, SROWS, D_MODEL)[:, :DEC_SEQ]
    kv_p = [kv.reshape(DEPTH, BATCH, ATT_WINDOWS[gi], 2, NH, HD) for gi, kv in enumerate(kv_p)]
    return (y_prompt, y_sample,
            kv_p[0], kv_p[1], kv_p[2],
            jnp.stack(kv_s[0]), jnp.stack(kv_s[1]), jnp.stack(kv_s[2]),
            jnp.stack(pool_p), jnp.stack(pool_s),
            jnp.stack(ret_p), jnp.stack(ret_s),
            jnp.stack(gv_s))
```

```python
import functools

import numpy as np
import jax
import jax.numpy as jnp
from jax import lax
from jax.experimental import pallas as pl
from jax.experimental.pallas import tpu as pltpu

F32 = jnp.float32
BF16 = jnp.bfloat16

D_MODEL = 2048
BATCH = 4
SEQ = 2048
DEPTH = 2
DEC_BATCH = 32
DEC_SEQ = 4
PAST_LEN = 8192
D_FF = 5632
IN_WIDTH = 16384
EPS = 1e-6
BW = 512
HD = 128
NH = 4
ATT_WINDOWS = (128, 512, 2048)
ATT_DILATIONS = (1, 4, 16)
NKEY = 129
REL_BUCKETS = 32
REL_MAX_DISTANCE = 2048
POOL_WINDOWS = (2, 4, 8, 16)
POOL_STATE = 15
ROPE_BASE = 10000.0
SROWS = 8
CH = 128
NEG = -1e30
LSE_W = 32

W_IN_ORDER = (0, 3, 6) + tuple(range(9, 32)) + (1, 4, 7) + (2, 5, 8)
COL_QKV = ((0, 1, 2), (26, 27, 28), (29, 30, 31))
COL_BU, COL_BV, COL_CIN = 3, 4, 5
COL_DQ, COL_DK, COL_DV, COL_DG = 6, 7, 8, 9
N_PLAIN = 10
N_GATE = 16
GATE_W = N_GATE * BW
ZBLK = IN_WIDTH // BW

VMEM_BIG = 56 * 1024 * 1024
VMEM_FFN = 62 * 1024 * 1024

TM_FFN = 1024
TM_IN, TN_IN = 2048, 1024
TM_MERGE = 256
TN_OUT_S = 512
MIX_ROWS = 8 * CH


def _cparams(n_axes, vmem=None):
    return pltpu.CompilerParams(dimension_semantics=("arbitrary",) * n_axes, vmem_limit_bytes=vmem)


def _rms(x, g):
    return x * lax.rsqrt(jnp.mean(x * x, axis=-1, keepdims=True) + EPS) * g


def _ffn_body(x_ref, gpre_ref, *refs, emit_x, emit_post, convert, nf, tf, row_split):
    refs = list(refs)
    if convert:
        wg_ref, wu_ref, wd_ref, gpost_ref = refs[:4]
        refs = refs[4:]
    else:
        wgu_ref, wd_ref, gpost_ref = refs[:3]
        refs = refs[3:]
    n_act = int(emit_x) + int(emit_post)
    outs = refs[:n_act]
    scratch = refs[n_act + (2 if convert else 0):]
    xn_ref = scratch[0] if scratch else outs[1]
    acc_ref = outs[0]
    f = pl.program_id(1)

    @pl.when(f == 0)
    def _():
        x = x_ref[...]
        xn_ref[...] = _rms(x, gpre_ref[...]).astype(BF16)
        acc_ref[...] = x

    if convert:
        wgu = jnp.concatenate([wg_ref[...].astype(BF16), wu_ref[...].astype(BF16)], axis=1)
        wd = wd_ref[...].astype(BF16)
        refs[n_act][...] = wgu
        refs[n_act + 1][...] = wd
    else:
        wgu, wd = wgu_ref[...], wd_ref[...]
    rows = xn_ref.shape[0] // row_split
    for part in range(row_split):
        rs = slice(part * rows, (part + 1) * rows)
        r = jnp.dot(xn_ref[rs, :], wgu, preferred_element_type=F32)
        g, u = r[:, :tf], r[:, tf:]
        h = (g * jax.nn.sigmoid(g) * (0.5 * u)).astype(BF16)
        acc_ref[rs, :] += jnp.dot(h, wd, preferred_element_type=F32)

    if emit_post:
        @pl.when(f == nf - 1)
        def _():
            outs[-1][...] = _rms(acc_ref[...], gpost_ref[...]).astype(outs[-1].dtype)


def _ffn(x, g_pre, weights, g_post, *, layer=None, emit_x, post_dtype, tm, tf=512):
    m = x.shape[0]
    nf = D_FF // tf
    emit_post = post_dtype is not None
    convert = layer is not None
    assert emit_x or post_dtype == F32, "the first output block is the f32 accumulator"
    once = dict(pipeline_mode=pl.Buffered(1)) if tm > 512 else {}
    row_spec = pl.BlockSpec((tm, D_MODEL), lambda i, f: (i, 0))
    row_out_spec = pl.BlockSpec((tm, D_MODEL), lambda i, f: (i, 0), **once)
    vec_spec = pl.BlockSpec((1, D_MODEL), lambda i, f: (0, 0))
    out_shape, out_specs = [], []
    if emit_x:
        out_shape.append(jax.ShapeDtypeStruct((m, D_MODEL), F32))
        out_specs.append(row_spec)
    post_holds_xn = emit_x and post_dtype == BF16 and tm > 512
    if emit_post:
        out_shape.append(jax.ShapeDtypeStruct((m, D_MODEL), post_dtype))
        out_specs.append(row_out_spec if emit_x and not post_holds_xn else row_spec)
    gu_spec = pl.BlockSpec((D_MODEL, 2 * tf), lambda i, f: (0, f))
    down_spec = pl.BlockSpec((tf, D_MODEL), lambda i, f: (f, 0))
    if convert:
        assert m == tm, "the bf16 weights are written once, by a single row tile"
        w_specs = [pl.BlockSpec((None, D_MODEL, tf), lambda i, f: (layer, 0, f)),
                   pl.BlockSpec((None, D_MODEL, tf), lambda i, f: (layer, 0, f)),
                   pl.BlockSpec((None, tf, D_MODEL), lambda i, f: (layer, f, 0))]
        out_shape += [jax.ShapeDtypeStruct((D_MODEL, 2 * D_FF), BF16), jax.ShapeDtypeStruct((D_FF, D_MODEL), BF16)]
        out_specs += [gu_spec, down_spec]
    else:
        w_specs = [gu_spec, down_spec]
    scratch = [] if post_holds_xn else [pltpu.VMEM((tm, D_MODEL), BF16)]
    return pl.pallas_call(
        functools.partial(_ffn_body, emit_x=emit_x, emit_post=emit_post, convert=convert, nf=nf, tf=tf,
                          row_split=max(tm // 512, 1)),
        grid=(m // tm, nf),
        in_specs=[row_spec, vec_spec, *w_specs, vec_spec],
        out_specs=out_specs,
        out_shape=out_shape,
        scratch_shapes=scratch,
        compiler_params=_cparams(2, VMEM_FFN),
        name="ffn",
    )(x, g_pre.reshape(1, -1), *weights, g_post.reshape(1, -1))


def _mm_body(a_ref, b_ref, o_ref):
    o_ref[...] = jnp.dot(a_ref[...], b_ref[...], preferred_element_type=F32)


def _mm_res_body(a_ref, b_ref, r_ref, o_ref):
    o_ref[...] = r_ref[...] + jnp.dot(a_ref[...], b_ref[...], preferred_element_type=F32)


def _sigmoid(x):
    return 0.5 * jnp.tanh(0.5 * x) + 0.5


def _mm_gate_body(a_ref, b_ref, o_ref):
    o_ref[...] = _sigmoid(jnp.dot(a_ref[...], b_ref[...], preferred_element_type=F32)).astype(BF16)


def _matmul(a, b, res=None, *, layer=None, col0=0, n=None, gate=False, tm, tn, name):
    m, k = a.shape
    n = b.shape[-1] if n is None else n
    if layer is None:
        b_spec = pl.BlockSpec((k, tn), lambda i, j: (0, col0 + j))
    else:
        b_spec = pl.BlockSpec((None, k, tn), lambda i, j: (layer, 0, col0 + j))
    in_specs = [pl.BlockSpec((tm, k), lambda i, j: (i, 0)), b_spec]
    args = [a, b]
    body = _mm_gate_body if gate else _mm_body
    if res is not None:
        in_specs.append(pl.BlockSpec((tm, tn), lambda i, j: (i, j)))
        args.append(res)
        body = _mm_res_body
    return pl.pallas_call(
        body,
        grid=(m // tm, n // tn),
        in_specs=in_specs,
        out_specs=pl.BlockSpec((tm, tn), lambda i, j: (i, j)),
        out_shape=jax.ShapeDtypeStruct((m, n), BF16 if gate else F32),
        compiler_params=_cparams(2, VMEM_BIG),
        name=name,
    )(*args)


def _mm_deint_body(a_ref, b_ref, o_ref, s_ref, *, dil):
    r = jnp.dot(a_ref[...], b_ref[...], preferred_element_type=F32)
    for h in range(NH):
        s_ref[h] = r[:, h * HD:(h + 1) * HD]
    n = SEQ // dil
    for h in range(NH):
        for res in range(dil):
            o_ref[h, pl.ds(res * n, n), :] = s_ref[h, pl.ds(res, n, stride=dil), :]


def _w_in_source_block(j, where=jnp.where):
    return where(j < 3, 3 * j, where(j < 26, j + 6, where(j < 29, 3 * (j - 26) + 1, 3 * (j - 29) + 2)))


assert tuple(_w_in_source_block(np.arange(ZBLK), np.where)) == W_IN_ORDER


def _in_proj_convert_body(a_ref, b_ref, z_ref, gate_ref, w_ref):
    j = pl.program_id(0)
    w = b_ref[...].astype(BF16)
    w_ref[...] = w
    r = jnp.dot(a_ref[...], w, preferred_element_type=F32)
    z_ref[...] = r

    @pl.when((j >= N_PLAIN) & (j < N_PLAIN + N_GATE))
    def _():
        gate_ref[...] = _sigmoid(r).astype(BF16)


def _in_proj_convert(a, w_in, *, layer):
    m, k = a.shape
    return pl.pallas_call(
        _in_proj_convert_body,
        grid=(ZBLK,),
        in_specs=[pl.BlockSpec((m, k), lambda j: (0, 0)),
                  pl.BlockSpec((None, k, BW), lambda j: (layer, 0, _w_in_source_block(j)))],
        out_specs=[pl.BlockSpec((m, BW), lambda j: (0, j)),
                   pl.BlockSpec((m, BW), lambda j: (0, jnp.clip(j - N_PLAIN, 0, N_GATE - 1))),
                   pl.BlockSpec((k, BW), lambda j: (0, j))],
        out_shape=[jax.ShapeDtypeStruct((m, IN_WIDTH), F32),
                   jax.ShapeDtypeStruct((m, GATE_W), BF16),
                   jax.ShapeDtypeStruct((k, IN_WIDTH), BF16)],
        compiler_params=_cparams(1, VMEM_BIG),
        name="in_proj_convert",
    )(a, w_in)


def _matmul_deint(a, b, *, col0, dil, name):
    m, k = a.shape
    return pl.pallas_call(
        functools.partial(_mm_deint_body, dil=dil),
        grid=(BATCH, 3),
        in_specs=[pl.BlockSpec((SEQ, k), lambda b, j: (b, 0)),
                  pl.BlockSpec((k, BW), lambda b, j: (0, col0 + j))],
        out_specs=pl.BlockSpec((None, NH, SEQ, HD), lambda b, j: (j, 0, b, 0)),
        out_shape=jax.ShapeDtypeStruct((3, NH, m, HD), F32),
        scratch_shapes=[pltpu.VMEM((NH, SEQ, HD), F32)],
        compiler_params=_cparams(2, VMEM_BIG),
        name=name,
    )(a, b)


def _branch_body(oa_ref, ob_ref, oc_ref, od_ref, gate_ref, wb_ref, o_ref):
    acc = None
    for n, br in enumerate((oa_ref, ob_ref, oc_ref, od_ref)):
        proj = jnp.dot(br[...], wb_ref[n], preferred_element_type=F32)
        t = gate_ref[:, n * D_MODEL:(n + 1) * D_MODEL].astype(F32) * proj
        acc = t if acc is None else acc + t
    o_ref[...] = acc.astype(BF16)


def _branch_merge(oa, ob, oc, od, gates, wb, *, layer, tm):
    m = oa.shape[0]
    br_spec = pl.BlockSpec((tm, BW), lambda i: (i, 0))
    return pl.pallas_call(
        _branch_body,
        grid=(m // tm,),
        in_specs=[br_spec] * 4 + [pl.BlockSpec((tm, GATE_W), lambda i: (i, 0)),
                                  pl.BlockSpec((None, 4, BW, D_MODEL), lambda i: (layer, 0, 0, 0))],
        out_specs=pl.BlockSpec((tm, D_MODEL), lambda i: (i, 0)),
        out_shape=jax.ShapeDtypeStruct((m, D_MODEL), BF16),
        compiler_params=_cparams(1, VMEM_BIG),
        name="branch_merge",
    )(oa, ob, oc, od, gates, wb)


def _t5_buckets(dist):
    max_exact = REL_BUCKETS // 2
    d = np.maximum(dist, 1).astype(np.float32)
    large = max_exact + (np.log(d / max_exact) / np.log(REL_MAX_DISTANCE / max_exact)
                         * (REL_BUCKETS - max_exact)).astype(np.int32)
    large = np.minimum(large, REL_BUCKETS - 1)
    return np.where(dist < max_exact, dist, large).astype(np.int32)


def _bucket_of_step(dil):
    return _t5_buckets(dil * np.arange(NKEY))


def _prompt_bucket_matrix(dil):
    iq = np.arange(CH)[:, None]
    col = np.arange(2 * CH)[None, :]
    j = iq + CH - col
    valid = (j >= 0) & (j < NKEY)
    return np.where(valid, _bucket_of_step(dil)[np.clip(j, 0, NKEY - 1)], -1).astype(np.int32)


SAMPLE_CACHE_ROWS = (8 * ATT_WINDOWS[0], 8 * ATT_WINDOWS[1], 8 * DEC_SEQ * CH)


def _sample_bucket_tables():
    h = np.repeat(np.arange(NH), SROWS)[:, None]
    t = np.tile(np.arange(SROWS), NH)[:, None]
    lane = np.arange(CH)[None, :]
    tables = []
    for gi, dil in enumerate(ATT_DILATIONS):
        bos = _bucket_of_step(dil)
        flat = np.arange(SAMPLE_CACHE_ROWS[gi])[None, :]
        head, is_k = flat % NH, (flat // NH) % 2 == 0
        if gi < 2:
            w = flat // 8
            steps = ATT_WINDOWS[gi] + t - w
            valid = (steps % dil == 0) & (steps // dil < NKEY)
            j = steps // dil
        else:
            res, i = (flat // 8) % DEC_SEQ, flat // (8 * DEC_SEQ)
            valid = res == t
            j = np.broadcast_to(CH - i, valid.shape)
        valid = valid & is_k & (head == h) & (t < DEC_SEQ)
        cache_tbl = np.where(valid, bos[np.clip(j, 0, NKEY - 1)], -1)
        jn = t - lane
        validn = (t < DEC_SEQ) & (lane < DEC_SEQ) & (jn >= 0) & ((dil == 1) | (jn == 0))
        new_tbl = np.where(validn, bos[np.clip(jn, 0, NKEY - 1)], -1)
        tables.append((cache_tbl.astype(np.int32), new_tbl.astype(np.int32)))
    return tables


def _bias_from_buckets(bk, rb_ref, col):
    out = jnp.full(bk.shape, NEG, F32)
    for b in range(REL_BUCKETS):
        out = jnp.where(bk == b, rb_ref[b, col], out)
    return out


ATT_SUB = 4


def _attn_prompt_body(rb_ref, bk_ref, q_ref, k_ref, v_ref, *refs, gi, headed):
    dil = ATT_DILATIONS[gi]
    has_prev = gi == 0
    if has_prev:
        kp_ref, vp_ref = refs[:2]
        refs = refs[2:]
    o_ref, lse_ref, kvo_ref, bias_ref = refs[-4:]
    if len(refs) == 5:
        kvo_ref[0] = refs[0][...]
        kvo_ref = kvo_ref.at[1]
    step = pl.program_id(1)

    @pl.when((pl.program_id(0) == 0) & (step == 0))
    def _():
        bk = bk_ref[...]
        for h in range(NH):
            bias_ref[h] = _bias_from_buckets(bk, rb_ref, gi * NH + h)

    def tile(ref, h, t):
        rs = slice(t * CH, (t + 1) * CH)
        return ref[h, rs, :] if headed else ref[rs, h * HD:(h + 1) * HD]

    scale = HD ** -0.5
    lane_head = lax.broadcasted_iota(jnp.int32, (CH, NH * LSE_W), 1) // LSE_W
    if has_prev:
        col = lax.broadcasted_iota(jnp.int32, (CH, 2 * CH), 1)
        no_prev = (step == 0) & (col < CH)
    for t in range(ATT_SUB):
        if gi == 0:
            dst = slice(t * CH, (t + 1) * CH)
        elif gi == 1:
            dst = pl.ds(t * CH * dil + step, CH, stride=dil)
        else:
            dst = pl.ds(step * ATT_SUB + t, CH, stride=dil)
        keep = gi == 2 or t == ATT_SUB - 1
        kv_row0 = (t if gi == 2 else 0) * 2 * NH
        lse_tile = None
        for h in range(NH):
            kc = tile(k_ref, h, t)
            vc = tile(v_ref, h, t)
            if keep:
                kvo_ref[:, kv_row0 + h, :] = kc
                kvo_ref[:, kv_row0 + NH + h, :] = vc
            qh = tile(q_ref, h, t).astype(BF16)
            if gi < 2 and (t > 0 or has_prev):
                if t > 0:
                    kp, vp, bias = tile(k_ref, h, t - 1), tile(v_ref, h, t - 1), bias_ref[h]
                else:
                    kp, vp = kp_ref[:, h * HD:(h + 1) * HD], vp_ref[:, h * HD:(h + 1) * HD]
                    bias = jnp.where(no_prev, NEG, bias_ref[h])
                kh = jnp.concatenate([kp, kc], axis=0).astype(BF16)
                vh = jnp.concatenate([vp, vc], axis=0).astype(BF16)
            else:
                kh = kc.astype(BF16)
                vh = vc.astype(BF16)
                bias = bias_ref[h][:, CH:]
            s = lax.dot_general(qh, kh, (((1,), (1,)), ((), ())), preferred_element_type=F32) * scale + bias
            m = jnp.max(s, axis=-1, keepdims=True)
            p = jnp.exp(s - m)
            den = jnp.sum(p, axis=-1, keepdims=True)
            acc = jnp.dot(p.astype(BF16), vh, preferred_element_type=F32)
            o_ref[h, dst, :] = acc / den
            lse = m + jnp.log(den)
            lse_tile = lse if h == 0 else jnp.where(lane_head == h, lse, lse_tile)
        lse_ref[dst, :] = lse_tile


def _attn_prompt(zq, rel_bias, gi, kv_prev):
    dil = ATT_DILATIONS[gi]
    rows = BATCH * SEQ
    nsteps = SEQ // (ATT_SUB * CH)
    assert SEQ // dil // CH in (1, ATT_SUB, ATT_SUB * nsteps)
    headed = gi > 0
    bk = jnp.asarray(_prompt_bucket_matrix(dil))
    span = ATT_SUB * CH

    def zspec(c):
        if headed:
            return pl.BlockSpec((None, NH, span, HD), lambda b, s: (c, 0, b * nsteps + s, 0))
        return pl.BlockSpec((span, BW), lambda b, s: (b * nsteps + s, COL_QKV[0][c]))

    in_specs = [pl.BlockSpec(memory_space=pltpu.SMEM),
                pl.BlockSpec((CH, 2 * CH), lambda b, s: (0, 0)),
                zspec(0), zspec(1), zspec(2)]
    args = [rel_bias, bk, zq, zq, zq]
    if gi == 0:
        above = lambda c: pl.BlockSpec(
            (CH, BW), lambda b, s: (jnp.maximum((b * nsteps + s) * ATT_SUB - 1, 0), COL_QKV[0][c]))
        in_specs += [above(1), above(2)]
        args += [zq, zq]
        out_rows, out_index = span, (lambda b, s: b * nsteps + s)
    else:
        out_rows, out_index = SEQ, (lambda b, s: b)
    n_res = ATT_SUB if gi == 2 else 1
    kv_blk = (CH, n_res * 2 * NH, HD)
    kv_index = lambda b, s: (b, 0, s if gi else 0, 0)
    kv_shape = (BATCH, CH, dil * 2 * NH, HD)
    if kv_prev is None:
        kv_spec = pl.BlockSpec((None, *kv_blk), kv_index)
    else:
        in_specs.append(pl.BlockSpec((None, *kv_blk), kv_index))
        args.append(kv_prev)
        kv_spec = pl.BlockSpec((2, None, *kv_blk), lambda b, s: (0, *kv_index(b, s)))
        kv_shape = (2, *kv_shape)
    return pl.pallas_call(
        functools.partial(_attn_prompt_body, gi=gi, headed=headed),
        grid=(BATCH, nsteps),
        in_specs=in_specs,
        out_specs=[pl.BlockSpec((NH, out_rows, HD), lambda b, s: (0, out_index(b, s), 0)),
                   pl.BlockSpec((out_rows, NH * LSE_W), lambda b, s: (out_index(b, s), 0)),
                   kv_spec],
        out_shape=[jax.ShapeDtypeStruct((NH, rows, HD), F32),
                   jax.ShapeDtypeStruct((rows, NH * LSE_W), F32),
                   jax.ShapeDtypeStruct(kv_shape, F32)],
        scratch_shapes=[pltpu.VMEM((NH, CH, 2 * CH), F32)],
        compiler_params=_cparams(2),
        name=f"attn_prompt_g{gi}",
    )(*args)


def _merge_body(o0, l0, o1, l1, o2, l2, ob_ref, oc_ref, od_ref, gate_ref, wb_ref, wo_ref, x_ref, out_ref):
    heads = []
    for h in range(NH):
        a0, a1, a2 = (l[:, h * LSE_W:h * LSE_W + 1] for l in (l0, l1, l2))
        m = jnp.maximum(jnp.maximum(a0, a1), a2)
        w0, w1, w2 = jnp.exp(a0 - m), jnp.exp(a1 - m), jnp.exp(a2 - m)
        heads.append(((w0 * o0[h] + w1 * o1[h] + w2 * o2[h]) / (w0 + w1 + w2)).astype(BF16))
    oa = jnp.concatenate(heads, axis=1)
    acc = None
    for n, br in enumerate((oa, ob_ref[...], oc_ref[...], od_ref[...])):
        proj = jnp.dot(br, wb_ref[n], preferred_element_type=F32)
        t = gate_ref[:, n * D_MODEL:(n + 1) * D_MODEL].astype(F32) * proj
        acc = t if acc is None else acc + t
    out_ref[...] = x_ref[...] + jnp.dot(acc.astype(BF16), wo_ref[...], preferred_element_type=F32)


def _merge(parts, ob, oc, od, gates, wb, wo, x, *, layer, tm=TM_MERGE):
    m = x.shape[0]
    once = dict(pipeline_mode=pl.Buffered(1))
    ospec = pl.BlockSpec((NH, tm, HD), lambda i: (0, i, 0))
    lspec = pl.BlockSpec((tm, NH * LSE_W), lambda i: (i, 0))
    br_spec = pl.BlockSpec((tm, BW), lambda i: (i, 0))
    row_spec = pl.BlockSpec((tm, D_MODEL), lambda i: (i, 0))
    return pl.pallas_call(
        _merge_body,
        grid=(m // tm,),
        in_specs=[ospec, lspec] * 3 + [br_spec] * 3 + [
            pl.BlockSpec((tm, GATE_W), lambda i: (i, 0)),
            pl.BlockSpec((None, 4, BW, D_MODEL), lambda i: (layer, 0, 0, 0), **once),
            pl.BlockSpec((None, D_MODEL, D_MODEL), lambda i: (layer, 0, 0), **once),
            row_spec],
        out_specs=row_spec,
        out_shape=jax.ShapeDtypeStruct((m, D_MODEL), F32),
        compiler_params=_cparams(1, VMEM_BIG),
        name="merge",
    )(*parts, ob, oc, od, gates, wb, wo, x)


def _attn_sample_body(rb_ref, bc0_ref, bc1_ref, bc2_ref, bn_ref, z_ref, c0_ref, c1_ref, c2_ref,
                      oa_ref, kv0_ref, kv1_ref, kv2_ref, b0_ref, b1_ref, b2_ref, bnew_ref):
    bucket_refs = (bc0_ref, bc1_ref, bc2_ref)
    bias_refs = (b0_ref, b1_ref, b2_ref)

    @pl.when(pl.program_id(0) == 0)
    def _():
        for gi in range(3):
            for h in range(NH):
                rs = slice(h * SROWS, (h + 1) * SROWS)
                bias_refs[gi][rs, :] = _bias_from_buckets(bucket_refs[gi][rs, :], rb_ref, gi * NH + h)
                bnew_ref[gi, rs, :] = _bias_from_buckets(bn_ref[gi, rs, :], rb_ref, gi * NH + h)

    scale = HD ** -0.5
    rows = NH * SROWS
    head_of_row = lax.broadcasted_iota(jnp.int32, (rows, BW), 0) // SROWS
    head_of_lane = lax.broadcasted_iota(jnp.int32, (rows, BW), 1) // HD
    head_mask = head_of_row == head_of_lane
    caches = (c0_ref, c1_ref, c2_ref)
    kv_refs = (kv0_ref, kv1_ref, kv2_ref)
    stats = []
    for gi in range(3):
        cq, ck, cv = COL_QKV[gi]
        q = z_ref[:, cq * BW:(cq + 1) * BW]
        kn = z_ref[:, ck * BW:(ck + 1) * BW]
        vn = z_ref[:, cv * BW:(cv + 1) * BW]
        kv_refs[gi][:, :BW] = kn
        kv_refs[gi][:, BW:] = vn
        qm = jnp.concatenate([q[:, h * HD:(h + 1) * HD] for h in range(NH)], axis=0).astype(BF16)
        kf = caches[gi][...].reshape(SAMPLE_CACHE_ROWS[gi], HD).astype(BF16)
        s = lax.dot_general(qm, kf, (((1,), (1,)), ((), ())), preferred_element_type=F32) * scale + bias_refs[gi][...]
        qrows = jnp.where(head_mask, jnp.concatenate([q] * NH, axis=0), 0.0)
        bias_n = bnew_ref[gi]
        s_new = []
        for tp in range(DEC_SEQ):
            dotp = jnp.sum(qrows * kn[tp:tp + 1, :], axis=-1, keepdims=True)
            s_new.append(dotp * scale + bias_n[:, tp:tp + 1])
        m = jnp.max(s, axis=-1, keepdims=True)
        for sn in s_new:
            m = jnp.maximum(m, sn)
        p = jnp.exp(s - m)
        den = jnp.sum(p, axis=-1, keepdims=True)
        pv = pltpu.roll(p, NH, axis=1).astype(BF16)
        acc = jnp.dot(pv, kf, preferred_element_type=F32)
        acc_n = jnp.zeros((rows, BW), F32)
        for tp, sn in enumerate(s_new):
            pn = jnp.exp(sn - m)
            den = den + pn
            acc_n = acc_n + pn * vn[tp:tp + 1, :]
        acc = acc + jnp.concatenate([acc_n[h * SROWS:(h + 1) * SROWS, h * HD:(h + 1) * HD] for h in range(NH)], axis=0)
        stats.append((m, den, acc))
    mm = jnp.maximum(jnp.maximum(stats[0][0], stats[1][0]), stats[2][0])
    den = jnp.zeros((rows, 1), F32)
    acc = jnp.zeros((rows, HD), F32)
    for m, d, a in stats:
        w = jnp.exp(m - mm)
        den = den + w * d
        acc = acc + w * a
    o = acc / den
    for h in range(NH):
        oa_ref[:, h * HD:(h + 1) * HD] = o[h * SROWS:(h + 1) * SROWS, :].astype(BF16)


def _attn_sample(z, caches, rel_bias, layer):
    tables = _sample_bucket_tables()
    rows = DEC_BATCH * SROWS
    qrows = NH * SROWS
    n0, n1, n2 = SAMPLE_CACHE_ROWS
    c0 = caches[0].reshape(DEPTH, DEC_BATCH, n0, HD)
    c1 = caches[1].reshape(DEPTH, DEC_BATCH, n1, HD)
    c2 = caches[2].reshape(DEPTH, DEC_BATCH, CH, 16 * 8, HD)
    new_tbl = jnp.asarray(np.stack([t[1] for t in tables]))
    kv_spec = pl.BlockSpec((SROWS, 2 * BW), lambda b: (b, 0))
    kv_shape = jax.ShapeDtypeStruct((rows, 2 * BW), F32)
    const2 = lambda b: (0, 0)
    return pl.pallas_call(
        _attn_sample_body,
        grid=(DEC_BATCH,),
        in_specs=[pl.BlockSpec(memory_space=pltpu.SMEM),
                  pl.BlockSpec((qrows, n0), const2), pl.BlockSpec((qrows, n1), const2),
                  pl.BlockSpec((qrows, n2), const2),
                  pl.BlockSpec((3, qrows, CH), lambda b: (0, 0, 0)),
                  pl.BlockSpec((SROWS, IN_WIDTH), lambda b: (b, 0)),
                  pl.BlockSpec((None, None, n0, HD), lambda b: (layer, b, 0, 0)),
                  pl.BlockSpec((None, None, n1, HD), lambda b: (layer, b, 0, 0)),
                  pl.BlockSpec((None, None, CH, 8 * DEC_SEQ, HD), lambda b: (layer, b, 0, 0, 0))],
        out_specs=[pl.BlockSpec((SROWS, BW), lambda b: (b, 0)), kv_spec, kv_spec, kv_spec],
        out_shape=[jax.ShapeDtypeStruct((rows, BW), BF16), kv_shape, kv_shape, kv_shape],
        scratch_shapes=[pltpu.VMEM((qrows, n0), F32), pltpu.VMEM((qrows, n1), F32), pltpu.VMEM((qrows, n2), F32),
                        pltpu.VMEM((3, qrows, CH), F32)],
        compiler_params=_cparams(1),
        name="attn_sample",
    )(rel_bias, jnp.asarray(tables[0][0]), jnp.asarray(tables[1][0]), jnp.asarray(tables[2][0]), new_tbl,
      z, c0, c1, c2)


def _gmlp_body(bu_ref, bv_ref, g_ref, ws_ref, bs_ref, o_ref, *, rows):
    tril = lax.broadcasted_iota(jnp.int32, (CH, CH), 0) >= lax.broadcasted_iota(jnp.int32, (CH, CH), 1)
    w = [jnp.where(tril, ws_ref[g], 0.0).astype(BF16) for g in range(NH)]
    for c in range(rows // CH):
        rs = slice(c * CH, (c + 1) * CH)
        u = jax.nn.gelu(bu_ref[rs, :])
        vn = _rms(jax.nn.gelu(bv_ref[rs, :]), g_ref[...])
        for g in range(NH):
            sl = slice(g * HD, (g + 1) * HD)
            mixed = jnp.dot(w[g], vn[:, sl].astype(BF16), preferred_element_type=F32) + bs_ref[:, sl]
            o_ref[rs, sl] = (u[:, sl] * mixed).astype(BF16)


def _gmlp(z, g_gmlp, w_spatial, b_spatial, *, rows):
    m = z.shape[0]
    bs_full = jnp.repeat(b_spatial.T, HD, axis=1)
    return pl.pallas_call(
        functools.partial(_gmlp_body, rows=rows),
        grid=(m // rows,),
        in_specs=[pl.BlockSpec((rows, BW), lambda i: (i, COL_BU)),
                  pl.BlockSpec((rows, BW), lambda i: (i, COL_BV)),
                  pl.BlockSpec((1, BW), lambda i: (0, 0)),
                  pl.BlockSpec((NH, CH, CH), lambda i: (0, 0, 0)),
                  pl.BlockSpec((CH, BW), lambda i: (0, 0))],
        out_specs=pl.BlockSpec((rows, BW), lambda i: (i, 0)),
        out_shape=jax.ShapeDtypeStruct((m, BW), BF16),
        compiler_params=_cparams(1),
        name="gmlp",
    )(z, z, g_gmlp.reshape(1, -1), w_spatial, bs_full)


POOL_PAD = 16


def _pool_body(x_ref, prev_ref, wp_ref, sc_ref, o_ref, st_ref, ext_ref, *, rows):
    ib = pl.program_id(1)
    prev = jnp.where(ib == 0, 0.0, prev_ref[...])
    x = x_ref[...]
    ext_ref[0:POOL_PAD, :] = prev
    ext_ref[POOL_PAD:POOL_PAD + rows, :] = x
    st_ref[...] = ext_ref[pl.ds(rows + 1, POOL_STATE), :]
    ext = ext_ref[...]
    pos = ib * rows + lax.broadcasted_iota(jnp.int32, (rows, 1), 0)
    for gi, win in enumerate(POOL_WINDOWS):
        sl = slice(gi * HD, (gi + 1) * HD)
        s = ext[:, sl]
        k = 1
        while k < win:
            s = s + pltpu.roll(s, k, axis=0)
            k *= 2
        cnt = jnp.minimum(pos + 1, win).astype(F32)
        diff = s[POOL_PAD:] / cnt - x[:, sl]
        y = jnp.dot(diff.astype(BF16), wp_ref[gi].astype(BF16), preferred_element_type=F32)
        o_ref[:, sl] = (y * sc_ref[:, sl]).astype(BF16)


def _pool(z, w_pool, pool_scale, *, rows):
    m = z.shape[0]
    nblk = SEQ // rows
    per_pad = rows // POOL_PAD
    return pl.pallas_call(
        functools.partial(_pool_body, rows=rows),
        grid=(BATCH, nblk),
        in_specs=[pl.BlockSpec((rows, BW), lambda b, i: (b * nblk + i, COL_CIN)),
                  pl.BlockSpec((POOL_PAD, BW),
                               lambda b, i: (jnp.maximum((b * nblk + i) * per_pad - 1, 0), COL_CIN)),
                  pl.BlockSpec((NH, HD, HD), lambda b, i: (0, 0, 0)),
                  pl.BlockSpec((1, BW), lambda b, i: (0, 0))],
        out_specs=[pl.BlockSpec((rows, BW), lambda b, i: (b * nblk + i, 0)),
                   pl.BlockSpec((None, POOL_STATE, BW), lambda b, i: (b, 0, 0))],
        out_shape=[jax.ShapeDtypeStruct((m, BW), BF16),
                   jax.ShapeDtypeStruct((BATCH, POOL_STATE, BW), F32)],
        scratch_shapes=[pltpu.VMEM((POOL_PAD + rows, BW), F32)],
        compiler_params=_cparams(2),
        name="pool",
    )(z, z, w_pool, pool_scale.reshape(1, -1))


def _gmlp_sample_body(bu_ref, bv_ref, g_ref, wk_ref, bs_ref, o_ref, vn_ref):
    u = jax.nn.gelu(bu_ref[...])
    vn = _rms(jax.nn.gelu(bv_ref[...]), g_ref[...])
    vn_ref[...] = vn
    mixed = jnp.tile(bs_ref[...], (DEC_BATCH, 1))
    for k in range(DEC_SEQ):
        shifted = vn if k == 0 else pltpu.roll(vn, k, axis=0)
        mixed = mixed + jnp.tile(wk_ref[k], (DEC_BATCH, 1)) * shifted
    o_ref[...] = (u * mixed).astype(BF16)


def _gmlp_sample(z, g_gmlp, w_spatial, b_spatial):
    m = z.shape[0]
    t = np.arange(SROWS)
    live = t < DEC_SEQ
    pick = np.zeros((DEC_SEQ, SROWS, SROWS), np.float32)
    for k in range(DEC_SEQ):
        pick[k, t[live & (t >= k)], t[live & (t >= k)] - k] = 1.0
    diag = jnp.sum(w_spatial[None, :, :SROWS, :SROWS] * pick[:, None], axis=-1)
    wk = jnp.repeat(diag.transpose(0, 2, 1), HD, axis=2)
    bs = jnp.repeat((b_spatial[:, :SROWS] * jnp.asarray(live, F32)).T, HD, axis=1)
    return pl.pallas_call(
        _gmlp_sample_body,
        grid=(1,),
        in_specs=[pl.BlockSpec((m, BW), lambda i: (0, COL_BU)),
                  pl.BlockSpec((m, BW), lambda i: (0, COL_BV)),
                  pl.BlockSpec((1, BW), lambda i: (0, 0)),
                  pl.BlockSpec((DEC_SEQ, SROWS, BW), lambda i: (0, 0, 0)),
                  pl.BlockSpec((SROWS, BW), lambda i: (0, 0))],
        out_specs=[pl.BlockSpec((m, BW), lambda i: (0, 0))] * 2,
        out_shape=[jax.ShapeDtypeStruct((m, BW), BF16), jax.ShapeDtypeStruct((m, BW), F32)],
        compiler_params=_cparams(1),
        name="gmlp_sample",
    )(z, z, g_gmlp.reshape(1, -1), wk, bs)


def _pool_sample_body(x_ref, prev_ref, wp_ref, sc_ref, o_ref, st_ref, ext_ref, *, start):
    per = POOL_PAD + SROWS
    ext_ref[:, 0:POOL_PAD, :] = prev_ref[...]
    ext_ref[:, POOL_PAD:per, :] = x_ref[...]
    st_ref[...] = ext_ref[:, pl.ds(DEC_SEQ + 1, POOL_STATE), :]
    ext = ext_ref[...].reshape(DEC_BATCH * per, BW)
    pos = start + (lax.broadcasted_iota(jnp.int32, (DEC_BATCH * per, 1), 0) % per - POOL_PAD)
    for gi, win in enumerate(POOL_WINDOWS):
        sl = slice(gi * HD, (gi + 1) * HD)
        x = ext[:, sl]
        s = x
        k = 1
        while k < win:
            s = s + pltpu.roll(s, k, axis=0)
            k *= 2
        cnt = jnp.minimum(jnp.maximum(pos, 0) + 1, win).astype(F32)
        y = jnp.dot((s / cnt - x).astype(BF16), wp_ref[gi].astype(BF16), preferred_element_type=F32)
        o_ref[:, :, sl] = (y * sc_ref[:, sl]).reshape(DEC_BATCH, per, HD)[:, POOL_PAD:, :].astype(BF16)


def _pool_sample(z, prev, w_pool, pool_scale, *, start, layer):
    m, zw = z.shape
    z3 = z.reshape(DEC_BATCH, SROWS, zw)
    o, st = pl.pallas_call(
        functools.partial(_pool_sample_body, start=start),
        grid=(1,),
        in_specs=[pl.BlockSpec((DEC_BATCH, SROWS, BW), lambda i: (0, 0, COL_CIN)),
                  pl.BlockSpec((None, DEC_BATCH, POOL_PAD, BW), lambda i: (layer, 0, 0, 0)),
                  pl.BlockSpec((NH, HD, HD), lambda i: (0, 0, 0)),
                  pl.BlockSpec((1, BW), lambda i: (0, 0))],
        out_specs=[pl.BlockSpec((DEC_BATCH, SROWS, BW), lambda i: (0, 0, 0)),
                   pl.BlockSpec((DEC_BATCH, POOL_STATE, BW), lambda i: (0, 0, 0))],
        out_shape=[jax.ShapeDtypeStruct((DEC_BATCH, SROWS, BW), BF16),
                   jax.ShapeDtypeStruct((DEC_BATCH, POOL_STATE, BW), F32)],
        scratch_shapes=[pltpu.VMEM((DEC_BATCH, POOL_PAD + SROWS, BW), F32)],
        compiler_params=_cparams(1),
        name="pool_sample",
    )(z3, prev, w_pool, pool_scale.reshape(1, -1))
    return o.reshape(m, BW), st


def _ret_tables(c_eff, positions, size):
    lg = np.log1p(-np.power(2.0, -5.0 - np.arange(NH, dtype=np.float64)))
    i = np.arange(size, dtype=np.float64)
    live = (i < c_eff)
    diff = i[:, None] - i[None, :]
    inner = np.where((diff >= 0) & live[:, None] & live[None, :], np.exp(np.maximum(diff, 0.0)[None] * lg[:, None, None]), 0.0)
    qd = np.where(live[None, :], np.exp((i + 1.0)[None, :] * lg[:, None]), 0.0)
    kd = np.where(live[None, :], np.exp((c_eff - 1.0 - i)[None, :] * lg[:, None]), 0.0)
    chunk = tuple(float(v) for v in np.exp(c_eff * lg))
    qd_full = np.repeat(qd.T, HD, axis=1)
    kd_full = np.repeat(kd.T, HD, axis=1) * (HD ** -0.5)
    half = HD // 2
    inv = ROPE_BASE ** (-np.arange(half, dtype=np.float64) / half)
    ang = np.asarray(positions, np.float64)[:, None] * inv[None, :]
    cosf = np.concatenate([np.cos(ang), np.cos(ang)], axis=1)
    sinf = np.concatenate([-np.sin(ang), np.sin(ang)], axis=1)
    to32 = lambda a: jnp.asarray(a.astype(np.float32))
    return to32(inner), to32(qd_full), to32(kd_full), chunk, to32(cosf), to32(sinf)


def _ret_body(q_ref, k_ref, v_ref, g_ref, cos_ref, sin_ref, inner_ref, qd_ref, kd_ref, gr_ref, *refs,
              nseq, rows, chunk_decay, has_state):
    refs = list(refs)
    s0_ref = refs.pop(0) if has_state else None
    o_ref, sn_ref, s_ref = refs[:3]
    pad_ref = refs[3] if rows < CH else None
    ic = pl.program_id(1)

    @pl.when(ic == 0)
    def _():
        if has_state:
            s_ref[...] = s0_ref[...]
        else:
            s_ref[...] = jnp.zeros_like(s_ref)

    def chunk(ref, b, k):
        if rows >= CH:
            return ref[b]
        pad_ref[k] = jnp.zeros((CH, BW), F32)
        pad_ref[k, 0:rows, :] = ref[b]
        return pad_ref[k]

    cosf = cos_ref[...]
    sinf = sin_ref[...]
    for b in range(nseq):
        q = chunk(q_ref, b, 0)
        k = chunk(k_ref, b, 1)
        v = chunk(v_ref, b, 2)
        gate = chunk(g_ref, b, 3)
        for h in range(NH):
            sl = slice(h * HD, (h + 1) * HD)
            qh = q[:, sl]
            kh = k[:, sl]
            rq = qh * cosf + pltpu.roll(qh, HD // 2, axis=1) * sinf
            rk = kh * cosf + pltpu.roll(kh, HD // 2, axis=1) * sinf
            vb = v[:, sl].astype(BF16)
            rqb = rq.astype(BF16)
            state = s_ref[b, h]
            att = lax.dot_general(rqb, (rk * (HD ** -0.5)).astype(BF16), (((1,), (1,)), ((), ())),
                                  preferred_element_type=F32) * inner_ref[h]
            o = (jnp.dot(att.astype(BF16), vb, preferred_element_type=F32)
                 + jnp.dot(rqb, state.astype(BF16), preferred_element_type=F32) * qd_ref[:, sl])
            kdec = (rk * kd_ref[:, sl]).astype(BF16)
            new_state = state * chunk_decay[h] + lax.dot_general(
                kdec, vb, (((0,), (0,)), ((), ())), preferred_element_type=F32)
            s_ref[b, h] = new_state
            sn_ref[b, h] = new_state
            o = o * lax.rsqrt(jnp.mean(o * o, axis=-1, keepdims=True) + EPS)
            gt = gate[:, sl]
            o_ref[b, :, sl] = (o * gr_ref[:, sl] * (gt * jax.nn.sigmoid(gt)))[:rows].astype(BF16)


def _retention(z, g_ret, state, *, nseq, per_step, rows, c_eff, positions, layer=None):
    m, zw = z.shape
    seq_rows = m // nseq
    nchunk = seq_rows // rows
    size = max(rows, CH)
    inner, qd, kd, chunk_decay, cosf, sinf = _ret_tables(c_eff, positions, size)
    has_state = state is not None
    z3 = z.reshape(nseq, seq_rows, zw)

    def zspec(col):
        return pl.BlockSpec((per_step, rows, BW), lambda g, i: (g, i, col))

    const2 = lambda g, i: (0, 0)
    in_specs = [zspec(COL_DQ), zspec(COL_DK), zspec(COL_DV), zspec(COL_DG),
                pl.BlockSpec((size, HD), lambda g, i: (i, 0)), pl.BlockSpec((size, HD), lambda g, i: (i, 0)),
                pl.BlockSpec((NH, size, size), lambda g, i: (0, 0, 0)),
                pl.BlockSpec((size, BW), const2), pl.BlockSpec((size, BW), const2), pl.BlockSpec((1, BW), const2)]
    args = [z3, z3, z3, z3, cosf, sinf, inner, qd, kd, g_ret.reshape(1, -1)]
    if has_state:
        in_specs.append(pl.BlockSpec((None, per_step, NH, HD, HD), lambda g, i: (layer, g, 0, 0, 0)))
        args.append(state)
    scratch = [pltpu.VMEM((per_step, NH, HD, HD), F32)]
    if rows < CH:
        scratch.append(pltpu.VMEM((4, CH, BW), F32))
    o, sn = pl.pallas_call(
        functools.partial(_ret_body, nseq=per_step, rows=rows, chunk_decay=chunk_decay, has_state=has_state),
        grid=(nseq // per_step, nchunk),
        in_specs=in_specs,
        out_specs=[pl.BlockSpec((per_step, rows, BW), lambda g, i: (g, i, 0)),
                   pl.BlockSpec((per_step, NH, HD, HD), lambda g, i: (g, 0, 0, 0))],
        out_shape=[jax.ShapeDtypeStruct((nseq, seq_rows, BW), BF16),
                   jax.ShapeDtypeStruct((nseq, NH, HD, HD), F32)],
        scratch_shapes=scratch,
        compiler_params=_cparams(2),
        name="retention",
    )(*args)
    return o.reshape(m, BW), sn


def kernel(x_prompt, x_sample, cache_attn_kv_w128, cache_attn_kv_w512, cache_attn_kv_w2048, state_pool, state_ret, rel_bias, g_ffn1, w_ffn1_gate, w_ffn1_up, w_ffn1_down, g_mix, w_in, g_gmlp, w_spatial, b_spatial, w_pool, pool_scale, g_ret, w_branch, w_out, g_ffn2, w_ffn2_gate, w_ffn2_up, w_ffn2_down, g_final):
    caches = (cache_attn_kv_w128, cache_attn_kv_w512, cache_attn_kv_w2048)
    xp = x_prompt.reshape(BATCH * SEQ, D_MODEL)
    xs = jnp.pad(x_sample, ((0, 0), (0, SROWS - DEC_SEQ), (0, 0))).reshape(DEC_BATCH * SROWS, D_MODEL)
    pool_state = jnp.pad(state_pool, ((0, 0), (0, 0), (POOL_PAD - POOL_STATE, 0), (0, 0)))
    tm_p, tm_s = TM_FFN, DEC_BATCH * SROWS
    sample_pos = PAST_LEN + np.arange(CH)
    wb = w_branch.astype(BF16)
    wo = w_out.astype(BF16)
    gate_col0 = N_PLAIN * BW // TN_IN
    qkv1, qkv2 = COL_QKV[1][0], COL_QKV[2][0]

    kv_p = [None, None, None]
    kv_s = [[], [], []]
    pool_p, pool_s, ret_p, ret_s, gv_s = [], [], [], [], []
    yp = ys = None
    for l in range(DEPTH):
        last = l == DEPTH - 1

        xs, hs, *w1 = _ffn(xs, g_ffn1[l], (w_ffn1_gate, w_ffn1_up, w_ffn1_down), g_mix[l], layer=l,
                           emit_x=True, post_dtype=BF16, tm=tm_s)
        zs, gs, win = _in_proj_convert(hs, w_in, layer=l)
        oa, k0, k1, k2 = _attn_sample(zs, caches, rel_bias, l)
        for gi, kv in enumerate((k0, k1, k2)):
            kv_s[gi].append(kv.reshape(DEC_BATCH, SROWS, 2, NH, HD)[:, :DEC_SEQ])
        ob, vn = _gmlp_sample(zs, g_gmlp[l], w_spatial[l], b_spatial[l])
        gv_s.append(vn.reshape(DEC_BATCH, SROWS, BW)[:, :DEC_SEQ])
        oc, pn = _pool_sample(zs, pool_state, w_pool[l], pool_scale[l], start=PAST_LEN, layer=l)
        od, rn = _retention(zs, g_ret[l], state_ret, nseq=DEC_BATCH, per_step=4, rows=SROWS, c_eff=DEC_SEQ,
                            positions=sample_pos, layer=l)
        pool_s.append(pn)
        ret_s.append(rn)
        mix = _branch_merge(oa, ob, oc, od, gs, wb, layer=l, tm=tm_s)
        xs = _matmul(mix, wo, xs, layer=l, tm=tm_s, tn=TN_OUT_S, name="out_proj")
        if last:
            ys, *w2 = _ffn(xs, g_ffn2[l], (w_ffn2_gate, w_ffn2_up, w_ffn2_down), g_final, layer=l,
                           emit_x=False, post_dtype=F32, tm=tm_s)
        else:
            xs, *w2 = _ffn(xs, g_ffn2[l], (w_ffn2_gate, w_ffn2_up, w_ffn2_down), g_final, layer=l,
                           emit_x=True, post_dtype=None, tm=tm_s)

        xp, hp = _ffn(xp, g_ffn1[l], w1, g_mix[l], emit_x=True, post_dtype=BF16, tm=tm_p)
        zp = _matmul(hp, win, n=N_PLAIN * BW, tm=TM_IN, tn=TN_IN, name="in_proj")
        gp = _matmul(hp, win, col0=gate_col0, n=GATE_W, gate=True, tm=TM_IN, tn=TN_IN, name="in_proj_gate")
        zq = (zp,
              _matmul_deint(hp, win, col0=qkv1, dil=ATT_DILATIONS[1], name="in_proj_g1"),
              _matmul_deint(hp, win, col0=qkv2, dil=ATT_DILATIONS[2], name="in_proj_g2"))
        parts = []
        for gi in range(3):
            o, lse, kv_p[gi] = _attn_prompt(zq[gi], rel_bias, gi, kv_p[gi])
            parts += [o, lse]
        ob = _gmlp(zp, g_gmlp[l], w_spatial[l], b_spatial[l], rows=MIX_ROWS)
        oc, pn = _pool(zp, w_pool[l], pool_scale[l], rows=MIX_ROWS)
        od, rn = _retention(zp, g_ret[l], None, nseq=BATCH, per_step=BATCH, rows=CH, c_eff=CH,
                            positions=np.arange(SEQ))
        pool_p.append(pn)
        ret_p.append(rn)
        xp = _merge(parts, ob, oc, od, gp, wb, wo, xp, layer=l)
        if last:
            yp, = _ffn(xp, g_ffn2[l], w2, g_final, emit_x=False, post_dtype=F32, tm=tm_p)
        else:
            xp, = _ffn(xp, g_ffn2[l], w2, g_final, emit_x=True, post_dtype=None, tm=tm_p)

    y_prompt = yp.reshape(BATCH, SEQ, D_MODEL)
    y_sample = ys.reshape(DEC_BATCH, SROWS, D_MODEL)[:, :DEC_SEQ]
    kv_p = [kv.reshape(DEPTH, BATCH, ATT_WINDOWS[gi], 2, NH, HD) for gi, kv in enumerate(kv_p)]
    return (y_prompt, y_sample,
            kv_p[0], kv_p[1], kv_p[2],
            jnp.stack(kv_s[0]), jnp.stack(kv_s[1]), jnp.stack(kv_s[2]),
            jnp.stack(pool_p), jnp.stack(pool_s),
            jnp.stack(ret_p), jnp.stack(ret_s),
            jnp.stack(gv_s))
```

```python
import functools

import numpy as np
import jax
import jax.numpy as jnp
from jax import lax
from jax.experimental import pallas as pl
from jax.experimental.pallas import tpu as pltpu

F32 = jnp.float32
BF16 = jnp.bfloat16

D_MODEL = 2048
BATCH = 4
SEQ = 2048
DEPTH = 2
DEC_BATCH = 32
DEC_SEQ = 4
PAST_LEN = 8192
D_FF = 5632
IN_WIDTH = 16384
EPS = 1e-6
BW = 512
HD = 128
NH = 4
ATT_WINDOWS = (128, 512, 2048)
ATT_DILATIONS = (1, 4, 16)
NKEY = 129
REL_BUCKETS = 32
REL_MAX_DISTANCE = 2048
POOL_WINDOWS = (2, 4, 8, 16)
POOL_STATE = 15
ROPE_BASE = 10000.0
SROWS = 8
CH = 128
NEG = -1e30
LSE_W = 32

W_IN_ORDER = (0, 3, 6) + tuple(range(9, 32)) + (1, 4, 7) + (2, 5, 8)
COL_QKV = ((0, 1, 2), (26, 27, 28), (29, 30, 31))
COL_BU, COL_BV, COL_CIN = 3, 4, 5
COL_DQ, COL_DK, COL_DV, COL_DG = 6, 7, 8, 9
N_PLAIN = 10
N_GATE = 16
GATE_W = N_GATE * BW
ZBLK = IN_WIDTH // BW

VMEM_BIG = 56 * 1024 * 1024
VMEM_FFN = 62 * 1024 * 1024

TM_FFN = 1024
TM_IN, TN_IN = 2048, 1024
TM_MERGE = 256
TN_OUT_S = 512
MIX_ROWS = 8 * CH


def _cparams(n_axes, vmem=None):
    return pltpu.CompilerParams(dimension_semantics=("arbitrary",) * n_axes, vmem_limit_bytes=vmem)


def _rms(x, g):
    return x * lax.rsqrt(jnp.mean(x * x, axis=-1, keepdims=True) + EPS) * g


def _ffn_body(x_ref, gpre_ref, *refs, emit_x, emit_post, convert, nf, tf, row_split):
    refs = list(refs)
    if convert:
        wg_ref, wu_ref, wd_ref, gpost_ref = refs[:4]
        refs = refs[4:]
    else:
        wgu_ref, wd_ref, gpost_ref = refs[:3]
        refs = refs[3:]
    n_act = int(emit_x) + int(emit_post)
    outs = refs[:n_act]
    scratch = refs[n_act + (2 if convert else 0):]
    xn_ref = scratch[0] if scratch else outs[1]
    acc_ref = outs[0]
    f = pl.program_id(1)

    @pl.when(f == 0)
    def _():
        x = x_ref[...]
        xn_ref[...] = _rms(x, gpre_ref[...]).astype(BF16)
        acc_ref[...] = x

    if convert:
        wgu = jnp.concatenate([wg_ref[...].astype(BF16), wu_ref[...].astype(BF16)], axis=1)
        wd = wd_ref[...].astype(BF16)
        refs[n_act][...] = wgu
        refs[n_act + 1][...] = wd
    else:
        wgu, wd = wgu_ref[...], wd_ref[...]
    rows = xn_ref.shape[0] // row_split
    for part in range(row_split):
        rs = slice(part * rows, (part + 1) * rows)
        r = jnp.dot(xn_ref[rs, :], wgu, preferred_element_type=F32)
        g, u = r[:, :tf], r[:, tf:]
        h = (g * jax.nn.sigmoid(g) * (0.5 * u)).astype(BF16)
        acc_ref[rs, :] += jnp.dot(h, wd, preferred_element_type=F32)

    if emit_post:
        @pl.when(f == nf - 1)
        def _():
            outs[-1][...] = _rms(acc_ref[...], gpost_ref[...]).astype(outs[-1].dtype)


def _ffn(x, g_pre, weights, g_post, *, layer=None, emit_x, post_dtype, tm, tf=512):
    m = x.shape[0]
    nf = D_FF // tf
    emit_post = post_dtype is not None
    convert = layer is not None
    assert emit_x or post_dtype == F32, "the first output block is the f32 accumulator"
    once = dict(pipeline_mode=pl.Buffered(1)) if tm > 512 else {}
    row_spec = pl.BlockSpec((tm, D_MODEL), lambda i, f: (i, 0))
    row_out_spec = pl.BlockSpec((tm, D_MODEL), lambda i, f: (i, 0), **once)
    vec_spec = pl.BlockSpec((1, D_MODEL), lambda i, f: (0, 0))
    out_shape, out_specs = [], []
    if emit_x:
        out_shape.append(jax.ShapeDtypeStruct((m, D_MODEL), F32))
        out_specs.append(row_spec)
    post_holds_xn = emit_x and post_dtype == BF16 and tm > 512
    if emit_post:
        out_shape.append(jax.ShapeDtypeStruct((m, D_MODEL), post_dtype))
        out_specs.append(row_out_spec if emit_x and not post_holds_xn else row_spec)
    gu_spec = pl.BlockSpec((D_MODEL, 2 * tf), lambda i, f: (0, f))
    down_spec = pl.BlockSpec((tf, D_MODEL), lambda i, f: (f, 0))
    if convert:
        assert m == tm, "the bf16 weights are written once, by a single row tile"
        w_specs = [pl.BlockSpec((None, D_MODEL, tf), lambda i, f: (layer, 0, f)),
                   pl.BlockSpec((None, D_MODEL, tf), lambda i, f: (layer, 0, f)),
                   pl.BlockSpec((None, tf, D_MODEL), lambda i, f: (layer, f, 0))]
        out_shape += [jax.ShapeDtypeStruct((D_MODEL, 2 * D_FF), BF16), jax.ShapeDtypeStruct((D_FF, D_MODEL), BF16)]
        out_specs += [gu_spec, down_spec]
    else:
        w_specs = [gu_spec, down_spec]
    scratch = [] if post_holds_xn else [pltpu.VMEM((tm, D_MODEL), BF16)]
    return pl.pallas_call(
        functools.partial(_ffn_body, emit_x=emit_x, emit_post=emit_post, convert=convert, nf=nf, tf=tf,
                          row_split=max(tm // 512, 1)),
        grid=(m // tm, nf),
        in_specs=[row_spec, vec_spec, *w_specs, vec_spec],
        out_specs=out_specs,
        out_shape=out_shape,
        scratch_shapes=scratch,
        compiler_params=_cparams(2, VMEM_FFN),
        name="ffn",
    )(x, g_pre.reshape(1, -1), *weights, g_post.reshape(1, -1))


def _mm_body(a_ref, b_ref, o_ref):
    o_ref[...] = jnp.dot(a_ref[...], b_ref[...], preferred_element_type=F32)


def _sigmoid(x):
    return 0.5 * jnp.tanh(0.5 * x) + 0.5


def _mm_gate_body(a_ref, b_ref, o_ref):
    o_ref[...] = _sigmoid(jnp.dot(a_ref[...], b_ref[...], preferred_element_type=F32)).astype(BF16)


def _matmul(a, b, *, col0=0, n, gate=False, tm, tn, name):
    m, k = a.shape
    return pl.pallas_call(
        _mm_gate_body if gate else _mm_body,
        grid=(m // tm, n // tn),
        in_specs=[pl.BlockSpec((tm, k), lambda i, j: (i, 0)),
                  pl.BlockSpec((k, tn), lambda i, j: (0, col0 + j))],
        out_specs=pl.BlockSpec((tm, tn), lambda i, j: (i, j)),
        out_shape=jax.ShapeDtypeStruct((m, n), BF16 if gate else F32),
        compiler_params=_cparams(2, VMEM_BIG),
        name=name,
    )(a, b)


def _out_proj_convert_body(a_ref, b_ref, r_ref, o_ref, w_ref):
    w = b_ref[...].astype(BF16)
    w_ref[...] = w
    o_ref[...] = r_ref[...] + jnp.dot(a_ref[...], w, preferred_element_type=F32)


def _out_proj_convert(a, w_out, res, *, layer, tn):
    m, k = a.shape
    return pl.pallas_call(
        _out_proj_convert_body,
        grid=(D_MODEL // tn,),
        in_specs=[pl.BlockSpec((m, k), lambda j: (0, 0)),
                  pl.BlockSpec((None, k, tn), lambda j: (layer, 0, j)),
                  pl.BlockSpec((m, tn), lambda j: (0, j))],
        out_specs=[pl.BlockSpec((m, tn), lambda j: (0, j)), pl.BlockSpec((k, tn), lambda j: (0, j))],
        out_shape=[jax.ShapeDtypeStruct((m, D_MODEL), F32), jax.ShapeDtypeStruct((k, D_MODEL), BF16)],
        compiler_params=_cparams(1, VMEM_BIG),
        name="out_proj",
    )(a, w_out, res)


def _mm_deint_body(a_ref, b_ref, o_ref, s_ref, *, dil):
    r = jnp.dot(a_ref[...], b_ref[...], preferred_element_type=F32)
    for h in range(NH):
        s_ref[h] = r[:, h * HD:(h + 1) * HD]
    n = SEQ // dil
    for h in range(NH):
        for res in range(dil):
            o_ref[h, pl.ds(res * n, n), :] = s_ref[h, pl.ds(res, n, stride=dil), :]


def _w_in_source_block(j, where=jnp.where):
    return where(j < 3, 3 * j, where(j < 26, j + 6, where(j < 29, 3 * (j - 26) + 1, 3 * (j - 29) + 2)))


assert tuple(_w_in_source_block(np.arange(ZBLK), np.where)) == W_IN_ORDER


def _in_proj_convert_body(a_ref, b_ref, z_ref, gate_ref, w_ref):
    j = pl.program_id(0)
    w = b_ref[...].astype(BF16)
    w_ref[...] = w
    r = jnp.dot(a_ref[...], w, preferred_element_type=F32)
    z_ref[...] = r

    @pl.when((j >= N_PLAIN) & (j < N_PLAIN + N_GATE))
    def _():
        gate_ref[...] = _sigmoid(r).astype(BF16)


def _in_proj_convert(a, w_in, *, layer):
    m, k = a.shape
    return pl.pallas_call(
        _in_proj_convert_body,
        grid=(ZBLK,),
        in_specs=[pl.BlockSpec((m, k), lambda j: (0, 0)),
                  pl.BlockSpec((None, k, BW), lambda j: (layer, 0, _w_in_source_block(j)))],
        out_specs=[pl.BlockSpec((m, BW), lambda j: (0, j)),
                   pl.BlockSpec((m, BW), lambda j: (0, jnp.clip(j - N_PLAIN, 0, N_GATE - 1))),
                   pl.BlockSpec((k, BW), lambda j: (0, j))],
        out_shape=[jax.ShapeDtypeStruct((m, IN_WIDTH), F32),
                   jax.ShapeDtypeStruct((m, GATE_W), BF16),
                   jax.ShapeDtypeStruct((k, IN_WIDTH), BF16)],
        compiler_params=_cparams(1, VMEM_BIG),
        name="in_proj_convert",
    )(a, w_in)


def _matmul_deint(a, b, *, col0, dil, name):
    m, k = a.shape
    return pl.pallas_call(
        functools.partial(_mm_deint_body, dil=dil),
        grid=(BATCH, 3),
        in_specs=[pl.BlockSpec((SEQ, k), lambda b, j: (b, 0)),
                  pl.BlockSpec((k, BW), lambda b, j: (0, col0 + j))],
        out_specs=pl.BlockSpec((None, NH, SEQ, HD), lambda b, j: (j, 0, b, 0)),
        out_shape=jax.ShapeDtypeStruct((3, NH, m, HD), F32),
        scratch_shapes=[pltpu.VMEM((NH, SEQ, HD), F32)],
        compiler_params=_cparams(2, VMEM_BIG),
        name=name,
    )(a, b)


def _branch_body(oa_ref, ob_ref, oc_ref, od_ref, gate_ref, wb_ref, o_ref, wbo_ref):
    acc = None
    for n, br in enumerate((oa_ref, ob_ref, oc_ref, od_ref)):
        w = wb_ref[n].astype(BF16)
        wbo_ref[n] = w
        proj = jnp.dot(br[...], w, preferred_element_type=F32)
        t = gate_ref[:, n * D_MODEL:(n + 1) * D_MODEL].astype(F32) * proj
        acc = t if acc is None else acc + t
    o_ref[...] = acc.astype(BF16)


def _branch_merge(oa, ob, oc, od, gates, w_branch, *, layer):
    m = oa.shape[0]
    once = dict(pipeline_mode=pl.Buffered(1))
    br_spec = pl.BlockSpec((m, BW), lambda i: (0, 0))
    return pl.pallas_call(
        _branch_body,
        grid=(1,),
        in_specs=[br_spec] * 4 + [pl.BlockSpec((m, GATE_W), lambda i: (0, 0)),
                                  pl.BlockSpec((None, 4, BW, D_MODEL), lambda i: (layer, 0, 0, 0), **once)],
        out_specs=[pl.BlockSpec((m, D_MODEL), lambda i: (0, 0)),
                   pl.BlockSpec((4, BW, D_MODEL), lambda i: (0, 0, 0), **once)],
        out_shape=[jax.ShapeDtypeStruct((m, D_MODEL), BF16), jax.ShapeDtypeStruct((4, BW, D_MODEL), BF16)],
        compiler_params=_cparams(1, VMEM_BIG),
        name="branch_merge",
    )(oa, ob, oc, od, gates, w_branch)


def _t5_buckets(dist):
    max_exact = REL_BUCKETS // 2
    d = np.maximum(dist, 1).astype(np.float32)
    large = max_exact + (np.log(d / max_exact) / np.log(REL_MAX_DISTANCE / max_exact)
                         * (REL_BUCKETS - max_exact)).astype(np.int32)
    large = np.minimum(large, REL_BUCKETS - 1)
    return np.where(dist < max_exact, dist, large).astype(np.int32)


def _bucket_of_step(dil):
    return _t5_buckets(dil * np.arange(NKEY))


def _prompt_bucket_matrix(dil):
    iq = np.arange(CH)[:, None]
    col = np.arange(2 * CH)[None, :]
    j = iq + CH - col
    valid = (j >= 0) & (j < NKEY)
    return np.where(valid, _bucket_of_step(dil)[np.clip(j, 0, NKEY - 1)], -1).astype(np.int32)


SAMPLE_CACHE_ROWS = (8 * ATT_WINDOWS[0], 8 * ATT_WINDOWS[1], 8 * DEC_SEQ * CH)


def _sample_bucket_tables():
    h = np.repeat(np.arange(NH), SROWS)[:, None]
    t = np.tile(np.arange(SROWS), NH)[:, None]
    lane = np.arange(CH)[None, :]
    tables = []
    for gi, dil in enumerate(ATT_DILATIONS):
        bos = _bucket_of_step(dil)
        flat = np.arange(SAMPLE_CACHE_ROWS[gi])[None, :]
        head, is_k = flat % NH, (flat // NH) % 2 == 0
        if gi < 2:
            w = flat // 8
            steps = ATT_WINDOWS[gi] + t - w
            valid = (steps % dil == 0) & (steps // dil < NKEY)
            j = steps // dil
        else:
            res, i = (flat // 8) % DEC_SEQ, flat // (8 * DEC_SEQ)
            valid = res == t
            j = np.broadcast_to(CH - i, valid.shape)
        valid = valid & is_k & (head == h) & (t < DEC_SEQ)
        cache_tbl = np.where(valid, bos[np.clip(j, 0, NKEY - 1)], -1)
        jn = t - lane
        validn = (t < DEC_SEQ) & (lane < DEC_SEQ) & (jn >= 0) & ((dil == 1) | (jn == 0))
        new_tbl = np.where(validn, bos[np.clip(jn, 0, NKEY - 1)], -1)
        tables.append((cache_tbl.astype(np.int32), new_tbl.astype(np.int32)))
    return tables


def _bias_from_buckets(bk, rb_ref, col):
    out = jnp.full(bk.shape, NEG, F32)
    for b in range(REL_BUCKETS):
        out = jnp.where(bk == b, rb_ref[b, col], out)
    return out


ATT_SUB = 4


def _attn_prompt_body(rb_ref, bk_ref, q_ref, k_ref, v_ref, *refs, gi, headed):
    dil = ATT_DILATIONS[gi]
    has_prev = gi == 0
    if has_prev:
        kp_ref, vp_ref = refs[:2]
        refs = refs[2:]
    o_ref, lse_ref, kvo_ref, bias_ref = refs[-4:]
    if len(refs) == 5:
        kvo_ref[0] = refs[0][...]
        kvo_ref = kvo_ref.at[1]
    step = pl.program_id(1)

    @pl.when((pl.program_id(0) == 0) & (step == 0))
    def _():
        bk = bk_ref[...]
        for h in range(NH):
            bias_ref[h] = _bias_from_buckets(bk, rb_ref, gi * NH + h)

    def tile(ref, h, t):
        rs = slice(t * CH, (t + 1) * CH)
        return ref[h, rs, :] if headed else ref[rs, h * HD:(h + 1) * HD]

    scale = HD ** -0.5
    lane_head = lax.broadcasted_iota(jnp.int32, (CH, NH * LSE_W), 1) // LSE_W
    if has_prev:
        col = lax.broadcasted_iota(jnp.int32, (CH, 2 * CH), 1)
        no_prev = (step == 0) & (col < CH)
    for t in range(ATT_SUB):
        if gi == 0:
            dst = slice(t * CH, (t + 1) * CH)
        elif gi == 1:
            dst = pl.ds(t * CH * dil + step, CH, stride=dil)
        else:
            dst = pl.ds(step * ATT_SUB + t, CH, stride=dil)
        keep = gi == 2 or t == ATT_SUB - 1
        kv_row0 = (t if gi == 2 else 0) * 2 * NH
        lse_tile = None
        for h in range(NH):
            kc = tile(k_ref, h, t)
            vc = tile(v_ref, h, t)
            if keep:
                kvo_ref[:, kv_row0 + h, :] = kc
                kvo_ref[:, kv_row0 + NH + h, :] = vc
            qh = tile(q_ref, h, t).astype(BF16)
            if gi < 2 and (t > 0 or has_prev):
                if t > 0:
                    kp, vp, bias = tile(k_ref, h, t - 1), tile(v_ref, h, t - 1), bias_ref[h]
                else:
                    kp, vp = kp_ref[:, h * HD:(h + 1) * HD], vp_ref[:, h * HD:(h + 1) * HD]
                    bias = jnp.where(no_prev, NEG, bias_ref[h])
                kh = jnp.concatenate([kp, kc], axis=0).astype(BF16)
                vh = jnp.concatenate([vp, vc], axis=0).astype(BF16)
            else:
                kh = kc.astype(BF16)
                vh = vc.astype(BF16)
                bias = bias_ref[h][:, CH:]
            s = lax.dot_general(qh, kh, (((1,), (1,)), ((), ())), preferred_element_type=F32) * scale + bias
            m = jnp.max(s, axis=-1, keepdims=True)
            p = jnp.exp(s - m)
            den = jnp.sum(p, axis=-1, keepdims=True)
            acc = jnp.dot(p.astype(BF16), vh, preferred_element_type=F32)
            o_ref[h, dst, :] = acc / den
            lse = m + jnp.log(den)
            lse_tile = lse if h == 0 else jnp.where(lane_head == h, lse, lse_tile)
        lse_ref[dst, :] = lse_tile


def _attn_prompt(zq, rel_bias, gi, kv_prev):
    dil = ATT_DILATIONS[gi]
    rows = BATCH * SEQ
    nsteps = SEQ // (ATT_SUB * CH)
    assert SEQ // dil // CH in (1, ATT_SUB, ATT_SUB * nsteps)
    headed = gi > 0
    bk = jnp.asarray(_prompt_bucket_matrix(dil))
    span = ATT_SUB * CH

    def zspec(c):
        if headed:
            return pl.BlockSpec((None, NH, span, HD), lambda b, s: (c, 0, b * nsteps + s, 0))
        return pl.BlockSpec((span, BW), lambda b, s: (b * nsteps + s, COL_QKV[0][c]))

    in_specs = [pl.BlockSpec(memory_space=pltpu.SMEM),
                pl.BlockSpec((CH, 2 * CH), lambda b, s: (0, 0)),
                zspec(0), zspec(1), zspec(2)]
    args = [rel_bias, bk, zq, zq, zq]
    if gi == 0:
        above = lambda c: pl.BlockSpec(
            (CH, BW), lambda b, s: (jnp.maximum((b * nsteps + s) * ATT_SUB - 1, 0), COL_QKV[0][c]))
        in_specs += [above(1), above(2)]
        args += [zq, zq]
        out_rows, out_index = span, (lambda b, s: b * nsteps + s)
    else:
        out_rows, out_index = SEQ, (lambda b, s: b)
    n_res = ATT_SUB if gi == 2 else 1
    kv_blk = (CH, n_res * 2 * NH, HD)
    kv_index = lambda b, s: (b, 0, s if gi else 0, 0)
    kv_shape = (BATCH, CH, dil * 2 * NH, HD)
    if kv_prev is None:
        kv_spec = pl.BlockSpec((None, *kv_blk), kv_index)
    else:
        in_specs.append(pl.BlockSpec((None, *kv_blk), kv_index))
        args.append(kv_prev)
        kv_spec = pl.BlockSpec((2, None, *kv_blk), lambda b, s: (0, *kv_index(b, s)))
        kv_shape = (2, *kv_shape)
    return pl.pallas_call(
        functools.partial(_attn_prompt_body, gi=gi, headed=headed),
        grid=(BATCH, nsteps),
        in_specs=in_specs,
        out_specs=[pl.BlockSpec((NH, out_rows, HD), lambda b, s: (0, out_index(b, s), 0)),
                   pl.BlockSpec((out_rows, NH * LSE_W), lambda b, s: (out_index(b, s), 0)),
                   kv_spec],
        out_shape=[jax.ShapeDtypeStruct((NH, rows, HD), F32),
                   jax.ShapeDtypeStruct((rows, NH * LSE_W), F32),
                   jax.ShapeDtypeStruct(kv_shape, F32)],
        scratch_shapes=[pltpu.VMEM((NH, CH, 2 * CH), F32)],
        compiler_params=_cparams(2),
        name=f"attn_prompt_g{gi}",
    )(*args)


def _merge_body(o0, l0, o1, l1, o2, l2, ob_ref, oc_ref, od_ref, gate_ref, wb_ref, wo_ref, x_ref, out_ref):
    heads = []
    for h in range(NH):
        a0, a1, a2 = (l[:, h * LSE_W:h * LSE_W + 1] for l in (l0, l1, l2))
        m = jnp.maximum(jnp.maximum(a0, a1), a2)
        w0, w1, w2 = jnp.exp(a0 - m), jnp.exp(a1 - m), jnp.exp(a2 - m)
        heads.append(((w0 * o0[h] + w1 * o1[h] + w2 * o2[h]) / (w0 + w1 + w2)).astype(BF16))
    oa = jnp.concatenate(heads, axis=1)
    acc = None
    for n, br in enumerate((oa, ob_ref[...], oc_ref[...], od_ref[...])):
        proj = jnp.dot(br, wb_ref[n], preferred_element_type=F32)
        t = gate_ref[:, n * D_MODEL:(n + 1) * D_MODEL].astype(F32) * proj
        acc = t if acc is None else acc + t
    out_ref[...] = x_ref[...] + jnp.dot(acc.astype(BF16), wo_ref[...], preferred_element_type=F32)


def _merge(parts, ob, oc, od, gates, wb, wo, x, *, tm=TM_MERGE):
    m = x.shape[0]
    once = dict(pipeline_mode=pl.Buffered(1))
    ospec = pl.BlockSpec((NH, tm, HD), lambda i: (0, i, 0))
    lspec = pl.BlockSpec((tm, NH * LSE_W), lambda i: (i, 0))
    br_spec = pl.BlockSpec((tm, BW), lambda i: (i, 0))
    row_spec = pl.BlockSpec((tm, D_MODEL), lambda i: (i, 0))
    return pl.pallas_call(
        _merge_body,
        grid=(m // tm,),
        in_specs=[ospec, lspec] * 3 + [br_spec] * 3 + [
            pl.BlockSpec((tm, GATE_W), lambda i: (i, 0)),
            pl.BlockSpec((4, BW, D_MODEL), lambda i: (0, 0, 0), **once),
            pl.BlockSpec((D_MODEL, D_MODEL), lambda i: (0, 0), **once),
            row_spec],
        out_specs=row_spec,
        out_shape=jax.ShapeDtypeStruct((m, D_MODEL), F32),
        compiler_params=_cparams(1, VMEM_BIG),
        name="merge",
    )(*parts, ob, oc, od, gates, wb, wo, x)


def _attn_sample_body(rb_ref, bc0_ref, bc1_ref, bc2_ref, bn_ref, z_ref, c0_ref, c1_ref, c2_ref,
                      oa_ref, kv0_ref, kv1_ref, kv2_ref, b0_ref, b1_ref, b2_ref, bnew_ref):
    bucket_refs = (bc0_ref, bc1_ref, bc2_ref)
    bias_refs = (b0_ref, b1_ref, b2_ref)

    @pl.when(pl.program_id(0) == 0)
    def _():
        for gi in range(3):
            for h in range(NH):
                rs = slice(h * SROWS, (h + 1) * SROWS)
                bias_refs[gi][rs, :] = _bias_from_buckets(bucket_refs[gi][rs, :], rb_ref, gi * NH + h)
                bnew_ref[gi, rs, :] = _bias_from_buckets(bn_ref[gi, rs, :], rb_ref, gi * NH + h)

    scale = HD ** -0.5
    rows = NH * SROWS
    head_of_row = lax.broadcasted_iota(jnp.int32, (rows, BW), 0) // SROWS
    head_of_lane = lax.broadcasted_iota(jnp.int32, (rows, BW), 1) // HD
    head_mask = head_of_row == head_of_lane
    caches = (c0_ref, c1_ref, c2_ref)
    kv_refs = (kv0_ref, kv1_ref, kv2_ref)
    stats = []
    for gi in range(3):
        cq, ck, cv = COL_QKV[gi]
        q = z_ref[:, cq * BW:(cq + 1) * BW]
        kn = z_ref[:, ck * BW:(ck + 1) * BW]
        vn = z_ref[:, cv * BW:(cv + 1) * BW]
        kv_refs[gi][:, :BW] = kn
        kv_refs[gi][:, BW:] = vn
        qm = jnp.concatenate([q[:, h * HD:(h + 1) * HD] for h in range(NH)], axis=0).astype(BF16)
        kf = caches[gi][...].reshape(SAMPLE_CACHE_ROWS[gi], HD).astype(BF16)
        s = lax.dot_general(qm, kf, (((1,), (1,)), ((), ())), preferred_element_type=F32) * scale + bias_refs[gi][...]
        qrows = jnp.where(head_mask, jnp.concatenate([q] * NH, axis=0), 0.0)
        bias_n = bnew_ref[gi]
        s_new = []
        for tp in range(DEC_SEQ):
            dotp = jnp.sum(qrows * kn[tp:tp + 1, :], axis=-1, keepdims=True)
            s_new.append(dotp * scale + bias_n[:, tp:tp + 1])
        m = jnp.max(s, axis=-1, keepdims=True)
        for sn in s_new:
            m = jnp.maximum(m, sn)
        p = jnp.exp(s - m)
        den = jnp.sum(p, axis=-1, keepdims=True)
        pv = pltpu.roll(p, NH, axis=1).astype(BF16)
        acc = jnp.dot(pv, kf, preferred_element_type=F32)
        acc_n = jnp.zeros((rows, BW), F32)
        for tp, sn in enumerate(s_new):
            pn = jnp.exp(sn - m)
            den = den + pn
            acc_n = acc_n + pn * vn[tp:tp + 1, :]
        acc = acc + jnp.concatenate([acc_n[h * SROWS:(h + 1) * SROWS, h * HD:(h + 1) * HD] for h in range(NH)], axis=0)
        stats.append((m, den, acc))
    mm = jnp.maximum(jnp.maximum(stats[0][0], stats[1][0]), stats[2][0])
    den = jnp.zeros((rows, 1), F32)
    acc = jnp.zeros((rows, HD), F32)
    for m, d, a in stats:
        w = jnp.exp(m - mm)
        den = den + w * d
        acc = acc + w * a
    o = acc / den
    for h in range(NH):
        oa_ref[:, h * HD:(h + 1) * HD] = o[h * SROWS:(h + 1) * SROWS, :].astype(BF16)


def _attn_sample(z, caches, rel_bias, layer):
    tables = _sample_bucket_tables()
    rows = DEC_BATCH * SROWS
    qrows = NH * SROWS
    n0, n1, n2 = SAMPLE_CACHE_ROWS
    c0 = caches[0].reshape(DEPTH, DEC_BATCH, n0, HD)
    c1 = caches[1].reshape(DEPTH, DEC_BATCH, n1, HD)
    c2 = caches[2].reshape(DEPTH, DEC_BATCH, CH, 16 * 8, HD)
    new_tbl = jnp.asarray(np.stack([t[1] for t in tables]))
    kv_spec = pl.BlockSpec((SROWS, 2 * BW), lambda b: (b, 0))
    kv_shape = jax.ShapeDtypeStruct((rows, 2 * BW), F32)
    const2 = lambda b: (0, 0)
    return pl.pallas_call(
        _attn_sample_body,
        grid=(DEC_BATCH,),
        in_specs=[pl.BlockSpec(memory_space=pltpu.SMEM),
                  pl.BlockSpec((qrows, n0), const2), pl.BlockSpec((qrows, n1), const2),
                  pl.BlockSpec((qrows, n2), const2),
                  pl.BlockSpec((3, qrows, CH), lambda b: (0, 0, 0)),
                  pl.BlockSpec((SROWS, IN_WIDTH), lambda b: (b, 0)),
                  pl.BlockSpec((None, None, n0, HD), lambda b: (layer, b, 0, 0)),
                  pl.BlockSpec((None, None, n1, HD), lambda b: (layer, b, 0, 0)),
                  pl.BlockSpec((None, None, CH, 8 * DEC_SEQ, HD), lambda b: (layer, b, 0, 0, 0))],
        out_specs=[pl.BlockSpec((SROWS, BW), lambda b: (b, 0)), kv_spec, kv_spec, kv_spec],
        out_shape=[jax.ShapeDtypeStruct((rows, BW), BF16), kv_shape, kv_shape, kv_shape],
        scratch_shapes=[pltpu.VMEM((qrows, n0), F32), pltpu.VMEM((qrows, n1), F32), pltpu.VMEM((qrows, n2), F32),
                        pltpu.VMEM((3, qrows, CH), F32)],
        compiler_params=_cparams(1),
        name="attn_sample",
    )(rel_bias, jnp.asarray(tables[0][0]), jnp.asarray(tables[1][0]), jnp.asarray(tables[2][0]), new_tbl,
      z, c0, c1, c2)


def _gmlp_body(bu_ref, bv_ref, g_ref, ws_ref, bs_ref, o_ref, *, rows):
    tril = lax.broadcasted_iota(jnp.int32, (CH, CH), 0) >= lax.broadcasted_iota(jnp.int32, (CH, CH), 1)
    w = [jnp.where(tril, ws_ref[g], 0.0).astype(BF16) for g in range(NH)]
    for c in range(rows // CH):
        rs = slice(c * CH, (c + 1) * CH)
        u = jax.nn.gelu(bu_ref[rs, :])
        vn = _rms(jax.nn.gelu(bv_ref[rs, :]), g_ref[...])
        for g in range(NH):
            sl = slice(g * HD, (g + 1) * HD)
            mixed = jnp.dot(w[g], vn[:, sl].astype(BF16), preferred_element_type=F32) + bs_ref[:, sl]
            o_ref[rs, sl] = (u[:, sl] * mixed).astype(BF16)


def _gmlp(z, g_gmlp, w_spatial, b_spatial, *, rows):
    m = z.shape[0]
    bs_full = jnp.repeat(b_spatial.T, HD, axis=1)
    return pl.pallas_call(
        functools.partial(_gmlp_body, rows=rows),
        grid=(m // rows,),
        in_specs=[pl.BlockSpec((rows, BW), lambda i: (i, COL_BU)),
                  pl.BlockSpec((rows, BW), lambda i: (i, COL_BV)),
                  pl.BlockSpec((1, BW), lambda i: (0, 0)),
                  pl.BlockSpec((NH, CH, CH), lambda i: (0, 0, 0)),
                  pl.BlockSpec((CH, BW), lambda i: (0, 0))],
        out_specs=pl.BlockSpec((rows, BW), lambda i: (i, 0)),
        out_shape=jax.ShapeDtypeStruct((m, BW), BF16),
        compiler_params=_cparams(1),
        name="gmlp",
    )(z, z, g_gmlp.reshape(1, -1), w_spatial, bs_full)


POOL_PAD = 16


def _pool_body(x_ref, prev_ref, wp_ref, sc_ref, o_ref, st_ref, ext_ref, *, rows):
    ib = pl.program_id(1)
    prev = jnp.where(ib == 0, 0.0, prev_ref[...])
    x = x_ref[...]
    ext_ref[0:POOL_PAD, :] = prev
    ext_ref[POOL_PAD:POOL_PAD + rows, :] = x
    st_ref[...] = ext_ref[pl.ds(rows + 1, POOL_STATE), :]
    ext = ext_ref[...]
    pos = ib * rows + lax.broadcasted_iota(jnp.int32, (rows, 1), 0)
    for gi, win in enumerate(POOL_WINDOWS):
        sl = slice(gi * HD, (gi + 1) * HD)
        s = ext[:, sl]
        k = 1
        while k < win:
            s = s + pltpu.roll(s, k, axis=0)
            k *= 2
        cnt = jnp.minimum(pos + 1, win).astype(F32)
        diff = s[POOL_PAD:] / cnt - x[:, sl]
        y = jnp.dot(diff.astype(BF16), wp_ref[gi].astype(BF16), preferred_element_type=F32)
        o_ref[:, sl] = (y * sc_ref[:, sl]).astype(BF16)


def _pool(z, w_pool, pool_scale, *, rows):
    m = z.shape[0]
    nblk = SEQ // rows
    per_pad = rows // POOL_PAD
    return pl.pallas_call(
        functools.partial(_pool_body, rows=rows),
        grid=(BATCH, nblk),
        in_specs=[pl.BlockSpec((rows, BW), lambda b, i: (b * nblk + i, COL_CIN)),
                  pl.BlockSpec((POOL_PAD, BW),
                               lambda b, i: (jnp.maximum((b * nblk + i) * per_pad - 1, 0), COL_CIN)),
                  pl.BlockSpec((NH, HD, HD), lambda b, i: (0, 0, 0)),
                  pl.BlockSpec((1, BW), lambda b, i: (0, 0))],
        out_specs=[pl.BlockSpec((rows, BW), lambda b, i: (b * nblk + i, 0)),
                   pl.BlockSpec((None, POOL_STATE, BW), lambda b, i: (b, 0, 0))],
        out_shape=[jax.ShapeDtypeStruct((m, BW), BF16),
                   jax.ShapeDtypeStruct((BATCH, POOL_STATE, BW), F32)],
        scratch_shapes=[pltpu.VMEM((POOL_PAD + rows, BW), F32)],
        compiler_params=_cparams(2),
        name="pool",
    )(z, z, w_pool, pool_scale.reshape(1, -1))


def _gmlp_sample_body(bu_ref, bv_ref, g_ref, wk_ref, bs_ref, o_ref, vn_ref):
    u = jax.nn.gelu(bu_ref[...])
    vn = _rms(jax.nn.gelu(bv_ref[...]), g_ref[...])
    vn_ref[...] = vn
    mixed = jnp.tile(bs_ref[...], (DEC_BATCH, 1))
    for k in range(DEC_SEQ):
        shifted = vn if k == 0 else pltpu.roll(vn, k, axis=0)
        mixed = mixed + jnp.tile(wk_ref[k], (DEC_BATCH, 1)) * shifted
    o_ref[...] = (u * mixed).astype(BF16)


def _gmlp_sample(z, g_gmlp, w_spatial, b_spatial):
    m = z.shape[0]
    t = np.arange(SROWS)
    live = t < DEC_SEQ
    pick = np.zeros((DEC_SEQ, SROWS, SROWS), np.float32)
    for k in range(DEC_SEQ):
        pick[k, t[live & (t >= k)], t[live & (t >= k)] - k] = 1.0
    diag = jnp.sum(w_spatial[None, :, :SROWS, :SROWS] * pick[:, None], axis=-1)
    wk = jnp.repeat(diag.transpose(0, 2, 1), HD, axis=2)
    bs = jnp.repeat((b_spatial[:, :SROWS] * jnp.asarray(live, F32)).T, HD, axis=1)
    return pl.pallas_call(
        _gmlp_sample_body,
        grid=(1,),
        in_specs=[pl.BlockSpec((m, BW), lambda i: (0, COL_BU)),
                  pl.BlockSpec((m, BW), lambda i: (0, COL_BV)),
                  pl.BlockSpec((1, BW), lambda i: (0, 0)),
                  pl.BlockSpec((DEC_SEQ, SROWS, BW), lambda i: (0, 0, 0)),
                  pl.BlockSpec((SROWS, BW), lambda i: (0, 0))],
        out_specs=[pl.BlockSpec((m, BW), lambda i: (0, 0))] * 2,
        out_shape=[jax.ShapeDtypeStruct((m, BW), BF16), jax.ShapeDtypeStruct((m, BW), F32)],
        compiler_params=_cparams(1),
        name="gmlp_sample",
    )(z, z, g_gmlp.reshape(1, -1), wk, bs)


def _pool_sample_body(x_ref, prev_ref, wp_ref, sc_ref, o_ref, st_ref, ext_ref, *, start):
    per = POOL_PAD + SROWS
    ext_ref[:, 0:POOL_PAD, :] = prev_ref[...]
    ext_ref[:, POOL_PAD:per, :] = x_ref[...]
    st_ref[...] = ext_ref[:, pl.ds(DEC_SEQ + 1, POOL_STATE), :]
    ext = ext_ref[...].reshape(DEC_BATCH * per, BW)
    pos = start + (lax.broadcasted_iota(jnp.int32, (DEC_BATCH * per, 1), 0) % per - POOL_PAD)
    for gi, win in enumerate(POOL_WINDOWS):
        sl = slice(gi * HD, (gi + 1) * HD)
        x = ext[:, sl]
        s = x
        k = 1
        while k < win:
            s = s + pltpu.roll(s, k, axis=0)
            k *= 2
        cnt = jnp.minimum(jnp.maximum(pos, 0) + 1, win).astype(F32)
        y = jnp.dot((s / cnt - x).astype(BF16), wp_ref[gi].astype(BF16), preferred_element_type=F32)
        o_ref[:, :, sl] = (y * sc_ref[:, sl]).reshape(DEC_BATCH, per, HD)[:, POOL_PAD:, :].astype(BF16)


def _pool_sample(z, prev, w_pool, pool_scale, *, start, layer):
    m, zw = z.shape
    z3 = z.reshape(DEC_BATCH, SROWS, zw)
    o, st = pl.pallas_call(
        functools.partial(_pool_sample_body, start=start),
        grid=(1,),
        in_specs=[pl.BlockSpec((DEC_BATCH, SROWS, BW), lambda i: (0, 0, COL_CIN)),
                  pl.BlockSpec((None, DEC_BATCH, POOL_PAD, BW), lambda i: (layer, 0, 0, 0)),
                  pl.BlockSpec((NH, HD, HD), lambda i: (0, 0, 0)),
                  pl.BlockSpec((1, BW), lambda i: (0, 0))],
        out_specs=[pl.BlockSpec((DEC_BATCH, SROWS, BW), lambda i: (0, 0, 0)),
                   pl.BlockSpec((DEC_BATCH, POOL_STATE, BW), lambda i: (0, 0, 0))],
        out_shape=[jax.ShapeDtypeStruct((DEC_BATCH, SROWS, BW), BF16),
                   jax.ShapeDtypeStruct((DEC_BATCH, POOL_STATE, BW), F32)],
        scratch_shapes=[pltpu.VMEM((DEC_BATCH, POOL_PAD + SROWS, BW), F32)],
        compiler_params=_cparams(1),
        name="pool_sample",
    )(z3, prev, w_pool, pool_scale.reshape(1, -1))
    return o.reshape(m, BW), st


def _ret_tables(c_eff, positions, size):
    lg = np.log1p(-np.power(2.0, -5.0 - np.arange(NH, dtype=np.float64)))
    i = np.arange(size, dtype=np.float64)
    live = (i < c_eff)
    diff = i[:, None] - i[None, :]
    inner = np.where((diff >= 0) & live[:, None] & live[None, :], np.exp(np.maximum(diff, 0.0)[None] * lg[:, None, None]), 0.0)
    qd = np.where(live[None, :], np.exp((i + 1.0)[None, :] * lg[:, None]), 0.0)
    kd = np.where(live[None, :], np.exp((c_eff - 1.0 - i)[None, :] * lg[:, None]), 0.0)
    chunk = tuple(float(v) for v in np.exp(c_eff * lg))
    qd_full = np.repeat(qd.T, HD, axis=1)
    kd_full = np.repeat(kd.T, HD, axis=1) * (HD ** -0.5)
    half = HD // 2
    inv = ROPE_BASE ** (-np.arange(half, dtype=np.float64) / half)
    ang = np.asarray(positions, np.float64)[:, None] * inv[None, :]
    cosf = np.concatenate([np.cos(ang), np.cos(ang)], axis=1)
    sinf = np.concatenate([-np.sin(ang), np.sin(ang)], axis=1)
    to32 = lambda a: jnp.asarray(a.astype(np.float32))
    return to32(inner), to32(qd_full), to32(kd_full), chunk, to32(cosf), to32(sinf)


def _ret_body(q_ref, k_ref, v_ref, g_ref, cos_ref, sin_ref, inner_ref, qd_ref, kd_ref, gr_ref, *refs,
              nseq, rows, chunk_decay, has_state):
    refs = list(refs)
    s0_ref = refs.pop(0) if has_state else None
    o_ref, sn_ref, s_ref = refs[:3]
    pad_ref = refs[3] if rows < CH else None
    ic = pl.program_id(1)

    @pl.when(ic == 0)
    def _():
        if has_state:
            s_ref[...] = s0_ref[...]
        else:
            s_ref[...] = jnp.zeros_like(s_ref)

    def chunk(ref, b, k):
        if rows >= CH:
            return ref[b]
        pad_ref[k] = jnp.zeros((CH, BW), F32)
        pad_ref[k, 0:rows, :] = ref[b]
        return pad_ref[k]

    cosf = cos_ref[...]
    sinf = sin_ref[...]
    for b in range(nseq):
        q = chunk(q_ref, b, 0)
        k = chunk(k_ref, b, 1)
        v = chunk(v_ref, b, 2)
        gate = chunk(g_ref, b, 3)
        for h in range(NH):
            sl = slice(h * HD, (h + 1) * HD)
            qh = q[:, sl]
            kh = k[:, sl]
            rq = qh * cosf + pltpu.roll(qh, HD // 2, axis=1) * sinf
            rk = kh * cosf + pltpu.roll(kh, HD // 2, axis=1) * sinf
            vb = v[:, sl].astype(BF16)
            rqb = rq.astype(BF16)
            state = s_ref[b, h]
            att = lax.dot_general(rqb, (rk * (HD ** -0.5)).astype(BF16), (((1,), (1,)), ((), ())),
                                  preferred_element_type=F32) * inner_ref[h]
            o = (jnp.dot(att.astype(BF16), vb, preferred_element_type=F32)
                 + jnp.dot(rqb, state.astype(BF16), preferred_element_type=F32) * qd_ref[:, sl])
            kdec = (rk * kd_ref[:, sl]).astype(BF16)
            new_state = state * chunk_decay[h] + lax.dot_general(
                kdec, vb, (((0,), (0,)), ((), ())), preferred_element_type=F32)
            s_ref[b, h] = new_state
            sn_ref[b, h] = new_state
            o = o * lax.rsqrt(jnp.mean(o * o, axis=-1, keepdims=True) + EPS)
            gt = gate[:, sl]
            o_ref[b, :, sl] = (o * gr_ref[:, sl] * (gt * jax.nn.sigmoid(gt)))[:rows].astype(BF16)


def _retention(z, g_ret, state, *, nseq, per_step, rows, c_eff, positions, layer=None):
    m, zw = z.shape
    seq_rows = m // nseq
    nchunk = seq_rows // rows
    size = max(rows, CH)
    inner, qd, kd, chunk_decay, cosf, sinf = _ret_tables(c_eff, positions, size)
    has_state = state is not None
    z3 = z.reshape(nseq, seq_rows, zw)

    def zspec(col):
        return pl.BlockSpec((per_step, rows, BW), lambda g, i: (g, i, col))

    const2 = lambda g, i: (0, 0)
    in_specs = [zspec(COL_DQ), zspec(COL_DK), zspec(COL_DV), zspec(COL_DG),
                pl.BlockSpec((size, HD), lambda g, i: (i, 0)), pl.BlockSpec((size, HD), lambda g, i: (i, 0)),
                pl.BlockSpec((NH, size, size), lambda g, i: (0, 0, 0)),
                pl.BlockSpec((size, BW), const2), pl.BlockSpec((size, BW), const2), pl.BlockSpec((1, BW), const2)]
    args = [z3, z3, z3, z3, cosf, sinf, inner, qd, kd, g_ret.reshape(1, -1)]
    if has_state:
        in_specs.append(pl.BlockSpec((None, per_step, NH, HD, HD), lambda g, i: (layer, g, 0, 0, 0)))
        args.append(state)
    scratch = [pltpu.VMEM((per_step, NH, HD, HD), F32)]
    if rows < CH:
        scratch.append(pltpu.VMEM((4, CH, BW), F32))
    o, sn = pl.pallas_call(
        functools.partial(_ret_body, nseq=per_step, rows=rows, chunk_decay=chunk_decay, has_state=has_state),
        grid=(nseq // per_step, nchunk),
        in_specs=in_specs,
        out_specs=[pl.BlockSpec((per_step, rows, BW), lambda g, i: (g, i, 0)),
                   pl.BlockSpec((per_step, NH, HD, HD), lambda g, i: (g, 0, 0, 0))],
        out_shape=[jax.ShapeDtypeStruct((nseq, seq_rows, BW), BF16),
                   jax.ShapeDtypeStruct((nseq, NH, HD, HD), F32)],
        scratch_shapes=scratch,
        compiler_params=_cparams(2),
        name="retention",
    )(*args)
    return o.reshape(m, BW), sn


def kernel(x_prompt, x_sample, cache_attn_kv_w128, cache_attn_kv_w512, cache_attn_kv_w2048, state_pool, state_ret, rel_bias, g_ffn1, w_ffn1_gate, w_ffn1_up, w_ffn1_down, g_mix, w_in, g_gmlp, w_spatial, b_spatial, w_pool, pool_scale, g_ret, w_branch, w_out, g_ffn2, w_ffn2_gate, w_ffn2_up, w_ffn2_down, g_final):
    caches = (cache_attn_kv_w128, cache_attn_kv_w512, cache_attn_kv_w2048)
    xp = x_prompt.reshape(BATCH * SEQ, D_MODEL)
    xs = jnp.pad(x_sample, ((0, 0), (0, SROWS - DEC_SEQ), (0, 0))).reshape(DEC_BATCH * SROWS, D_MODEL)
    pool_state = jnp.pad(state_pool, ((0, 0), (0, 0), (POOL_PAD - POOL_STATE, 0), (0, 0)))
    tm_p, tm_s = TM_FFN, DEC_BATCH * SROWS
    sample_pos = PAST_LEN + np.arange(CH)
    gate_col0 = N_PLAIN * BW // TN_IN
    qkv1, qkv2 = COL_QKV[1][0], COL_QKV[2][0]

    kv_p = [None, None, None]
    kv_s = [[], [], []]
    pool_p, pool_s, ret_p, ret_s, gv_s = [], [], [], [], []
    yp = ys = None
    for l in range(DEPTH):
        last = l == DEPTH - 1

        xs, hs, *w1 = _ffn(xs, g_ffn1[l], (w_ffn1_gate, w_ffn1_up, w_ffn1_down), g_mix[l], layer=l,
                           emit_x=True, post_dtype=BF16, tm=tm_s)
        zs, gs, win = _in_proj_convert(hs, w_in, layer=l)
        oa, k0, k1, k2 = _attn_sample(zs, caches, rel_bias, l)
        for gi, kv in enumerate((k0, k1, k2)):
            kv_s[gi].append(kv.reshape(DEC_BATCH, SROWS, 2, NH, HD)[:, :DEC_SEQ])
        ob, vn = _gmlp_sample(zs, g_gmlp[l], w_spatial[l], b_spatial[l])
        gv_s.append(vn.reshape(DEC_BATCH, SROWS, BW)[:, :DEC_SEQ])
        oc, pn = _pool_sample(zs, pool_state, w_pool[l], pool_scale[l], start=PAST_LEN, layer=l)
        od, rn = _retention(zs, g_ret[l], state_ret, nseq=DEC_BATCH, per_step=4, rows=SROWS, c_eff=DEC_SEQ,
                            positions=sample_pos, layer=l)
        pool_s.append(pn)
        ret_s.append(rn)
        mix, wb = _branch_merge(oa, ob, oc, od, gs, w_branch, layer=l)
        xs, wo = _out_proj_convert(mix, w_out, xs, layer=l, tn=TN_OUT_S)
        if last:
            ys, *w2 = _ffn(xs, g_ffn2[l], (w_ffn2_gate, w_ffn2_up, w_ffn2_down), g_final, layer=l,
                           emit_x=False, post_dtype=F32, tm=tm_s)
        else:
            xs, *w2 = _ffn(xs, g_ffn2[l], (w_ffn2_gate, w_ffn2_up, w_ffn2_down), g_final, layer=l,
                           emit_x=True, post_dtype=None, tm=tm_s)

        xp, hp = _ffn(xp, g_ffn1[l], w1, g_mix[l], emit_x=True, post_dtype=BF16, tm=tm_p)
        zp = _matmul(hp, win, n=N_PLAIN * BW, tm=TM_IN, tn=TN_IN, name="in_proj")
        gp = _matmul(hp, win, col0=gate_col0, n=GATE_W, gate=True, tm=TM_IN, tn=TN_IN, name="in_proj_gate")
        zq = (zp,
              _matmul_deint(hp, win, col0=qkv1, dil=ATT_DILATIONS[1], name="in_proj_g1"),
              _matmul_deint(hp, win, col0=qkv2, dil=ATT_DILATIONS[2], name="in_proj_g2"))
        parts = []
        for gi in range(3):
            o, lse, kv_p[gi] = _attn_prompt(zq[gi], rel_bias, gi, kv_p[gi])
            parts += [o, lse]
        ob = _gmlp(zp, g_gmlp[l], w_spatial[l], b_spatial[l], rows=MIX_ROWS)
        oc, pn = _pool(zp, w_pool[l], pool_scale[l], rows=MIX_ROWS)
        od, rn = _retention(zp, g_ret[l], None, nseq=BATCH, per_step=BATCH, rows=CH, c_eff=CH,
                            positions=np.arange(SEQ))
        pool_p.append(pn)
        ret_p.append(rn)
        xp = _merge(parts, ob, oc, od, gp, wb, wo, xp)
        if last:
            yp, = _ffn(xp, g_ffn2[l], w2, g_final, emit_x=False, post_dtype=F32, tm=tm_p)
        else:
            xp, = _ffn(xp, g_ffn2[l], w2, g_final, emit_x=True, post_dtype=None, tm=tm_p)

    y_prompt = yp.reshape(BATCH, SEQ, D_MODEL)
    y_sample = ys.reshape(DEC_BATCH, SROWS, D_MODEL)[:, :DEC_SEQ]
    kv_p = [kv.reshape(DEPTH, BATCH, ATT_WINDOWS[gi], 2, NH, HD) for gi, kv in enumerate(kv_p)]
    return (y_prompt, y_sample,
            kv_p[0], kv_p[1], kv_p[2],
            jnp.stack(kv_s[0]), jnp.stack(kv_s[1]), jnp.stack(kv_s[2]),
            jnp.stack(pool_p), jnp.stack(pool_s),
            jnp.stack(ret_p), jnp.stack(ret_s),
            jnp.stack(gv_s))
```

```python
import functools

import numpy as np
import jax
import jax.numpy as jnp
from jax import lax
from jax.experimental import pallas as pl
from jax.experimental.pallas import tpu as pltpu

F32 = jnp.float32
BF16 = jnp.bfloat16

D_MODEL = 2048
BATCH = 4
SEQ = 2048
DEPTH = 2
DEC_BATCH = 32
DEC_SEQ = 4
PAST_LEN = 8192
D_FF = 5632
IN_WIDTH = 16384
EPS = 1e-6
BW = 512
HD = 128
NH = 4
ATT_WINDOWS = (128, 512, 2048)
ATT_DILATIONS = (1, 4, 16)
NKEY = 129
REL_BUCKETS = 32
REL_MAX_DISTANCE = 2048
POOL_WINDOWS = (2, 4, 8, 16)
POOL_STATE = 15
ROPE_BASE = 10000.0
SROWS = 8
CH = 128
NEG = -1e30
LSE_W = 32

W_IN_ORDER = (0, 3, 6) + tuple(range(9, 32)) + (1, 4, 7) + (2, 5, 8)
COL_QKV = ((0, 1, 2), (26, 27, 28), (29, 30, 31))
COL_BU, COL_BV, COL_CIN = 3, 4, 5
COL_DQ, COL_DK, COL_DV, COL_DG = 6, 7, 8, 9
N_PLAIN = 10
N_GATE = 16
GATE_W = N_GATE * BW
ZBLK = IN_WIDTH // BW

VMEM_BIG = 56 * 1024 * 1024
VMEM_FFN = 62 * 1024 * 1024

TM_FFN = 1024
TM_IN, TN_IN = 2048, 1024
TM_MERGE = 256
TN_OUT_S = 512
MIX_ROWS = 8 * CH


def _cparams(n_axes, vmem=None):
    return pltpu.CompilerParams(dimension_semantics=("arbitrary",) * n_axes, vmem_limit_bytes=vmem)


def _rms(x, g):
    return x * lax.rsqrt(jnp.mean(x * x, axis=-1, keepdims=True) + EPS) * g


def _ffn_body(x_ref, gpre_ref, *refs, emit_x, emit_post, convert, nf, tf, row_split):
    refs = list(refs)
    if convert:
        wg_ref, wu_ref, wd_ref, gpost_ref = refs[:4]
        refs = refs[4:]
    else:
        wgu_ref, wd_ref, gpost_ref = refs[:3]
        refs = refs[3:]
    n_act = int(emit_x) + int(emit_post)
    outs = refs[:n_act]
    scratch = refs[n_act + (2 if convert else 0):]
    xn_ref = scratch[0] if scratch else outs[1]
    acc_ref = outs[0]
    f = pl.program_id(1)

    @pl.when(f == 0)
    def _():
        x = x_ref[...]
        xn_ref[...] = _rms(x, gpre_ref[...]).astype(BF16)
        acc_ref[...] = x

    if convert:
        wgu = jnp.concatenate([wg_ref[...].astype(BF16), wu_ref[...].astype(BF16)], axis=1)
        wd = wd_ref[...].astype(BF16)
        refs[n_act][...] = wgu
        refs[n_act + 1][...] = wd
    else:
        wgu, wd = wgu_ref[...], wd_ref[...]
    rows = xn_ref.shape[0] // row_split
    for part in range(row_split):
        rs = slice(part * rows, (part + 1) * rows)
        r = jnp.dot(xn_ref[rs, :], wgu, preferred_element_type=F32)
        g, u = r[:, :tf], r[:, tf:]
        h = (g * jax.nn.sigmoid(g) * (0.5 * u)).astype(BF16)
        acc_ref[rs, :] += jnp.dot(h, wd, preferred_element_type=F32)

    if emit_post:
        @pl.when(f == nf - 1)
        def _():
            outs[-1][...] = _rms(acc_ref[...], gpost_ref[...]).astype(outs[-1].dtype)


def _ffn(x, g_pre, weights, g_post, *, layer=None, emit_x, post_dtype, tm, tf=512):
    m = x.shape[0]
    nf = D_FF // tf
    emit_post = post_dtype is not None
    convert = layer is not None
    assert emit_x or post_dtype == F32, "the first output block is the f32 accumulator"
    once = dict(pipeline_mode=pl.Buffered(1)) if tm > 512 else {}
    row_spec = pl.BlockSpec((tm, D_MODEL), lambda i, f: (i, 0))
    row_out_spec = pl.BlockSpec((tm, D_MODEL), lambda i, f: (i, 0), **once)
    vec_spec = pl.BlockSpec((1, D_MODEL), lambda i, f: (0, 0))
    out_shape, out_specs = [], []
    if emit_x:
        out_shape.append(jax.ShapeDtypeStruct((m, D_MODEL), F32))
        out_specs.append(row_spec)
    post_holds_xn = emit_x and post_dtype == BF16 and tm > 512
    if emit_post:
        out_shape.append(jax.ShapeDtypeStruct((m, D_MODEL), post_dtype))
        out_specs.append(row_out_spec if emit_x and not post_holds_xn else row_spec)
    gu_spec = pl.BlockSpec((D_MODEL, 2 * tf), lambda i, f: (0, f))
    down_spec = pl.BlockSpec((tf, D_MODEL), lambda i, f: (f, 0))
    if convert:
        assert m == tm, "the bf16 weights are written once, by a single row tile"
        w_specs = [pl.BlockSpec((None, D_MODEL, tf), lambda i, f: (layer, 0, f)),
                   pl.BlockSpec((None, D_MODEL, tf), lambda i, f: (layer, 0, f)),
                   pl.BlockSpec((None, tf, D_MODEL), lambda i, f: (layer, f, 0))]
        out_shape += [jax.ShapeDtypeStruct((D_MODEL, 2 * D_FF), BF16), jax.ShapeDtypeStruct((D_FF, D_MODEL), BF16)]
        out_specs += [gu_spec, down_spec]
    else:
        w_specs = [gu_spec, down_spec]
    scratch = [] if post_holds_xn else [pltpu.VMEM((tm, D_MODEL), BF16)]
    return pl.pallas_call(
        functools.partial(_ffn_body, emit_x=emit_x, emit_post=emit_post, convert=convert, nf=nf, tf=tf,
                          row_split=max(tm // 512, 1)),
        grid=(m // tm, nf),
        in_specs=[row_spec, vec_spec, *w_specs, vec_spec],
        out_specs=out_specs,
        out_shape=out_shape,
        scratch_shapes=scratch,
        compiler_params=_cparams(2, VMEM_FFN),
        name="ffn",
    )(x, g_pre.reshape(1, -1), *weights, g_post.reshape(1, -1))


def _mm_body(a_ref, b_ref, o_ref):
    o_ref[...] = jnp.dot(a_ref[...], b_ref[...], preferred_element_type=F32)


def _sigmoid(x):
    return 0.5 * jnp.tanh(0.5 * x) + 0.5


def _mm_gate_body(a_ref, b_ref, o_ref):
    o_ref[...] = _sigmoid(jnp.dot(a_ref[...], b_ref[...], preferred_element_type=F32)).astype(BF16)


def _matmul(a, b, *, col0=0, n, gate=False, tm, tn, name):
    m, k = a.shape
    return pl.pallas_call(
        _mm_gate_body if gate else _mm_body,
        grid=(m // tm, n // tn),
        in_specs=[pl.BlockSpec((tm, k), lambda i, j: (i, 0)),
                  pl.BlockSpec((k, tn), lambda i, j: (0, col0 + j))],
        out_specs=pl.BlockSpec((tm, tn), lambda i, j: (i, j)),
        out_shape=jax.ShapeDtypeStruct((m, n), BF16 if gate else F32),
        compiler_params=_cparams(2, VMEM_BIG),
        name=name,
    )(a, b)


def _out_proj_convert_body(a_ref, b_ref, r_ref, o_ref, w_ref):
    w = b_ref[...].astype(BF16)
    w_ref[...] = w
    o_ref[...] = r_ref[...] + jnp.dot(a_ref[...], w, preferred_element_type=F32)


def _out_proj_convert(a, w_out, res, *, layer, tn):
    m, k = a.shape
    return pl.pallas_call(
        _out_proj_convert_body,
        grid=(D_MODEL // tn,),
        in_specs=[pl.BlockSpec((m, k), lambda j: (0, 0)),
                  pl.BlockSpec((None, k, tn), lambda j: (layer, 0, j)),
                  pl.BlockSpec((m, tn), lambda j: (0, j))],
        out_specs=[pl.BlockSpec((m, tn), lambda j: (0, j)), pl.BlockSpec((k, tn), lambda j: (0, j))],
        out_shape=[jax.ShapeDtypeStruct((m, D_MODEL), F32), jax.ShapeDtypeStruct((k, D_MODEL), BF16)],
        compiler_params=_cparams(1, VMEM_BIG),
        name="out_proj",
    )(a, w_out, res)


DEINT_PARTS = 4


def _mm_deint_body(a_ref, b_ref, o_ref, s_ref, *, dil):
    part = SEQ // DEINT_PARTS
    n = part // dil
    for p in range(DEINT_PARTS):
        rows = slice(p * part, (p + 1) * part)
        r = jnp.dot(a_ref[rows, :], b_ref[...], preferred_element_type=F32)
        for h in range(NH):
            s_ref[h, rows, :] = r[:, h * HD:(h + 1) * HD]
        for h in range(NH):
            for res in range(dil):
                o_ref[h, pl.ds(res * (SEQ // dil) + p * n, n), :] = s_ref[h, pl.ds(p * part + res, n, stride=dil), :]


def _w_in_source_block(j, where=jnp.where):
    return where(j < 3, 3 * j, where(j < 26, j + 6, where(j < 29, 3 * (j - 26) + 1, 3 * (j - 29) + 2)))


assert tuple(_w_in_source_block(np.arange(ZBLK), np.where)) == W_IN_ORDER


def _in_proj_convert_body(a_ref, b_ref, z_ref, gate_ref, w_ref):
    j = pl.program_id(0)
    w = b_ref[...].astype(BF16)
    w_ref[...] = w
    r = jnp.dot(a_ref[...], w, preferred_element_type=F32)
    z_ref[...] = r

    @pl.when((j >= N_PLAIN) & (j < N_PLAIN + N_GATE))
    def _():
        gate_ref[...] = _sigmoid(r).astype(BF16)


def _in_proj_convert(a, w_in, *, layer):
    m, k = a.shape
    return pl.pallas_call(
        _in_proj_convert_body,
        grid=(ZBLK,),
        in_specs=[pl.BlockSpec((m, k), lambda j: (0, 0)),
                  pl.BlockSpec((None, k, BW), lambda j: (layer, 0, _w_in_source_block(j)))],
        out_specs=[pl.BlockSpec((m, BW), lambda j: (0, j)),
                   pl.BlockSpec((m, BW), lambda j: (0, jnp.clip(j - N_PLAIN, 0, N_GATE - 1))),
                   pl.BlockSpec((k, BW), lambda j: (0, j))],
        out_shape=[jax.ShapeDtypeStruct((m, IN_WIDTH), F32),
                   jax.ShapeDtypeStruct((m, GATE_W), BF16),
                   jax.ShapeDtypeStruct((k, IN_WIDTH), BF16)],
        compiler_params=_cparams(1, VMEM_BIG),
        name="in_proj_convert",
    )(a, w_in)


def _matmul_deint(a, b, *, col0, dil, name):
    m, k = a.shape
    return pl.pallas_call(
        functools.partial(_mm_deint_body, dil=dil),
        grid=(BATCH, 3),
        in_specs=[pl.BlockSpec((SEQ, k), lambda b, j: (b, 0)),
                  pl.BlockSpec((k, BW), lambda b, j: (0, col0 + j))],
        out_specs=pl.BlockSpec((None, NH, SEQ, HD), lambda b, j: (j, 0, b, 0)),
        out_shape=jax.ShapeDtypeStruct((3, NH, m, HD), F32),
        scratch_shapes=[pltpu.VMEM((NH, SEQ, HD), F32)],
        compiler_params=_cparams(2, VMEM_BIG),
        name=name,
    )(a, b)


def _branch_body(oa_ref, ob_ref, oc_ref, od_ref, gate_ref, wb_ref, o_ref, wbo_ref):
    acc = None
    for n, br in enumerate((oa_ref, ob_ref, oc_ref, od_ref)):
        w = wb_ref[n].astype(BF16)
        wbo_ref[n] = w
        proj = jnp.dot(br[...], w, preferred_element_type=F32)
        t = gate_ref[:, n * D_MODEL:(n + 1) * D_MODEL].astype(F32) * proj
        acc = t if acc is None else acc + t
    o_ref[...] = acc.astype(BF16)


def _branch_merge(oa, ob, oc, od, gates, w_branch, *, layer):
    m = oa.shape[0]
    once = dict(pipeline_mode=pl.Buffered(1))
    br_spec = pl.BlockSpec((m, BW), lambda i: (0, 0))
    return pl.pallas_call(
        _branch_body,
        grid=(1,),
        in_specs=[br_spec] * 4 + [pl.BlockSpec((m, GATE_W), lambda i: (0, 0)),
                                  pl.BlockSpec((None, 4, BW, D_MODEL), lambda i: (layer, 0, 0, 0), **once)],
        out_specs=[pl.BlockSpec((m, D_MODEL), lambda i: (0, 0)),
                   pl.BlockSpec((4, BW, D_MODEL), lambda i: (0, 0, 0), **once)],
        out_shape=[jax.ShapeDtypeStruct((m, D_MODEL), BF16), jax.ShapeDtypeStruct((4, BW, D_MODEL), BF16)],
        compiler_params=_cparams(1, VMEM_BIG),
        name="branch_merge",
    )(oa, ob, oc, od, gates, w_branch)


def _t5_buckets(dist):
    max_exact = REL_BUCKETS // 2
    d = np.maximum(dist, 1).astype(np.float32)
    large = max_exact + (np.log(d / max_exact) / np.log(REL_MAX_DISTANCE / max_exact)
                         * (REL_BUCKETS - max_exact)).astype(np.int32)
    large = np.minimum(large, REL_BUCKETS - 1)
    return np.where(dist < max_exact, dist, large).astype(np.int32)


def _bucket_of_step(dil):
    return _t5_buckets(dil * np.arange(NKEY))


def _prompt_bucket_matrix(dil):
    iq = np.arange(CH)[:, None]
    col = np.arange(2 * CH)[None, :]
    j = iq + CH - col
    valid = (j >= 0) & (j < NKEY)
    return np.where(valid, _bucket_of_step(dil)[np.clip(j, 0, NKEY - 1)], -1).astype(np.int32)


SAMPLE_CACHE_ROWS = (8 * ATT_WINDOWS[0], 8 * ATT_WINDOWS[1], 8 * DEC_SEQ * CH)


def _sample_bucket_tables():
    h = np.repeat(np.arange(NH), SROWS)[:, None]
    t = np.tile(np.arange(SROWS), NH)[:, None]
    lane = np.arange(CH)[None, :]
    tables = []
    for gi, dil in enumerate(ATT_DILATIONS):
        bos = _bucket_of_step(dil)
        flat = np.arange(SAMPLE_CACHE_ROWS[gi])[None, :]
        head, is_k = flat % NH, (flat // NH) % 2 == 0
        if gi < 2:
            w = flat // 8
            steps = ATT_WINDOWS[gi] + t - w
            valid = (steps % dil == 0) & (steps // dil < NKEY)
            j = steps // dil
        else:
            res, i = (flat // 8) % DEC_SEQ, flat // (8 * DEC_SEQ)
            valid = res == t
            j = np.broadcast_to(CH - i, valid.shape)
        valid = valid & is_k & (head == h) & (t < DEC_SEQ)
        cache_tbl = np.where(valid, bos[np.clip(j, 0, NKEY - 1)], -1)
        jn = t - lane
        validn = (t < DEC_SEQ) & (lane < DEC_SEQ) & (jn >= 0) & ((dil == 1) | (jn == 0))
        new_tbl = np.where(validn, bos[np.clip(jn, 0, NKEY - 1)], -1)
        tables.append((cache_tbl.astype(np.int32), new_tbl.astype(np.int32)))
    return tables


def _bias_from_buckets(bk, rb_ref, col):
    out = jnp.full(bk.shape, NEG, F32)
    for b in range(REL_BUCKETS):
        out = jnp.where(bk == b, rb_ref[b, col], out)
    return out


ATT_SUB = 4


def _attn_prompt_body(rb_ref, bk_ref, q_ref, k_ref, v_ref, *refs, gi, headed):
    dil = ATT_DILATIONS[gi]
    has_prev = gi == 0
    if has_prev:
        kp_ref, vp_ref = refs[:2]
        refs = refs[2:]
    o_ref, lse_ref, kvo_ref, bias_ref = refs[-4:]
    if len(refs) == 5:
        kvo_ref[0] = refs[0][...]
        kvo_ref = kvo_ref.at[1]
    step = pl.program_id(1)

    @pl.when((pl.program_id(0) == 0) & (step == 0))
    def _():
        bk = bk_ref[...]
        for h in range(NH):
            bias_ref[h] = _bias_from_buckets(bk, rb_ref, gi * NH + h)

    def tile(ref, h, t):
        rs = slice(t * CH, (t + 1) * CH)
        return ref[h, rs, :] if headed else ref[rs, h * HD:(h + 1) * HD]

    scale = HD ** -0.5
    lane_head = lax.broadcasted_iota(jnp.int32, (CH, NH * LSE_W), 1) // LSE_W
    if has_prev:
        col = lax.broadcasted_iota(jnp.int32, (CH, 2 * CH), 1)
        no_prev = (step == 0) & (col < CH)
    for t in range(ATT_SUB):
        if gi == 0:
            dst = slice(t * CH, (t + 1) * CH)
        elif gi == 1:
            dst = pl.ds(t * CH * dil + step, CH, stride=dil)
        else:
            dst = pl.ds(step * ATT_SUB + t, CH, stride=dil)
        keep = gi == 2 or t == ATT_SUB - 1
        kv_row0 = (t if gi == 2 else 0) * 2 * NH
        lse_tile = None
        for h in range(NH):
            kc = tile(k_ref, h, t)
            vc = tile(v_ref, h, t)
            if keep:
                kvo_ref[:, kv_row0 + h, :] = kc
                kvo_ref[:, kv_row0 + NH + h, :] = vc
            qh = tile(q_ref, h, t).astype(BF16)
            if gi < 2 and (t > 0 or has_prev):
                if t > 0:
                    kp, vp, bias = tile(k_ref, h, t - 1), tile(v_ref, h, t - 1), bias_ref[h]
                else:
                    kp, vp = kp_ref[:, h * HD:(h + 1) * HD], vp_ref[:, h * HD:(h + 1) * HD]
                    bias = jnp.where(no_prev, NEG, bias_ref[h])
                kh = jnp.concatenate([kp, kc], axis=0).astype(BF16)
                vh = jnp.concatenate([vp, vc], axis=0).astype(BF16)
            else:
                kh = kc.astype(BF16)
                vh = vc.astype(BF16)
                bias = bias_ref[h][:, CH:]
            s = lax.dot_general(qh, kh, (((1,), (1,)), ((), ())), preferred_element_type=F32) * scale + bias
            m = jnp.max(s, axis=-1, keepdims=True)
            p = jnp.exp(s - m)
            den = jnp.sum(p, axis=-1, keepdims=True)
            acc = jnp.dot(p.astype(BF16), vh, preferred_element_type=F32)
            o_ref[h, dst, :] = acc / den
            lse = m + jnp.log(den)
            lse_tile = lse if h == 0 else jnp.where(lane_head == h, lse, lse_tile)
        lse_ref[dst, :] = lse_tile


def _attn_prompt(zq, rel_bias, gi, kv_prev):
    dil = ATT_DILATIONS[gi]
    rows = BATCH * SEQ
    nsteps = SEQ // (ATT_SUB * CH)
    assert SEQ // dil // CH in (1, ATT_SUB, ATT_SUB * nsteps)
    headed = gi > 0
    bk = jnp.asarray(_prompt_bucket_matrix(dil))
    span = ATT_SUB * CH

    def zspec(c):
        if headed:
            return pl.BlockSpec((None, NH, span, HD), lambda b, s: (c, 0, b * nsteps + s, 0))
        return pl.BlockSpec((span, BW), lambda b, s: (b * nsteps + s, COL_QKV[0][c]))

    in_specs = [pl.BlockSpec(memory_space=pltpu.SMEM),
                pl.BlockSpec((CH, 2 * CH), lambda b, s: (0, 0)),
                zspec(0), zspec(1), zspec(2)]
    args = [rel_bias, bk, zq, zq, zq]
    if gi == 0:
        above = lambda c: pl.BlockSpec(
            (CH, BW), lambda b, s: (jnp.maximum((b * nsteps + s) * ATT_SUB - 1, 0), COL_QKV[0][c]))
        in_specs += [above(1), above(2)]
        args += [zq, zq]
        out_rows, out_index = span, (lambda b, s: b * nsteps + s)
    else:
        out_rows, out_index = SEQ, (lambda b, s: b)
    n_res = ATT_SUB if gi == 2 else 1
    kv_blk = (CH, n_res * 2 * NH, HD)
    kv_index = lambda b, s: (b, 0, s if gi else 0, 0)
    kv_shape = (BATCH, CH, dil * 2 * NH, HD)
    if kv_prev is None:
        kv_spec = pl.BlockSpec((None, *kv_blk), kv_index)
    else:
        in_specs.append(pl.BlockSpec((None, *kv_blk), kv_index))
        args.append(kv_prev)
        kv_spec = pl.BlockSpec((2, None, *kv_blk), lambda b, s: (0, *kv_index(b, s)))
        kv_shape = (2, *kv_shape)
    return pl.pallas_call(
        functools.partial(_attn_prompt_body, gi=gi, headed=headed),
        grid=(BATCH, nsteps),
        in_specs=in_specs,
        out_specs=[pl.BlockSpec((NH, out_rows, HD), lambda b, s: (0, out_index(b, s), 0)),
                   pl.BlockSpec((out_rows, NH * LSE_W), lambda b, s: (out_index(b, s), 0)),
                   kv_spec],
        out_shape=[jax.ShapeDtypeStruct((NH, rows, HD), F32),
                   jax.ShapeDtypeStruct((rows, NH * LSE_W), F32),
                   jax.ShapeDtypeStruct(kv_shape, F32)],
        scratch_shapes=[pltpu.VMEM((NH, CH, 2 * CH), F32)],
        compiler_params=_cparams(2),
        name=f"attn_prompt_g{gi}",
    )(*args)


def _merge_body(o0, l0, o1, l1, o2, l2, ob_ref, oc_ref, od_ref, gate_ref, wb_ref, wo_ref, x_ref, out_ref):
    heads = []
    for h in range(NH):
        a0, a1, a2 = (l[:, h * LSE_W:h * LSE_W + 1] for l in (l0, l1, l2))
        m = jnp.maximum(jnp.maximum(a0, a1), a2)
        w0, w1, w2 = jnp.exp(a0 - m), jnp.exp(a1 - m), jnp.exp(a2 - m)
        heads.append(((w0 * o0[h] + w1 * o1[h] + w2 * o2[h]) / (w0 + w1 + w2)).astype(BF16))
    oa = jnp.concatenate(heads, axis=1)
    acc = None
    for n, br in enumerate((oa, ob_ref[...], oc_ref[...], od_ref[...])):
        proj = jnp.dot(br, wb_ref[n], preferred_element_type=F32)
        t = gate_ref[:, n * D_MODEL:(n + 1) * D_MODEL].astype(F32) * proj
        acc = t if acc is None else acc + t
    out_ref[...] = x_ref[...] + jnp.dot(acc.astype(BF16), wo_ref[...], preferred_element_type=F32)


def _merge(parts, ob, oc, od, gates, wb, wo, x, *, tm=TM_MERGE):
    m = x.shape[0]
    once = dict(pipeline_mode=pl.Buffered(1))
    ospec = pl.BlockSpec((NH, tm, HD), lambda i: (0, i, 0))
    lspec = pl.BlockSpec((tm, NH * LSE_W), lambda i: (i, 0))
    br_spec = pl.BlockSpec((tm, BW), lambda i: (i, 0))
    row_spec = pl.BlockSpec((tm, D_MODEL), lambda i: (i, 0))
    return pl.pallas_call(
        _merge_body,
        grid=(m // tm,),
        in_specs=[ospec, lspec] * 3 + [br_spec] * 3 + [
            pl.BlockSpec((tm, GATE_W), lambda i: (i, 0)),
            pl.BlockSpec((4, BW, D_MODEL), lambda i: (0, 0, 0), **once),
            pl.BlockSpec((D_MODEL, D_MODEL), lambda i: (0, 0), **once),
            row_spec],
        out_specs=row_spec,
        out_shape=jax.ShapeDtypeStruct((m, D_MODEL), F32),
        compiler_params=_cparams(1, VMEM_BIG),
        name="merge",
    )(*parts, ob, oc, od, gates, wb, wo, x)


def _attn_sample_body(rb_ref, bc0_ref, bc1_ref, bc2_ref, bn_ref, z_ref, c0_ref, c1_ref, c2_ref,
                      oa_ref, kv0_ref, kv1_ref, kv2_ref, b0_ref, b1_ref, b2_ref, bnew_ref):
    bucket_refs = (bc0_ref, bc1_ref, bc2_ref)
    bias_refs = (b0_ref, b1_ref, b2_ref)

    @pl.when(pl.program_id(0) == 0)
    def _():
        for gi in range(3):
            for h in range(NH):
                rs = slice(h * SROWS, (h + 1) * SROWS)
                bias_refs[gi][rs, :] = _bias_from_buckets(bucket_refs[gi][rs, :], rb_ref, gi * NH + h)
                bnew_ref[gi, rs, :] = _bias_from_buckets(bn_ref[gi, rs, :], rb_ref, gi * NH + h)

    scale = HD ** -0.5
    rows = NH * SROWS
    head_of_row = lax.broadcasted_iota(jnp.int32, (rows, BW), 0) // SROWS
    head_of_lane = lax.broadcasted_iota(jnp.int32, (rows, BW), 1) // HD
    head_mask = head_of_row == head_of_lane
    caches = (c0_ref, c1_ref, c2_ref)
    kv_refs = (kv0_ref, kv1_ref, kv2_ref)
    stats = []
    for gi in range(3):
        cq, ck, cv = COL_QKV[gi]
        q = z_ref[:, cq * BW:(cq + 1) * BW]
        kn = z_ref[:, ck * BW:(ck + 1) * BW]
        vn = z_ref[:, cv * BW:(cv + 1) * BW]
        kv_refs[gi][:, :BW] = kn
        kv_refs[gi][:, BW:] = vn
        qm = jnp.concatenate([q[:, h * HD:(h + 1) * HD] for h in range(NH)], axis=0).astype(BF16)
        kf = caches[gi][...].reshape(SAMPLE_CACHE_ROWS[gi], HD).astype(BF16)
        s = lax.dot_general(qm, kf, (((1,), (1,)), ((), ())), preferred_element_type=F32) * scale + bias_refs[gi][...]
        qrows = jnp.where(head_mask, jnp.concatenate([q] * NH, axis=0), 0.0)
        bias_n = bnew_ref[gi]
        s_new = []
        for tp in range(DEC_SEQ):
            dotp = jnp.sum(qrows * kn[tp:tp + 1, :], axis=-1, keepdims=True)
            s_new.append(dotp * scale + bias_n[:, tp:tp + 1])
        m = jnp.max(s, axis=-1, keepdims=True)
        for sn in s_new:
            m = jnp.maximum(m, sn)
        p = jnp.exp(s - m)
        den = jnp.sum(p, axis=-1, keepdims=True)
        pv = pltpu.roll(p, NH, axis=1).astype(BF16)
        acc = jnp.dot(pv, kf, preferred_element_type=F32)
        acc_n = jnp.zeros((rows, BW), F32)
        for tp, sn in enumerate(s_new):
            pn = jnp.exp(sn - m)
            den = den + pn
            acc_n = acc_n + pn * vn[tp:tp + 1, :]
        acc = acc + jnp.concatenate([acc_n[h * SROWS:(h + 1) * SROWS, h * HD:(h + 1) * HD] for h in range(NH)], axis=0)
        stats.append((m, den, acc))
    mm = jnp.maximum(jnp.maximum(stats[0][0], stats[1][0]), stats[2][0])
    den = jnp.zeros((rows, 1), F32)
    acc = jnp.zeros((rows, HD), F32)
    for m, d, a in stats:
        w = jnp.exp(m - mm)
        den = den + w * d
        acc = acc + w * a
    o = acc / den
    for h in range(NH):
        oa_ref[:, h * HD:(h + 1) * HD] = o[h * SROWS:(h + 1) * SROWS, :].astype(BF16)


def _attn_sample(z, caches, rel_bias, layer):
    tables = _sample_bucket_tables()
    rows = DEC_BATCH * SROWS
    qrows = NH * SROWS
    n0, n1, n2 = SAMPLE_CACHE_ROWS
    c0 = caches[0].reshape(DEPTH, DEC_BATCH, n0, HD)
    c1 = caches[1].reshape(DEPTH, DEC_BATCH, n1, HD)
    c2 = caches[2].reshape(DEPTH, DEC_BATCH, CH, 16 * 8, HD)
    new_tbl = jnp.asarray(np.stack([t[1] for t in tables]))
    kv_spec = pl.BlockSpec((SROWS, 2 * BW), lambda b: (b, 0))
    kv_shape = jax.ShapeDtypeStruct((rows, 2 * BW), F32)
    const2 = lambda b: (0, 0)
    return pl.pallas_call(
        _attn_sample_body,
        grid=(DEC_BATCH,),
        in_specs=[pl.BlockSpec(memory_space=pltpu.SMEM),
                  pl.BlockSpec((qrows, n0), const2), pl.BlockSpec((qrows, n1), const2),
                  pl.BlockSpec((qrows, n2), const2),
                  pl.BlockSpec((3, qrows, CH), lambda b: (0, 0, 0)),
                  pl.BlockSpec((SROWS, IN_WIDTH), lambda b: (b, 0)),
                  pl.BlockSpec((None, None, n0, HD), lambda b: (layer, b, 0, 0)),
                  pl.BlockSpec((None, None, n1, HD), lambda b: (layer, b, 0, 0)),
                  pl.BlockSpec((None, None, CH, 8 * DEC_SEQ, HD), lambda b: (layer, b, 0, 0, 0))],
        out_specs=[pl.BlockSpec((SROWS, BW), lambda b: (b, 0)), kv_spec, kv_spec, kv_spec],
        out_shape=[jax.ShapeDtypeStruct((rows, BW), BF16), kv_shape, kv_shape, kv_shape],
        scratch_shapes=[pltpu.VMEM((qrows, n0), F32), pltpu.VMEM((qrows, n1), F32), pltpu.VMEM((qrows, n2), F32),
                        pltpu.VMEM((3, qrows, CH), F32)],
        compiler_params=_cparams(1),
        name="attn_sample",
    )(rel_bias, jnp.asarray(tables[0][0]), jnp.asarray(tables[1][0]), jnp.asarray(tables[2][0]), new_tbl,
      z, c0, c1, c2)


def _gmlp_body(bu_ref, bv_ref, g_ref, ws_ref, bs_ref, o_ref, *, rows):
    tril = lax.broadcasted_iota(jnp.int32, (CH, CH), 0) >= lax.broadcasted_iota(jnp.int32, (CH, CH), 1)
    w = [jnp.where(tril, ws_ref[g], 0.0).astype(BF16) for g in range(NH)]
    for c in range(rows // CH):
        rs = slice(c * CH, (c + 1) * CH)
        u = jax.nn.gelu(bu_ref[rs, :])
        vn = _rms(jax.nn.gelu(bv_ref[rs, :]), g_ref[...])
        for g in range(NH):
            sl = slice(g * HD, (g + 1) * HD)
            mixed = jnp.dot(w[g], vn[:, sl].astype(BF16), preferred_element_type=F32) + bs_ref[:, sl]
            o_ref[rs, sl] = (u[:, sl] * mixed).astype(BF16)


def _gmlp(z, g_gmlp, w_spatial, b_spatial, *, rows):
    m = z.shape[0]
    bs_full = jnp.repeat(b_spatial.T, HD, axis=1)
    return pl.pallas_call(
        functools.partial(_gmlp_body, rows=rows),
        grid=(m // rows,),
        in_specs=[pl.BlockSpec((rows, BW), lambda i: (i, COL_BU)),
                  pl.BlockSpec((rows, BW), lambda i: (i, COL_BV)),
                  pl.BlockSpec((1, BW), lambda i: (0, 0)),
                  pl.BlockSpec((NH, CH, CH), lambda i: (0, 0, 0)),
                  pl.BlockSpec((CH, BW), lambda i: (0, 0))],
        out_specs=pl.BlockSpec((rows, BW), lambda i: (i, 0)),
        out_shape=jax.ShapeDtypeStruct((m, BW), BF16),
        compiler_params=_cparams(1),
        name="gmlp",
    )(z, z, g_gmlp.reshape(1, -1), w_spatial, bs_full)


POOL_PAD = 16


def _pool_body(x_ref, prev_ref, wp_ref, sc_ref, o_ref, st_ref, ext_ref, *, rows):
    ib = pl.program_id(1)
    prev = jnp.where(ib == 0, 0.0, prev_ref[...])
    x = x_ref[...]
    ext_ref[0:POOL_PAD, :] = prev
    ext_ref[POOL_PAD:POOL_PAD + rows, :] = x
    st_ref[...] = ext_ref[pl.ds(rows + 1, POOL_STATE), :]
    ext = ext_ref[...]
    pos = ib * rows + lax.broadcasted_iota(jnp.int32, (rows, 1), 0)
    for gi, win in enumerate(POOL_WINDOWS):
        sl = slice(gi * HD, (gi + 1) * HD)
        s = ext[:, sl]
        k = 1
        while k < win:
            s = s + pltpu.roll(s, k, axis=0)
            k *= 2
        cnt = jnp.minimum(pos + 1, win).astype(F32)
        diff = s[POOL_PAD:] / cnt - x[:, sl]
        y = jnp.dot(diff.astype(BF16), wp_ref[gi].astype(BF16), preferred_element_type=F32)
        o_ref[:, sl] = (y * sc_ref[:, sl]).astype(BF16)


def _pool(z, w_pool, pool_scale, *, rows):
    m = z.shape[0]
    nblk = SEQ // rows
    per_pad = rows // POOL_PAD
    return pl.pallas_call(
        functools.partial(_pool_body, rows=rows),
        grid=(BATCH, nblk),
        in_specs=[pl.BlockSpec((rows, BW), lambda b, i: (b * nblk + i, COL_CIN)),
                  pl.BlockSpec((POOL_PAD, BW),
                               lambda b, i: (jnp.maximum((b * nblk + i) * per_pad - 1, 0), COL_CIN)),
                  pl.BlockSpec((NH, HD, HD), lambda b, i: (0, 0, 0)),
                  pl.BlockSpec((1, BW), lambda b, i: (0, 0))],
        out_specs=[pl.BlockSpec((rows, BW), lambda b, i: (b * nblk + i, 0)),
                   pl.BlockSpec((None, POOL_STATE, BW), lambda b, i: (b, 0, 0))],
        out_shape=[jax.ShapeDtypeStruct((m, BW), BF16),
                   jax.ShapeDtypeStruct((BATCH, POOL_STATE, BW), F32)],
        scratch_shapes=[pltpu.VMEM((POOL_PAD + rows, BW), F32)],
        compiler_params=_cparams(2),
        name="pool",
    )(z, z, w_pool, pool_scale.reshape(1, -1))


def _gmlp_sample_body(bu_ref, bv_ref, g_ref, wk_ref, bs_ref, o_ref, vn_ref):
    u = jax.nn.gelu(bu_ref[...])
    vn = _rms(jax.nn.gelu(bv_ref[...]), g_ref[...])
    vn_ref[...] = vn
    mixed = jnp.tile(bs_ref[...], (DEC_BATCH, 1))
    for k in range(DEC_SEQ):
        shifted = vn if k == 0 else pltpu.roll(vn, k, axis=0)
        mixed = mixed + jnp.tile(wk_ref[k], (DEC_BATCH, 1)) * shifted
    o_ref[...] = (u * mixed).astype(BF16)


def _gmlp_sample(z, g_gmlp, w_spatial, b_spatial):
    m = z.shape[0]
    t = np.arange(SROWS)
    live = t < DEC_SEQ
    pick = np.zeros((DEC_SEQ, SROWS, SROWS), np.float32)
    for k in range(DEC_SEQ):
        pick[k, t[live & (t >= k)], t[live & (t >= k)] - k] = 1.0
    diag = jnp.sum(w_spatial[None, :, :SROWS, :SROWS] * pick[:, None], axis=-1)
    wk = jnp.repeat(diag.transpose(0, 2, 1), HD, axis=2)
    bs = jnp.repeat((b_spatial[:, :SROWS] * jnp.asarray(live, F32)).T, HD, axis=1)
    return pl.pallas_call(
        _gmlp_sample_body,
        grid=(1,),
        in_specs=[pl.BlockSpec((m, BW), lambda i: (0, COL_BU)),
                  pl.BlockSpec((m, BW), lambda i: (0, COL_BV)),
                  pl.BlockSpec((1, BW), lambda i: (0, 0)),
                  pl.BlockSpec((DEC_SEQ, SROWS, BW), lambda i: (0, 0, 0)),
                  pl.BlockSpec((SROWS, BW), lambda i: (0, 0))],
        out_specs=[pl.BlockSpec((m, BW), lambda i: (0, 0))] * 2,
        out_shape=[jax.ShapeDtypeStruct((m, BW), BF16), jax.ShapeDtypeStruct((m, BW), F32)],
        compiler_params=_cparams(1),
        name="gmlp_sample",
    )(z, z, g_gmlp.reshape(1, -1), wk, bs)


def _pool_sample_body(x_ref, prev_ref, wp_ref, sc_ref, o_ref, st_ref, ext_ref, *, start):
    per = POOL_PAD + SROWS
    ext_ref[:, 0:POOL_PAD, :] = prev_ref[...]
    ext_ref[:, POOL_PAD:per, :] = x_ref[...]
    st_ref[...] = ext_ref[:, pl.ds(DEC_SEQ + 1, POOL_STATE), :]
    ext = ext_ref[...].reshape(DEC_BATCH * per, BW)
    pos = start + (lax.broadcasted_iota(jnp.int32, (DEC_BATCH * per, 1), 0) % per - POOL_PAD)
    for gi, win in enumerate(POOL_WINDOWS):
        sl = slice(gi * HD, (gi + 1) * HD)
        x = ext[:, sl]
        s = x
        k = 1
        while k < win:
            s = s + pltpu.roll(s, k, axis=0)
            k *= 2
        cnt = jnp.minimum(jnp.maximum(pos, 0) + 1, win).astype(F32)
        y = jnp.dot((s / cnt - x).astype(BF16), wp_ref[gi].astype(BF16), preferred_element_type=F32)
        o_ref[:, :, sl] = (y * sc_ref[:, sl]).reshape(DEC_BATCH, per, HD)[:, POOL_PAD:, :].astype(BF16)


def _pool_sample(z, prev, w_pool, pool_scale, *, start, layer):
    m, zw = z.shape
    z3 = z.reshape(DEC_BATCH, SROWS, zw)
    o, st = pl.pallas_call(
        functools.partial(_pool_sample_body, start=start),
        grid=(1,),
        in_specs=[pl.BlockSpec((DEC_BATCH, SROWS, BW), lambda i: (0, 0, COL_CIN)),
                  pl.BlockSpec((None, DEC_BATCH, POOL_PAD, BW), lambda i: (layer, 0, 0, 0)),
                  pl.BlockSpec((NH, HD, HD), lambda i: (0, 0, 0)),
                  pl.BlockSpec((1, BW), lambda i: (0, 0))],
        out_specs=[pl.BlockSpec((DEC_BATCH, SROWS, BW), lambda i: (0, 0, 0)),
                   pl.BlockSpec((DEC_BATCH, POOL_STATE, BW), lambda i: (0, 0, 0))],
        out_shape=[jax.ShapeDtypeStruct((DEC_BATCH, SROWS, BW), BF16),
                   jax.ShapeDtypeStruct((DEC_BATCH, POOL_STATE, BW), F32)],
        scratch_shapes=[pltpu.VMEM((DEC_BATCH, POOL_PAD + SROWS, BW), F32)],
        compiler_params=_cparams(1),
        name="pool_sample",
    )(z3, prev, w_pool, pool_scale.reshape(1, -1))
    return o.reshape(m, BW), st


def _ret_tables(c_eff, positions, size):
    lg = np.log1p(-np.power(2.0, -5.0 - np.arange(NH, dtype=np.float64)))
    i = np.arange(size, dtype=np.float64)
    live = (i < c_eff)
    diff = i[:, None] - i[None, :]
    inner = np.where((diff >= 0) & live[:, None] & live[None, :], np.exp(np.maximum(diff, 0.0)[None] * lg[:, None, None]), 0.0)
    qd = np.where(live[None, :], np.exp((i + 1.0)[None, :] * lg[:, None]), 0.0)
    kd = np.where(live[None, :], np.exp((c_eff - 1.0 - i)[None, :] * lg[:, None]), 0.0)
    chunk = tuple(float(v) for v in np.exp(c_eff * lg))
    qd_full = np.repeat(qd.T, HD, axis=1)
    kd_full = np.repeat(kd.T, HD, axis=1) * (HD ** -0.5)
    half = HD // 2
    inv = ROPE_BASE ** (-np.arange(half, dtype=np.float64) / half)
    ang = np.asarray(positions, np.float64)[:, None] * inv[None, :]
    cosf = np.concatenate([np.cos(ang), np.cos(ang)], axis=1)
    sinf = np.concatenate([-np.sin(ang), np.sin(ang)], axis=1)
    to32 = lambda a: jnp.asarray(a.astype(np.float32))
    return to32(inner), to32(qd_full), to32(kd_full), chunk, to32(cosf), to32(sinf)


def _ret_body(q_ref, k_ref, v_ref, g_ref, cos_ref, sin_ref, inner_ref, qd_ref, kd_ref, gr_ref, *refs,
              nseq, rows, chunk_decay, has_state):
    refs = list(refs)
    s0_ref = refs.pop(0) if has_state else None
    o_ref, sn_ref, s_ref = refs[:3]
    pad_ref = refs[3] if rows < CH else None
    ic = pl.program_id(1)

    @pl.when(ic == 0)
    def _():
        if has_state:
            s_ref[...] = s0_ref[...]
        else:
            s_ref[...] = jnp.zeros_like(s_ref)

    def chunk(ref, b, k):
        if rows >= CH:
            return ref[b]
        pad_ref[k] = jnp.zeros((CH, BW), F32)
        pad_ref[k, 0:rows, :] = ref[b]
        return pad_ref[k]

    cosf = cos_ref[...]
    sinf = sin_ref[...]
    for b in range(nseq):
        q = chunk(q_ref, b, 0)
        k = chunk(k_ref, b, 1)
        v = chunk(v_ref, b, 2)
        gate = chunk(g_ref, b, 3)
        for h in range(NH):
            sl = slice(h * HD, (h + 1) * HD)
            qh = q[:, sl]
            kh = k[:, sl]
            rq = qh * cosf + pltpu.roll(qh, HD // 2, axis=1) * sinf
            rk = kh * cosf + pltpu.roll(kh, HD // 2, axis=1) * sinf
            vb = v[:, sl].astype(BF16)
            rqb = rq.astype(BF16)
            state = s_ref[b, h]
            att = lax.dot_general(rqb, (rk * (HD ** -0.5)).astype(BF16), (((1,), (1,)), ((), ())),
                                  preferred_element_type=F32) * inner_ref[h]
            o = (jnp.dot(att.astype(BF16), vb, preferred_element_type=F32)
                 + jnp.dot(rqb, state.astype(BF16), preferred_element_type=F32) * qd_ref[:, sl])
            kdec = (rk * kd_ref[:, sl]).astype(BF16)
            new_state = state * chunk_decay[h] + lax.dot_general(
                kdec, vb, (((0,), (0,)), ((), ())), preferred_element_type=F32)
            s_ref[b, h] = new_state
            sn_ref[b, h] = new_state
            o = o * lax.rsqrt(jnp.mean(o * o, axis=-1, keepdims=True) + EPS)
            gt = gate[:, sl]
            o_ref[b, :, sl] = (o * gr_ref[:, sl] * (gt * jax.nn.sigmoid(gt)))[:rows].astype(BF16)


def _retention(z, g_ret, state, *, nseq, per_step, rows, c_eff, positions, layer=None):
    m, zw = z.shape
    seq_rows = m // nseq
    nchunk = seq_rows // rows
    size = max(rows, CH)
    inner, qd, kd, chunk_decay, cosf, sinf = _ret_tables(c_eff, positions, size)
    has_state = state is not None
    z3 = z.reshape(nseq, seq_rows, zw)

    def zspec(col):
        return pl.BlockSpec((per_step, rows, BW), lambda g, i: (g, i, col))

    const2 = lambda g, i: (0, 0)
    in_specs = [zspec(COL_DQ), zspec(COL_DK), zspec(COL_DV), zspec(COL_DG),
                pl.BlockSpec((size, HD), lambda g, i: (i, 0)), pl.BlockSpec((size, HD), lambda g, i: (i, 0)),
                pl.BlockSpec((NH, size, size), lambda g, i: (0, 0, 0)),
                pl.BlockSpec((size, BW), const2), pl.BlockSpec((size, BW), const2), pl.BlockSpec((1, BW), const2)]
    args = [z3, z3, z3, z3, cosf, sinf, inner, qd, kd, g_ret.reshape(1, -1)]
    if has_state:
        in_specs.append(pl.BlockSpec((None, per_step, NH, HD, HD), lambda g, i: (layer, g, 0, 0, 0)))
        args.append(state)
    scratch = [pltpu.VMEM((per_step, NH, HD, HD), F32)]
    if rows < CH:
        scratch.append(pltpu.VMEM((4, CH, BW), F32))
    o, sn = pl.pallas_call(
        functools.partial(_ret_body, nseq=per_step, rows=rows, chunk_decay=chunk_decay, has_state=has_state),
        grid=(nseq // per_step, nchunk),
        in_specs=in_specs,
        out_specs=[pl.BlockSpec((per_step, rows, BW), lambda g, i: (g, i, 0)),
                   pl.BlockSpec((per_step, NH, HD, HD), lambda g, i: (g, 0, 0, 0))],
        out_shape=[jax.ShapeDtypeStruct((nseq, seq_rows, BW), BF16),
                   jax.ShapeDtypeStruct((nseq, NH, HD, HD), F32)],
        scratch_shapes=scratch,
        compiler_params=_cparams(2),
        name="retention",
    )(*args)
    return o.reshape(m, BW), sn


def kernel(x_prompt, x_sample, cache_attn_kv_w128, cache_attn_kv_w512, cache_attn_kv_w2048, state_pool, state_ret, rel_bias, g_ffn1, w_ffn1_gate, w_ffn1_up, w_ffn1_down, g_mix, w_in, g_gmlp, w_spatial, b_spatial, w_pool, pool_scale, g_ret, w_branch, w_out, g_ffn2, w_ffn2_gate, w_ffn2_up, w_ffn2_down, g_final):
    caches = (cache_attn_kv_w128, cache_attn_kv_w512, cache_attn_kv_w2048)
    xp = x_prompt.reshape(BATCH * SEQ, D_MODEL)
    xs = jnp.pad(x_sample, ((0, 0), (0, SROWS - DEC_SEQ), (0, 0))).reshape(DEC_BATCH * SROWS, D_MODEL)
    pool_state = jnp.pad(state_pool, ((0, 0), (0, 0), (POOL_PAD - POOL_STATE, 0), (0, 0)))
    tm_p, tm_s = TM_FFN, DEC_BATCH * SROWS
    sample_pos = PAST_LEN + np.arange(CH)
    gate_col0 = N_PLAIN * BW // TN_IN
    qkv1, qkv2 = COL_QKV[1][0], COL_QKV[2][0]

    kv_p = [None, None, None]
    kv_s = [[], [], []]
    pool_p, pool_s, ret_p, ret_s, gv_s = [], [], [], [], []
    yp = ys = None
    for l in range(DEPTH):
        last = l == DEPTH - 1

        xs, hs, *w1 = _ffn(xs, g_ffn1[l], (w_ffn1_gate, w_ffn1_up, w_ffn1_down), g_mix[l], layer=l,
                           emit_x=True, post_dtype=BF16, tm=tm_s)
        zs, gs, win = _in_proj_convert(hs, w_in, layer=l)
        oa, k0, k1, k2 = _attn_sample(zs, caches, rel_bias, l)
        for gi, kv in enumerate((k0, k1, k2)):
            kv_s[gi].append(kv.reshape(DEC_BATCH, SROWS, 2, NH, HD)[:, :DEC_SEQ])
        ob, vn = _gmlp_sample(zs, g_gmlp[l], w_spatial[l], b_spatial[l])
        gv_s.append(vn.reshape(DEC_BATCH, SROWS, BW)[:, :DEC_SEQ])
        oc, pn = _pool_sample(zs, pool_state, w_pool[l], pool_scale[l], start=PAST_LEN, layer=l)
        od, rn = _retention(zs, g_ret[l], state_ret, nseq=DEC_BATCH, per_step=4, rows=SROWS, c_eff=DEC_SEQ,
                            positions=sample_pos, layer=l)
        pool_s.append(pn)
        ret_s.append(rn)
        mix, wb = _branch_merge(oa, ob, oc, od, gs, w_branch, layer=l)
        xs, wo = _out_proj_convert(mix, w_out, xs, layer=l, tn=TN_OUT_S)
        if last:
            ys, *w2 = _ffn(xs, g_ffn2[l], (w_ffn2_gate, w_ffn2_up, w_ffn2_down), g_final, layer=l,
                           emit_x=False, post_dtype=F32, tm=tm_s)
        else:
            xs, *w2 = _ffn(xs, g_ffn2[l], (w_ffn2_gate, w_ffn2_up, w_ffn2_down), g_final, layer=l,
                           emit_x=True, post_dtype=None, tm=tm_s)

        xp, hp = _ffn(xp, g_ffn1[l], w1, g_mix[l], emit_x=True, post_dtype=BF16, tm=tm_p)
        zp = _matmul(hp, win, n=N_PLAIN * BW, tm=TM_IN, tn=TN_IN, name="in_proj")
        gp = _matmul(hp, win, col0=gate_col0, n=GATE_W, gate=True, tm=TM_IN, tn=TN_IN, name="in_proj_gate")
        zq = (zp,
              _matmul_deint(hp, win, col0=qkv1, dil=ATT_DILATIONS[1], name="in_proj_g1"),
              _matmul_deint(hp, win, col0=qkv2, dil=ATT_DILATIONS[2], name="in_proj_g2"))
        parts = []
        for gi in range(3):
            o, lse, kv_p[gi] = _attn_prompt(zq[gi], rel_bias, gi, kv_p[gi])
            parts += [o, lse]
        ob = _gmlp(zp, g_gmlp[l], w_spatial[l], b_spatial[l], rows=MIX_ROWS)
        oc, pn = _pool(zp, w_pool[l], pool_scale[l], rows=MIX_ROWS)
        od, rn = _retention(zp, g_ret[l], None, nseq=BATCH, per_step=BATCH, rows=CH, c_eff=CH,
                            positions=np.arange(SEQ))
        pool_p.append(pn)
        ret_p.append(rn)
        xp = _merge(parts, ob, oc, od, gp, wb, wo, xp)
        if last:
            yp, = _ffn(xp, g_ffn2[l], w2, g_final, emit_x=False, post_dtype=F32, tm=tm_p)
        else:
            xp, = _ffn(xp, g_ffn2[l], w2, g_final, emit_x=True, post_dtype=None, tm=tm_p)

    y_prompt = yp.reshape(BATCH, SEQ, D_MODEL)
    y_sample = ys.reshape(DEC_BATCH, SROWS, D_MODEL)[:, :DEC_SEQ]
    kv_p = [kv.reshape(DEPTH, BATCH, ATT_WINDOWS[gi], 2, NH, HD) for gi, kv in enumerate(kv_p)]
    return (y_prompt, y_sample,
            kv_p[0], kv_p[1], kv_p[2],
            jnp.stack(kv_s[0]), jnp.stack(kv_s[1]), jnp.stack(kv_s[2]),
            jnp.stack(pool_p), jnp.stack(pool_s),
            jnp.stack(ret_p), jnp.stack(ret_s),
            jnp.stack(gv_s))
```

```python
import functools

import numpy as np
import jax
import jax.numpy as jnp
from jax import lax
from jax.experimental import pallas as pl
from jax.experimental.pallas import tpu as pltpu

F32 = jnp.float32
BF16 = jnp.bfloat16

D_MODEL = 2048
BATCH = 4
SEQ = 2048
DEPTH = 2
DEC_BATCH = 32
DEC_SEQ = 4
PAST_LEN = 8192
D_FF = 5632
IN_WIDTH = 16384
EPS = 1e-6
BW = 512
HD = 128
NH = 4
ATT_WINDOWS = (128, 512, 2048)
ATT_DILATIONS = (1, 4, 16)
NKEY = 129
REL_BUCKETS = 32
REL_MAX_DISTANCE = 2048
POOL_WINDOWS = (2, 4, 8, 16)
POOL_STATE = 15
ROPE_BASE = 10000.0
SROWS = 8
CH = 128
NEG = -1e30
LSE_W = 32

W_IN_ORDER = (0, 3, 6) + tuple(range(9, 32)) + (1, 4, 7) + (2, 5, 8)
COL_QKV = ((0, 1, 2), (26, 27, 28), (29, 30, 31))
COL_BU, COL_BV, COL_CIN = 3, 4, 5
COL_DQ, COL_DK, COL_DV, COL_DG = 6, 7, 8, 9
N_PLAIN = 10
N_GATE = 16
GATE_W = N_GATE * BW
ZBLK = IN_WIDTH // BW

VMEM_BIG = 56 * 1024 * 1024
VMEM_FFN = 62 * 1024 * 1024

TM_FFN = 1024
TM_IN, TN_IN = 2048, 1024
TM_MERGE = 256
TN_OUT_S = 512
MIX_ROWS = SEQ


def _cparams(n_axes, vmem=None):
    return pltpu.CompilerParams(dimension_semantics=("arbitrary",) * n_axes, vmem_limit_bytes=vmem)


def _rms(x, g):
    return x * lax.rsqrt(jnp.mean(x * x, axis=-1, keepdims=True) + EPS) * g


def _ffn_body(x_ref, gpre_ref, *refs, emit_x, emit_post, convert, nf, tf, row_split):
    refs = list(refs)
    if convert:
        wg_ref, wu_ref, wd_ref, gpost_ref = refs[:4]
        refs = refs[4:]
    else:
        wgu_ref, wd_ref, gpost_ref = refs[:3]
        refs = refs[3:]
    n_act = int(emit_x) + int(emit_post)
    outs = refs[:n_act]
    scratch = refs[n_act + (2 if convert else 0):]
    xn_ref = scratch[0] if scratch else outs[1]
    acc_ref = outs[0]
    f = pl.program_id(1)

    @pl.when(f == 0)
    def _():
        x = x_ref[...]
        xn_ref[...] = _rms(x, gpre_ref[...]).astype(BF16)
        acc_ref[...] = x

    if convert:
        wgu = jnp.concatenate([wg_ref[...].astype(BF16), wu_ref[...].astype(BF16)], axis=1)
        wd = wd_ref[...].astype(BF16)
        refs[n_act][...] = wgu
        refs[n_act + 1][...] = wd
    else:
        wgu, wd = wgu_ref[...], wd_ref[...]
    rows = xn_ref.shape[0] // row_split
    for part in range(row_split):
        rs = slice(part * rows, (part + 1) * rows)
        r = jnp.dot(xn_ref[rs, :], wgu, preferred_element_type=F32)
        g, u = r[:, :tf], r[:, tf:]
        h = (g * jax.nn.sigmoid(g) * (0.5 * u)).astype(BF16)
        acc_ref[rs, :] += jnp.dot(h, wd, preferred_element_type=F32)

    if emit_post:
        @pl.when(f == nf - 1)
        def _():
            outs[-1][...] = _rms(acc_ref[...], gpost_ref[...]).astype(outs[-1].dtype)


def _ffn(x, g_pre, weights, g_post, *, layer=None, emit_x, post_dtype, tm, tf=512):
    m = x.shape[0]
    nf = D_FF // tf
    emit_post = post_dtype is not None
    convert = layer is not None
    assert emit_x or post_dtype == F32, "the first output block is the f32 accumulator"
    once = dict(pipeline_mode=pl.Buffered(1)) if tm > 512 else {}
    row_spec = pl.BlockSpec((tm, D_MODEL), lambda i, f: (i, 0))
    row_out_spec = pl.BlockSpec((tm, D_MODEL), lambda i, f: (i, 0), **once)
    vec_spec = pl.BlockSpec((1, D_MODEL), lambda i, f: (0, 0))
    out_shape, out_specs = [], []
    if emit_x:
        out_shape.append(jax.ShapeDtypeStruct((m, D_MODEL), F32))
        out_specs.append(row_spec)
    post_holds_xn = emit_x and post_dtype == BF16 and tm > 512
    if emit_post:
        out_shape.append(jax.ShapeDtypeStruct((m, D_MODEL), post_dtype))
        out_specs.append(row_out_spec if emit_x and not post_holds_xn else row_spec)
    gu_spec = pl.BlockSpec((D_MODEL, 2 * tf), lambda i, f: (0, f))
    down_spec = pl.BlockSpec((tf, D_MODEL), lambda i, f: (f, 0))
    if convert:
        assert m == tm, "the bf16 weights are written once, by a single row tile"
        w_specs = [pl.BlockSpec((None, D_MODEL, tf), lambda i, f: (layer, 0, f)),
                   pl.BlockSpec((None, D_MODEL, tf), lambda i, f: (layer, 0, f)),
                   pl.BlockSpec((None, tf, D_MODEL), lambda i, f: (layer, f, 0))]
        out_shape += [jax.ShapeDtypeStruct((D_MODEL, 2 * D_FF), BF16), jax.ShapeDtypeStruct((D_FF, D_MODEL), BF16)]
        out_specs += [gu_spec, down_spec]
    else:
        w_specs = [gu_spec, down_spec]
    scratch = [] if post_holds_xn else [pltpu.VMEM((tm, D_MODEL), BF16)]
    return pl.pallas_call(
        functools.partial(_ffn_body, emit_x=emit_x, emit_post=emit_post, convert=convert, nf=nf, tf=tf,
                          row_split=max(tm // 512, 1)),
        grid=(m // tm, nf),
        in_specs=[row_spec, vec_spec, *w_specs, vec_spec],
        out_specs=out_specs,
        out_shape=out_shape,
        scratch_shapes=scratch,
        compiler_params=_cparams(2, VMEM_FFN),
        name="ffn",
    )(x, g_pre.reshape(1, -1), *weights, g_post.reshape(1, -1))


def _mm_body(a_ref, b_ref, o_ref):
    o_ref[...] = jnp.dot(a_ref[...], b_ref[...], preferred_element_type=F32)


def _sigmoid(x):
    return 0.5 * jnp.tanh(0.5 * x) + 0.5


def _mm_gate_body(a_ref, b_ref, o_ref):
    o_ref[...] = _sigmoid(jnp.dot(a_ref[...], b_ref[...], preferred_element_type=F32)).astype(BF16)


def _matmul(a, b, *, col0=0, n, gate=False, tm, tn, name):
    m, k = a.shape
    return pl.pallas_call(
        _mm_gate_body if gate else _mm_body,
        grid=(m // tm, n // tn),
        in_specs=[pl.BlockSpec((tm, k), lambda i, j: (i, 0)),
                  pl.BlockSpec((k, tn), lambda i, j: (0, col0 + j))],
        out_specs=pl.BlockSpec((tm, tn), lambda i, j: (i, j)),
        out_shape=jax.ShapeDtypeStruct((m, n), BF16 if gate else F32),
        compiler_params=_cparams(2, VMEM_BIG),
        name=name,
    )(a, b)


def _out_proj_convert_body(a_ref, b_ref, r_ref, o_ref, w_ref):
    w = b_ref[...].astype(BF16)
    w_ref[...] = w
    o_ref[...] = r_ref[...] + jnp.dot(a_ref[...], w, preferred_element_type=F32)


def _out_proj_convert(a, w_out, res, *, layer, tn):
    m, k = a.shape
    return pl.pallas_call(
        _out_proj_convert_body,
        grid=(D_MODEL // tn,),
        in_specs=[pl.BlockSpec((m, k), lambda j: (0, 0)),
                  pl.BlockSpec((None, k, tn), lambda j: (layer, 0, j)),
                  pl.BlockSpec((m, tn), lambda j: (0, j))],
        out_specs=[pl.BlockSpec((m, tn), lambda j: (0, j)), pl.BlockSpec((k, tn), lambda j: (0, j))],
        out_shape=[jax.ShapeDtypeStruct((m, D_MODEL), F32), jax.ShapeDtypeStruct((k, D_MODEL), BF16)],
        compiler_params=_cparams(1, VMEM_BIG),
        name="out_proj",
    )(a, w_out, res)


DEINT_PARTS = 4


def _mm_deint_body(a_ref, b_ref, o_ref, s_ref, *, dil):
    part = SEQ // DEINT_PARTS
    n = part // dil
    for p in range(DEINT_PARTS):
        rows = slice(p * part, (p + 1) * part)
        r = jnp.dot(a_ref[rows, :], b_ref[...], preferred_element_type=F32)
        for h in range(NH):
            s_ref[h, rows, :] = r[:, h * HD:(h + 1) * HD]
        for h in range(NH):
            for res in range(dil):
                o_ref[h, pl.ds(res * (SEQ // dil) + p * n, n), :] = s_ref[h, pl.ds(p * part + res, n, stride=dil), :]


def _w_in_source_block(j, where=jnp.where):
    return where(j < 3, 3 * j, where(j < 26, j + 6, where(j < 29, 3 * (j - 26) + 1, 3 * (j - 29) + 2)))


assert tuple(_w_in_source_block(np.arange(ZBLK), np.where)) == W_IN_ORDER


def _in_proj_convert_body(a_ref, b_ref, z_ref, gate_ref, w_ref):
    j = pl.program_id(0)
    w = b_ref[...].astype(BF16)
    w_ref[...] = w
    r = jnp.dot(a_ref[...], w, preferred_element_type=F32)
    z_ref[...] = r

    @pl.when((j >= N_PLAIN) & (j < N_PLAIN + N_GATE))
    def _():
        gate_ref[...] = _sigmoid(r).astype(BF16)


def _in_proj_convert(a, w_in, *, layer):
    m, k = a.shape
    return pl.pallas_call(
        _in_proj_convert_body,
        grid=(ZBLK,),
        in_specs=[pl.BlockSpec((m, k), lambda j: (0, 0)),
                  pl.BlockSpec((None, k, BW), lambda j: (layer, 0, _w_in_source_block(j)))],
        out_specs=[pl.BlockSpec((m, BW), lambda j: (0, j)),
                   pl.BlockSpec((m, BW), lambda j: (0, jnp.clip(j - N_PLAIN, 0, N_GATE - 1))),
                   pl.BlockSpec((k, BW), lambda j: (0, j))],
        out_shape=[jax.ShapeDtypeStruct((m, IN_WIDTH), F32),
                   jax.ShapeDtypeStruct((m, GATE_W), BF16),
                   jax.ShapeDtypeStruct((k, IN_WIDTH), BF16)],
        compiler_params=_cparams(1, VMEM_BIG),
        name="in_proj_convert",
    )(a, w_in)


def _matmul_deint(a, b, *, col0, dil, name):
    m, k = a.shape
    return pl.pallas_call(
        functools.partial(_mm_deint_body, dil=dil),
        grid=(BATCH, 3),
        in_specs=[pl.BlockSpec((SEQ, k), lambda b, j: (b, 0)),
                  pl.BlockSpec((k, BW), lambda b, j: (0, col0 + j))],
        out_specs=pl.BlockSpec((None, NH, SEQ, HD), lambda b, j: (j, 0, b, 0)),
        out_shape=jax.ShapeDtypeStruct((3, NH, m, HD), F32),
        scratch_shapes=[pltpu.VMEM((NH, SEQ, HD), F32)],
        compiler_params=_cparams(2, VMEM_BIG),
        name=name,
    )(a, b)


def _branch_body(oa_ref, ob_ref, oc_ref, od_ref, gate_ref, wb_ref, o_ref, wbo_ref):
    acc = None
    for n, br in enumerate((oa_ref, ob_ref, oc_ref, od_ref)):
        w = wb_ref[n].astype(BF16)
        wbo_ref[n] = w
        proj = jnp.dot(br[...], w, preferred_element_type=F32)
        t = gate_ref[:, n * D_MODEL:(n + 1) * D_MODEL].astype(F32) * proj
        acc = t if acc is None else acc + t
    o_ref[...] = acc.astype(BF16)


def _branch_merge(oa, ob, oc, od, gates, w_branch, *, layer):
    m = oa.shape[0]
    once = dict(pipeline_mode=pl.Buffered(1))
    br_spec = pl.BlockSpec((m, BW), lambda i: (0, 0))
    return pl.pallas_call(
        _branch_body,
        grid=(1,),
        in_specs=[br_spec] * 4 + [pl.BlockSpec((m, GATE_W), lambda i: (0, 0)),
                                  pl.BlockSpec((None, 4, BW, D_MODEL), lambda i: (layer, 0, 0, 0), **once)],
        out_specs=[pl.BlockSpec((m, D_MODEL), lambda i: (0, 0)),
                   pl.BlockSpec((4, BW, D_MODEL), lambda i: (0, 0, 0), **once)],
        out_shape=[jax.ShapeDtypeStruct((m, D_MODEL), BF16), jax.ShapeDtypeStruct((4, BW, D_MODEL), BF16)],
        compiler_params=_cparams(1, VMEM_BIG),
        name="branch_merge",
    )(oa, ob, oc, od, gates, w_branch)


def _t5_buckets(dist):
    max_exact = REL_BUCKETS // 2
    d = np.maximum(dist, 1).astype(np.float32)
    large = max_exact + (np.log(d / max_exact) / np.log(REL_MAX_DISTANCE / max_exact)
                         * (REL_BUCKETS - max_exact)).astype(np.int32)
    large = np.minimum(large, REL_BUCKETS - 1)
    return np.where(dist < max_exact, dist, large).astype(np.int32)


def _bucket_of_step(dil):
    return _t5_buckets(dil * np.arange(NKEY))


def _prompt_bucket_matrix(dil):
    iq = np.arange(CH)[:, None]
    col = np.arange(2 * CH)[None, :]
    j = iq + CH - col
    valid = (j >= 0) & (j < NKEY)
    return np.where(valid, _bucket_of_step(dil)[np.clip(j, 0, NKEY - 1)], -1).astype(np.int32)


SAMPLE_CACHE_ROWS = (8 * ATT_WINDOWS[0], 8 * ATT_WINDOWS[1], 8 * DEC_SEQ * CH)


def _sample_bucket_tables():
    h = np.repeat(np.arange(NH), SROWS)[:, None]
    t = np.tile(np.arange(SROWS), NH)[:, None]
    lane = np.arange(CH)[None, :]
    tables = []
    for gi, dil in enumerate(ATT_DILATIONS):
        bos = _bucket_of_step(dil)
        flat = np.arange(SAMPLE_CACHE_ROWS[gi])[None, :]
        head, is_k = flat % NH, (flat // NH) % 2 == 0
        if gi < 2:
            w = flat // 8
            steps = ATT_WINDOWS[gi] + t - w
            valid = (steps % dil == 0) & (steps // dil < NKEY)
            j = steps // dil
        else:
            res, i = (flat // 8) % DEC_SEQ, flat // (8 * DEC_SEQ)
            valid = res == t
            j = np.broadcast_to(CH - i, valid.shape)
        valid = valid & is_k & (head == h) & (t < DEC_SEQ)
        cache_tbl = np.where(valid, bos[np.clip(j, 0, NKEY - 1)], -1)
        jn = t - lane
        validn = (t < DEC_SEQ) & (lane < DEC_SEQ) & (jn >= 0) & ((dil == 1) | (jn == 0))
        new_tbl = np.where(validn, bos[np.clip(jn, 0, NKEY - 1)], -1)
        tables.append((cache_tbl.astype(np.int32), new_tbl.astype(np.int32)))
    return tables


def _bias_from_buckets(bk, rb_ref, col):
    out = jnp.full(bk.shape, NEG, F32)
    for b in range(REL_BUCKETS):
        out = jnp.where(bk == b, rb_ref[b, col], out)
    return out


ATT_SUB = 4


def _attn_prompt_body(rb_ref, bk_ref, q_ref, k_ref, v_ref, *refs, gi, headed):
    dil = ATT_DILATIONS[gi]
    has_prev = gi == 0
    if has_prev:
        kp_ref, vp_ref = refs[:2]
        refs = refs[2:]
    o_ref, lse_ref, kvo_ref, bias_ref = refs[-4:]
    if len(refs) == 5:
        kvo_ref[0] = refs[0][...]
        kvo_ref = kvo_ref.at[1]
    step = pl.program_id(1)

    @pl.when((pl.program_id(0) == 0) & (step == 0))
    def _():
        bk = bk_ref[...]
        for h in range(NH):
            bias_ref[h] = _bias_from_buckets(bk, rb_ref, gi * NH + h)

    def tile(ref, h, t):
        rs = slice(t * CH, (t + 1) * CH)
        return ref[h, rs, :] if headed else ref[rs, h * HD:(h + 1) * HD]

    scale = HD ** -0.5
    lane_head = lax.broadcasted_iota(jnp.int32, (CH, NH * LSE_W), 1) // LSE_W
    if has_prev:
        col = lax.broadcasted_iota(jnp.int32, (CH, 2 * CH), 1)
        no_prev = (step == 0) & (col < CH)
    for t in range(ATT_SUB):
        if gi == 0:
            dst = slice(t * CH, (t + 1) * CH)
        elif gi == 1:
            dst = pl.ds(t * CH * dil + step, CH, stride=dil)
        else:
            dst = pl.ds(step * ATT_SUB + t, CH, stride=dil)
        keep = gi == 2 or t == ATT_SUB - 1
        kv_row0 = (t if gi == 2 else 0) * 2 * NH
        lse_tile = None
        for h in range(NH):
            kc = tile(k_ref, h, t)
            vc = tile(v_ref, h, t)
            if keep:
                kvo_ref[:, kv_row0 + h, :] = kc
                kvo_ref[:, kv_row0 + NH + h, :] = vc
            qh = tile(q_ref, h, t).astype(BF16)
            if gi < 2 and (t > 0 or has_prev):
                if t > 0:
                    kp, vp, bias = tile(k_ref, h, t - 1), tile(v_ref, h, t - 1), bias_ref[h]
                else:
                    kp, vp = kp_ref[:, h * HD:(h + 1) * HD], vp_ref[:, h * HD:(h + 1) * HD]
                    bias = jnp.where(no_prev, NEG, bias_ref[h])
                kh = jnp.concatenate([kp, kc], axis=0).astype(BF16)
                vh = jnp.concatenate([vp, vc], axis=0).astype(BF16)
            else:
                kh = kc.astype(BF16)
                vh = vc.astype(BF16)
                bias = bias_ref[h][:, CH:]
            s = lax.dot_general(qh, kh, (((1,), (1,)), ((), ())), preferred_element_type=F32) * scale + bias
            m = jnp.max(s, axis=-1, keepdims=True)
            p = jnp.exp(s - m)
            den = jnp.sum(p, axis=-1, keepdims=True)
            acc = jnp.dot(p.astype(BF16), vh, preferred_element_type=F32)
            o_ref[h, dst, :] = acc / den
            lse = m + jnp.log(den)
            lse_tile = lse if h == 0 else jnp.where(lane_head == h, lse, lse_tile)
        lse_ref[dst, :] = lse_tile


def _attn_prompt(zq, rel_bias, gi, kv_prev):
    dil = ATT_DILATIONS[gi]
    rows = BATCH * SEQ
    nsteps = SEQ // (ATT_SUB * CH)
    assert SEQ // dil // CH in (1, ATT_SUB, ATT_SUB * nsteps)
    headed = gi > 0
    bk = jnp.asarray(_prompt_bucket_matrix(dil))
    span = ATT_SUB * CH

    def zspec(c):
        if headed:
            return pl.BlockSpec((None, NH, span, HD), lambda b, s: (c, 0, b * nsteps + s, 0))
        return pl.BlockSpec((span, BW), lambda b, s: (b * nsteps + s, COL_QKV[0][c]))

    in_specs = [pl.BlockSpec(memory_space=pltpu.SMEM),
                pl.BlockSpec((CH, 2 * CH), lambda b, s: (0, 0)),
                zspec(0), zspec(1), zspec(2)]
    args = [rel_bias, bk, zq, zq, zq]
    if gi == 0:
        above = lambda c: pl.BlockSpec(
            (CH, BW), lambda b, s: (jnp.maximum((b * nsteps + s) * ATT_SUB - 1, 0), COL_QKV[0][c]))
        in_specs += [above(1), above(2)]
        args += [zq, zq]
        out_rows, out_index = span, (lambda b, s: b * nsteps + s)
    else:
        out_rows, out_index = SEQ, (lambda b, s: b)
    n_res = ATT_SUB if gi == 2 else 1
    kv_blk = (CH, n_res * 2 * NH, HD)
    kv_index = lambda b, s: (b, 0, s if gi else 0, 0)
    kv_shape = (BATCH, CH, dil * 2 * NH, HD)
    if kv_prev is None:
        kv_spec = pl.BlockSpec((None, *kv_blk), kv_index)
    else:
        in_specs.append(pl.BlockSpec((None, *kv_blk), kv_index))
        args.append(kv_prev)
        kv_spec = pl.BlockSpec((2, None, *kv_blk), lambda b, s: (0, *kv_index(b, s)))
        kv_shape = (2, *kv_shape)
    return pl.pallas_call(
        functools.partial(_attn_prompt_body, gi=gi, headed=headed),
        grid=(BATCH, nsteps),
        in_specs=in_specs,
        out_specs=[pl.BlockSpec((NH, out_rows, HD), lambda b, s: (0, out_index(b, s), 0)),
                   pl.BlockSpec((out_rows, NH * LSE_W), lambda b, s: (out_index(b, s), 0)),
                   kv_spec],
        out_shape=[jax.ShapeDtypeStruct((NH, rows, HD), F32),
                   jax.ShapeDtypeStruct((rows, NH * LSE_W), F32),
                   jax.ShapeDtypeStruct(kv_shape, F32)],
        scratch_shapes=[pltpu.VMEM((NH, CH, 2 * CH), F32)],
        compiler_params=_cparams(2),
        name=f"attn_prompt_g{gi}",
    )(*args)


def _merge_body(o0, l0, o1, l1, o2, l2, ob_ref, oc_ref, od_ref, gate_ref, wb_ref, wo_ref, x_ref, out_ref):
    heads = []
    for h in range(NH):
        a0, a1, a2 = (l[:, h * LSE_W:h * LSE_W + 1] for l in (l0, l1, l2))
        m = jnp.maximum(jnp.maximum(a0, a1), a2)
        w0, w1, w2 = jnp.exp(a0 - m), jnp.exp(a1 - m), jnp.exp(a2 - m)
        heads.append(((w0 * o0[h] + w1 * o1[h] + w2 * o2[h]) / (w0 + w1 + w2)).astype(BF16))
    oa = jnp.concatenate(heads, axis=1)
    acc = None
    for n, br in enumerate((oa, ob_ref[...], oc_ref[...], od_ref[...])):
        proj = jnp.dot(br, wb_ref[n], preferred_element_type=F32)
        t = gate_ref[:, n * D_MODEL:(n + 1) * D_MODEL].astype(F32) * proj
        acc = t if acc is None else acc + t
    out_ref[...] = x_ref[...] + jnp.dot(acc.astype(BF16), wo_ref[...], preferred_element_type=F32)


def _merge(parts, ob, oc, od, gates, wb, wo, x, *, tm=TM_MERGE):
    m = x.shape[0]
    once = dict(pipeline_mode=pl.Buffered(1))
    ospec = pl.BlockSpec((NH, tm, HD), lambda i: (0, i, 0))
    lspec = pl.BlockSpec((tm, NH * LSE_W), lambda i: (i, 0))
    br_spec = pl.BlockSpec((tm, BW), lambda i: (i, 0))
    row_spec = pl.BlockSpec((tm, D_MODEL), lambda i: (i, 0))
    return pl.pallas_call(
        _merge_body,
        grid=(m // tm,),
        in_specs=[ospec, lspec] * 3 + [br_spec] * 3 + [
            pl.BlockSpec((tm, GATE_W), lambda i: (i, 0)),
            pl.BlockSpec((4, BW, D_MODEL), lambda i: (0, 0, 0), **once),
            pl.BlockSpec((D_MODEL, D_MODEL), lambda i: (0, 0), **once),
            row_spec],
        out_specs=row_spec,
        out_shape=jax.ShapeDtypeStruct((m, D_MODEL), F32),
        compiler_params=_cparams(1, VMEM_BIG),
        name="merge",
    )(*parts, ob, oc, od, gates, wb, wo, x)


def _attn_sample_body(rb_ref, bc0_ref, bc1_ref, bc2_ref, bn_ref, z_ref, c0_ref, c1_ref, c2_ref,
                      oa_ref, kv0_ref, kv1_ref, kv2_ref, b0_ref, b1_ref, b2_ref, bnew_ref):
    bucket_refs = (bc0_ref, bc1_ref, bc2_ref)
    bias_refs = (b0_ref, b1_ref, b2_ref)

    @pl.when(pl.program_id(0) == 0)
    def _():
        for gi in range(3):
            for h in range(NH):
                rs = slice(h * SROWS, (h + 1) * SROWS)
                bias_refs[gi][rs, :] = _bias_from_buckets(bucket_refs[gi][rs, :], rb_ref, gi * NH + h)
                bnew_ref[gi, rs, :] = _bias_from_buckets(bn_ref[gi, rs, :], rb_ref, gi * NH + h)

    scale = HD ** -0.5
    rows = NH * SROWS
    head_of_row = lax.broadcasted_iota(jnp.int32, (rows, BW), 0) // SROWS
    head_of_lane = lax.broadcasted_iota(jnp.int32, (rows, BW), 1) // HD
    head_mask = head_of_row == head_of_lane
    caches = (c0_ref, c1_ref, c2_ref)
    kv_refs = (kv0_ref, kv1_ref, kv2_ref)
    stats = []
    for gi in range(3):
        cq, ck, cv = COL_QKV[gi]
        q = z_ref[:, cq * BW:(cq + 1) * BW]
        kn = z_ref[:, ck * BW:(ck + 1) * BW]
        vn = z_ref[:, cv * BW:(cv + 1) * BW]
        kv_refs[gi][:, :BW] = kn
        kv_refs[gi][:, BW:] = vn
        qm = jnp.concatenate([q[:, h * HD:(h + 1) * HD] for h in range(NH)], axis=0).astype(BF16)
        kf = caches[gi][...].reshape(SAMPLE_CACHE_ROWS[gi], HD).astype(BF16)
        s = lax.dot_general(qm, kf, (((1,), (1,)), ((), ())), preferred_element_type=F32) * scale + bias_refs[gi][...]
        qrows = jnp.where(head_mask, jnp.concatenate([q] * NH, axis=0), 0.0)
        bias_n = bnew_ref[gi]
        s_new = []
        for tp in range(DEC_SEQ):
            dotp = jnp.sum(qrows * kn[tp:tp + 1, :], axis=-1, keepdims=True)
            s_new.append(dotp * scale + bias_n[:, tp:tp + 1])
        m = jnp.max(s, axis=-1, keepdims=True)
        for sn in s_new:
            m = jnp.maximum(m, sn)
        p = jnp.exp(s - m)
        den = jnp.sum(p, axis=-1, keepdims=True)
        pv = pltpu.roll(p, NH, axis=1).astype(BF16)
        acc = jnp.dot(pv, kf, preferred_element_type=F32)
        acc_n = jnp.zeros((rows, BW), F32)
        for tp, sn in enumerate(s_new):
            pn = jnp.exp(sn - m)
            den = den + pn
            acc_n = acc_n + pn * vn[tp:tp + 1, :]
        acc = acc + jnp.concatenate([acc_n[h * SROWS:(h + 1) * SROWS, h * HD:(h + 1) * HD] for h in range(NH)], axis=0)
        stats.append((m, den, acc))
    mm = jnp.maximum(jnp.maximum(stats[0][0], stats[1][0]), stats[2][0])
    den = jnp.zeros((rows, 1), F32)
    acc = jnp.zeros((rows, HD), F32)
    for m, d, a in stats:
        w = jnp.exp(m - mm)
        den = den + w * d
        acc = acc + w * a
    o = acc / den
    for h in range(NH):
        oa_ref[:, h * HD:(h + 1) * HD] = o[h * SROWS:(h + 1) * SROWS, :].astype(BF16)


def _attn_sample(z, caches, rel_bias, layer):
    tables = _sample_bucket_tables()
    rows = DEC_BATCH * SROWS
    qrows = NH * SROWS
    n0, n1, n2 = SAMPLE_CACHE_ROWS
    c0 = caches[0].reshape(DEPTH, DEC_BATCH, n0, HD)
    c1 = caches[1].reshape(DEPTH, DEC_BATCH, n1, HD)
    c2 = caches[2].reshape(DEPTH, DEC_BATCH, CH, 16 * 8, HD)
    new_tbl = jnp.asarray(np.stack([t[1] for t in tables]))
    kv_spec = pl.BlockSpec((SROWS, 2 * BW), lambda b: (b, 0))
    kv_shape = jax.ShapeDtypeStruct((rows, 2 * BW), F32)
    const2 = lambda b: (0, 0)
    return pl.pallas_call(
        _attn_sample_body,
        grid=(DEC_BATCH,),
        in_specs=[pl.BlockSpec(memory_space=pltpu.SMEM),
                  pl.BlockSpec((qrows, n0), const2), pl.BlockSpec((qrows, n1), const2),
                  pl.BlockSpec((qrows, n2), const2),
                  pl.BlockSpec((3, qrows, CH), lambda b: (0, 0, 0)),
                  pl.BlockSpec((SROWS, IN_WIDTH), lambda b: (b, 0)),
                  pl.BlockSpec((None, None, n0, HD), lambda b: (layer, b, 0, 0)),
                  pl.BlockSpec((None, None, n1, HD), lambda b: (layer, b, 0, 0)),
                  pl.BlockSpec((None, None, CH, 8 * DEC_SEQ, HD), lambda b: (layer, b, 0, 0, 0))],
        out_specs=[pl.BlockSpec((SROWS, BW), lambda b: (b, 0)), kv_spec, kv_spec, kv_spec],
        out_shape=[jax.ShapeDtypeStruct((rows, BW), BF16), kv_shape, kv_shape, kv_shape],
        scratch_shapes=[pltpu.VMEM((qrows, n0), F32), pltpu.VMEM((qrows, n1), F32), pltpu.VMEM((qrows, n2), F32),
                        pltpu.VMEM((3, qrows, CH), F32)],
        compiler_params=_cparams(1),
        name="attn_sample",
    )(rel_bias, jnp.asarray(tables[0][0]), jnp.asarray(tables[1][0]), jnp.asarray(tables[2][0]), new_tbl,
      z, c0, c1, c2)


def _gmlp_body(bu_ref, bv_ref, g_ref, ws_ref, bs_ref, o_ref, *, rows):
    tril = lax.broadcasted_iota(jnp.int32, (CH, CH), 0) >= lax.broadcasted_iota(jnp.int32, (CH, CH), 1)
    w = [jnp.where(tril, ws_ref[g], 0.0).astype(BF16) for g in range(NH)]
    for c in range(rows // CH):
        rs = slice(c * CH, (c + 1) * CH)
        u = jax.nn.gelu(bu_ref[rs, :])
        vn = _rms(jax.nn.gelu(bv_ref[rs, :]), g_ref[...])
        for g in range(NH):
            sl = slice(g * HD, (g + 1) * HD)
            mixed = jnp.dot(w[g], vn[:, sl].astype(BF16), preferred_element_type=F32) + bs_ref[:, sl]
            o_ref[rs, sl] = (u[:, sl] * mixed).astype(BF16)


def _gmlp(z, g_gmlp, w_spatial, b_spatial, *, rows):
    m = z.shape[0]
    bs_full = jnp.repeat(b_spatial.T, HD, axis=1)
    return pl.pallas_call(
        functools.partial(_gmlp_body, rows=rows),
        grid=(m // rows,),
        in_specs=[pl.BlockSpec((rows, BW), lambda i: (i, COL_BU)),
                  pl.BlockSpec((rows, BW), lambda i: (i, COL_BV)),
                  pl.BlockSpec((1, BW), lambda i: (0, 0)),
                  pl.BlockSpec((NH, CH, CH), lambda i: (0, 0, 0)),
                  pl.BlockSpec((CH, BW), lambda i: (0, 0))],
        out_specs=pl.BlockSpec((rows, BW), lambda i: (i, 0)),
        out_shape=jax.ShapeDtypeStruct((m, BW), BF16),
        compiler_params=_cparams(1),
        name="gmlp",
    )(z, z, g_gmlp.reshape(1, -1), w_spatial, bs_full)


POOL_PAD = 16


def _pool_body(x_ref, prev_ref, wp_ref, sc_ref, o_ref, st_ref, ext_ref, *, rows):
    ib = pl.program_id(1)
    prev = jnp.where(ib == 0, 0.0, prev_ref[...])
    x = x_ref[...]
    ext_ref[0:POOL_PAD, :] = prev
    ext_ref[POOL_PAD:POOL_PAD + rows, :] = x
    st_ref[...] = ext_ref[pl.ds(rows + 1, POOL_STATE), :]
    ext = ext_ref[...]
    pos = ib * rows + lax.broadcasted_iota(jnp.int32, (rows, 1), 0)
    for gi, win in enumerate(POOL_WINDOWS):
        sl = slice(gi * HD, (gi + 1) * HD)
        s = ext[:, sl]
        k = 1
        while k < win:
            s = s + pltpu.roll(s, k, axis=0)
            k *= 2
        cnt = jnp.minimum(pos + 1, win).astype(F32)
        diff = s[POOL_PAD:] / cnt - x[:, sl]
        y = jnp.dot(diff.astype(BF16), wp_ref[gi].astype(BF16), preferred_element_type=F32)
        o_ref[:, sl] = (y * sc_ref[:, sl]).astype(BF16)


def _pool(z, w_pool, pool_scale, *, rows):
    m = z.shape[0]
    nblk = SEQ // rows
    per_pad = rows // POOL_PAD
    return pl.pallas_call(
        functools.partial(_pool_body, rows=rows),
        grid=(BATCH, nblk),
        in_specs=[pl.BlockSpec((rows, BW), lambda b, i: (b * nblk + i, COL_CIN)),
                  pl.BlockSpec((POOL_PAD, BW),
                               lambda b, i: (jnp.maximum((b * nblk + i) * per_pad - 1, 0), COL_CIN)),
                  pl.BlockSpec((NH, HD, HD), lambda b, i: (0, 0, 0)),
                  pl.BlockSpec((1, BW), lambda b, i: (0, 0))],
        out_specs=[pl.BlockSpec((rows, BW), lambda b, i: (b * nblk + i, 0)),
                   pl.BlockSpec((None, POOL_STATE, BW), lambda b, i: (b, 0, 0))],
        out_shape=[jax.ShapeDtypeStruct((m, BW), BF16),
                   jax.ShapeDtypeStruct((BATCH, POOL_STATE, BW), F32)],
        scratch_shapes=[pltpu.VMEM((POOL_PAD + rows, BW), F32)],
        compiler_params=_cparams(2),
        name="pool",
    )(z, z, w_pool, pool_scale.reshape(1, -1))


def _gmlp_sample_body(bu_ref, bv_ref, g_ref, wk_ref, bs_ref, o_ref, vn_ref):
    u = jax.nn.gelu(bu_ref[...])
    vn = _rms(jax.nn.gelu(bv_ref[...]), g_ref[...])
    vn_ref[...] = vn
    mixed = jnp.tile(bs_ref[...], (DEC_BATCH, 1))
    for k in range(DEC_SEQ):
        shifted = vn if k == 0 else pltpu.roll(vn, k, axis=0)
        mixed = mixed + jnp.tile(wk_ref[k], (DEC_BATCH, 1)) * shifted
    o_ref[...] = (u * mixed).astype(BF16)


def _gmlp_sample(z, g_gmlp, w_spatial, b_spatial):
    m = z.shape[0]
    t = np.arange(SROWS)
    live = t < DEC_SEQ
    pick = np.zeros((DEC_SEQ, SROWS, SROWS), np.float32)
    for k in range(DEC_SEQ):
        pick[k, t[live & (t >= k)], t[live & (t >= k)] - k] = 1.0
    diag = jnp.sum(w_spatial[None, :, :SROWS, :SROWS] * pick[:, None], axis=-1)
    wk = jnp.repeat(diag.transpose(0, 2, 1), HD, axis=2)
    bs = jnp.repeat((b_spatial[:, :SROWS] * jnp.asarray(live, F32)).T, HD, axis=1)
    return pl.pallas_call(
        _gmlp_sample_body,
        grid=(1,),
        in_specs=[pl.BlockSpec((m, BW), lambda i: (0, COL_BU)),
                  pl.BlockSpec((m, BW), lambda i: (0, COL_BV)),
                  pl.BlockSpec((1, BW), lambda i: (0, 0)),
                  pl.BlockSpec((DEC_SEQ, SROWS, BW), lambda i: (0, 0, 0)),
                  pl.BlockSpec((SROWS, BW), lambda i: (0, 0))],
        out_specs=[pl.BlockSpec((m, BW), lambda i: (0, 0))] * 2,
        out_shape=[jax.ShapeDtypeStruct((m, BW), BF16), jax.ShapeDtypeStruct((m, BW), F32)],
        compiler_params=_cparams(1),
        name="gmlp_sample",
    )(z, z, g_gmlp.reshape(1, -1), wk, bs)


def _pool_sample_body(x_ref, prev_ref, wp_ref, sc_ref, o_ref, st_ref, ext_ref, *, start):
    per = POOL_PAD + SROWS
    ext_ref[:, 0:POOL_PAD, :] = prev_ref[...]
    ext_ref[:, POOL_PAD:per, :] = x_ref[...]
    st_ref[...] = ext_ref[:, pl.ds(DEC_SEQ + 1, POOL_STATE), :]
    ext = ext_ref[...].reshape(DEC_BATCH * per, BW)
    pos = start + (lax.broadcasted_iota(jnp.int32, (DEC_BATCH * per, 1), 0) % per - POOL_PAD)
    for gi, win in enumerate(POOL_WINDOWS):
        sl = slice(gi * HD, (gi + 1) * HD)
        x = ext[:, sl]
        s = x
        k = 1
        while k < win:
            s = s + pltpu.roll(s, k, axis=0)
            k *= 2
        cnt = jnp.minimum(jnp.maximum(pos, 0) + 1, win).astype(F32)
        y = jnp.dot((s / cnt - x).astype(BF16), wp_ref[gi].astype(BF16), preferred_element_type=F32)
        o_ref[:, :, sl] = (y * sc_ref[:, sl]).reshape(DEC_BATCH, per, HD)[:, POOL_PAD:, :].astype(BF16)


def _pool_sample(z, prev, w_pool, pool_scale, *, start, layer):
    m, zw = z.shape
    z3 = z.reshape(DEC_BATCH, SROWS, zw)
    o, st = pl.pallas_call(
        functools.partial(_pool_sample_body, start=start),
        grid=(1,),
        in_specs=[pl.BlockSpec((DEC_BATCH, SROWS, BW), lambda i: (0, 0, COL_CIN)),
                  pl.BlockSpec((None, DEC_BATCH, POOL_PAD, BW), lambda i: (layer, 0, 0, 0)),
                  pl.BlockSpec((NH, HD, HD), lambda i: (0, 0, 0)),
                  pl.BlockSpec((1, BW), lambda i: (0, 0))],
        out_specs=[pl.BlockSpec((DEC_BATCH, SROWS, BW), lambda i: (0, 0, 0)),
                   pl.BlockSpec((DEC_BATCH, POOL_STATE, BW), lambda i: (0, 0, 0))],
        out_shape=[jax.ShapeDtypeStruct((DEC_BATCH, SROWS, BW), BF16),
                   jax.ShapeDtypeStruct((DEC_BATCH, POOL_STATE, BW), F32)],
        scratch_shapes=[pltpu.VMEM((DEC_BATCH, POOL_PAD + SROWS, BW), F32)],
        compiler_params=_cparams(1),
        name="pool_sample",
    )(z3, prev, w_pool, pool_scale.reshape(1, -1))
    return o.reshape(m, BW), st


def _ret_tables(c_eff, positions, size):
    lg = np.log1p(-np.power(2.0, -5.0 - np.arange(NH, dtype=np.float64)))
    i = np.arange(size, dtype=np.float64)
    live = (i < c_eff)
    diff = i[:, None] - i[None, :]
    inner = np.where((diff >= 0) & live[:, None] & live[None, :], np.exp(np.maximum(diff, 0.0)[None] * lg[:, None, None]), 0.0)
    qd = np.where(live[None, :], np.exp((i + 1.0)[None, :] * lg[:, None]), 0.0)
    kd = np.where(live[None, :], np.exp((c_eff - 1.0 - i)[None, :] * lg[:, None]), 0.0)
    chunk = tuple(float(v) for v in np.exp(c_eff * lg))
    qd_full = np.repeat(qd.T, HD, axis=1)
    kd_full = np.repeat(kd.T, HD, axis=1) * (HD ** -0.5)
    half = HD // 2
    inv = ROPE_BASE ** (-np.arange(half, dtype=np.float64) / half)
    ang = np.asarray(positions, np.float64)[:, None] * inv[None, :]
    cosf = np.concatenate([np.cos(ang), np.cos(ang)], axis=1)
    sinf = np.concatenate([-np.sin(ang), np.sin(ang)], axis=1)
    to32 = lambda a: jnp.asarray(a.astype(np.float32))
    return to32(inner), to32(qd_full), to32(kd_full), chunk, to32(cosf), to32(sinf)


def _ret_body(q_ref, k_ref, v_ref, g_ref, cos_ref, sin_ref, inner_ref, qd_ref, kd_ref, gr_ref, *refs,
              nseq, rows, chunk_decay, has_state):
    refs = list(refs)
    s0_ref = refs.pop(0) if has_state else None
    o_ref, sn_ref, s_ref = refs[:3]
    pad_ref = refs[3] if rows < CH else None
    ic = pl.program_id(1)

    @pl.when(ic == 0)
    def _():
        if has_state:
            s_ref[...] = s0_ref[...]
        else:
            s_ref[...] = jnp.zeros_like(s_ref)

    def chunk(ref, b, k):
        if rows >= CH:
            return ref[b]
        pad_ref[k] = jnp.zeros((CH, BW), F32)
        pad_ref[k, 0:rows, :] = ref[b]
        return pad_ref[k]

    cosf = cos_ref[...]
    sinf = sin_ref[...]
    for b in range(nseq):
        q = chunk(q_ref, b, 0)
        k = chunk(k_ref, b, 1)
        v = chunk(v_ref, b, 2)
        gate = chunk(g_ref, b, 3)
        for h in range(NH):
            sl = slice(h * HD, (h + 1) * HD)
            qh = q[:, sl]
            kh = k[:, sl]
            rq = qh * cosf + pltpu.roll(qh, HD // 2, axis=1) * sinf
            rk = kh * cosf + pltpu.roll(kh, HD // 2, axis=1) * sinf
            vb = v[:, sl].astype(BF16)
            rqb = rq.astype(BF16)
            state = s_ref[b, h]
            att = lax.dot_general(rqb, (rk * (HD ** -0.5)).astype(BF16), (((1,), (1,)), ((), ())),
                                  preferred_element_type=F32) * inner_ref[h]
            o = (jnp.dot(att.astype(BF16), vb, preferred_element_type=F32)
                 + jnp.dot(rqb, state.astype(BF16), preferred_element_type=F32) * qd_ref[:, sl])
            kdec = (rk * kd_ref[:, sl]).astype(BF16)
            new_state = state * chunk_decay[h] + lax.dot_general(
                kdec, vb, (((0,), (0,)), ((), ())), preferred_element_type=F32)
            s_ref[b, h] = new_state
            sn_ref[b, h] = new_state
            o = o * lax.rsqrt(jnp.mean(o * o, axis=-1, keepdims=True) + EPS)
            gt = gate[:, sl]
            o_ref[b, :, sl] = (o * gr_ref[:, sl] * (gt * jax.nn.sigmoid(gt)))[:rows].astype(BF16)


def _retention(z, g_ret, state, *, nseq, per_step, rows, c_eff, positions, layer=None):
    m, zw = z.shape
    seq_rows = m // nseq
    nchunk = seq_rows // rows
    size = max(rows, CH)
    inner, qd, kd, chunk_decay, cosf, sinf = _ret_tables(c_eff, positions, size)
    has_state = state is not None
    z3 = z.reshape(nseq, seq_rows, zw)

    def zspec(col):
        return pl.BlockSpec((per_step, rows, BW), lambda g, i: (g, i, col))

    const2 = lambda g, i: (0, 0)
    in_specs = [zspec(COL_DQ), zspec(COL_DK), zspec(COL_DV), zspec(COL_DG),
                pl.BlockSpec((size, HD), lambda g, i: (i, 0)), pl.BlockSpec((size, HD), lambda g, i: (i, 0)),
                pl.BlockSpec((NH, size, size), lambda g, i: (0, 0, 0)),
                pl.BlockSpec((size, BW), const2), pl.BlockSpec((size, BW), const2), pl.BlockSpec((1, BW), const2)]
    args = [z3, z3, z3, z3, cosf, sinf, inner, qd, kd, g_ret.reshape(1, -1)]
    if has_state:
        in_specs.append(pl.BlockSpec((None, per_step, NH, HD, HD), lambda g, i: (layer, g, 0, 0, 0)))
        args.append(state)
    scratch = [pltpu.VMEM((per_step, NH, HD, HD), F32)]
    if rows < CH:
        scratch.append(pltpu.VMEM((4, CH, BW), F32))
    o, sn = pl.pallas_call(
        functools.partial(_ret_body, nseq=per_step, rows=rows, chunk_decay=chunk_decay, has_state=has_state),
        grid=(nseq // per_step, nchunk),
        in_specs=in_specs,
        out_specs=[pl.BlockSpec((per_step, rows, BW), lambda g, i: (g, i, 0)),
                   pl.BlockSpec((per_step, NH, HD, HD), lambda g, i: (g, 0, 0, 0))],
        out_shape=[jax.ShapeDtypeStruct((nseq, seq_rows, BW), BF16),
                   jax.ShapeDtypeStruct((nseq, NH, HD, HD), F32)],
        scratch_shapes=scratch,
        compiler_params=_cparams(2),
        name="retention",
    )(*args)
    return o.reshape(m, BW), sn


def kernel(x_prompt, x_sample, cache_attn_kv_w128, cache_attn_kv_w512, cache_attn_kv_w2048, state_pool, state_ret, rel_bias, g_ffn1, w_ffn1_gate, w_ffn1_up, w_ffn1_down, g_mix, w_in, g_gmlp, w_spatial, b_spatial, w_pool, pool_scale, g_ret, w_branch, w_out, g_ffn2, w_ffn2_gate, w_ffn2_up, w_ffn2_down, g_final):
    caches = (cache_attn_kv_w128, cache_attn_kv_w512, cache_attn_kv_w2048)
    xp = x_prompt.reshape(BATCH * SEQ, D_MODEL)
    xs = jnp.pad(x_sample, ((0, 0), (0, SROWS - DEC_SEQ), (0, 0))).reshape(DEC_BATCH * SROWS, D_MODEL)
    pool_state = jnp.pad(state_pool, ((0, 0), (0, 0), (POOL_PAD - POOL_STATE, 0), (0, 0)))
    tm_p, tm_s = TM_FFN, DEC_BATCH * SROWS
    sample_pos = PAST_LEN + np.arange(CH)
    gate_col0 = N_PLAIN * BW // TN_IN
    qkv1, qkv2 = COL_QKV[1][0], COL_QKV[2][0]

    kv_p = [None, None, None]
    kv_s = [[], [], []]
    pool_p, pool_s, ret_p, ret_s, gv_s = [], [], [], [], []
    yp = ys = None
    for l in range(DEPTH):
        last = l == DEPTH - 1

        xs, hs, *w1 = _ffn(xs, g_ffn1[l], (w_ffn1_gate, w_ffn1_up, w_ffn1_down), g_mix[l], layer=l,
                           emit_x=True, post_dtype=BF16, tm=tm_s)
        zs, gs, win = _in_proj_convert(hs, w_in, layer=l)
        oa, k0, k1, k2 = _attn_sample(zs, caches, rel_bias, l)
        for gi, kv in enumerate((k0, k1, k2)):
            kv_s[gi].append(kv.reshape(DEC_BATCH, SROWS, 2, NH, HD)[:, :DEC_SEQ])
        ob, vn = _gmlp_sample(zs, g_gmlp[l], w_spatial[l], b_spatial[l])
        gv_s.append(vn.reshape(DEC_BATCH, SROWS, BW)[:, :DEC_SEQ])
        oc, pn = _pool_sample(zs, pool_state, w_pool[l], pool_scale[l], start=PAST_LEN, layer=l)
        od, rn = _retention(zs, g_ret[l], state_ret, nseq=DEC_BATCH, per_step=8, rows=SROWS, c_eff=DEC_SEQ,
                            positions=sample_pos, layer=l)
        pool_s.append(pn)
        ret_s.append(rn)
        mix, wb = _branch_merge(oa, ob, oc, od, gs, w_branch, layer=l)
        xs, wo = _out_proj_convert(mix, w_out, xs, layer=l, tn=TN_OUT_S)
        if last:
            ys, *w2 = _ffn(xs, g_ffn2[l], (w_ffn2_gate, w_ffn2_up, w_ffn2_down), g_final, layer=l,
                           emit_x=False, post_dtype=F32, tm=tm_s)
        else:
            xs, *w2 = _ffn(xs, g_ffn2[l], (w_ffn2_gate, w_ffn2_up, w_ffn2_down), g_final, layer=l,
                           emit_x=True, post_dtype=None, tm=tm_s)

        xp, hp = _ffn(xp, g_ffn1[l], w1, g_mix[l], emit_x=True, post_dtype=BF16, tm=tm_p)
        zp = _matmul(hp, win, n=N_PLAIN * BW, tm=TM_IN, tn=TN_IN, name="in_proj")
        gp = _matmul(hp, win, col0=gate_col0, n=GATE_W, gate=True, tm=TM_IN, tn=TN_IN, name="in_proj_gate")
        zq = (zp,
              _matmul_deint(hp, win, col0=qkv1, dil=ATT_DILATIONS[1], name="in_proj_g1"),
              _matmul_deint(hp, win, col0=qkv2, dil=ATT_DILATIONS[2], name="in_proj_g2"))
        parts = []
        for gi in range(3):
            o, lse, kv_p[gi] = _attn_prompt(zq[gi], rel_bias, gi, kv_p[gi])
            parts += [o, lse]
        ob = _gmlp(zp, g_gmlp[l], w_spatial[l], b_spatial[l], rows=MIX_ROWS)
        oc, pn = _pool(zp, w_pool[l], pool_scale[l], rows=MIX_ROWS)
        od, rn = _retention(zp, g_ret[l], None, nseq=BATCH, per_step=BATCH, rows=CH, c_eff=CH,
                            positions=np.arange(SEQ))
        pool_p.append(pn)
        ret_p.append(rn)
        xp = _merge(parts, ob, oc, od, gp, wb, wo, xp)
        if last:
            yp, = _ffn(xp, g_ffn2[l], w2, g_final, emit_x=False, post_dtype=F32, tm=tm_p)
        else:
            xp, = _ffn(xp, g_ffn2[l], w2, g_final, emit_x=True, post_dtype=None, tm=tm_p)

    y_prompt = yp.reshape(BATCH, SEQ, D_MODEL)
    y_sample = ys.reshape(DEC_BATCH, SROWS, D_MODEL)[:, :DEC_SEQ]
    kv_p = [kv.reshape(DEPTH, BATCH, ATT_WINDOWS[gi], 2, NH, HD) for gi, kv in enumerate(kv_p)]
    return (y_prompt, y_sample,
            kv_p[0], kv_p[1], kv_p[2],
            jnp.stack(kv_s[0]), jnp.stack(kv_s[1]), jnp.stack(kv_s[2]),
            jnp.stack(pool_p), jnp.stack(pool_s),
            jnp.stack(ret_p), jnp.stack(ret_s),
            jnp.stack(gv_s))
```
